```python
import jax, jax.numpy as jnp
from jax import lax
import numpy as np

D_MODEL = 1024
BATCH = 32
SEQ = 2048
DEPTH = 1

D_MIX = D_MODEL
D_GMLP = D_MIX // 2
GMLP_GROUPS = 4
GMLP_GROUP_DIM = D_GMLP // GMLP_GROUPS
GMLP_CHUNK = 128
D_HGRN = D_MIX - D_GMLP
HGRN_HEADS = 4
HGRN_DK = D_HGRN // HGRN_HEADS
HGRN_DV = D_HGRN // HGRN_HEADS
HGRN_CHUNK = 64
D_IN_PROJ = 2 * D_GMLP + 4 * D_HGRN
N_MEM = 256
XATTN_HEADS = 4
XATTN_HEAD_DIM = D_MODEL // XATTN_HEADS
N_GROUPS = 4
EXPERTS_PER_GROUP = 8
N_EXPERTS = N_GROUPS * EXPERTS_PER_GROUP
TOP_K_IN_GROUP = 2
D_EXPERT = D_MODEL // 2
EXPERT_BLOCK = 256
EPS = 1e-6

kernel_name = "hybrid_gmlp_hgrn2_xattn_hmoe"


def rms_norm(x, gain):
    x32 = x.astype(jnp.float32)
    y = x32 * lax.rsqrt(jnp.mean(x32 * x32, axis=-1, keepdims=True) + EPS)
    return (y * gain.astype(jnp.float32)).astype(x.dtype)


def layer_norm(x, gain):
    x32 = x.astype(jnp.float32)
    mu = jnp.mean(x32, axis=-1, keepdims=True)
    xc = x32 - mu
    y = xc * lax.rsqrt(jnp.mean(xc * xc, axis=-1, keepdims=True) + EPS)
    return (y * gain.astype(jnp.float32)).astype(x.dtype)


def gmlp_heads(u, v, ln_gain, w_s, b_s, beta):
    B, S, _ = u.shape
    n_chunks = S // GMLP_CHUNK
    u = jax.nn.gelu(u)
    v = layer_norm(jax.nn.gelu(v), ln_gain)
    v = v.reshape(B, n_chunks, GMLP_CHUNK, GMLP_GROUPS, GMLP_GROUP_DIM)
    causal = jnp.tril(jnp.ones((GMLP_CHUNK, GMLP_CHUNK), dtype=bool))
    w = jnp.where(causal[None], w_s, jnp.zeros((), w_s.dtype))
    z = jnp.einsum('gts,bnsgc->bntgc', w, v) + b_s.T[:, :, None]
    y = u * z.reshape(B, S, D_GMLP)
    return rms_norm(y, beta)


def hgrn2_chunked(q, k, v, log_f):
    B, S, H, Dk = q.shape
    Dv = v.shape[-1]
    n_chunks = S // HGRN_CHUNK

    def to_chunks(t):
        return t.reshape(B, n_chunks, HGRN_CHUNK, H, t.shape[-1]).transpose(1, 0, 3, 2, 4)

    q, k, v, log_f = to_chunks(q), to_chunks(k), to_chunks(v), to_chunks(log_f)
    b = jnp.cumsum(log_f, axis=-2)
    b_last = b[..., -1:, :]
    q_dec = q * jnp.exp(b)
    k_inv = k * jnp.exp(-b)
    k_to_end = k * jnp.exp(b_last - b)
    causal = jnp.tril(jnp.ones((HGRN_CHUNK, HGRN_CHUNK), dtype=bool))
    scores = jnp.einsum('nbhid,nbhjd->nbhij', q_dec, k_inv)
    scores = jnp.where(causal, scores, jnp.zeros((), scores.dtype))
    o_intra = jnp.einsum('nbhij,nbhjv->nbhiv', scores, v)

    def step(state, inp):
        q_c, k_c, v_c, decay_c = inp
        o_c = jnp.einsum('bhid,bhdv->bhiv', q_c, state)
        state = decay_c[..., 0, :, None] * state + jnp.einsum('bhjd,bhjv->bhdv', k_c, v_c)
        return state, o_c

    state0 = jnp.zeros((B, H, Dk, Dv), jnp.float32)
    _, o_inter = lax.scan(step, state0, (q_dec, k_to_end, v, jnp.exp(b_last)))
    o = o_intra + o_inter
    return o.transpose(1, 0, 3, 2, 4).reshape(B, S, H, Dv)


def hgrn2_heads(q, f_logit, i, g, lower_bound, out_gain):
    B, S, _ = q.shape
    f32 = jnp.float32
    qk_shape = (B, S, HGRN_HEADS, HGRN_DK)
    qh = jax.nn.silu(q.astype(f32)).reshape(qk_shape)
    f = lower_bound + (1.0 - lower_bound) * jax.nn.sigmoid(f_logit.astype(f32))
    kh = (1.0 - f).reshape(qk_shape)
    log_f = jnp.log(f).reshape(qk_shape)
    vh = i.astype(f32).reshape(B, S, HGRN_HEADS, HGRN_DV)
    o = hgrn2_chunked(qh, kh, vh, log_f)
    o = o * lax.rsqrt(jnp.mean(o * o, axis=-1, keepdims=True) + EPS)
    o = o.reshape(B, S, D_HGRN) * out_gain.astype(f32) * jax.nn.silu(g.astype(f32))
    return o.astype(g.dtype)


def memory_cross_attention(h, mem, w_q, w_kv, w_o):
    B, S, _ = h.shape
    M = mem.shape[1]
    q = (h @ w_q).reshape(B, S, XATTN_HEADS, XATTN_HEAD_DIM)
    k, v = jnp.split(mem @ w_kv, 2, axis=-1)
    k = k.reshape(B, M, XATTN_HEADS, XATTN_HEAD_DIM)
    v = v.reshape(B, M, XATTN_HEADS, XATTN_HEAD_DIM)
    s = jnp.einsum('bshd,bmhd->bhsm', q, k).astype(jnp.float32) * (XATTN_HEAD_DIM ** -0.5)
    p = jax.nn.softmax(s, axis=-1).astype(v.dtype)
    o = jnp.einsum('bhsm,bmhd->bshd', p, v).reshape(B, S, D_MODEL)
    return o @ w_o


def routed_expert_ffn(xt, expert_ids, gates, w_gate, w_up, w_down):
    T, K = expert_ids.shape
    A = T * K
    n_blocks = -(-A // EXPERT_BLOCK) + N_EXPERTS
    flat_e = expert_ids.reshape(A)
    order = jnp.argsort(flat_e)
    sorted_e = flat_e[order]
    counts = jnp.zeros((N_EXPERTS,), jnp.int32).at[flat_e].add(1)
    padded = (counts + EXPERT_BLOCK - 1) // EXPERT_BLOCK * EXPERT_BLOCK
    seg_start = jnp.cumsum(counts) - counts
    pad_end = jnp.cumsum(padded)
    pad_start = pad_end - padded
    dest_sorted = pad_start[sorted_e] + (jnp.arange(A, dtype=jnp.int32) - seg_start[sorted_e])
    slot_token = jnp.full((n_blocks * EXPERT_BLOCK,), T, jnp.int32).at[dest_sorted].set(
        (order // K).astype(jnp.int32))
    x_pad = jnp.concatenate([xt, jnp.zeros((1, xt.shape[1]), xt.dtype)], axis=0)
    xb = x_pad[slot_token].reshape(n_blocks, EXPERT_BLOCK, xt.shape[1])
    block_start = jnp.arange(n_blocks, dtype=jnp.int32) * EXPERT_BLOCK
    block_e = jnp.minimum(jnp.searchsorted(pad_end, block_start, side='right'), N_EXPERTS - 1)

    def expert_block(args):
        xb_i, e = args
        hid = jax.nn.silu(xb_i @ w_gate[e]) * (xb_i @ w_up[e])
        return hid @ w_down[e]

    yb = lax.map(expert_block, (xb, block_e)).reshape(n_blocks * EXPERT_BLOCK, xt.shape[1])
    dest = jnp.zeros((A,), jnp.int32).at[order].set(dest_sorted)
    y = yb[dest].reshape(T, K, xt.shape[1])
    return jnp.einsum('tk,tkd->td', gates.astype(y.dtype), y)


def hierarchical_moe(x, w_rg, b_rg, w_re, b_re, w_gate, w_up, w_down):
    B, S, D = x.shape
    T = B * S
    f32 = jnp.float32
    xt = x.reshape(T, D)
    g_logits = (xt @ w_rg).astype(f32) + b_rg.astype(f32)
    g_prob = jax.nn.softmax(g_logits, axis=-1)
    _, g_sel = lax.top_k(g_logits, 1)
    g_gate = jnp.take_along_axis(g_prob, g_sel, axis=-1)
    e_logits = ((xt @ w_re).astype(f32) + b_re.astype(f32)).reshape(T, N_GROUPS, EXPERTS_PER_GROUP)
    e_in_group = jnp.take_along_axis(e_logits, g_sel[:, :, None], axis=1)[:, 0]
    top_vals, top_idx = lax.top_k(e_in_group, TOP_K_IN_GROUP)
    gates = jax.nn.softmax(top_vals, axis=-1) * g_gate
    expert_ids = g_sel * EXPERTS_PER_GROUP + top_idx
    y = routed_expert_ffn(xt, expert_ids, gates, w_gate, w_up, w_down)
    return y.reshape(B, S, D)


def setup_inputs(seed: int = 0) -> dict:
    key = jax.random.key(seed)
    ks = jax.random.split(key, 26)
    f32 = jnp.float32

    def normal(k, shape, scale):
        return jax.random.normal(k, shape, f32) * scale

    def gain(k, shape):
        return 1.0 + 0.02 * jax.random.normal(k, shape, f32)

    L = DEPTH
    return {
        "x": normal(ks[0], (BATCH, SEQ, D_MODEL), 1.0),
        "mem": normal(ks[1], (BATCH, N_MEM, D_MODEL), 1.0),
        "norm_mix": gain(ks[2], (L, D_MODEL)),
        "w_in": normal(ks[3], (L, D_MODEL, D_IN_PROJ), D_MODEL ** -0.5),
        "gmlp_ln": gain(ks[4], (L, D_GMLP)),
        "gmlp_w_spatial": normal(ks[5], (L, GMLP_GROUPS, GMLP_CHUNK, GMLP_CHUNK), GMLP_CHUNK ** -0.5),
        "gmlp_b_spatial": gain(ks[6], (L, GMLP_GROUPS, GMLP_CHUNK)),
        "gmlp_beta": gain(ks[7], (L, D_GMLP)),
        "hgrn_lb_logits": normal(ks[8], (L + 1, D_HGRN), 0.1),
        "hgrn_out_gain": gain(ks[9], (L, D_HGRN)),
        "w_out": normal(ks[10], (L, D_MIX, D_MODEL), D_MIX ** -0.5),
        "norm_xattn": gain(ks[11], (L, D_MODEL)),
        "norm_mem": gain(ks[12], (L, D_MODEL)),
        "w_xq": normal(ks[13], (L, D_MODEL, D_MODEL), D_MODEL ** -0.5),
        "w_xkv": normal(ks[14], (L, D_MODEL, 2 * D_MODEL), D_MODEL ** -0.5),
        "w_xo": normal(ks[15], (L, D_MODEL, D_MODEL), D_MODEL ** -0.5),
        "norm_ffn": gain(ks[16], (L, D_MODEL)),
        "w_router_group": normal(ks[17], (L, D_MODEL, N_GROUPS), D_MODEL ** -0.5),
        "b_router_group": normal(ks[18], (L, N_GROUPS), 0.01),
        "w_router_expert": normal(ks[19], (L, D_MODEL, N_EXPERTS), D_MODEL ** -0.5),
        "b_router_expert": normal(ks[20], (L, N_EXPERTS), 0.01),
        "w_expert_gate": normal(ks[21], (L, N_EXPERTS, D_MODEL, D_EXPERT), D_MODEL ** -0.5),
        "w_expert_up": normal(ks[22], (L, N_EXPERTS, D_MODEL, D_EXPERT), D_MODEL ** -0.5),
        "w_expert_down": normal(ks[23], (L, N_EXPERTS, D_EXPERT, D_MODEL), D_EXPERT ** -0.5),
        "norm_final": gain(ks[24], (D_MODEL,)),
    }


def reference(x, mem, norm_mix, w_in, gmlp_ln, gmlp_w_spatial, gmlp_b_spatial, gmlp_beta,
              hgrn_lb_logits, hgrn_out_gain, w_out, norm_xattn, norm_mem, w_xq, w_xkv, w_xo,
              norm_ffn, w_router_group, b_router_group, w_router_expert, b_router_expert,
              w_expert_gate, w_expert_up, w_expert_down, norm_final):
    lower_bounds = jnp.cumsum(jax.nn.softmax(hgrn_lb_logits.astype(jnp.float32), axis=0), axis=0)
    split_at = [D_GMLP, 2 * D_GMLP, 2 * D_GMLP + D_HGRN, 2 * D_GMLP + 2 * D_HGRN,
                2 * D_GMLP + 3 * D_HGRN]
    h = x
    for l in range(DEPTH):
        a = rms_norm(h, norm_mix[l])
        u, v, q, f_logit, i, g = jnp.split(a @ w_in[l], split_at, axis=-1)
        y_a = gmlp_heads(u, v, gmlp_ln[l], gmlp_w_spatial[l], gmlp_b_spatial[l], gmlp_beta[l])
        y_b = hgrn2_heads(q, f_logit, i, g, lower_bounds[l], hgrn_out_gain[l])
        h = h + jnp.concatenate([y_a, y_b], axis=-1) @ w_out[l]
        h = h + memory_cross_attention(rms_norm(h, norm_xattn[l]), rms_norm(mem, norm_mem[l]),
                                       w_xq[l], w_xkv[l], w_xo[l])
        h = h + hierarchical_moe(rms_norm(h, norm_ffn[l]), w_router_group[l], b_router_group[l],
                                 w_router_expert[l], b_router_expert[l], w_expert_gate[l],
                                 w_expert_up[l], w_expert_down[l])
    return rms_norm(h, norm_final)
```

```python
import functools

import jax
import jax.numpy as jnp
from jax import lax
from jax.experimental import pallas as pl
from jax.experimental.pallas import tpu as pltpu

F32 = jnp.float32
BF16 = jnp.bfloat16
EPS = 1e-6

D_MODEL = 1024
D_GMLP = 512
GMLP_GROUPS = 4
GMLP_CHUNK = 128
D_HGRN = 512
HGRN_HEADS = 4
HGRN_DK = 128
HGRN_CHUNK = 64
D_IN_PROJ = 2 * D_GMLP + 4 * D_HGRN
N_MEM = 256
XATTN_HEADS = 4
XATTN_HEAD_DIM = D_MODEL // XATTN_HEADS
N_GROUPS = 4
EXPERTS_PER_GROUP = 8
N_EXPERTS = N_GROUPS * EXPERTS_PER_GROUP
D_EXPERT = 512

LANES = 128
MIX_ROWS = 256
ATT_ROWS = 512
ROUTER_ROWS = 40
ROUTE_SUB = 16
EXPERT_ROWS = 256
MOVE_ROWS = 256
VMEM_LIMIT = 48 * 1024 * 1024


def _rms(x, gain):
    return x * lax.rsqrt(jnp.mean(x * x, axis=-1, keepdims=True) + EPS) * gain


def _dot(a, b):
    return jnp.dot(a, b, preferred_element_type=F32)


def _dot_nt(a, b):
    return lax.dot_general(a, b, (((1,), (1,)), ((), ())), preferred_element_type=F32)


def _dot_tn(a, b):
    return lax.dot_general(a, b, (((0,), (0,)), ((), ())), preferred_element_type=F32)


def _gelu(x):
    return 0.5 * x * (1.0 + jnp.tanh(0.7978845608028654 * (x + 0.044715 * (x * x * x))))


def _sigmoid(x):
    return 1.0 / (1.0 + jnp.exp(-x))


def _block_id(idx, size):
    assert size & (size - 1) == 0
    return lax.shift_right_logical(idx, size.bit_length() - 1)


def _mixer_kernel(x_ref, nmix_ref, win_ref, gln_ref, ws_ref, bst_ref, beta_ref, lbl_ref, og_ref,
                  wout_ref, o_ref, proj_ref, ycat_ref, state_ref):
    rows = x_ref.shape[1]

    @pl.when(pl.program_id(1) == 0)
    def _():
        state_ref[...] = jnp.zeros_like(state_ref)

    x = x_ref[0]
    a = _rms(x, nmix_ref[...]).astype(BF16)
    proj_ref[...] = _dot(a, win_ref[...])

    u = _gelu(proj_ref[:, 0:D_GMLP])
    v = _gelu(proj_ref[:, D_GMLP:2 * D_GMLP])
    vc = v - jnp.mean(v, axis=-1, keepdims=True)
    vn = (vc * lax.rsqrt(jnp.mean(vc * vc, axis=-1, keepdims=True) + EPS) * gln_ref[...]).astype(BF16)
    r_i = lax.broadcasted_iota(jnp.int32, (GMLP_CHUNK, GMLP_CHUNK), 0)
    c_i = lax.broadcasted_iota(jnp.int32, (GMLP_CHUNK, GMLP_CHUNK), 1)
    causal = c_i <= r_i
    w_tril = [jnp.where(causal, ws_ref[g], 0.0).astype(BF16) for g in range(GMLP_GROUPS)]
    z_rows = []
    for c in range(rows // GMLP_CHUNK):
        z_cols = []
        for g in range(GMLP_GROUPS):
            vg = vn[c * GMLP_CHUNK:(c + 1) * GMLP_CHUNK, g * LANES:(g + 1) * LANES]
            z_cols.append(_dot(w_tril[g], vg) + bst_ref[:, g:g + 1])
        z_rows.append(jnp.concatenate(z_cols, axis=1))
    z = jnp.concatenate(z_rows, axis=0)
    ycat_ref[:, 0:D_GMLP] = _rms(u * z, beta_ref[...]).astype(BF16)

    lbl = lbl_ref[...]
    e_lb = jnp.exp(lbl - jnp.max(lbl, axis=0, keepdims=True))
    lb = e_lb[0:1] / jnp.sum(e_lb, axis=0, keepdims=True)
    base = 2 * D_GMLP
    f = lb + (1.0 - lb) * _sigmoid(proj_ref[:, base + D_HGRN:base + 2 * D_HGRN])
    log_f = jnp.log(f)
    rr = lax.broadcasted_iota(jnp.int32, (rows, rows), 0)
    cc = lax.broadcasted_iota(jnp.int32, (rows, rows), 1)
    tri = jnp.where((_block_id(rr, HGRN_CHUNK) == _block_id(cc, HGRN_CHUNK)) & (cc <= rr),
                    1.0, 0.0).astype(BF16)
    lf_hi = log_f.astype(BF16)
    lf_lo = (log_f - lf_hi.astype(F32)).astype(BF16)
    b_all = _dot(tri, lf_hi) + _dot(tri, lf_lo)
    r64 = lax.broadcasted_iota(jnp.int32, (HGRN_CHUNK, HGRN_CHUNK), 0)
    c64 = lax.broadcasted_iota(jnp.int32, (HGRN_CHUNK, HGRN_CHUNK), 1)
    causal64 = c64 <= r64
    for c in range(rows // HGRN_CHUNK):
        rs = slice(c * HGRN_CHUNK, (c + 1) * HGRN_CHUNK)
        bc = b_all[rs]
        bl = bc[HGRN_CHUNK - 1:HGRN_CHUNK]
        q_c = proj_ref[rs, base:base + D_HGRN]
        q_c = q_c * _sigmoid(q_c)
        k_c = 1.0 - f[rs]
        qd = (q_c * jnp.exp(bc)).astype(BF16)
        ki = (k_c * jnp.exp(-bc)).astype(BF16)
        kte = (k_c * jnp.exp(bl - bc)).astype(BF16)
        decay = jnp.exp(bl)
        v_c = proj_ref[rs, base + 2 * D_HGRN:base + 3 * D_HGRN].astype(BF16)
        g_c = proj_ref[rs, base + 3 * D_HGRN:base + 4 * D_HGRN]
        gate = og_ref[...] * (g_c * _sigmoid(g_c))
        for h in range(HGRN_HEADS):
            cs = slice(h * HGRN_DK, (h + 1) * HGRN_DK)
            scores = jnp.where(causal64, _dot_nt(qd[:, cs], ki[:, cs]), 0.0).astype(BF16)
            st = state_ref[h]
            o = _dot(scores, v_c[:, cs]) + _dot_nt(qd[:, cs], st.astype(BF16))
            state_ref[h] = st * decay[:, cs] + _dot_tn(v_c[:, cs], kte[:, cs])
            o = o * lax.rsqrt(jnp.mean(o * o, axis=-1, keepdims=True) + EPS)
            ycat_ref[rs, D_GMLP + h * HGRN_DK:D_GMLP + (h + 1) * HGRN_DK] = (o * gate[:, cs]).astype(BF16)

    o_ref[0] = x + _dot(ycat_ref[...], wout_ref[...])


def _mixer(x, norm_mix, w_in, gmlp_ln, w_s, b_s_t, beta, lb_logits, out_gain, w_out):
    B, S, D = x.shape
    const2 = lambda b, s: (0, 0)
    return pl.pallas_call(
        _mixer_kernel,
        grid=(B, S // MIX_ROWS),
        in_specs=[
            pl.BlockSpec((1, MIX_ROWS, D), lambda b, s: (b, s, 0)),
            pl.BlockSpec((1, D), const2),
            pl.BlockSpec((D, D_IN_PROJ), const2),
            pl.BlockSpec((1, D_GMLP), const2),
            pl.BlockSpec((GMLP_GROUPS, GMLP_CHUNK, GMLP_CHUNK), lambda b, s: (0, 0, 0)),
            pl.BlockSpec((GMLP_CHUNK, GMLP_GROUPS), const2),
            pl.BlockSpec((1, D_GMLP), const2),
            pl.BlockSpec(lb_logits.shape, const2),
            pl.BlockSpec((1, D_HGRN), const2),
            pl.BlockSpec((D, D), const2),
        ],
        out_specs=pl.BlockSpec((1, MIX_ROWS, D), lambda b, s: (b, s, 0)),
        out_shape=jax.ShapeDtypeStruct((B, S, D), F32),
        scratch_shapes=[
            pltpu.VMEM((MIX_ROWS, D_IN_PROJ), F32),
            pltpu.VMEM((MIX_ROWS, D), BF16),
            pltpu.VMEM((HGRN_HEADS, HGRN_DK, HGRN_DK), F32),
        ],
        compiler_params=pltpu.CompilerParams(
            dimension_semantics=("arbitrary", "arbitrary"), vmem_limit_bytes=VMEM_LIMIT),
        name="mixer",
    )(x, norm_mix, w_in, gmlp_ln, w_s, b_s_t, beta, lb_logits, out_gain, w_out)


def _memkv_kernel(mem_ref, nm_ref, wkv_ref, k_ref, v_ref):
    m = _rms(mem_ref[0], nm_ref[...]).astype(BF16)
    kv = _dot(m, wkv_ref[...])
    k_ref[0] = kv[:, :D_MODEL].astype(BF16)
    v_ref[0] = kv[:, D_MODEL:].astype(BF16)


def _memkv(mem, norm_mem, w_kv):
    B, M, D = mem.shape
    out = jax.ShapeDtypeStruct((B, M, D), BF16)
    return pl.pallas_call(
        _memkv_kernel,
        grid=(B,),
        in_specs=[
            pl.BlockSpec((1, M, D), lambda b: (b, 0, 0)),
            pl.BlockSpec((1, D), lambda b: (0, 0)),
            pl.BlockSpec((D, 2 * D), lambda b: (0, 0)),
        ],
        out_specs=[pl.BlockSpec((1, M, D), lambda b: (b, 0, 0))] * 2,
        out_shape=[out, out],
        compiler_params=pltpu.CompilerParams(
            dimension_semantics=("arbitrary",), vmem_limit_bytes=VMEM_LIMIT),
        name="memkv",
    )(mem, norm_mem, w_kv)


def _xattn_kernel(h_ref, nx_ref, wq_ref, k_ref, v_ref, wo_ref, nf_ref, wrh_ref, wrl_ref,
                  h2_ref, xn_ref, lg_ref, att_ref):
    h = h_ref[0]
    hn = _rms(h, nx_ref[...]).astype(BF16)
    q = (_dot(hn, wq_ref[...]) * (XATTN_HEAD_DIM ** -0.5)).astype(BF16)
    for hd in range(XATTN_HEADS):
        cs = slice(hd * XATTN_HEAD_DIM, (hd + 1) * XATTN_HEAD_DIM)
        s = _dot_nt(q[:, cs], k_ref[0, :, cs])
        p = jnp.exp(s - jnp.max(s, axis=-1, keepdims=True))
        p = (p / jnp.sum(p, axis=-1, keepdims=True)).astype(BF16)
        att_ref[:, cs] = _dot(p, v_ref[0, :, cs]).astype(BF16)
    h2 = h + _dot(att_ref[...], wo_ref[...])
    h2_ref[0] = h2
    xn = _rms(h2, nf_ref[...])
    xn_ref[...] = xn
    x_hi = xn.astype(BF16)
    x_lo = (xn - x_hi.astype(F32)).astype(BF16)
    lg_ref[...] = (_dot_nt(wrh_ref[...], x_hi) + _dot_nt(wrh_ref[...], x_lo)
                   + _dot_nt(wrl_ref[...], x_hi))


def _xattn(h1, norm_x, w_q, k_mem, v_mem, w_o, norm_ffn, wr_hi, wr_lo):
    B, S, D = h1.shape
    n_s = S // ATT_ROWS
    const2 = lambda b, s: (0, 0)
    return pl.pallas_call(
        _xattn_kernel,
        grid=(B, n_s),
        in_specs=[
            pl.BlockSpec((1, ATT_ROWS, D), lambda b, s: (b, s, 0)),
            pl.BlockSpec((1, D), const2),
            pl.BlockSpec((D, D), const2),
            pl.BlockSpec((1, N_MEM, D), lambda b, s: (b, 0, 0)),
            pl.BlockSpec((1, N_MEM, D), lambda b, s: (b, 0, 0)),
            pl.BlockSpec((D, D), const2),
            pl.BlockSpec((1, D), const2),
            pl.BlockSpec((ROUTER_ROWS, D), const2),
            pl.BlockSpec((ROUTER_ROWS, D), const2),
        ],
        out_specs=[
            pl.BlockSpec((1, ATT_ROWS, D), lambda b, s: (b, s, 0)),
            pl.BlockSpec((ATT_ROWS, D), lambda b, s: (b * n_s + s, 0)),
            pl.BlockSpec((ROUTER_ROWS, ATT_ROWS), lambda b, s: (0, b * n_s + s)),
        ],
        out_shape=[
            jax.ShapeDtypeStruct((B, S, D), F32),
            jax.ShapeDtypeStruct((B * S, D), F32),
            jax.ShapeDtypeStruct((ROUTER_ROWS, B * S), F32),
        ],
        scratch_shapes=[pltpu.VMEM((ATT_ROWS, D), BF16)],
        compiler_params=pltpu.CompilerParams(
            dimension_semantics=("arbitrary", "arbitrary"), vmem_limit_bytes=VMEM_LIMIT),
        name="xattn",
    )(h1, norm_x, w_q, k_mem, v_mem, w_o, norm_ffn, wr_hi, wr_lo)


def _route_kernel(bias_ref, lg_ref, ids_ref, gates_ref, rank_ref, cnt_ref, base_ref):
    sub = lg_ref.shape[1]

    @pl.when(pl.program_id(0) == 0)
    def _():
        base_ref[...] = jnp.zeros_like(base_ref)

    best = lg_ref[0] + bias_ref[0]
    gl = [best]
    sel = jnp.zeros(best.shape, jnp.int32)
    for g in range(1, N_GROUPS):
        cur = lg_ref[g] + bias_ref[g]
        gl.append(cur)
        better = cur > best
        best = jnp.where(better, cur, best)
        sel = jnp.where(better, g, sel)
    denom = jnp.exp(gl[0] - best)
    for g in range(1, N_GROUPS):
        denom = denom + jnp.exp(gl[g] - best)
    g_gate = 1.0 / denom

    ev = []
    for j in range(EXPERTS_PER_GROUP):
        val = lg_ref[N_GROUPS + j] + bias_ref[N_GROUPS + j]
        for g in range(1, N_GROUPS):
            e = g * EXPERTS_PER_GROUP + j
            val = jnp.where(sel == g, lg_ref[N_GROUPS + e] + bias_ref[N_GROUPS + e], val)
        ev.append(val)
    v1, i1 = ev[0], jnp.zeros(best.shape, jnp.int32)
    for j in range(1, EXPERTS_PER_GROUP):
        better = ev[j] > v1
        v1 = jnp.where(better, ev[j], v1)
        i1 = jnp.where(better, j, i1)
    rest = [jnp.where(i1 == j, -jnp.inf, ev[j]) for j in range(EXPERTS_PER_GROUP)]
    v2, i2 = rest[0], jnp.zeros(best.shape, jnp.int32)
    for j in range(1, EXPERTS_PER_GROUP):
        better = rest[j] > v2
        v2 = jnp.where(better, rest[j], v2)
        i2 = jnp.where(better, j, i2)
    e2 = jnp.exp(v2 - v1)
    inv = 1.0 / (1.0 + e2)
    id1 = sel * EXPERTS_PER_GROUP + i1
    id2 = sel * EXPERTS_PER_GROUP + i2
    ids_ref[0] = id1
    ids_ref[1] = id2
    gates_ref[0] = inv * g_gate
    gates_ref[1] = e2 * inv * g_gate

    member = jnp.concatenate(
        [jnp.where((id1 == e) | (id2 == e), 1.0, 0.0) for e in range(N_EXPERTS)], axis=0).astype(BF16)
    n = N_EXPERTS * sub
    li = lax.broadcasted_iota(jnp.int32, (LANES, LANES), 0)
    lj = lax.broadcasted_iota(jnp.int32, (LANES, LANES), 1)
    before_lane = jnp.where(li < lj, 1.0, 0.0).astype(BF16)
    ones = jnp.ones((LANES, LANES), BF16)
    ri = lax.broadcasted_iota(jnp.int32, (n, n), 0)
    rj = lax.broadcasted_iota(jnp.int32, (n, n), 1)
    same = _block_id(ri, sub) == _block_id(rj, sub)
    before_row = jnp.where(same & (rj < ri), 1.0, 0.0).astype(BF16)
    all_row = jnp.where(same, 1.0, 0.0).astype(BF16)
    in_row = _dot(member, before_lane)
    prev_rows = _dot(_dot(before_row, member).astype(BF16), ones)
    total = _dot(_dot(all_row, member).astype(BF16), ones)
    base = base_ref[...]
    pos = base + prev_rows + in_row
    r1 = jnp.zeros(best.shape, F32)
    r2 = jnp.zeros(best.shape, F32)
    for e in range(N_EXPERTS):
        pe = pos[e * sub:(e + 1) * sub]
        r1 = jnp.where(id1 == e, pe, r1)
        r2 = jnp.where(id2 == e, pe, r2)
    rank_ref[0] = r1.astype(jnp.int32)
    rank_ref[1] = r2.astype(jnp.int32)
    base_ref[...] = base + total
    cnt_ref[...] = base + total


def _route(bias, logits3):
    rp, n_sub, _ = logits3.shape
    blk = lambda i: (0, i, 0)
    pair_i = jax.ShapeDtypeStruct((2, n_sub, LANES), jnp.int32)
    return pl.pallas_call(
        _route_kernel,
        grid=(n_sub // ROUTE_SUB,),
        in_specs=[
            pl.BlockSpec(memory_space=pltpu.SMEM),
            pl.BlockSpec((rp, ROUTE_SUB, LANES), blk),
        ],
        out_specs=[
            pl.BlockSpec((2, ROUTE_SUB, LANES), blk),
            pl.BlockSpec((2, ROUTE_SUB, LANES), blk),
            pl.BlockSpec((2, ROUTE_SUB, LANES), blk),
            pl.BlockSpec((N_EXPERTS * ROUTE_SUB, LANES), lambda i: (0, 0)),
        ],
        out_shape=[
            pair_i,
            jax.ShapeDtypeStruct((2, n_sub, LANES), F32),
            pair_i,
            jax.ShapeDtypeStruct((N_EXPERTS * ROUTE_SUB, LANES), F32),
        ],
        scratch_shapes=[pltpu.VMEM((N_EXPERTS * ROUTE_SUB, LANES), F32)],
        compiler_params=pltpu.CompilerParams(
            dimension_semantics=("arbitrary",), vmem_limit_bytes=VMEM_LIMIT),
        name="route",
    )(bias, logits3)


def _dest_kernel(start_ref, ids_ref, rank_ref, dest_ref):
    ids = ids_ref[...]
    off = jnp.zeros(ids.shape, jnp.int32)
    for e in range(N_EXPERTS):
        off = jnp.where(ids == e, start_ref[e], off)
    dest_ref[...] = rank_ref[...] + off


def _dest(seg_start, ids, rank):
    _, n_sub, _ = ids.shape
    blk = pl.BlockSpec((2, ROUTE_SUB, LANES), lambda i: (0, i, 0))
    return pl.pallas_call(
        _dest_kernel,
        grid=(n_sub // ROUTE_SUB,),
        in_specs=[pl.BlockSpec(memory_space=pltpu.SMEM), blk, blk],
        out_specs=blk,
        out_shape=jax.ShapeDtypeStruct(ids.shape, jnp.int32),
        name="dest",
    )(seg_start, ids, rank)


def _dispatch_kernel(dest_ref, x_hbm, zero_hbm, xs_hbm, sem):
    del zero_hbm
    n = dest_ref.shape[2]
    t0 = pl.program_id(0) * n

    def row_copy(t, k):
        return pltpu.make_async_copy(
            x_hbm.at[pl.ds(t0 + t, 1)], xs_hbm.at[pl.ds(dest_ref[0, k, t], 1)], sem)

    def issue(t, carry):
        row_copy(t, 0).start()
        row_copy(t, 1).start()
        return carry

    lax.fori_loop(0, n, issue, 0)

    def drain(t, carry):
        row_copy(t, 0).wait()
        row_copy(t, 1).wait()
        return carry

    lax.fori_loop(0, n, drain, 0)


def _dispatch(dest_tiles, xn, n_slots):
    T, D = xn.shape
    zeros = jnp.zeros((n_slots, D), xn.dtype)
    return pl.pallas_call(
        _dispatch_kernel,
        grid=(T // MOVE_ROWS,),
        in_specs=[
            pl.BlockSpec((1, 2, MOVE_ROWS), lambda i: (i, 0, 0), memory_space=pltpu.SMEM),
            pl.BlockSpec(memory_space=pl.ANY),
            pl.BlockSpec(memory_space=pl.ANY),
        ],
        out_specs=pl.BlockSpec(memory_space=pl.ANY),
        out_shape=jax.ShapeDtypeStruct((n_slots, D), xn.dtype),
        scratch_shapes=[pltpu.SemaphoreType.DMA],
        input_output_aliases={2: 0},
        compiler_params=pltpu.CompilerParams(dimension_semantics=("arbitrary",)),
        name="dispatch",
    )(dest_tiles, xn, zeros)


def _expert_kernel(be_ref, nb_ref, x_ref, wg_ref, wu_ref, wd_ref, y_ref):
    used = pl.program_id(0) < nb_ref[0]

    @pl.when(used)
    def _():
        x = x_ref[...].astype(BF16)
        g = _dot(x, wg_ref[0])
        u = _dot(x, wu_ref[0])
        hid = (g * _sigmoid(g) * u).astype(BF16)
        y_ref[...] = _dot(hid, wd_ref[0])

    @pl.when(jnp.logical_not(used))
    def _():
        y_ref[...] = jnp.zeros_like(y_ref)


def _experts(block_expert, n_used, xs, w_gate, w_up, w_down):
    n_slots, D = xs.shape
    n_blocks = n_slots // EXPERT_ROWS
    row_blk = lambda i, be, nb: (jnp.minimum(i, nb[0] - 1), 0)
    w_blk = lambda i, be, nb: (be[i], 0, 0)
    grid_spec = pltpu.PrefetchScalarGridSpec(
        num_scalar_prefetch=2,
        grid=(n_blocks,),
        in_specs=[
            pl.BlockSpec((EXPERT_ROWS, D), row_blk),
            pl.BlockSpec((1, D, D_EXPERT), w_blk),
            pl.BlockSpec((1, D, D_EXPERT), w_blk),
            pl.BlockSpec((1, D_EXPERT, D), w_blk),
        ],
        out_specs=pl.BlockSpec((EXPERT_ROWS, D), lambda i, be, nb: (i, 0)),
    )
    return pl.pallas_call(
        _expert_kernel,
        grid_spec=grid_spec,
        out_shape=jax.ShapeDtypeStruct((n_slots, D), F32),
        compiler_params=pltpu.CompilerParams(
            dimension_semantics=("arbitrary",), vmem_limit_bytes=VMEM_LIMIT),
        name="experts",
    )(block_expert, n_used, xs, w_gate, w_up, w_down)


def _combine_kernel(dest_ref, yb_hbm, h_ref, gates_ref, nfin_ref, o_ref, ybuf, sem):
    n = dest_ref.shape[2]

    def row_copy(t, k):
        return pltpu.make_async_copy(
            yb_hbm.at[pl.ds(dest_ref[0, k, t], 1)], ybuf.at[k, pl.ds(t, 1)], sem)

    def issue(t, carry):
        row_copy(t, 0).start()
        row_copy(t, 1).start()
        return carry

    lax.fori_loop(0, n, issue, 0)

    def drain(t, carry):
        row_copy(t, 0).wait()
        row_copy(t, 1).wait()
        return carry

    lax.fori_loop(0, n, drain, 0)
    gates = gates_ref[...]
    h = h_ref[...] + gates[:, 0:1] * ybuf[0] + gates[:, 1:2] * ybuf[1]
    o_ref[...] = _rms(h, nfin_ref[...])


def _combine(dest_tiles, yb, h2, gates_t, norm_final):
    T, D = h2.shape
    return pl.pallas_call(
        _combine_kernel,
        grid=(T // MOVE_ROWS,),
        in_specs=[
            pl.BlockSpec((1, 2, MOVE_ROWS), lambda i: (i, 0, 0), memory_space=pltpu.SMEM),
            pl.BlockSpec(memory_space=pl.ANY),
            pl.BlockSpec((MOVE_ROWS, D), lambda i: (i, 0)),
            pl.BlockSpec((MOVE_ROWS, 2), lambda i: (i, 0)),
            pl.BlockSpec((1, D), lambda i: (0, 0)),
        ],
        out_specs=pl.BlockSpec((MOVE_ROWS, D), lambda i: (i, 0)),
        out_shape=jax.ShapeDtypeStruct((T, D), F32),
        scratch_shapes=[pltpu.VMEM((2, MOVE_ROWS, D), F32), pltpu.SemaphoreType.DMA],
        compiler_params=pltpu.CompilerParams(
            dimension_semantics=("arbitrary",), vmem_limit_bytes=VMEM_LIMIT),
        name="combine",
    )(dest_tiles, yb, h2, gates_t, norm_final)


def kernel(x, mem, norm_mix, w_in, gmlp_ln, gmlp_w_spatial, gmlp_b_spatial, gmlp_beta, hgrn_lb_logits, hgrn_out_gain, w_out, norm_xattn, norm_mem, w_xq, w_xkv, w_xo, norm_ffn, w_router_group, b_router_group, w_router_expert, b_router_expert, w_expert_gate, w_expert_up, w_expert_down, norm_final):
    B, S, D = x.shape
    T = B * S
    depth = w_in.shape[0]
    assert depth == 1 and hgrn_lb_logits.shape[0] == 2
    l = 0
    row = lambda p: p.reshape(1, -1)

    h1 = _mixer(x, row(norm_mix[l]), w_in[l].astype(BF16), row(gmlp_ln[l]), gmlp_w_spatial[l],
                gmlp_b_spatial[l].T, row(gmlp_beta[l]), hgrn_lb_logits, row(hgrn_out_gain[l]),
                w_out[l].astype(BF16))
    k_mem, v_mem = _memkv(mem, row(norm_mem[l]), w_xkv[l].astype(BF16))

    w_router = jnp.concatenate([w_router_group[l].T, w_router_expert[l].T], axis=0)
    w_router = jnp.pad(w_router, ((0, ROUTER_ROWS - w_router.shape[0]), (0, 0)))
    wr_hi = w_router.astype(BF16)
    wr_lo = (w_router - wr_hi.astype(F32)).astype(BF16)
    h2, xn, logits = _xattn(h1, row(norm_xattn[l]), w_xq[l].astype(BF16), k_mem, v_mem,
                            w_xo[l].astype(BF16), row(norm_ffn[l]), wr_hi, wr_lo)

    bias = jnp.concatenate([b_router_group[l], b_router_expert[l]]).astype(F32)
    ids, gates, rank, counts = _route(bias, logits.reshape(ROUTER_ROWS, T // LANES, LANES))

    counts = counts[::ROUTE_SUB, 0].astype(jnp.int32)
    padded = (counts + EXPERT_ROWS - 1) // EXPERT_ROWS * EXPERT_ROWS
    seg_end = jnp.cumsum(padded)
    seg_start = seg_end - padded
    n_blocks = (2 * T) // EXPERT_ROWS + N_EXPERTS
    block_first_row = jnp.arange(n_blocks, dtype=jnp.int32) * EXPERT_ROWS
    block_expert = jnp.minimum(
        jnp.sum(block_first_row[:, None] >= seg_end[None, :], axis=1), N_EXPERTS - 1).astype(jnp.int32)
    n_used = (seg_end[-1:] // EXPERT_ROWS).astype(jnp.int32)

    dest = _dest(seg_start, ids, rank)
    dest_tiles = dest.reshape(2, T // MOVE_ROWS, MOVE_ROWS).transpose(1, 0, 2)

    xs = _dispatch(dest_tiles, xn, n_blocks * EXPERT_ROWS)
    yb = _experts(block_expert, n_used, xs, w_expert_gate[l].astype(BF16), w_expert_up[l].astype(BF16),
                  w_expert_down[l].astype(BF16))
    gates_t = gates.reshape(2, T).T
    out = _combine(dest_tiles, yb, h2.reshape(T, D), gates_t, row(norm_final))
    return out.reshape(B, S, D)
```

```python
import functools

import jax
import jax.numpy as jnp
from jax import lax
from jax.experimental import pallas as pl
from jax.experimental.pallas import tpu as pltpu
from jax.experimental.pallas import tpu_sc as plsc

F32 = jnp.float32
BF16 = jnp.bfloat16
EPS = 1e-6

D_MODEL = 1024
D_GMLP = 512
GMLP_GROUPS = 4
GMLP_CHUNK = 128
D_HGRN = 512
HGRN_HEADS = 4
HGRN_DK = 128
HGRN_CHUNK = 64
D_IN_PROJ = 2 * D_GMLP + 4 * D_HGRN
N_MEM = 256
XATTN_HEADS = 4
XATTN_HEAD_DIM = D_MODEL // XATTN_HEADS
N_GROUPS = 4
EXPERTS_PER_GROUP = 8
N_EXPERTS = N_GROUPS * EXPERTS_PER_GROUP
D_EXPERT = 512

LANES = 128
MIX_ROWS = 256
ATT_ROWS = 512
ROUTER_ROWS = 40
ROUTE_SUB = 16
EXPERT_ROWS = 256
MOVE_ROWS = 256
SC_WINDOW = 32
SC_CORES = 2
SC_SUBCORES = 16
_SC_WORKERS = SC_CORES * SC_SUBCORES
VMEM_LIMIT = 48 * 1024 * 1024


def _rms(x, gain):
    return x * lax.rsqrt(jnp.mean(x * x, axis=-1, keepdims=True) + EPS) * gain


def _dot(a, b):
    return jnp.dot(a, b, preferred_element_type=F32)


def _dot_nt(a, b):
    return lax.dot_general(a, b, (((1,), (1,)), ((), ())), preferred_element_type=F32)


def _dot_tn(a, b):
    return lax.dot_general(a, b, (((0,), (0,)), ((), ())), preferred_element_type=F32)


def _gelu(x):
    return 0.5 * x * (1.0 + jnp.tanh(0.7978845608028654 * (x + 0.044715 * (x * x * x))))


def _sigmoid(x):
    return 1.0 / (1.0 + jnp.exp(-x))


def _block_id(idx, size):
    assert size & (size - 1) == 0
    return lax.shift_right_logical(idx, size.bit_length() - 1)


def _mixer_kernel(x_ref, nmix_ref, win_ref, gln_ref, ws_ref, bst_ref, beta_ref, lbl_ref, og_ref,
                  wout_ref, o_ref, proj_ref, ycat_ref, state_ref):
    rows = x_ref.shape[1]

    @pl.when(pl.program_id(1) == 0)
    def _():
        state_ref[...] = jnp.zeros_like(state_ref)

    x = x_ref[0]
    a = _rms(x, nmix_ref[...]).astype(BF16)
    proj_ref[...] = _dot(a, win_ref[...])

    u = _gelu(proj_ref[:, 0:D_GMLP])
    v = _gelu(proj_ref[:, D_GMLP:2 * D_GMLP])
    vc = v - jnp.mean(v, axis=-1, keepdims=True)
    vn = (vc * lax.rsqrt(jnp.mean(vc * vc, axis=-1, keepdims=True) + EPS) * gln_ref[...]).astype(BF16)
    r_i = lax.broadcasted_iota(jnp.int32, (GMLP_CHUNK, GMLP_CHUNK), 0)
    c_i = lax.broadcasted_iota(jnp.int32, (GMLP_CHUNK, GMLP_CHUNK), 1)
    causal = c_i <= r_i
    w_tril = [jnp.where(causal, ws_ref[g], 0.0).astype(BF16) for g in range(GMLP_GROUPS)]
    z_rows = []
    for c in range(rows // GMLP_CHUNK):
        z_cols = []
        for g in range(GMLP_GROUPS):
            vg = vn[c * GMLP_CHUNK:(c + 1) * GMLP_CHUNK, g * LANES:(g + 1) * LANES]
            z_cols.append(_dot(w_tril[g], vg) + bst_ref[:, g:g + 1])
        z_rows.append(jnp.concatenate(z_cols, axis=1))
    z = jnp.concatenate(z_rows, axis=0)
    ycat_ref[:, 0:D_GMLP] = _rms(u * z, beta_ref[...]).astype(BF16)

    lbl = lbl_ref[...]
    e_lb = jnp.exp(lbl - jnp.max(lbl, axis=0, keepdims=True))
    lb = e_lb[0:1] / jnp.sum(e_lb, axis=0, keepdims=True)
    base = 2 * D_GMLP
    f = lb + (1.0 - lb) * _sigmoid(proj_ref[:, base + D_HGRN:base + 2 * D_HGRN])
    log_f = jnp.log(f)
    rr = lax.broadcasted_iota(jnp.int32, (rows, rows), 0)
    cc = lax.broadcasted_iota(jnp.int32, (rows, rows), 1)
    tri = jnp.where((_block_id(rr, HGRN_CHUNK) == _block_id(cc, HGRN_CHUNK)) & (cc <= rr),
                    1.0, 0.0).astype(BF16)
    lf_hi = log_f.astype(BF16)
    lf_lo = (log_f - lf_hi.astype(F32)).astype(BF16)
    b_all = _dot(tri, lf_hi) + _dot(tri, lf_lo)
    r64 = lax.broadcasted_iota(jnp.int32, (HGRN_CHUNK, HGRN_CHUNK), 0)
    c64 = lax.broadcasted_iota(jnp.int32, (HGRN_CHUNK, HGRN_CHUNK), 1)
    causal64 = c64 <= r64
    for c in range(rows // HGRN_CHUNK):
        rs = slice(c * HGRN_CHUNK, (c + 1) * HGRN_CHUNK)
        bc = b_all[rs]
        bl = bc[HGRN_CHUNK - 1:HGRN_CHUNK]
        q_c = proj_ref[rs, base:base + D_HGRN]
        q_c = q_c * _sigmoid(q_c)
        k_c = 1.0 - f[rs]
        qd = (q_c * jnp.exp(bc)).astype(BF16)
        ki = (k_c * jnp.exp(-bc)).astype(BF16)
        kte = (k_c * jnp.exp(bl - bc)).astype(BF16)
        decay = jnp.exp(bl)
        v_c = proj_ref[rs, base + 2 * D_HGRN:base + 3 * D_HGRN].astype(BF16)
        g_c = proj_ref[rs, base + 3 * D_HGRN:base + 4 * D_HGRN]
        gate = og_ref[...] * (g_c * _sigmoid(g_c))
        for h in range(HGRN_HEADS):
            cs = slice(h * HGRN_DK, (h + 1) * HGRN_DK)
            scores = jnp.where(causal64, _dot_nt(qd[:, cs], ki[:, cs]), 0.0).astype(BF16)
            st = state_ref[h]
            o = _dot(scores, v_c[:, cs]) + _dot_nt(qd[:, cs], st.astype(BF16))
            state_ref[h] = st * decay[:, cs] + _dot_tn(v_c[:, cs], kte[:, cs])
            o = o * lax.rsqrt(jnp.mean(o * o, axis=-1, keepdims=True) + EPS)
            ycat_ref[rs, D_GMLP + h * HGRN_DK:D_GMLP + (h + 1) * HGRN_DK] = (o * gate[:, cs]).astype(BF16)

    o_ref[0] = x + _dot(ycat_ref[...], wout_ref[...])


def _mixer(x, norm_mix, w_in, gmlp_ln, w_s, b_s_t, beta, lb_logits, out_gain, w_out):
    B, S, D = x.shape
    const2 = lambda b, s: (0, 0)
    return pl.pallas_call(
        _mixer_kernel,
        grid=(B, S // MIX_ROWS),
        in_specs=[
            pl.BlockSpec((1, MIX_ROWS, D), lambda b, s: (b, s, 0)),
            pl.BlockSpec((1, D), const2),
            pl.BlockSpec((D, D_IN_PROJ), const2),
            pl.BlockSpec((1, D_GMLP), const2),
            pl.BlockSpec((GMLP_GROUPS, GMLP_CHUNK, GMLP_CHUNK), lambda b, s: (0, 0, 0)),
            pl.BlockSpec((GMLP_CHUNK, GMLP_GROUPS), const2),
            pl.BlockSpec((1, D_GMLP), const2),
            pl.BlockSpec(lb_logits.shape, const2),
            pl.BlockSpec((1, D_HGRN), const2),
            pl.BlockSpec((D, D), const2),
        ],
        out_specs=pl.BlockSpec((1, MIX_ROWS, D), lambda b, s: (b, s, 0)),
        out_shape=jax.ShapeDtypeStruct((B, S, D), F32),
        scratch_shapes=[
            pltpu.VMEM((MIX_ROWS, D_IN_PROJ), F32),
            pltpu.VMEM((MIX_ROWS, D), BF16),
            pltpu.VMEM((HGRN_HEADS, HGRN_DK, HGRN_DK), F32),
        ],
        compiler_params=pltpu.CompilerParams(
            dimension_semantics=("arbitrary", "arbitrary"), vmem_limit_bytes=VMEM_LIMIT),
        name="mixer",
    )(x, norm_mix, w_in, gmlp_ln, w_s, b_s_t, beta, lb_logits, out_gain, w_out)


def _memkv_kernel(mem_ref, nm_ref, wkv_ref, k_ref, v_ref):
    m = _rms(mem_ref[0], nm_ref[...]).astype(BF16)
    kv = _dot(m, wkv_ref[...])
    k_ref[0] = kv[:, :D_MODEL].astype(BF16)
    v_ref[0] = kv[:, D_MODEL:].astype(BF16)


def _memkv(mem, norm_mem, w_kv):
    B, M, D = mem.shape
    out = jax.ShapeDtypeStruct((B, M, D), BF16)
    return pl.pallas_call(
        _memkv_kernel,
        grid=(B,),
        in_specs=[
            pl.BlockSpec((1, M, D), lambda b: (b, 0, 0)),
            pl.BlockSpec((1, D), lambda b: (0, 0)),
            pl.BlockSpec((D, 2 * D), lambda b: (0, 0)),
        ],
        out_specs=[pl.BlockSpec((1, M, D), lambda b: (b, 0, 0))] * 2,
        out_shape=[out, out],
        compiler_params=pltpu.CompilerParams(
            dimension_semantics=("arbitrary",), vmem_limit_bytes=VMEM_LIMIT),
        name="memkv",
    )(mem, norm_mem, w_kv)


def _xattn_kernel(h_ref, nx_ref, wq_ref, k_ref, v_ref, wo_ref, nf_ref, wrh_ref, wrl_ref,
                  h2_ref, xn_ref, lg_ref, att_ref):
    h = h_ref[0]
    hn = _rms(h, nx_ref[...]).astype(BF16)
    q = (_dot(hn, wq_ref[...]) * (XATTN_HEAD_DIM ** -0.5)).astype(BF16)
    for hd in range(XATTN_HEADS):
        cs = slice(hd * XATTN_HEAD_DIM, (hd + 1) * XATTN_HEAD_DIM)
        s = _dot_nt(q[:, cs], k_ref[0, :, cs])
        p = jnp.exp(s - jnp.max(s, axis=-1, keepdims=True))
        p = (p / jnp.sum(p, axis=-1, keepdims=True)).astype(BF16)
        att_ref[:, cs] = _dot(p, v_ref[0, :, cs]).astype(BF16)
    h2 = h + _dot(att_ref[...], wo_ref[...])
    h2_ref[0] = h2
    xn = _rms(h2, nf_ref[...])
    xn_ref[...] = xn
    x_hi = xn.astype(BF16)
    x_lo = (xn - x_hi.astype(F32)).astype(BF16)
    lg_ref[...] = (_dot_nt(wrh_ref[...], x_hi) + _dot_nt(wrh_ref[...], x_lo)
                   + _dot_nt(wrl_ref[...], x_hi))


def _xattn(h1, norm_x, w_q, k_mem, v_mem, w_o, norm_ffn, wr_hi, wr_lo):
    B, S, D = h1.shape
    n_s = S // ATT_ROWS
    const2 = lambda b, s: (0, 0)
    return pl.pallas_call(
        _xattn_kernel,
        grid=(B, n_s),
        in_specs=[
            pl.BlockSpec((1, ATT_ROWS, D), lambda b, s: (b, s, 0)),
            pl.BlockSpec((1, D), const2),
            pl.BlockSpec((D, D), const2),
            pl.BlockSpec((1, N_MEM, D), lambda b, s: (b, 0, 0)),
            pl.BlockSpec((1, N_MEM, D), lambda b, s: (b, 0, 0)),
            pl.BlockSpec((D, D), const2),
            pl.BlockSpec((1, D), const2),
            pl.BlockSpec((ROUTER_ROWS, D), const2),
            pl.BlockSpec((ROUTER_ROWS, D), const2),
        ],
        out_specs=[
            pl.BlockSpec((1, ATT_ROWS, D), lambda b, s: (b, s, 0)),
            pl.BlockSpec((ATT_ROWS, D), lambda b, s: (b * n_s + s, 0)),
            pl.BlockSpec((ROUTER_ROWS, ATT_ROWS), lambda b, s: (0, b * n_s + s)),
        ],
        out_shape=[
            jax.ShapeDtypeStruct((B, S, D), F32),
            jax.ShapeDtypeStruct((B * S, D), F32),
            jax.ShapeDtypeStruct((ROUTER_ROWS, B * S), F32),
        ],
        scratch_shapes=[pltpu.VMEM((ATT_ROWS, D), BF16)],
        compiler_params=pltpu.CompilerParams(
            dimension_semantics=("arbitrary", "arbitrary"), vmem_limit_bytes=VMEM_LIMIT),
        name="xattn",
    )(h1, norm_x, w_q, k_mem, v_mem, w_o, norm_ffn, wr_hi, wr_lo)


def _route_kernel(bias_ref, lg_ref, ids_ref, gates_ref, rank_ref, cnt_ref, base_ref):
    sub = lg_ref.shape[1]

    @pl.when(pl.program_id(0) == 0)
    def _():
        base_ref[...] = jnp.zeros_like(base_ref)

    best = lg_ref[0] + bias_ref[0]
    gl = [best]
    sel = jnp.zeros(best.shape, jnp.int32)
    for g in range(1, N_GROUPS):
        cur = lg_ref[g] + bias_ref[g]
        gl.append(cur)
        better = cur > best
        best = jnp.where(better, cur, best)
        sel = jnp.where(better, g, sel)
    denom = jnp.exp(gl[0] - best)
    for g in range(1, N_GROUPS):
        denom = denom + jnp.exp(gl[g] - best)
    g_gate = 1.0 / denom

    ev = []
    for j in range(EXPERTS_PER_GROUP):
        val = lg_ref[N_GROUPS + j] + bias_ref[N_GROUPS + j]
        for g in range(1, N_GROUPS):
            e = g * EXPERTS_PER_GROUP + j
            val = jnp.where(sel == g, lg_ref[N_GROUPS + e] + bias_ref[N_GROUPS + e], val)
        ev.append(val)
    v1, i1 = ev[0], jnp.zeros(best.shape, jnp.int32)
    for j in range(1, EXPERTS_PER_GROUP):
        better = ev[j] > v1
        v1 = jnp.where(better, ev[j], v1)
        i1 = jnp.where(better, j, i1)
    rest = [jnp.where(i1 == j, -jnp.inf, ev[j]) for j in range(EXPERTS_PER_GROUP)]
    v2, i2 = rest[0], jnp.zeros(best.shape, jnp.int32)
    for j in range(1, EXPERTS_PER_GROUP):
        better = rest[j] > v2
        v2 = jnp.where(better, rest[j], v2)
        i2 = jnp.where(better, j, i2)
    e2 = jnp.exp(v2 - v1)
    inv = 1.0 / (1.0 + e2)
    id1 = sel * EXPERTS_PER_GROUP + i1
    id2 = sel * EXPERTS_PER_GROUP + i2
    ids_ref[0] = id1
    ids_ref[1] = id2
    gates_ref[0] = inv * g_gate
    gates_ref[1] = e2 * inv * g_gate

    member = jnp.concatenate(
        [jnp.where((id1 == e) | (id2 == e), 1.0, 0.0) for e in range(N_EXPERTS)], axis=0).astype(BF16)
    n = N_EXPERTS * sub
    li = lax.broadcasted_iota(jnp.int32, (LANES, LANES), 0)
    lj = lax.broadcasted_iota(jnp.int32, (LANES, LANES), 1)
    before_lane = jnp.where(li < lj, 1.0, 0.0).astype(BF16)
    ones = jnp.ones((LANES, LANES), BF16)
    ri = lax.broadcasted_iota(jnp.int32, (n, n), 0)
    rj = lax.broadcasted_iota(jnp.int32, (n, n), 1)
    same = _block_id(ri, sub) == _block_id(rj, sub)
    before_row = jnp.where(same & (rj < ri), 1.0, 0.0).astype(BF16)
    all_row = jnp.where(same, 1.0, 0.0).astype(BF16)
    in_row = _dot(member, before_lane)
    prev_rows = _dot(_dot(before_row, member).astype(BF16), ones)
    total = _dot(_dot(all_row, member).astype(BF16), ones)
    base = base_ref[...]
    pos = base + prev_rows + in_row
    r1 = jnp.zeros(best.shape, F32)
    r2 = jnp.zeros(best.shape, F32)
    for e in range(N_EXPERTS):
        pe = pos[e * sub:(e + 1) * sub]
        r1 = jnp.where(id1 == e, pe, r1)
        r2 = jnp.where(id2 == e, pe, r2)
    rank_ref[0] = r1.astype(jnp.int32)
    rank_ref[1] = r2.astype(jnp.int32)
    base_ref[...] = base + total
    cnt_ref[...] = base + total


def _route(bias, logits3):
    rp, n_sub, _ = logits3.shape
    blk = lambda i: (0, i, 0)
    pair_i = jax.ShapeDtypeStruct((2, n_sub, LANES), jnp.int32)
    return pl.pallas_call(
        _route_kernel,
        grid=(n_sub // ROUTE_SUB,),
        in_specs=[
            pl.BlockSpec(memory_space=pltpu.SMEM),
            pl.BlockSpec((rp, ROUTE_SUB, LANES), blk),
        ],
        out_specs=[
            pl.BlockSpec((2, ROUTE_SUB, LANES), blk),
            pl.BlockSpec((2, ROUTE_SUB, LANES), blk),
            pl.BlockSpec((2, ROUTE_SUB, LANES), blk),
            pl.BlockSpec((N_EXPERTS * ROUTE_SUB, LANES), lambda i: (0, 0)),
        ],
        out_shape=[
            pair_i,
            jax.ShapeDtypeStruct((2, n_sub, LANES), F32),
            pair_i,
            jax.ShapeDtypeStruct((N_EXPERTS * ROUTE_SUB, LANES), F32),
        ],
        scratch_shapes=[pltpu.VMEM((N_EXPERTS * ROUTE_SUB, LANES), F32)],
        compiler_params=pltpu.CompilerParams(
            dimension_semantics=("arbitrary",), vmem_limit_bytes=VMEM_LIMIT),
        name="route",
    )(bias, logits3)


def _dest_kernel(start_ref, ids_ref, rank_ref, dest_ref):
    ids = ids_ref[...]
    off = jnp.zeros(ids.shape, jnp.int32)
    for e in range(N_EXPERTS):
        off = jnp.where(ids == e, start_ref[e], off)
    dest_ref[...] = rank_ref[...] + off


def _dest(seg_start, ids, rank):
    _, n_sub, _ = ids.shape
    blk = pl.BlockSpec((2, ROUTE_SUB, LANES), lambda i: (0, i, 0))
    return pl.pallas_call(
        _dest_kernel,
        grid=(n_sub // ROUTE_SUB,),
        in_specs=[pl.BlockSpec(memory_space=pltpu.SMEM), blk, blk],
        out_specs=blk,
        out_shape=jax.ShapeDtypeStruct(ids.shape, jnp.int32),
        name="dest",
    )(seg_start, ids, rank)


def _sc_mesh():
    return plsc.VectorSubcoreMesh(core_axis_name="core", subcore_axis_name="subcore")


def _sc_worker(rows_total):
    rows = rows_total // _SC_WORKERS
    wid = lax.axis_index("core") * SC_SUBCORES + lax.axis_index("subcore")
    return wid * rows, rows


def _dispatch(dest0, dest1, xn, n_slots):
    T, D = xn.shape
    W = SC_WINDOW
    rows = T // _SC_WORKERS

    @pl.kernel(out_type=jax.ShapeDtypeStruct((n_slots, D), xn.dtype), mesh=_sc_mesh(),
               scratch_types=[pltpu.VMEM((rows,), jnp.int32), pltpu.VMEM((rows,), jnp.int32),
                              pltpu.VMEM((2, W, D), xn.dtype),
                              pltpu.SemaphoreType.DMA((2,)), pltpu.SemaphoreType.DMA((2,))])
    def scatter_rows(x_hbm, d0_hbm, d1_hbm, xs_hbm, d0_v, d1_v, buf, in_sem, out_sem):
        base, _ = _sc_worker(T)
        pltpu.sync_copy(d0_hbm.at[pl.ds(base, rows)], d0_v)
        pltpu.sync_copy(d1_hbm.at[pl.ds(base, rows)], d1_v)

        def load(w, slot):
            return pltpu.make_async_copy(x_hbm.at[pl.ds(base + w * W, W)], buf.at[slot], in_sem.at[slot])

        def store(w, slot, d_v):
            return pltpu.make_async_copy(buf.at[slot], xs_hbm.at[d_v.at[pl.ds(w * W, W)]], out_sem.at[slot])

        def step(w, slot):
            load(w, slot).wait()
            store(w, slot, d0_v).start()
            store(w, slot, d1_v).start()
            store(w, slot, d0_v).wait()
            store(w, slot, d1_v).wait()

        n = rows // W
        load(0, 0).start()

        @pl.loop(0, n, step=2)
        def _(w):
            load(w + 1, 1).start()
            step(w, 0)

            @pl.when(w + 2 < n)
            def _():
                load(w + 2, 0).start()

            step(w + 1, 1)

    return scatter_rows(xn, dest0, dest1)


def _gather_rows(src, idx):
    M = idx.shape[0]
    D = src.shape[1]
    W = SC_WINDOW
    rows = M // _SC_WORKERS

    @pl.kernel(out_type=jax.ShapeDtypeStruct((M, D), src.dtype), mesh=_sc_mesh(),
               scratch_types=[pltpu.VMEM((rows,), jnp.int32), pltpu.VMEM((2, W, D), src.dtype),
                              pltpu.SemaphoreType.DMA((2,)), pltpu.SemaphoreType.DMA((2,))])
    def gather_rows(src_hbm, i_hbm, o_hbm, i_v, buf, in_sem, out_sem):
        base, _ = _sc_worker(M)
        pltpu.sync_copy(i_hbm.at[pl.ds(base, rows)], i_v)

        def load(w, slot):
            return pltpu.make_async_copy(src_hbm.at[i_v.at[pl.ds(w * W, W)]], buf.at[slot], in_sem.at[slot])

        def store(w, slot):
            return pltpu.make_async_copy(buf.at[slot], o_hbm.at[pl.ds(base + w * W, W)], out_sem.at[slot])

        n = rows // W
        load(0, 0).start()

        @pl.loop(0, n, step=2)
        def _(w):
            @pl.when(w > 0)
            def _():
                store(w - 1, 1).wait()

            load(w + 1, 1).start()
            load(w, 0).wait()
            store(w, 0).start()
            store(w, 0).wait()

            @pl.when(w + 2 < n)
            def _():
                load(w + 2, 0).start()

            load(w + 1, 1).wait()
            store(w + 1, 1).start()

        store(n - 1, 1).wait()

    return gather_rows(src, idx)


def _expert_kernel(be_ref, nb_ref, x_ref, wg_ref, wu_ref, wd_ref, y_ref):
    used = pl.program_id(0) < nb_ref[0]

    @pl.when(used)
    def _():
        x = x_ref[...].astype(BF16)
        g = _dot(x, wg_ref[0])
        u = _dot(x, wu_ref[0])
        hid = (g * _sigmoid(g) * u).astype(BF16)
        y_ref[...] = _dot(hid, wd_ref[0])

    @pl.when(jnp.logical_not(used))
    def _():
        y_ref[...] = jnp.zeros_like(y_ref)


def _experts(block_expert, n_used, xs, w_gate, w_up, w_down):
    n_slots, D = xs.shape
    n_blocks = n_slots // EXPERT_ROWS
    row_blk = lambda i, be, nb: (jnp.minimum(i, nb[0] - 1), 0)
    w_blk = lambda i, be, nb: (be[i], 0, 0)
    grid_spec = pltpu.PrefetchScalarGridSpec(
        num_scalar_prefetch=2,
        grid=(n_blocks,),
        in_specs=[
            pl.BlockSpec((EXPERT_ROWS, D), row_blk),
            pl.BlockSpec((1, D, D_EXPERT), w_blk),
            pl.BlockSpec((1, D, D_EXPERT), w_blk),
            pl.BlockSpec((1, D_EXPERT, D), w_blk),
        ],
        out_specs=pl.BlockSpec((EXPERT_ROWS, D), lambda i, be, nb: (i, 0)),
    )
    return pl.pallas_call(
        _expert_kernel,
        grid_spec=grid_spec,
        out_shape=jax.ShapeDtypeStruct((n_slots, D), F32),
        compiler_params=pltpu.CompilerParams(
            dimension_semantics=("arbitrary",), vmem_limit_bytes=VMEM_LIMIT),
        name="experts",
    )(block_expert, n_used, xs, w_gate, w_up, w_down)


def _combine_kernel(y0_ref, y1_ref, h_ref, gates_ref, nfin_ref, o_ref):
    gates = gates_ref[...]
    h = h_ref[...] + gates[:, 0:1] * y0_ref[...] + gates[:, 1:2] * y1_ref[...]
    o_ref[...] = _rms(h, nfin_ref[...])


def _combine(y01, h2, gates_t, norm_final):
    T, D = h2.shape
    n_t = T // MOVE_ROWS
    return pl.pallas_call(
        _combine_kernel,
        grid=(n_t,),
        in_specs=[
            pl.BlockSpec((MOVE_ROWS, D), lambda i: (i, 0)),
            pl.BlockSpec((MOVE_ROWS, D), lambda i: (i + n_t, 0)),
            pl.BlockSpec((MOVE_ROWS, D), lambda i: (i, 0)),
            pl.BlockSpec((MOVE_ROWS, 2), lambda i: (i, 0)),
            pl.BlockSpec((1, D), lambda i: (0, 0)),
        ],
        out_specs=pl.BlockSpec((MOVE_ROWS, D), lambda i: (i, 0)),
        out_shape=jax.ShapeDtypeStruct((T, D), F32),
        compiler_params=pltpu.CompilerParams(
            dimension_semantics=("arbitrary",), vmem_limit_bytes=VMEM_LIMIT),
        name="combine",
    )(y01, y01, h2, gates_t, norm_final)


def kernel(x, mem, norm_mix, w_in, gmlp_ln, gmlp_w_spatial, gmlp_b_spatial, gmlp_beta, hgrn_lb_logits, hgrn_out_gain, w_out, norm_xattn, norm_mem, w_xq, w_xkv, w_xo, norm_ffn, w_router_group, b_router_group, w_router_expert, b_router_expert, w_expert_gate, w_expert_up, w_expert_down, norm_final):
    B, S, D = x.shape
    T = B * S
    depth = w_in.shape[0]
    assert depth == 1 and hgrn_lb_logits.shape[0] == 2
    l = 0
    row = lambda p: p.reshape(1, -1)

    h1 = _mixer(x, row(norm_mix[l]), w_in[l].astype(BF16), row(gmlp_ln[l]), gmlp_w_spatial[l],
                gmlp_b_spatial[l].T, row(gmlp_beta[l]), hgrn_lb_logits, row(hgrn_out_gain[l]),
                w_out[l].astype(BF16))
    k_mem, v_mem = _memkv(mem, row(norm_mem[l]), w_xkv[l].astype(BF16))

    w_router = jnp.concatenate([w_router_group[l].T, w_router_expert[l].T], axis=0)
    w_router = jnp.pad(w_router, ((0, ROUTER_ROWS - w_router.shape[0]), (0, 0)))
    wr_hi = w_router.astype(BF16)
    wr_lo = (w_router - wr_hi.astype(F32)).astype(BF16)
    h2, xn, logits = _xattn(h1, row(norm_xattn[l]), w_xq[l].astype(BF16), k_mem, v_mem,
                            w_xo[l].astype(BF16), row(norm_ffn[l]), wr_hi, wr_lo)

    bias = jnp.concatenate([b_router_group[l], b_router_expert[l]]).astype(F32)
    ids, gates, rank, counts = _route(bias, logits.reshape(ROUTER_ROWS, T // LANES, LANES))

    counts = counts[::ROUTE_SUB, 0].astype(jnp.int32)
    padded = (counts + EXPERT_ROWS - 1) // EXPERT_ROWS * EXPERT_ROWS
    seg_end = jnp.cumsum(padded)
    seg_start = seg_end - padded
    n_blocks = (2 * T) // EXPERT_ROWS + N_EXPERTS
    block_first_row = jnp.arange(n_blocks, dtype=jnp.int32) * EXPERT_ROWS
    block_expert = jnp.minimum(
        jnp.sum(block_first_row[:, None] >= seg_end[None, :], axis=1), N_EXPERTS - 1).astype(jnp.int32)
    n_used = (seg_end[-1:] // EXPERT_ROWS).astype(jnp.int32)

    dest = _dest(seg_start, ids, rank).reshape(2, T)

    xs = _dispatch(dest[0], dest[1], xn, n_blocks * EXPERT_ROWS)
    yb = _experts(block_expert, n_used, xs, w_expert_gate[l].astype(BF16), w_expert_up[l].astype(BF16),
                  w_expert_down[l].astype(BF16))
    gates_t = gates.reshape(2, T).T
    y01 = _gather_rows(yb, dest.reshape(2 * T))
    out = _combine(y01, h2.reshape(T, D), gates_t, row(norm_final))
    return out.reshape(B, S, D)
```

```python
import functools

import jax
import jax.numpy as jnp
from jax import lax
from jax.experimental import pallas as pl
from jax.experimental.pallas import tpu as pltpu
from jax.experimental.pallas import tpu_sc as plsc

F32 = jnp.float32
BF16 = jnp.bfloat16
EPS = 1e-6

D_MODEL = 1024
D_GMLP = 512
GMLP_GROUPS = 4
GMLP_CHUNK = 128
D_HGRN = 512
HGRN_HEADS = 4
HGRN_DK = 128
HGRN_CHUNK = 64
D_IN_PROJ = 2 * D_GMLP + 4 * D_HGRN
N_MEM = 256
XATTN_HEADS = 4
XATTN_HEAD_DIM = D_MODEL // XATTN_HEADS
N_GROUPS = 4
EXPERTS_PER_GROUP = 8
N_EXPERTS = N_GROUPS * EXPERTS_PER_GROUP
D_EXPERT = 512

LANES = 128
MIX_ROWS = 256
ATT_ROWS = 512
ROUTER_ROWS = 40
ROUTE_SUB = 16
EXPERT_ROWS = 256
MOVE_ROWS = 256
SC_WINDOW = 64
SC_CORES = 2
SC_SUBCORES = 16
_SC_WORKERS = SC_CORES * SC_SUBCORES
VMEM_LIMIT = 48 * 1024 * 1024


def _rms(x, gain):
    return x * lax.rsqrt(jnp.mean(x * x, axis=-1, keepdims=True) + EPS) * gain


def _dot(a, b):
    return jnp.dot(a, b, preferred_element_type=F32)


def _dot_nt(a, b):
    return lax.dot_general(a, b, (((1,), (1,)), ((), ())), preferred_element_type=F32)


def _dot_tn(a, b):
    return lax.dot_general(a, b, (((0,), (0,)), ((), ())), preferred_element_type=F32)


def _gelu(x):
    return 0.5 * x * (1.0 + jnp.tanh(0.7978845608028654 * (x + 0.044715 * (x * x * x))))


def _sigmoid(x):
    return 1.0 / (1.0 + jnp.exp(-x))


def _block_id(idx, size):
    assert size & (size - 1) == 0
    return lax.shift_right_logical(idx, size.bit_length() - 1)


_HIGH_HALF = 0xFFFF0000


def _pack_pairs(x):
    c = x.shape[1] // 2
    bits = lax.bitcast_convert_type(x.astype(BF16).astype(F32), jnp.uint32)
    return (bits[:, c:] & jnp.uint32(_HIGH_HALF)) | lax.shift_right_logical(bits[:, :c], jnp.uint32(16))


def _unpack_pairs(w):
    lo = lax.bitcast_convert_type(lax.shift_left(w, jnp.uint32(16)), F32)
    hi = lax.bitcast_convert_type(w & jnp.uint32(_HIGH_HALF), F32)
    return jnp.concatenate([lo, hi], axis=1)


def _mixer_kernel(x_ref, nmix_ref, win_ref, gln_ref, ws_ref, bst_ref, beta_ref, lbl_ref, og_ref,
                  wout_ref, o_ref, proj_ref, ycat_ref, state_ref):
    rows = x_ref.shape[1]

    @pl.when(pl.program_id(1) == 0)
    def _():
        state_ref[...] = jnp.zeros_like(state_ref)

    x = x_ref[0]
    a = _rms(x, nmix_ref[...]).astype(BF16)
    proj_ref[...] = _dot(a, win_ref[...])

    u = _gelu(proj_ref[:, 0:D_GMLP])
    v = _gelu(proj_ref[:, D_GMLP:2 * D_GMLP])
    vc = v - jnp.mean(v, axis=-1, keepdims=True)
    vn = (vc * lax.rsqrt(jnp.mean(vc * vc, axis=-1, keepdims=True) + EPS) * gln_ref[...]).astype(BF16)
    r_i = lax.broadcasted_iota(jnp.int32, (GMLP_CHUNK, GMLP_CHUNK), 0)
    c_i = lax.broadcasted_iota(jnp.int32, (GMLP_CHUNK, GMLP_CHUNK), 1)
    causal = c_i <= r_i
    w_tril = [jnp.where(causal, ws_ref[g], 0.0).astype(BF16) for g in range(GMLP_GROUPS)]
    z_rows = []
    for c in range(rows // GMLP_CHUNK):
        z_cols = []
        for g in range(GMLP_GROUPS):
            vg = vn[c * GMLP_CHUNK:(c + 1) * GMLP_CHUNK, g * LANES:(g + 1) * LANES]
            z_cols.append(_dot(w_tril[g], vg) + bst_ref[:, g:g + 1])
        z_rows.append(jnp.concatenate(z_cols, axis=1))
    z = jnp.concatenate(z_rows, axis=0)
    ycat_ref[:, 0:D_GMLP] = _rms(u * z, beta_ref[...]).astype(BF16)

    lbl = lbl_ref[...]
    e_lb = jnp.exp(lbl - jnp.max(lbl, axis=0, keepdims=True))
    lb = e_lb[0:1] / jnp.sum(e_lb, axis=0, keepdims=True)
    base = 2 * D_GMLP
    f = lb + (1.0 - lb) * _sigmoid(proj_ref[:, base + D_HGRN:base + 2 * D_HGRN])
    log_f = jnp.log(f)
    rr = lax.broadcasted_iota(jnp.int32, (rows, rows), 0)
    cc = lax.broadcasted_iota(jnp.int32, (rows, rows), 1)
    tri = jnp.where((_block_id(rr, HGRN_CHUNK) == _block_id(cc, HGRN_CHUNK)) & (cc <= rr),
                    1.0, 0.0).astype(BF16)
    lf_hi = log_f.astype(BF16)
    lf_lo = (log_f - lf_hi.astype(F32)).astype(BF16)
    b_all = _dot(tri, lf_hi) + _dot(tri, lf_lo)
    r64 = lax.broadcasted_iota(jnp.int32, (HGRN_CHUNK, HGRN_CHUNK), 0)
    c64 = lax.broadcasted_iota(jnp.int32, (HGRN_CHUNK, HGRN_CHUNK), 1)
    causal64 = c64 <= r64
    for c in range(rows // HGRN_CHUNK):
        rs = slice(c * HGRN_CHUNK, (c + 1) * HGRN_CHUNK)
        bc = b_all[rs]
        bl = bc[HGRN_CHUNK - 1:HGRN_CHUNK]
        q_c = proj_ref[rs, base:base + D_HGRN]
        q_c = q_c * _sigmoid(q_c)
        k_c = 1.0 - f[rs]
        qd = (q_c * jnp.exp(bc)).astype(BF16)
        ki = (k_c * jnp.exp(-bc)).astype(BF16)
        kte = (k_c * jnp.exp(bl - bc)).astype(BF16)
        decay = jnp.exp(bl)
        v_c = proj_ref[rs, base + 2 * D_HGRN:base + 3 * D_HGRN].astype(BF16)
        g_c = proj_ref[rs, base + 3 * D_HGRN:base + 4 * D_HGRN]
        gate = og_ref[...] * (g_c * _sigmoid(g_c))
        for h in range(HGRN_HEADS):
            cs = slice(h * HGRN_DK, (h + 1) * HGRN_DK)
            scores = jnp.where(causal64, _dot_nt(qd[:, cs], ki[:, cs]), 0.0).astype(BF16)
            st = state_ref[h]
            o = _dot(scores, v_c[:, cs]) + _dot_nt(qd[:, cs], st.astype(BF16))
            state_ref[h] = st * decay[:, cs] + _dot_tn(v_c[:, cs], kte[:, cs])
            o = o * lax.rsqrt(jnp.mean(o * o, axis=-1, keepdims=True) + EPS)
            ycat_ref[rs, D_GMLP + h * HGRN_DK:D_GMLP + (h + 1) * HGRN_DK] = (o * gate[:, cs]).astype(BF16)

    o_ref[0] = x + _dot(ycat_ref[...], wout_ref[...])


def _mixer(x, norm_mix, w_in, gmlp_ln, w_s, b_s_t, beta, lb_logits, out_gain, w_out):
    B, S, D = x.shape
    const2 = lambda b, s: (0, 0)
    return pl.pallas_call(
        _mixer_kernel,
        grid=(B, S // MIX_ROWS),
        in_specs=[
            pl.BlockSpec((1, MIX_ROWS, D), lambda b, s: (b, s, 0)),
            pl.BlockSpec((1, D), const2),
            pl.BlockSpec((D, D_IN_PROJ), const2),
            pl.BlockSpec((1, D_GMLP), const2),
            pl.BlockSpec((GMLP_GROUPS, GMLP_CHUNK, GMLP_CHUNK), lambda b, s: (0, 0, 0)),
            pl.BlockSpec((GMLP_CHUNK, GMLP_GROUPS), const2),
            pl.BlockSpec((1, D_GMLP), const2),
            pl.BlockSpec(lb_logits.shape, const2),
            pl.BlockSpec((1, D_HGRN), const2),
            pl.BlockSpec((D, D), const2),
        ],
        out_specs=pl.BlockSpec((1, MIX_ROWS, D), lambda b, s: (b, s, 0)),
        out_shape=jax.ShapeDtypeStruct((B, S, D), F32),
        scratch_shapes=[
            pltpu.VMEM((MIX_ROWS, D_IN_PROJ), F32),
            pltpu.VMEM((MIX_ROWS, D), BF16),
            pltpu.VMEM((HGRN_HEADS, HGRN_DK, HGRN_DK), F32),
        ],
        compiler_params=pltpu.CompilerParams(
            dimension_semantics=("arbitrary", "arbitrary"), vmem_limit_bytes=VMEM_LIMIT),
        name="mixer",
    )(x, norm_mix, w_in, gmlp_ln, w_s, b_s_t, beta, lb_logits, out_gain, w_out)


def _memkv_kernel(mem_ref, nm_ref, wkv_ref, k_ref, v_ref):
    m = _rms(mem_ref[0], nm_ref[...]).astype(BF16)
    kv = _dot(m, wkv_ref[...])
    k_ref[0] = kv[:, :D_MODEL].astype(BF16)
    v_ref[0] = kv[:, D_MODEL:].astype(BF16)


def _memkv(mem, norm_mem, w_kv):
    B, M, D = mem.shape
    out = jax.ShapeDtypeStruct((B, M, D), BF16)
    return pl.pallas_call(
        _memkv_kernel,
        grid=(B,),
        in_specs=[
            pl.BlockSpec((1, M, D), lambda b: (b, 0, 0)),
            pl.BlockSpec((1, D), lambda b: (0, 0)),
            pl.BlockSpec((D, 2 * D), lambda b: (0, 0)),
        ],
        out_specs=[pl.BlockSpec((1, M, D), lambda b: (b, 0, 0))] * 2,
        out_shape=[out, out],
        compiler_params=pltpu.CompilerParams(
            dimension_semantics=("arbitrary",), vmem_limit_bytes=VMEM_LIMIT),
        name="memkv",
    )(mem, norm_mem, w_kv)


def _xattn_kernel(h_ref, nx_ref, wq_ref, k_ref, v_ref, wo_ref, nf_ref, wrh_ref, wrl_ref,
                  h2_ref, xn_ref, lg_ref, att_ref):
    h = h_ref[0]
    hn = _rms(h, nx_ref[...]).astype(BF16)
    q = (_dot(hn, wq_ref[...]) * (XATTN_HEAD_DIM ** -0.5)).astype(BF16)
    for hd in range(XATTN_HEADS):
        cs = slice(hd * XATTN_HEAD_DIM, (hd + 1) * XATTN_HEAD_DIM)
        s = _dot_nt(q[:, cs], k_ref[0, :, cs])
        p = jnp.exp(s - jnp.max(s, axis=-1, keepdims=True))
        p = (p / jnp.sum(p, axis=-1, keepdims=True)).astype(BF16)
        att_ref[:, cs] = _dot(p, v_ref[0, :, cs]).astype(BF16)
    h2 = h + _dot(att_ref[...], wo_ref[...])
    h2_ref[0] = h2
    xn = _rms(h2, nf_ref[...])
    xn_ref[...] = _pack_pairs(xn)
    x_hi = xn.astype(BF16)
    x_lo = (xn - x_hi.astype(F32)).astype(BF16)
    lg_ref[...] = (_dot_nt(wrh_ref[...], x_hi) + _dot_nt(wrh_ref[...], x_lo)
                   + _dot_nt(wrl_ref[...], x_hi))


def _xattn(h1, norm_x, w_q, k_mem, v_mem, w_o, norm_ffn, wr_hi, wr_lo):
    B, S, D = h1.shape
    n_s = S // ATT_ROWS
    const2 = lambda b, s: (0, 0)
    return pl.pallas_call(
        _xattn_kernel,
        grid=(B, n_s),
        in_specs=[
            pl.BlockSpec((1, ATT_ROWS, D), lambda b, s: (b, s, 0)),
            pl.BlockSpec((1, D), const2),
            pl.BlockSpec((D, D), const2),
            pl.BlockSpec((1, N_MEM, D), lambda b, s: (b, 0, 0)),
            pl.BlockSpec((1, N_MEM, D), lambda b, s: (b, 0, 0)),
            pl.BlockSpec((D, D), const2),
            pl.BlockSpec((1, D), const2),
            pl.BlockSpec((ROUTER_ROWS, D), const2),
            pl.BlockSpec((ROUTER_ROWS, D), const2),
        ],
        out_specs=[
            pl.BlockSpec((1, ATT_ROWS, D), lambda b, s: (b, s, 0)),
            pl.BlockSpec((ATT_ROWS, D // 2), lambda b, s: (b * n_s + s, 0)),
            pl.BlockSpec((ROUTER_ROWS, ATT_ROWS), lambda b, s: (0, b * n_s + s)),
        ],
        out_shape=[
            jax.ShapeDtypeStruct((B, S, D), F32),
            jax.ShapeDtypeStruct((B * S, D // 2), jnp.uint32),
            jax.ShapeDtypeStruct((ROUTER_ROWS, B * S), F32),
        ],
        scratch_shapes=[pltpu.VMEM((ATT_ROWS, D), BF16)],
        compiler_params=pltpu.CompilerParams(
            dimension_semantics=("arbitrary", "arbitrary"), vmem_limit_bytes=VMEM_LIMIT),
        name="xattn",
    )(h1, norm_x, w_q, k_mem, v_mem, w_o, norm_ffn, wr_hi, wr_lo)


def _route_kernel(bias_ref, lg_ref, ids_ref, gates_ref, rank_ref, cnt_ref, base_ref):
    sub = lg_ref.shape[1]

    @pl.when(pl.program_id(0) == 0)
    def _():
        base_ref[...] = jnp.zeros_like(base_ref)

    best = lg_ref[0] + bias_ref[0]
    gl = [best]
    sel = jnp.zeros(best.shape, jnp.int32)
    for g in range(1, N_GROUPS):
        cur = lg_ref[g] + bias_ref[g]
        gl.append(cur)
        better = cur > best
        best = jnp.where(better, cur, best)
        sel = jnp.where(better, g, sel)
    denom = jnp.exp(gl[0] - best)
    for g in range(1, N_GROUPS):
        denom = denom + jnp.exp(gl[g] - best)
    g_gate = 1.0 / denom

    ev = []
    for j in range(EXPERTS_PER_GROUP):
        val = lg_ref[N_GROUPS + j] + bias_ref[N_GROUPS + j]
        for g in range(1, N_GROUPS):
            e = g * EXPERTS_PER_GROUP + j
            val = jnp.where(sel == g, lg_ref[N_GROUPS + e] + bias_ref[N_GROUPS + e], val)
        ev.append(val)
    v1, i1 = ev[0], jnp.zeros(best.shape, jnp.int32)
    for j in range(1, EXPERTS_PER_GROUP):
        better = ev[j] > v1
        v1 = jnp.where(better, ev[j], v1)
        i1 = jnp.where(better, j, i1)
    rest = [jnp.where(i1 == j, -jnp.inf, ev[j]) for j in range(EXPERTS_PER_GROUP)]
    v2, i2 = rest[0], jnp.zeros(best.shape, jnp.int32)
    for j in range(1, EXPERTS_PER_GROUP):
        better = rest[j] > v2
        v2 = jnp.where(better, rest[j], v2)
        i2 = jnp.where(better, j, i2)
    e2 = jnp.exp(v2 - v1)
    inv = 1.0 / (1.0 + e2)
    id1 = sel * EXPERTS_PER_GROUP + i1
    id2 = sel * EXPERTS_PER_GROUP + i2
    ids_ref[0] = id1
    ids_ref[1] = id2
    gates_ref[0] = inv * g_gate
    gates_ref[1] = e2 * inv * g_gate

    member = jnp.concatenate(
        [jnp.where((id1 == e) | (id2 == e), 1.0, 0.0) for e in range(N_EXPERTS)], axis=0).astype(BF16)
    n = N_EXPERTS * sub
    li = lax.broadcasted_iota(jnp.int32, (LANES, LANES), 0)
    lj = lax.broadcasted_iota(jnp.int32, (LANES, LANES), 1)
    before_lane = jnp.where(li < lj, 1.0, 0.0).astype(BF16)
    ones = jnp.ones((LANES, LANES), BF16)
    ri = lax.broadcasted_iota(jnp.int32, (n, n), 0)
    rj = lax.broadcasted_iota(jnp.int32, (n, n), 1)
    same = _block_id(ri, sub) == _block_id(rj, sub)
    before_row = jnp.where(same & (rj < ri), 1.0, 0.0).astype(BF16)
    all_row = jnp.where(same, 1.0, 0.0).astype(BF16)
    in_row = _dot(member, before_lane)
    prev_rows = _dot(_dot(before_row, member).astype(BF16), ones)
    total = _dot(_dot(all_row, member).astype(BF16), ones)
    base = base_ref[...]
    pos = base + prev_rows + in_row
    r1 = jnp.zeros(best.shape, F32)
    r2 = jnp.zeros(best.shape, F32)
    for e in range(N_EXPERTS):
        pe = pos[e * sub:(e + 1) * sub]
        r1 = jnp.where(id1 == e, pe, r1)
        r2 = jnp.where(id2 == e, pe, r2)
    rank_ref[0] = r1.astype(jnp.int32)
    rank_ref[1] = r2.astype(jnp.int32)
    base_ref[...] = base + total
    cnt_ref[...] = base + total


def _route(bias, logits3):
    rp, n_sub, _ = logits3.shape
    blk = lambda i: (0, i, 0)
    pair_i = jax.ShapeDtypeStruct((2, n_sub, LANES), jnp.int32)
    return pl.pallas_call(
        _route_kernel,
        grid=(n_sub // ROUTE_SUB,),
        in_specs=[
            pl.BlockSpec(memory_space=pltpu.SMEM),
            pl.BlockSpec((rp, ROUTE_SUB, LANES), blk),
        ],
        out_specs=[
            pl.BlockSpec((2, ROUTE_SUB, LANES), blk),
            pl.BlockSpec((2, ROUTE_SUB, LANES), blk),
            pl.BlockSpec((2, ROUTE_SUB, LANES), blk),
            pl.BlockSpec((N_EXPERTS * ROUTE_SUB, LANES), lambda i: (0, 0)),
        ],
        out_shape=[
            pair_i,
            jax.ShapeDtypeStruct((2, n_sub, LANES), F32),
            pair_i,
            jax.ShapeDtypeStruct((N_EXPERTS * ROUTE_SUB, LANES), F32),
        ],
        scratch_shapes=[pltpu.VMEM((N_EXPERTS * ROUTE_SUB, LANES), F32)],
        compiler_params=pltpu.CompilerParams(
            dimension_semantics=("arbitrary",), vmem_limit_bytes=VMEM_LIMIT),
        name="route",
    )(bias, logits3)


def _dest_kernel(start_ref, ids_ref, rank_ref, dest_ref):
    ids = ids_ref[...]
    off = jnp.zeros(ids.shape, jnp.int32)
    for e in range(N_EXPERTS):
        off = jnp.where(ids == e, start_ref[e], off)
    dest_ref[...] = rank_ref[...] + off


def _dest(seg_start, ids, rank):
    _, n_sub, _ = ids.shape
    blk = pl.BlockSpec((2, ROUTE_SUB, LANES), lambda i: (0, i, 0))
    return pl.pallas_call(
        _dest_kernel,
        grid=(n_sub // ROUTE_SUB,),
        in_specs=[pl.BlockSpec(memory_space=pltpu.SMEM), blk, blk],
        out_specs=blk,
        out_shape=jax.ShapeDtypeStruct(ids.shape, jnp.int32),
        name="dest",
    )(seg_start, ids, rank)


def _sc_mesh():
    return plsc.VectorSubcoreMesh(core_axis_name="core", subcore_axis_name="subcore")


def _sc_worker(rows_total):
    rows = rows_total // _SC_WORKERS
    wid = lax.axis_index("core") * SC_SUBCORES + lax.axis_index("subcore")
    return wid * rows, rows


def _dispatch(dest0, dest1, xn, n_slots):
    T, D = xn.shape
    W = SC_WINDOW
    rows = T // _SC_WORKERS

    @pl.kernel(out_type=jax.ShapeDtypeStruct((n_slots, D), xn.dtype), mesh=_sc_mesh(),
               scratch_types=[pltpu.VMEM((rows,), jnp.int32), pltpu.VMEM((rows,), jnp.int32),
                              pltpu.VMEM((2, W, D), xn.dtype),
                              pltpu.SemaphoreType.DMA((2,)), pltpu.SemaphoreType.DMA((2,))])
    def scatter_rows(x_hbm, d0_hbm, d1_hbm, xs_hbm, d0_v, d1_v, buf, in_sem, out_sem):
        base, _ = _sc_worker(T)
        pltpu.sync_copy(d0_hbm.at[pl.ds(base, rows)], d0_v)
        pltpu.sync_copy(d1_hbm.at[pl.ds(base, rows)], d1_v)

        def load(w, slot):
            return pltpu.make_async_copy(x_hbm.at[pl.ds(base + w * W, W)], buf.at[slot], in_sem.at[slot])

        def store(w, slot, d_v):
            return pltpu.make_async_copy(buf.at[slot], xs_hbm.at[d_v.at[pl.ds(w * W, W)]], out_sem.at[slot])

        def step(w, slot):
            load(w, slot).wait()
            store(w, slot, d0_v).start()
            store(w, slot, d1_v).start()
            store(w, slot, d0_v).wait()
            store(w, slot, d1_v).wait()

        n = rows // W
        load(0, 0).start()

        @pl.loop(0, n, step=2)
        def _(w):
            load(w + 1, 1).start()
            step(w, 0)

            @pl.when(w + 2 < n)
            def _():
                load(w + 2, 0).start()

            step(w + 1, 1)

    return scatter_rows(xn, dest0, dest1)


def _gather_rows(src, idx):
    M = idx.shape[0]
    D = src.shape[1]
    W = SC_WINDOW
    rows = M // _SC_WORKERS

    @pl.kernel(out_type=jax.ShapeDtypeStruct((M, D), src.dtype), mesh=_sc_mesh(),
               scratch_types=[pltpu.VMEM((rows,), jnp.int32), pltpu.VMEM((2, W, D), src.dtype),
                              pltpu.SemaphoreType.DMA((2,)), pltpu.SemaphoreType.DMA((2,))])
    def gather_rows(src_hbm, i_hbm, o_hbm, i_v, buf, in_sem, out_sem):
        base, _ = _sc_worker(M)
        pltpu.sync_copy(i_hbm.at[pl.ds(base, rows)], i_v)

        def load(w, slot):
            return pltpu.make_async_copy(src_hbm.at[i_v.at[pl.ds(w * W, W)]], buf.at[slot], in_sem.at[slot])

        def store(w, slot):
            return pltpu.make_async_copy(buf.at[slot], o_hbm.at[pl.ds(base + w * W, W)], out_sem.at[slot])

        n = rows // W
        load(0, 0).start()

        @pl.loop(0, n, step=2)
        def _(w):
            @pl.when(w > 0)
            def _():
                store(w - 1, 1).wait()

            load(w + 1, 1).start()
            load(w, 0).wait()
            store(w, 0).start()
            store(w, 0).wait()

            @pl.when(w + 2 < n)
            def _():
                load(w + 2, 0).start()

            load(w + 1, 1).wait()
            store(w + 1, 1).start()

        store(n - 1, 1).wait()

    return gather_rows(src, idx)


def _expert_kernel(be_ref, nb_ref, x_ref, wg_ref, wu_ref, wd_ref, y_ref):
    used = pl.program_id(0) < nb_ref[0]

    @pl.when(used)
    def _():
        x = _unpack_pairs(x_ref[...]).astype(BF16)
        g = _dot(x, wg_ref[0])
        u = _dot(x, wu_ref[0])
        hid = (g * _sigmoid(g) * u).astype(BF16)
        y_ref[...] = _pack_pairs(_dot(hid, wd_ref[0]))

    @pl.when(jnp.logical_not(used))
    def _():
        y_ref[...] = jnp.zeros_like(y_ref)


def _experts(block_expert, n_used, xs, w_gate, w_up, w_down):
    n_slots, half = xs.shape
    D = 2 * half
    n_blocks = n_slots // EXPERT_ROWS
    row_blk = lambda i, be, nb: (jnp.minimum(i, nb[0] - 1), 0)
    w_blk = lambda i, be, nb: (be[i], 0, 0)
    grid_spec = pltpu.PrefetchScalarGridSpec(
        num_scalar_prefetch=2,
        grid=(n_blocks,),
        in_specs=[
            pl.BlockSpec((EXPERT_ROWS, half), row_blk),
            pl.BlockSpec((1, D, D_EXPERT), w_blk),
            pl.BlockSpec((1, D, D_EXPERT), w_blk),
            pl.BlockSpec((1, D_EXPERT, D), w_blk),
        ],
        out_specs=pl.BlockSpec((EXPERT_ROWS, half), lambda i, be, nb: (i, 0)),
    )
    return pl.pallas_call(
        _expert_kernel,
        grid_spec=grid_spec,
        out_shape=jax.ShapeDtypeStruct((n_slots, half), jnp.uint32),
        compiler_params=pltpu.CompilerParams(
            dimension_semantics=("arbitrary",), vmem_limit_bytes=VMEM_LIMIT),
        name="experts",
    )(block_expert, n_used, xs, w_gate, w_up, w_down)


def _combine_kernel(y0_ref, y1_ref, h_ref, gates_ref, nfin_ref, o_ref):
    gates = gates_ref[...]
    h = (h_ref[...] + gates[:, 0:1] * _unpack_pairs(y0_ref[...])
         + gates[:, 1:2] * _unpack_pairs(y1_ref[...]))
    o_ref[...] = _rms(h, nfin_ref[...])


def _combine(y01, h2, gates_t, norm_final):
    T, D = h2.shape
    n_t = T // MOVE_ROWS
    return pl.pallas_call(
        _combine_kernel,
        grid=(n_t,),
        in_specs=[
            pl.BlockSpec((MOVE_ROWS, D // 2), lambda i: (i, 0)),
            pl.BlockSpec((MOVE_ROWS, D // 2), lambda i: (i + n_t, 0)),
            pl.BlockSpec((MOVE_ROWS, D), lambda i: (i, 0)),
            pl.BlockSpec((MOVE_ROWS, 2), lambda i: (i, 0)),
            pl.BlockSpec((1, D), lambda i: (0, 0)),
        ],
        out_specs=pl.BlockSpec((MOVE_ROWS, D), lambda i: (i, 0)),
        out_shape=jax.ShapeDtypeStruct((T, D), F32),
        compiler_params=pltpu.CompilerParams(
            dimension_semantics=("arbitrary",), vmem_limit_bytes=VMEM_LIMIT),
        name="combine",
    )(y01, y01, h2, gates_t, norm_final)


def kernel(x, mem, norm_mix, w_in, gmlp_ln, gmlp_w_spatial, gmlp_b_spatial, gmlp_beta, hgrn_lb_logits, hgrn_out_gain, w_out, norm_xattn, norm_mem, w_xq, w_xkv, w_xo, norm_ffn, w_router_group, b_router_group, w_router_expert, b_router_expert, w_expert_gate, w_expert_up, w_expert_down, norm_final):
    B, S, D = x.shape
    T = B * S
    depth = w_in.shape[0]
    assert depth == 1 and hgrn_lb_logits.shape[0] == 2
    l = 0
    row = lambda p: p.reshape(1, -1)

    h1 = _mixer(x, row(norm_mix[l]), w_in[l].astype(BF16), row(gmlp_ln[l]), gmlp_w_spatial[l],
                gmlp_b_spatial[l].T, row(gmlp_beta[l]), hgrn_lb_logits, row(hgrn_out_gain[l]),
                w_out[l].astype(BF16))
    k_mem, v_mem = _memkv(mem, row(norm_mem[l]), w_xkv[l].astype(BF16))

    w_router = jnp.concatenate([w_router_group[l].T, w_router_expert[l].T], axis=0)
    w_router = jnp.pad(w_router, ((0, ROUTER_ROWS - w_router.shape[0]), (0, 0)))
    wr_hi = w_router.astype(BF16)
    wr_lo = (w_router - wr_hi.astype(F32)).astype(BF16)
    h2, xn, logits = _xattn(h1, row(norm_xattn[l]), w_xq[l].astype(BF16), k_mem, v_mem,
                            w_xo[l].astype(BF16), row(norm_ffn[l]), wr_hi, wr_lo)

    bias = jnp.concatenate([b_router_group[l], b_router_expert[l]]).astype(F32)
    ids, gates, rank, counts = _route(bias, logits.reshape(ROUTER_ROWS, T // LANES, LANES))

    counts = counts[::ROUTE_SUB, 0].astype(jnp.int32)
    padded = (counts + EXPERT_ROWS - 1) // EXPERT_ROWS * EXPERT_ROWS
    seg_end = jnp.cumsum(padded)
    seg_start = seg_end - padded
    n_blocks = (2 * T) // EXPERT_ROWS + N_EXPERTS
    block_first_row = jnp.arange(n_blocks, dtype=jnp.int32) * EXPERT_ROWS
    block_expert = jnp.minimum(
        jnp.sum(block_first_row[:, None] >= seg_end[None, :], axis=1), N_EXPERTS - 1).astype(jnp.int32)
    n_used = (seg_end[-1:] // EXPERT_ROWS).astype(jnp.int32)

    dest = _dest(seg_start, ids, rank).reshape(2, T)

    xs = _dispatch(dest[0], dest[1], xn, n_blocks * EXPERT_ROWS)
    yb = _experts(block_expert, n_used, xs, w_expert_gate[l].astype(BF16), w_expert_up[l].astype(BF16),
                  w_expert_down[l].astype(BF16))
    gates_t = gates.reshape(2, T).T
    y01 = _gather_rows(yb, dest.reshape(2 * T))
    out = _combine(y01, h2.reshape(T, D), gates_t, row(norm_final))
    return out.reshape(B, S, D)
```

```python
import functools

import jax
import jax.numpy as jnp
from jax import lax
from jax.experimental import pallas as pl
from jax.experimental.pallas import tpu as pltpu
from jax.experimental.pallas import tpu_sc as plsc

F32 = jnp.float32
BF16 = jnp.bfloat16
EPS = 1e-6

D_MODEL = 1024
D_GMLP = 512
GMLP_GROUPS = 4
GMLP_CHUNK = 128
D_HGRN = 512
HGRN_HEADS = 4
HGRN_DK = 128
HGRN_CHUNK = 64
D_IN_PROJ = 2 * D_GMLP + 4 * D_HGRN
N_MEM = 256
XATTN_HEADS = 4
XATTN_HEAD_DIM = D_MODEL // XATTN_HEADS
N_GROUPS = 4
EXPERTS_PER_GROUP = 8
N_EXPERTS = N_GROUPS * EXPERTS_PER_GROUP
D_EXPERT = 512

LANES = 128
MIX_ROWS = 512
MIX_CHAIN_ROWS = 256
ATT_ROWS = 512
ROUTER_ROWS = 40
ROUTE_SUB = 16
EXPERT_ROWS = 512
EXPERT_CHAIN_ROWS = 256
MOVE_ROWS = 256
SC_WINDOW = 64
SC_CORES = 2
SC_SUBCORES = 16
_SC_WORKERS = SC_CORES * SC_SUBCORES
VMEM_LIMIT = 48 * 1024 * 1024


def _rms(x, gain):
    return x * lax.rsqrt(jnp.mean(x * x, axis=-1, keepdims=True) + EPS) * gain


def _dot(a, b):
    return jnp.dot(a, b, preferred_element_type=F32)


def _dot_nt(a, b):
    return lax.dot_general(a, b, (((1,), (1,)), ((), ())), preferred_element_type=F32)


def _dot_tn(a, b):
    return lax.dot_general(a, b, (((0,), (0,)), ((), ())), preferred_element_type=F32)


def _gelu(x):
    return 0.5 * x * (1.0 + jnp.tanh(0.7978845608028654 * (x + 0.044715 * (x * x * x))))


def _sigmoid(x):
    return 1.0 / (1.0 + jnp.exp(-x))


def _block_id(idx, size):
    assert size & (size - 1) == 0
    return lax.shift_right_logical(idx, size.bit_length() - 1)


_HIGH_HALF = 0xFFFF0000


def _pack_pairs(x):
    c = x.shape[1] // 2
    bits = lax.bitcast_convert_type(x.astype(BF16).astype(F32), jnp.uint32)
    return (bits[:, c:] & jnp.uint32(_HIGH_HALF)) | lax.shift_right_logical(bits[:, :c], jnp.uint32(16))


def _unpack_pairs(w):
    lo = lax.bitcast_convert_type(lax.shift_left(w, jnp.uint32(16)), F32)
    hi = lax.bitcast_convert_type(w & jnp.uint32(_HIGH_HALF), F32)
    return jnp.concatenate([lo, hi], axis=1)


def _mixer_kernel(x_ref, nmix_ref, win_ref, gln_ref, ws_ref, bst_ref, beta_ref, lbl_ref, og_ref,
                  wout_ref, o_ref, proj_ref, ycat_ref, state_ref):
    @pl.when(pl.program_id(1) == 0)
    def _():
        state_ref[...] = jnp.zeros_like(state_ref)

    n = MIX_CHAIN_ROWS
    r_i = lax.broadcasted_iota(jnp.int32, (GMLP_CHUNK, GMLP_CHUNK), 0)
    c_i = lax.broadcasted_iota(jnp.int32, (GMLP_CHUNK, GMLP_CHUNK), 1)
    causal = c_i <= r_i
    w_tril = [jnp.where(causal, ws_ref[g], 0.0).astype(BF16) for g in range(GMLP_GROUPS)]
    lbl = lbl_ref[...]
    e_lb = jnp.exp(lbl - jnp.max(lbl, axis=0, keepdims=True))
    lb = e_lb[0:1] / jnp.sum(e_lb, axis=0, keepdims=True)
    rr = lax.broadcasted_iota(jnp.int32, (n, n), 0)
    cc = lax.broadcasted_iota(jnp.int32, (n, n), 1)
    tri = jnp.where((_block_id(rr, HGRN_CHUNK) == _block_id(cc, HGRN_CHUNK)) & (cc <= rr),
                    1.0, 0.0).astype(BF16)
    r64 = lax.broadcasted_iota(jnp.int32, (HGRN_CHUNK, HGRN_CHUNK), 0)
    c64 = lax.broadcasted_iota(jnp.int32, (HGRN_CHUNK, HGRN_CHUNK), 1)
    causal64 = c64 <= r64
    base = 2 * D_GMLP

    for chain in range(x_ref.shape[1] // n):
        r0 = chain * n
        x = x_ref[0, r0:r0 + n]
        a = _rms(x, nmix_ref[...]).astype(BF16)
        proj_ref[r0:r0 + n] = _dot(a, win_ref[...])

        u = _gelu(proj_ref[r0:r0 + n, 0:D_GMLP])
        v = _gelu(proj_ref[r0:r0 + n, D_GMLP:2 * D_GMLP])
        vc = v - jnp.mean(v, axis=-1, keepdims=True)
        vn = (vc * lax.rsqrt(jnp.mean(vc * vc, axis=-1, keepdims=True) + EPS) * gln_ref[...]).astype(BF16)
        z_rows = []
        for c in range(n // GMLP_CHUNK):
            z_cols = []
            for g in range(GMLP_GROUPS):
                vg = vn[c * GMLP_CHUNK:(c + 1) * GMLP_CHUNK, g * LANES:(g + 1) * LANES]
                z_cols.append(_dot(w_tril[g], vg) + bst_ref[:, g:g + 1])
            z_rows.append(jnp.concatenate(z_cols, axis=1))
        z = jnp.concatenate(z_rows, axis=0)
        ycat_ref[r0:r0 + n, 0:D_GMLP] = _rms(u * z, beta_ref[...]).astype(BF16)

        f = lb + (1.0 - lb) * _sigmoid(proj_ref[r0:r0 + n, base + D_HGRN:base + 2 * D_HGRN])
        log_f = jnp.log(f)
        lf_hi = log_f.astype(BF16)
        lf_lo = (log_f - lf_hi.astype(F32)).astype(BF16)
        b_all = _dot(tri, lf_hi) + _dot(tri, lf_lo)
        for c in range(n // HGRN_CHUNK):
            rs = slice(c * HGRN_CHUNK, (c + 1) * HGRN_CHUNK)
            ps = slice(r0 + c * HGRN_CHUNK, r0 + (c + 1) * HGRN_CHUNK)
            bc = b_all[rs]
            bl = bc[HGRN_CHUNK - 1:HGRN_CHUNK]
            q_c = proj_ref[ps, base:base + D_HGRN]
            q_c = q_c * _sigmoid(q_c)
            k_c = 1.0 - f[rs]
            qd = (q_c * jnp.exp(bc)).astype(BF16)
            ki = (k_c * jnp.exp(-bc)).astype(BF16)
            kte = (k_c * jnp.exp(bl - bc)).astype(BF16)
            decay = jnp.exp(bl)
            v_c = proj_ref[ps, base + 2 * D_HGRN:base + 3 * D_HGRN].astype(BF16)
            g_c = proj_ref[ps, base + 3 * D_HGRN:base + 4 * D_HGRN]
            gate = og_ref[...] * (g_c * _sigmoid(g_c))
            for h in range(HGRN_HEADS):
                cs = slice(h * HGRN_DK, (h + 1) * HGRN_DK)
                scores = jnp.where(causal64, _dot_nt(qd[:, cs], ki[:, cs]), 0.0).astype(BF16)
                st = state_ref[h]
                o = _dot(scores, v_c[:, cs]) + _dot_nt(qd[:, cs], st.astype(BF16))
                state_ref[h] = st * decay[:, cs] + _dot_tn(v_c[:, cs], kte[:, cs])
                o = o * lax.rsqrt(jnp.mean(o * o, axis=-1, keepdims=True) + EPS)
                ycat_ref[ps, D_GMLP + h * HGRN_DK:D_GMLP + (h + 1) * HGRN_DK] = (o * gate[:, cs]).astype(BF16)

        o_ref[0, r0:r0 + n] = x + _dot(ycat_ref[r0:r0 + n], wout_ref[...])


def _mixer(x, norm_mix, w_in, gmlp_ln, w_s, b_s_t, beta, lb_logits, out_gain, w_out):
    B, S, D = x.shape
    const2 = lambda b, s: (0, 0)
    return pl.pallas_call(
        _mixer_kernel,
        grid=(B, S // MIX_ROWS),
        in_specs=[
            pl.BlockSpec((1, MIX_ROWS, D), lambda b, s: (b, s, 0)),
            pl.BlockSpec((1, D), const2),
            pl.BlockSpec((D, D_IN_PROJ), const2),
            pl.BlockSpec((1, D_GMLP), const2),
            pl.BlockSpec((GMLP_GROUPS, GMLP_CHUNK, GMLP_CHUNK), lambda b, s: (0, 0, 0)),
            pl.BlockSpec((GMLP_CHUNK, GMLP_GROUPS), const2),
            pl.BlockSpec((1, D_GMLP), const2),
            pl.BlockSpec(lb_logits.shape, const2),
            pl.BlockSpec((1, D_HGRN), const2),
            pl.BlockSpec((D, D), const2),
        ],
        out_specs=pl.BlockSpec((1, MIX_ROWS, D), lambda b, s: (b, s, 0)),
        out_shape=jax.ShapeDtypeStruct((B, S, D), F32),
        scratch_shapes=[
            pltpu.VMEM((MIX_ROWS, D_IN_PROJ), F32),
            pltpu.VMEM((MIX_ROWS, D), BF16),
            pltpu.VMEM((HGRN_HEADS, HGRN_DK, HGRN_DK), F32),
        ],
        compiler_params=pltpu.CompilerParams(
            dimension_semantics=("arbitrary", "arbitrary"), vmem_limit_bytes=VMEM_LIMIT),
        name="mixer",
    )(x, norm_mix, w_in, gmlp_ln, w_s, b_s_t, beta, lb_logits, out_gain, w_out)


def _memkv_kernel(mem_ref, nm_ref, wkv_ref, k_ref, v_ref):
    m = _rms(mem_ref[0], nm_ref[...]).astype(BF16)
    kv = _dot(m, wkv_ref[...])
    k_ref[0] = kv[:, :D_MODEL].astype(BF16)
    v_ref[0] = kv[:, D_MODEL:].astype(BF16)


def _memkv(mem, norm_mem, w_kv):
    B, M, D = mem.shape
    out = jax.ShapeDtypeStruct((B, M, D), BF16)
    return pl.pallas_call(
        _memkv_kernel,
        grid=(B,),
        in_specs=[
            pl.BlockSpec((1, M, D), lambda b: (b, 0, 0)),
            pl.BlockSpec((1, D), lambda b: (0, 0)),
            pl.BlockSpec((D, 2 * D), lambda b: (0, 0)),
        ],
        out_specs=[pl.BlockSpec((1, M, D), lambda b: (b, 0, 0))] * 2,
        out_shape=[out, out],
        compiler_params=pltpu.CompilerParams(
            dimension_semantics=("arbitrary",), vmem_limit_bytes=VMEM_LIMIT),
        name="memkv",
    )(mem, norm_mem, w_kv)


def _xattn_kernel(h_ref, nx_ref, wq_ref, k_ref, v_ref, wo_ref, nf_ref, wrh_ref, wrl_ref,
                  h2_ref, xn_ref, lg_ref, att_ref):
    h = h_ref[0]
    hn = _rms(h, nx_ref[...]).astype(BF16)
    q = (_dot(hn, wq_ref[...]) * (XATTN_HEAD_DIM ** -0.5)).astype(BF16)
    for hd in range(XATTN_HEADS):
        cs = slice(hd * XATTN_HEAD_DIM, (hd + 1) * XATTN_HEAD_DIM)
        s = _dot_nt(q[:, cs], k_ref[0, :, cs])
        p = jnp.exp(s - jnp.max(s, axis=-1, keepdims=True))
        p = (p / jnp.sum(p, axis=-1, keepdims=True)).astype(BF16)
        att_ref[:, cs] = _dot(p, v_ref[0, :, cs]).astype(BF16)
    h2 = h + _dot(att_ref[...], wo_ref[...])
    h2_ref[0] = h2
    xn = _rms(h2, nf_ref[...])
    xn_ref[...] = _pack_pairs(xn)
    x_hi = xn.astype(BF16)
    x_lo = (xn - x_hi.astype(F32)).astype(BF16)
    lg_ref[...] = (_dot_nt(wrh_ref[...], x_hi) + _dot_nt(wrh_ref[...], x_lo)
                   + _dot_nt(wrl_ref[...], x_hi))


def _xattn(h1, norm_x, w_q, k_mem, v_mem, w_o, norm_ffn, wr_hi, wr_lo):
    B, S, D = h1.shape
    n_s = S // ATT_ROWS
    const2 = lambda b, s: (0, 0)
    return pl.pallas_call(
        _xattn_kernel,
        grid=(B, n_s),
        in_specs=[
            pl.BlockSpec((1, ATT_ROWS, D), lambda b, s: (b, s, 0)),
            pl.BlockSpec((1, D), const2),
            pl.BlockSpec((D, D), const2),
            pl.BlockSpec((1, N_MEM, D), lambda b, s: (b, 0, 0)),
            pl.BlockSpec((1, N_MEM, D), lambda b, s: (b, 0, 0)),
            pl.BlockSpec((D, D), const2),
            pl.BlockSpec((1, D), const2),
            pl.BlockSpec((ROUTER_ROWS, D), const2),
            pl.BlockSpec((ROUTER_ROWS, D), const2),
        ],
        out_specs=[
            pl.BlockSpec((1, ATT_ROWS, D), lambda b, s: (b, s, 0)),
            pl.BlockSpec((ATT_ROWS, D // 2), lambda b, s: (b * n_s + s, 0)),
            pl.BlockSpec((ROUTER_ROWS, ATT_ROWS), lambda b, s: (0, b * n_s + s)),
        ],
        out_shape=[
            jax.ShapeDtypeStruct((B, S, D), F32),
            jax.ShapeDtypeStruct((B * S, D // 2), jnp.uint32),
            jax.ShapeDtypeStruct((ROUTER_ROWS, B * S), F32),
        ],
        scratch_shapes=[pltpu.VMEM((ATT_ROWS, D), BF16)],
        compiler_params=pltpu.CompilerParams(
            dimension_semantics=("arbitrary", "arbitrary"), vmem_limit_bytes=VMEM_LIMIT),
        name="xattn",
    )(h1, norm_x, w_q, k_mem, v_mem, w_o, norm_ffn, wr_hi, wr_lo)


def _route_kernel(bias_ref, lg_ref, ids_ref, gates_ref, rank_ref, cnt_ref, base_ref):
    sub = lg_ref.shape[1]

    @pl.when(pl.program_id(0) == 0)
    def _():
        base_ref[...] = jnp.zeros_like(base_ref)

    best = lg_ref[0] + bias_ref[0]
    gl = [best]
    sel = jnp.zeros(best.shape, jnp.int32)
    for g in range(1, N_GROUPS):
        cur = lg_ref[g] + bias_ref[g]
        gl.append(cur)
        better = cur > best
        best = jnp.where(better, cur, best)
        sel = jnp.where(better, g, sel)
    denom = jnp.exp(gl[0] - best)
    for g in range(1, N_GROUPS):
        denom = denom + jnp.exp(gl[g] - best)
    g_gate = 1.0 / denom

    ev = []
    for j in range(EXPERTS_PER_GROUP):
        val = lg_ref[N_GROUPS + j] + bias_ref[N_GROUPS + j]
        for g in range(1, N_GROUPS):
            e = g * EXPERTS_PER_GROUP + j
            val = jnp.where(sel == g, lg_ref[N_GROUPS + e] + bias_ref[N_GROUPS + e], val)
        ev.append(val)
    v1, i1 = ev[0], jnp.zeros(best.shape, jnp.int32)
    for j in range(1, EXPERTS_PER_GROUP):
        better = ev[j] > v1
        v1 = jnp.where(better, ev[j], v1)
        i1 = jnp.where(better, j, i1)
    rest = [jnp.where(i1 == j, -jnp.inf, ev[j]) for j in range(EXPERTS_PER_GROUP)]
    v2, i2 = rest[0], jnp.zeros(best.shape, jnp.int32)
    for j in range(1, EXPERTS_PER_GROUP):
        better = rest[j] > v2
        v2 = jnp.where(better, rest[j], v2)
        i2 = jnp.where(better, j, i2)
    e2 = jnp.exp(v2 - v1)
    inv = 1.0 / (1.0 + e2)
    id1 = sel * EXPERTS_PER_GROUP + i1
    id2 = sel * EXPERTS_PER_GROUP + i2
    ids_ref[0] = id1
    ids_ref[1] = id2
    gates_ref[0] = inv * g_gate
    gates_ref[1] = e2 * inv * g_gate

    member = jnp.concatenate(
        [jnp.where((id1 == e) | (id2 == e), 1.0, 0.0) for e in range(N_EXPERTS)], axis=0).astype(BF16)
    n = N_EXPERTS * sub
    li = lax.broadcasted_iota(jnp.int32, (LANES, LANES), 0)
    lj = lax.broadcasted_iota(jnp.int32, (LANES, LANES), 1)
    before_lane = jnp.where(li < lj, 1.0, 0.0).astype(BF16)
    ones = jnp.ones((LANES, LANES), BF16)
    ri = lax.broadcasted_iota(jnp.int32, (n, n), 0)
    rj = lax.broadcasted_iota(jnp.int32, (n, n), 1)
    same = _block_id(ri, sub) == _block_id(rj, sub)
    before_row = jnp.where(same & (rj < ri), 1.0, 0.0).astype(BF16)
    all_row = jnp.where(same, 1.0, 0.0).astype(BF16)
    in_row = _dot(member, before_lane)
    prev_rows = _dot(_dot(before_row, member).astype(BF16), ones)
    total = _dot(_dot(all_row, member).astype(BF16), ones)
    base = base_ref[...]
    pos = base + prev_rows + in_row
    r1 = jnp.zeros(best.shape, F32)
    r2 = jnp.zeros(best.shape, F32)
    for e in range(N_EXPERTS):
        pe = pos[e * sub:(e + 1) * sub]
        r1 = jnp.where(id1 == e, pe, r1)
        r2 = jnp.where(id2 == e, pe, r2)
    rank_ref[0] = r1.astype(jnp.int32)
    rank_ref[1] = r2.astype(jnp.int32)
    base_ref[...] = base + total
    cnt_ref[...] = base + total


def _route(bias, logits3):
    rp, n_sub, _ = logits3.shape
    blk = lambda i: (0, i, 0)
    pair_i = jax.ShapeDtypeStruct((2, n_sub, LANES), jnp.int32)
    return pl.pallas_call(
        _route_kernel,
        grid=(n_sub // ROUTE_SUB,),
        in_specs=[
            pl.BlockSpec(memory_space=pltpu.SMEM),
            pl.BlockSpec((rp, ROUTE_SUB, LANES), blk),
        ],
        out_specs=[
            pl.BlockSpec((2, ROUTE_SUB, LANES), blk),
            pl.BlockSpec((2, ROUTE_SUB, LANES), blk),
            pl.BlockSpec((2, ROUTE_SUB, LANES), blk),
            pl.BlockSpec((N_EXPERTS * ROUTE_SUB, LANES), lambda i: (0, 0)),
        ],
        out_shape=[
            pair_i,
            jax.ShapeDtypeStruct((2, n_sub, LANES), F32),
            pair_i,
            jax.ShapeDtypeStruct((N_EXPERTS * ROUTE_SUB, LANES), F32),
        ],
        scratch_shapes=[pltpu.VMEM((N_EXPERTS * ROUTE_SUB, LANES), F32)],
        compiler_params=pltpu.CompilerParams(
            dimension_semantics=("arbitrary",), vmem_limit_bytes=VMEM_LIMIT),
        name="route",
    )(bias, logits3)


def _dest_kernel(start_ref, ids_ref, rank_ref, dest_ref):
    ids = ids_ref[...]
    off = jnp.zeros(ids.shape, jnp.int32)
    for e in range(N_EXPERTS):
        off = jnp.where(ids == e, start_ref[e], off)
    dest_ref[...] = rank_ref[...] + off


def _dest(seg_start, ids, rank):
    _, n_sub, _ = ids.shape
    blk = pl.BlockSpec((2, ROUTE_SUB, LANES), lambda i: (0, i, 0))
    return pl.pallas_call(
        _dest_kernel,
        grid=(n_sub // ROUTE_SUB,),
        in_specs=[pl.BlockSpec(memory_space=pltpu.SMEM), blk, blk],
        out_specs=blk,
        out_shape=jax.ShapeDtypeStruct(ids.shape, jnp.int32),
        name="dest",
    )(seg_start, ids, rank)


def _sc_mesh():
    return plsc.VectorSubcoreMesh(core_axis_name="core", subcore_axis_name="subcore")


def _sc_worker(rows_total):
    rows = rows_total // _SC_WORKERS
    wid = lax.axis_index("core") * SC_SUBCORES + lax.axis_index("subcore")
    return wid * rows, rows


def _dispatch(dest0, dest1, xn, n_slots):
    T, D = xn.shape
    W = SC_WINDOW
    rows = T // _SC_WORKERS

    @pl.kernel(out_type=jax.ShapeDtypeStruct((n_slots, D), xn.dtype), mesh=_sc_mesh(),
               scratch_types=[pltpu.VMEM((rows,), jnp.int32), pltpu.VMEM((rows,), jnp.int32),
                              pltpu.VMEM((2, W, D), xn.dtype),
                              pltpu.SemaphoreType.DMA((2,)), pltpu.SemaphoreType.DMA((2,))])
    def scatter_rows(x_hbm, d0_hbm, d1_hbm, xs_hbm, d0_v, d1_v, buf, in_sem, out_sem):
        base, _ = _sc_worker(T)
        pltpu.sync_copy(d0_hbm.at[pl.ds(base, rows)], d0_v)
        pltpu.sync_copy(d1_hbm.at[pl.ds(base, rows)], d1_v)

        def load(w, slot):
            return pltpu.make_async_copy(x_hbm.at[pl.ds(base + w * W, W)], buf.at[slot], in_sem.at[slot])

        def store(w, slot, d_v):
            return pltpu.make_async_copy(buf.at[slot], xs_hbm.at[d_v.at[pl.ds(w * W, W)]], out_sem.at[slot])

        def step(w, slot):
            load(w, slot).wait()
            store(w, slot, d0_v).start()
            store(w, slot, d1_v).start()
            store(w, slot, d0_v).wait()
            store(w, slot, d1_v).wait()

        n = rows // W
        load(0, 0).start()

        @pl.loop(0, n, step=2)
        def _(w):
            load(w + 1, 1).start()
            step(w, 0)

            @pl.when(w + 2 < n)
            def _():
                load(w + 2, 0).start()

            step(w + 1, 1)

    return scatter_rows(xn, dest0, dest1)


def _gather_rows(src, idx):
    M = idx.shape[0]
    D = src.shape[1]
    W = SC_WINDOW
    rows = M // _SC_WORKERS

    @pl.kernel(out_type=jax.ShapeDtypeStruct((M, D), src.dtype), mesh=_sc_mesh(),
               scratch_types=[pltpu.VMEM((rows,), jnp.int32), pltpu.VMEM((2, W, D), src.dtype),
                              pltpu.SemaphoreType.DMA((2,)), pltpu.SemaphoreType.DMA((2,))])
    def gather_rows(src_hbm, i_hbm, o_hbm, i_v, buf, in_sem, out_sem):
        base, _ = _sc_worker(M)
        pltpu.sync_copy(i_hbm.at[pl.ds(base, rows)], i_v)

        def load(w, slot):
            return pltpu.make_async_copy(src_hbm.at[i_v.at[pl.ds(w * W, W)]], buf.at[slot], in_sem.at[slot])

        def store(w, slot):
            return pltpu.make_async_copy(buf.at[slot], o_hbm.at[pl.ds(base + w * W, W)], out_sem.at[slot])

        n = rows // W
        load(0, 0).start()

        @pl.loop(0, n, step=2)
        def _(w):
            @pl.when(w > 0)
            def _():
                store(w - 1, 1).wait()

            load(w + 1, 1).start()
            load(w, 0).wait()
            store(w, 0).start()
            store(w, 0).wait()

            @pl.when(w + 2 < n)
            def _():
                load(w + 2, 0).start()

            load(w + 1, 1).wait()
            store(w + 1, 1).start()

        store(n - 1, 1).wait()

    return gather_rows(src, idx)


def _expert_kernel(be_ref, nb_ref, x_ref, wg_ref, wu_ref, wd_ref, y_ref):
    used = pl.program_id(0) < nb_ref[0]

    @pl.when(used)
    def _():
        for c in range(EXPERT_ROWS // EXPERT_CHAIN_ROWS):
            rs = slice(c * EXPERT_CHAIN_ROWS, (c + 1) * EXPERT_CHAIN_ROWS)
            x = _unpack_pairs(x_ref[rs]).astype(BF16)
            g = _dot(x, wg_ref[0])
            u = _dot(x, wu_ref[0])
            hid = (g * _sigmoid(g) * u).astype(BF16)
            y_ref[rs] = _pack_pairs(_dot(hid, wd_ref[0]))

    @pl.when(jnp.logical_not(used))
    def _():
        y_ref[...] = jnp.zeros_like(y_ref)


def _experts(block_expert, n_used, xs, w_gate, w_up, w_down):
    n_slots, half = xs.shape
    D = 2 * half
    n_blocks = n_slots // EXPERT_ROWS
    row_blk = lambda i, be, nb: (jnp.minimum(i, nb[0] - 1), 0)
    w_blk = lambda i, be, nb: (be[i], 0, 0)
    grid_spec = pltpu.PrefetchScalarGridSpec(
        num_scalar_prefetch=2,
        grid=(n_blocks,),
        in_specs=[
            pl.BlockSpec((EXPERT_ROWS, half), row_blk),
            pl.BlockSpec((1, D, D_EXPERT), w_blk),
            pl.BlockSpec((1, D, D_EXPERT), w_blk),
            pl.BlockSpec((1, D_EXPERT, D), w_blk),
        ],
        out_specs=pl.BlockSpec((EXPERT_ROWS, half), lambda i, be, nb: (i, 0)),
    )
    return pl.pallas_call(
        _expert_kernel,
        grid_spec=grid_spec,
        out_shape=jax.ShapeDtypeStruct((n_slots, half), jnp.uint32),
        compiler_params=pltpu.CompilerParams(
            dimension_semantics=("arbitrary",), vmem_limit_bytes=VMEM_LIMIT),
        name="experts",
    )(block_expert, n_used, xs, w_gate, w_up, w_down)


def _combine_kernel(y0_ref, y1_ref, h_ref, gates_ref, nfin_ref, o_ref):
    gates = gates_ref[...]
    h = (h_ref[...] + gates[:, 0:1] * _unpack_pairs(y0_ref[...])
         + gates[:, 1:2] * _unpack_pairs(y1_ref[...]))
    o_ref[...] = _rms(h, nfin_ref[...])


def _combine(y01, h2, gates_t, norm_final):
    T, D = h2.shape
    n_t = T // MOVE_ROWS
    return pl.pallas_call(
        _combine_kernel,
        grid=(n_t,),
        in_specs=[
            pl.BlockSpec((MOVE_ROWS, D // 2), lambda i: (i, 0)),
            pl.BlockSpec((MOVE_ROWS, D // 2), lambda i: (i + n_t, 0)),
            pl.BlockSpec((MOVE_ROWS, D), lambda i: (i, 0)),
            pl.BlockSpec((MOVE_ROWS, 2), lambda i: (i, 0)),
            pl.BlockSpec((1, D), lambda i: (0, 0)),
        ],
        out_specs=pl.BlockSpec((MOVE_ROWS, D), lambda i: (i, 0)),
        out_shape=jax.ShapeDtypeStruct((T, D), F32),
        compiler_params=pltpu.CompilerParams(
            dimension_semantics=("arbitrary",), vmem_limit_bytes=VMEM_LIMIT),
        name="combine",
    )(y01, y01, h2, gates_t, norm_final)


def kernel(x, mem, norm_mix, w_in, gmlp_ln, gmlp_w_spatial, gmlp_b_spatial, gmlp_beta, hgrn_lb_logits, hgrn_out_gain, w_out, norm_xattn, norm_mem, w_xq, w_xkv, w_xo, norm_ffn, w_router_group, b_router_group, w_router_expert, b_router_expert, w_expert_gate, w_expert_up, w_expert_down, norm_final):
    B, S, D = x.shape
    T = B * S
    depth = w_in.shape[0]
    assert depth == 1 and hgrn_lb_logits.shape[0] == 2
    l = 0
    row = lambda p: p.reshape(1, -1)

    h1 = _mixer(x, row(norm_mix[l]), w_in[l].astype(BF16), row(gmlp_ln[l]), gmlp_w_spatial[l],
                gmlp_b_spatial[l].T, row(gmlp_beta[l]), hgrn_lb_logits, row(hgrn_out_gain[l]),
                w_out[l].astype(BF16))
    k_mem, v_mem = _memkv(mem, row(norm_mem[l]), w_xkv[l].astype(BF16))

    w_router = jnp.concatenate([w_router_group[l].T, w_router_expert[l].T], axis=0)
    w_router = jnp.pad(w_router, ((0, ROUTER_ROWS - w_router.shape[0]), (0, 0)))
    wr_hi = w_router.astype(BF16)
    wr_lo = (w_router - wr_hi.astype(F32)).astype(BF16)
    h2, xn, logits = _xattn(h1, row(norm_xattn[l]), w_xq[l].astype(BF16), k_mem, v_mem,
                            w_xo[l].astype(BF16), row(norm_ffn[l]), wr_hi, wr_lo)

    bias = jnp.concatenate([b_router_group[l], b_router_expert[l]]).astype(F32)
    ids, gates, rank, counts = _route(bias, logits.reshape(ROUTER_ROWS, T // LANES, LANES))

    counts = counts[::ROUTE_SUB, 0].astype(jnp.int32)
    padded = (counts + EXPERT_ROWS - 1) // EXPERT_ROWS * EXPERT_ROWS
    seg_end = jnp.cumsum(padded)
    seg_start = seg_end - padded
    n_blocks = (2 * T) // EXPERT_ROWS + N_EXPERTS
    block_first_row = jnp.arange(n_blocks, dtype=jnp.int32) * EXPERT_ROWS
    block_expert = jnp.minimum(
        jnp.sum(block_first_row[:, None] >= seg_end[None, :], axis=1), N_EXPERTS - 1).astype(jnp.int32)
    n_used = (seg_end[-1:] // EXPERT_ROWS).astype(jnp.int32)

    dest = _dest(seg_start, ids, rank).reshape(2, T)

    xs = _dispatch(dest[0], dest[1], xn, n_blocks * EXPERT_ROWS)
    yb = _experts(block_expert, n_used, xs, w_expert_gate[l].astype(BF16), w_expert_up[l].astype(BF16),
                  w_expert_down[l].astype(BF16))
    gates_t = gates.reshape(2, T).T
    y01 = _gather_rows(yb, dest.reshape(2 * T))
    out = _combine(y01, h2.reshape(T, D), gates_t, row(norm_final))
    return out.reshape(B, S, D)
```

```python
import functools

import jax
import jax.numpy as jnp
from jax import lax
from jax.experimental import pallas as pl
from jax.experimental.pallas import tpu as pltpu
from jax.experimental.pallas import tpu_sc as plsc

F32 = jnp.float32
BF16 = jnp.bfloat16
EPS = 1e-6

D_MODEL = 1024
D_GMLP = 512
GMLP_GROUPS = 4
GMLP_CHUNK = 128
D_HGRN = 512
HGRN_HEADS = 4
HGRN_DK = 128
HGRN_CHUNK = 64
D_IN_PROJ = 2 * D_GMLP + 4 * D_HGRN
N_MEM = 256
XATTN_HEADS = 4
XATTN_HEAD_DIM = D_MODEL // XATTN_HEADS
N_GROUPS = 4
EXPERTS_PER_GROUP = 8
N_EXPERTS = N_GROUPS * EXPERTS_PER_GROUP
D_EXPERT = 512

LANES = 128
MIX_ROWS = 512
MIX_CHAIN_ROWS = 256
ATT_ROWS = 1024
ATT_CHAIN_ROWS = 1024
WEIGHT_STAGE_ROWS = 128
ROUTER_ROWS = 40
ROUTE_SUB = 16
EXPERT_ROWS = 512
EXPERT_CHAIN_ROWS = 256
MOVE_ROWS = 256
SC_WINDOW = 64
SC_CORES = 2
SC_SUBCORES = 16
_SC_WORKERS = SC_CORES * SC_SUBCORES
VMEM_LIMIT = 48 * 1024 * 1024


def _rms(x, gain):
    return x * lax.rsqrt(jnp.mean(x * x, axis=-1, keepdims=True) + EPS) * gain


def _dot(a, b):
    return jnp.dot(a, b, preferred_element_type=F32)


def _dot_nt(a, b):
    return lax.dot_general(a, b, (((1,), (1,)), ((), ())), preferred_element_type=F32)


def _dot_tn(a, b):
    return lax.dot_general(a, b, (((0,), (0,)), ((), ())), preferred_element_type=F32)


def _gelu(x):
    return 0.5 * x * (1.0 + jnp.tanh(0.7978845608028654 * (x + 0.044715 * (x * x * x))))


def _sigmoid(x):
    return 1.0 / (1.0 + jnp.exp(-x))


def _block_id(idx, size):
    assert size & (size - 1) == 0
    return lax.shift_right_logical(idx, size.bit_length() - 1)


def _stage_weight(w_hbm, w_bf16, stage_ref, sem):
    rows = stage_ref.shape[0]
    for k in range(w_hbm.shape[0] // rows):
        copy = pltpu.make_async_copy(w_hbm.at[pl.ds(k * rows, rows)], stage_ref, sem)
        copy.start()
        copy.wait()
        w_bf16[k * rows:(k + 1) * rows, :] = stage_ref[...].astype(BF16)


def _weight_scratch(k, n):
    return [pltpu.VMEM((k, n), BF16), pltpu.VMEM((WEIGHT_STAGE_ROWS, n), F32)]


_HIGH_HALF = 0xFFFF0000


def _pack_pairs(x):
    c = x.shape[1] // 2
    bits = lax.bitcast_convert_type(x.astype(BF16).astype(F32), jnp.uint32)
    return (bits[:, c:] & jnp.uint32(_HIGH_HALF)) | lax.shift_right_logical(bits[:, :c], jnp.uint32(16))


def _unpack_pairs(w):
    lo = lax.bitcast_convert_type(lax.shift_left(w, jnp.uint32(16)), F32)
    hi = lax.bitcast_convert_type(w & jnp.uint32(_HIGH_HALF), F32)
    return jnp.concatenate([lo, hi], axis=1)


def _mixer_kernel(x_ref, nmix_ref, win_hbm, gln_ref, ws_ref, bst_ref, beta_ref, lbl_ref, og_ref,
                  wout_hbm, o_ref, proj_ref, ycat_ref, state_ref, win_ref, win_stage, wout_ref, wout_stage,
                  sem):
    @pl.when((pl.program_id(0) == 0) & (pl.program_id(1) == 0))
    def _():
        _stage_weight(win_hbm, win_ref, win_stage, sem)
        _stage_weight(wout_hbm, wout_ref, wout_stage, sem)

    @pl.when(pl.program_id(1) == 0)
    def _():
        state_ref[...] = jnp.zeros_like(state_ref)

    n = MIX_CHAIN_ROWS
    r_i = lax.broadcasted_iota(jnp.int32, (GMLP_CHUNK, GMLP_CHUNK), 0)
    c_i = lax.broadcasted_iota(jnp.int32, (GMLP_CHUNK, GMLP_CHUNK), 1)
    causal = c_i <= r_i
    w_tril = [jnp.where(causal, ws_ref[g], 0.0).astype(BF16) for g in range(GMLP_GROUPS)]
    lbl = lbl_ref[...]
    e_lb = jnp.exp(lbl - jnp.max(lbl, axis=0, keepdims=True))
    lb = e_lb[0:1] / jnp.sum(e_lb, axis=0, keepdims=True)
    rr = lax.broadcasted_iota(jnp.int32, (n, n), 0)
    cc = lax.broadcasted_iota(jnp.int32, (n, n), 1)
    tri = jnp.where((_block_id(rr, HGRN_CHUNK) == _block_id(cc, HGRN_CHUNK)) & (cc <= rr),
                    1.0, 0.0).astype(BF16)
    r64 = lax.broadcasted_iota(jnp.int32, (HGRN_CHUNK, HGRN_CHUNK), 0)
    c64 = lax.broadcasted_iota(jnp.int32, (HGRN_CHUNK, HGRN_CHUNK), 1)
    causal64 = c64 <= r64
    base = 2 * D_GMLP

    for chain in range(x_ref.shape[1] // n):
        r0 = chain * n
        x = x_ref[0, r0:r0 + n]
        a = _rms(x, nmix_ref[...]).astype(BF16)
        proj_ref[r0:r0 + n] = _dot(a, win_ref[...])

        u = _gelu(proj_ref[r0:r0 + n, 0:D_GMLP])
        v = _gelu(proj_ref[r0:r0 + n, D_GMLP:2 * D_GMLP])
        vc = v - jnp.mean(v, axis=-1, keepdims=True)
        vn = (vc * lax.rsqrt(jnp.mean(vc * vc, axis=-1, keepdims=True) + EPS) * gln_ref[...]).astype(BF16)
        z_rows = []
        for c in range(n // GMLP_CHUNK):
            z_cols = []
            for g in range(GMLP_GROUPS):
                vg = vn[c * GMLP_CHUNK:(c + 1) * GMLP_CHUNK, g * LANES:(g + 1) * LANES]
                z_cols.append(_dot(w_tril[g], vg) + bst_ref[:, g:g + 1])
            z_rows.append(jnp.concatenate(z_cols, axis=1))
        z = jnp.concatenate(z_rows, axis=0)
        ycat_ref[r0:r0 + n, 0:D_GMLP] = _rms(u * z, beta_ref[...]).astype(BF16)

        f = lb + (1.0 - lb) * _sigmoid(proj_ref[r0:r0 + n, base + D_HGRN:base + 2 * D_HGRN])
        log_f = jnp.log(f)
        lf_hi = log_f.astype(BF16)
        lf_lo = (log_f - lf_hi.astype(F32)).astype(BF16)
        b_all = _dot(tri, lf_hi) + _dot(tri, lf_lo)
        for c in range(n // HGRN_CHUNK):
            rs = slice(c * HGRN_CHUNK, (c + 1) * HGRN_CHUNK)
            ps = slice(r0 + c * HGRN_CHUNK, r0 + (c + 1) * HGRN_CHUNK)
            bc = b_all[rs]
            bl = bc[HGRN_CHUNK - 1:HGRN_CHUNK]
            q_c = proj_ref[ps, base:base + D_HGRN]
            q_c = q_c * _sigmoid(q_c)
            k_c = 1.0 - f[rs]
            qd = (q_c * jnp.exp(bc)).astype(BF16)
            ki = (k_c * jnp.exp(-bc)).astype(BF16)
            kte = (k_c * jnp.exp(bl - bc)).astype(BF16)
            decay = jnp.exp(bl)
            v_c = proj_ref[ps, base + 2 * D_HGRN:base + 3 * D_HGRN].astype(BF16)
            g_c = proj_ref[ps, base + 3 * D_HGRN:base + 4 * D_HGRN]
            gate = og_ref[...] * (g_c * _sigmoid(g_c))
            for h in range(HGRN_HEADS):
                cs = slice(h * HGRN_DK, (h + 1) * HGRN_DK)
                scores = jnp.where(causal64, _dot_nt(qd[:, cs], ki[:, cs]), 0.0).astype(BF16)
                st = state_ref[h]
                o = _dot(scores, v_c[:, cs]) + _dot_nt(qd[:, cs], st.astype(BF16))
                state_ref[h] = st * decay[:, cs] + _dot_tn(v_c[:, cs], kte[:, cs])
                o = o * lax.rsqrt(jnp.mean(o * o, axis=-1, keepdims=True) + EPS)
                ycat_ref[ps, D_GMLP + h * HGRN_DK:D_GMLP + (h + 1) * HGRN_DK] = (o * gate[:, cs]).astype(BF16)

        o_ref[0, r0:r0 + n] = x + _dot(ycat_ref[r0:r0 + n], wout_ref[...])


def _mixer(x, norm_mix, w_in, gmlp_ln, w_s, b_s_t, beta, lb_logits, out_gain, w_out):
    B, S, D = x.shape
    const2 = lambda b, s: (0, 0)
    return pl.pallas_call(
        _mixer_kernel,
        grid=(B, S // MIX_ROWS),
        in_specs=[
            pl.BlockSpec((1, MIX_ROWS, D), lambda b, s: (b, s, 0)),
            pl.BlockSpec((1, D), const2),
            pl.BlockSpec(memory_space=pl.ANY),
            pl.BlockSpec((1, D_GMLP), const2),
            pl.BlockSpec((GMLP_GROUPS, GMLP_CHUNK, GMLP_CHUNK), lambda b, s: (0, 0, 0)),
            pl.BlockSpec((GMLP_CHUNK, GMLP_GROUPS), const2),
            pl.BlockSpec((1, D_GMLP), const2),
            pl.BlockSpec(lb_logits.shape, const2),
            pl.BlockSpec((1, D_HGRN), const2),
            pl.BlockSpec(memory_space=pl.ANY),
        ],
        out_specs=pl.BlockSpec((1, MIX_ROWS, D), lambda b, s: (b, s, 0)),
        out_shape=jax.ShapeDtypeStruct((B, S, D), F32),
        scratch_shapes=[
            pltpu.VMEM((MIX_ROWS, D_IN_PROJ), F32),
            pltpu.VMEM((MIX_ROWS, D), BF16),
            pltpu.VMEM((HGRN_HEADS, HGRN_DK, HGRN_DK), F32),
        ] + _weight_scratch(D, D_IN_PROJ) + _weight_scratch(D, D) + [pltpu.SemaphoreType.DMA],
        compiler_params=pltpu.CompilerParams(
            dimension_semantics=("arbitrary", "arbitrary"), vmem_limit_bytes=VMEM_LIMIT),
        name="mixer",
    )(x, norm_mix, w_in, gmlp_ln, w_s, b_s_t, beta, lb_logits, out_gain, w_out)


def _memkv_kernel(mem_ref, nm_ref, wkv_hbm, k_ref, v_ref, wkv_ref, wkv_stage, sem):
    @pl.when(pl.program_id(0) == 0)
    def _():
        _stage_weight(wkv_hbm, wkv_ref, wkv_stage, sem)

    m = _rms(mem_ref[0], nm_ref[...]).astype(BF16)
    kv = _dot(m, wkv_ref[...])
    k_ref[0] = kv[:, :D_MODEL].astype(BF16)
    v_ref[0] = kv[:, D_MODEL:].astype(BF16)


def _memkv(mem, norm_mem, w_kv):
    B, M, D = mem.shape
    out = jax.ShapeDtypeStruct((B, M, D), BF16)
    return pl.pallas_call(
        _memkv_kernel,
        grid=(B,),
        in_specs=[
            pl.BlockSpec((1, M, D), lambda b: (b, 0, 0)),
            pl.BlockSpec((1, D), lambda b: (0, 0)),
            pl.BlockSpec(memory_space=pl.ANY),
        ],
        out_specs=[pl.BlockSpec((1, M, D), lambda b: (b, 0, 0))] * 2,
        out_shape=[out, out],
        scratch_shapes=_weight_scratch(D, 2 * D) + [pltpu.SemaphoreType.DMA],
        compiler_params=pltpu.CompilerParams(
            dimension_semantics=("arbitrary",), vmem_limit_bytes=VMEM_LIMIT),
        name="memkv",
    )(mem, norm_mem, w_kv)


def _xattn_kernel(h_ref, nx_ref, wq_hbm, k_ref, v_ref, wo_hbm, nf_ref, wr_ref,
                  h2_ref, xn_ref, lg_ref, att_ref, wq_ref, wq_stage, wo_ref, wo_stage, sem):
    @pl.when((pl.program_id(0) == 0) & (pl.program_id(1) == 0))
    def _():
        _stage_weight(wq_hbm, wq_ref, wq_stage, sem)
        _stage_weight(wo_hbm, wo_ref, wo_stage, sem)

    for c in range(ATT_ROWS // ATT_CHAIN_ROWS):
        rs = slice(c * ATT_CHAIN_ROWS, (c + 1) * ATT_CHAIN_ROWS)
        h = h_ref[0, rs]
        hn = _rms(h, nx_ref[...]).astype(BF16)
        q = (_dot(hn, wq_ref[...]) * (XATTN_HEAD_DIM ** -0.5)).astype(BF16)
        for hd in range(XATTN_HEADS):
            cs = slice(hd * XATTN_HEAD_DIM, (hd + 1) * XATTN_HEAD_DIM)
            s = _dot_nt(q[:, cs], k_ref[0, :, cs])
            p = jnp.exp(s - jnp.max(s, axis=-1, keepdims=True))
            p = (p / jnp.sum(p, axis=-1, keepdims=True)).astype(BF16)
            att_ref[rs, cs] = _dot(p, v_ref[0, :, cs]).astype(BF16)
        h2 = h + _dot(att_ref[rs], wo_ref[...])
        h2_ref[0, rs] = h2
        xn = _rms(h2, nf_ref[...])
        xn_ref[rs] = _pack_pairs(xn)
        lg_ref[:, rs] = _dot_nt(wr_ref[...], xn.astype(BF16))


def _xattn(h1, norm_x, w_q, k_mem, v_mem, w_o, norm_ffn, w_router):
    B, S, D = h1.shape
    n_s = S // ATT_ROWS
    const2 = lambda b, s: (0, 0)
    return pl.pallas_call(
        _xattn_kernel,
        grid=(B, n_s),
        in_specs=[
            pl.BlockSpec((1, ATT_ROWS, D), lambda b, s: (b, s, 0)),
            pl.BlockSpec((1, D), const2),
            pl.BlockSpec(memory_space=pl.ANY),
            pl.BlockSpec((1, N_MEM, D), lambda b, s: (b, 0, 0)),
            pl.BlockSpec((1, N_MEM, D), lambda b, s: (b, 0, 0)),
            pl.BlockSpec(memory_space=pl.ANY),
            pl.BlockSpec((1, D), const2),
            pl.BlockSpec((ROUTER_ROWS, D), const2),
        ],
        out_specs=[
            pl.BlockSpec((1, ATT_ROWS, D), lambda b, s: (b, s, 0)),
            pl.BlockSpec((ATT_ROWS, D // 2), lambda b, s: (b * n_s + s, 0)),
            pl.BlockSpec((ROUTER_ROWS, ATT_ROWS), lambda b, s: (0, b * n_s + s)),
        ],
        out_shape=[
            jax.ShapeDtypeStruct((B, S, D), F32),
            jax.ShapeDtypeStruct((B * S, D // 2), jnp.uint32),
            jax.ShapeDtypeStruct((ROUTER_ROWS, B * S), F32),
        ],
        scratch_shapes=([pltpu.VMEM((ATT_ROWS, D), BF16)] + _weight_scratch(D, D) + _weight_scratch(D, D)
                        + [pltpu.SemaphoreType.DMA]),
        compiler_params=pltpu.CompilerParams(
            dimension_semantics=("arbitrary", "arbitrary"), vmem_limit_bytes=VMEM_LIMIT),
        name="xattn",
    )(h1, norm_x, w_q, k_mem, v_mem, w_o, norm_ffn, w_router)


def _route_kernel(bias_ref, lg_ref, ids_ref, gates_ref, rank_ref, cnt_ref, base_ref):
    sub = lg_ref.shape[1]

    @pl.when(pl.program_id(0) == 0)
    def _():
        base_ref[...] = jnp.zeros_like(base_ref)

    best = lg_ref[0] + bias_ref[0]
    gl = [best]
    sel = jnp.zeros(best.shape, jnp.int32)
    for g in range(1, N_GROUPS):
        cur = lg_ref[g] + bias_ref[g]
        gl.append(cur)
        better = cur > best
        best = jnp.where(better, cur, best)
        sel = jnp.where(better, g, sel)
    denom = jnp.exp(gl[0] - best)
    for g in range(1, N_GROUPS):
        denom = denom + jnp.exp(gl[g] - best)
    g_gate = 1.0 / denom

    ev = []
    for j in range(EXPERTS_PER_GROUP):
        val = lg_ref[N_GROUPS + j] + bias_ref[N_GROUPS + j]
        for g in range(1, N_GROUPS):
            e = g * EXPERTS_PER_GROUP + j
            val = jnp.where(sel == g, lg_ref[N_GROUPS + e] + bias_ref[N_GROUPS + e], val)
        ev.append(val)
    v1, i1 = ev[0], jnp.zeros(best.shape, jnp.int32)
    for j in range(1, EXPERTS_PER_GROUP):
        better = ev[j] > v1
        v1 = jnp.where(better, ev[j], v1)
        i1 = jnp.where(better, j, i1)
    rest = [jnp.where(i1 == j, -jnp.inf, ev[j]) for j in range(EXPERTS_PER_GROUP)]
    v2, i2 = rest[0], jnp.zeros(best.shape, jnp.int32)
    for j in range(1, EXPERTS_PER_GROUP):
        better = rest[j] > v2
        v2 = jnp.where(better, rest[j], v2)
        i2 = jnp.where(better, j, i2)
    e2 = jnp.exp(v2 - v1)
    inv = 1.0 / (1.0 + e2)
    id1 = sel * EXPERTS_PER_GROUP + i1
    id2 = sel * EXPERTS_PER_GROUP + i2
    ids_ref[0] = id1
    ids_ref[1] = id2
    gates_ref[0] = inv * g_gate
    gates_ref[1] = e2 * inv * g_gate

    member = jnp.concatenate(
        [jnp.where((id1 == e) | (id2 == e), 1.0, 0.0) for e in range(N_EXPERTS)], axis=0).astype(BF16)
    n = N_EXPERTS * sub
    li = lax.broadcasted_iota(jnp.int32, (LANES, LANES), 0)
    lj = lax.broadcasted_iota(jnp.int32, (LANES, LANES), 1)
    before_lane = jnp.where(li < lj, 1.0, 0.0).astype(BF16)
    ones = jnp.ones((LANES, LANES), BF16)
    ri = lax.broadcasted_iota(jnp.int32, (n, n), 0)
    rj = lax.broadcasted_iota(jnp.int32, (n, n), 1)
    same = _block_id(ri, sub) == _block_id(rj, sub)
    before_row = jnp.where(same & (rj < ri), 1.0, 0.0).astype(BF16)
    all_row = jnp.where(same, 1.0, 0.0).astype(BF16)
    in_row = _dot(member, before_lane)
    prev_rows = _dot(_dot(before_row, member).astype(BF16), ones)
    total = _dot(_dot(all_row, member).astype(BF16), ones)
    base = base_ref[...]
    pos = base + prev_rows + in_row
    r1 = jnp.zeros(best.shape, F32)
    r2 = jnp.zeros(best.shape, F32)
    for e in range(N_EXPERTS):
        pe = pos[e * sub:(e + 1) * sub]
        r1 = jnp.where(id1 == e, pe, r1)
        r2 = jnp.where(id2 == e, pe, r2)
    rank_ref[0] = r1.astype(jnp.int32)
    rank_ref[1] = r2.astype(jnp.int32)
    base_ref[...] = base + total
    cnt_ref[...] = base + total


def _route(bias, logits3):
    rp, n_sub, _ = logits3.shape
    blk = lambda i: (0, i, 0)
    pair_i = jax.ShapeDtypeStruct((2, n_sub, LANES), jnp.int32)
    return pl.pallas_call(
        _route_kernel,
        grid=(n_sub // ROUTE_SUB,),
        in_specs=[
            pl.BlockSpec(memory_space=pltpu.SMEM),
            pl.BlockSpec((rp, ROUTE_SUB, LANES), blk),
        ],
        out_specs=[
            pl.BlockSpec((2, ROUTE_SUB, LANES), blk),
            pl.BlockSpec((2, ROUTE_SUB, LANES), blk),
            pl.BlockSpec((2, ROUTE_SUB, LANES), blk),
            pl.BlockSpec((N_EXPERTS * ROUTE_SUB, LANES), lambda i: (0, 0)),
        ],
        out_shape=[
            pair_i,
            jax.ShapeDtypeStruct((2, n_sub, LANES), F32),
            pair_i,
            jax.ShapeDtypeStruct((N_EXPERTS * ROUTE_SUB, LANES), F32),
        ],
        scratch_shapes=[pltpu.VMEM((N_EXPERTS * ROUTE_SUB, LANES), F32)],
        compiler_params=pltpu.CompilerParams(
            dimension_semantics=("arbitrary",), vmem_limit_bytes=VMEM_LIMIT),
        name="route",
    )(bias, logits3)


def _dest_kernel(start_ref, ids_ref, rank_ref, dest_ref):
    ids = ids_ref[...]
    off = jnp.zeros(ids.shape, jnp.int32)
    for e in range(N_EXPERTS):
        off = jnp.where(ids == e, start_ref[e], off)
    dest_ref[...] = rank_ref[...] + off


def _dest(seg_start, ids, rank):
    _, n_sub, _ = ids.shape
    blk = pl.BlockSpec((2, ROUTE_SUB, LANES), lambda i: (0, i, 0))
    return pl.pallas_call(
        _dest_kernel,
        grid=(n_sub // ROUTE_SUB,),
        in_specs=[pl.BlockSpec(memory_space=pltpu.SMEM), blk, blk],
        out_specs=blk,
        out_shape=jax.ShapeDtypeStruct(ids.shape, jnp.int32),
        name="dest",
    )(seg_start, ids, rank)


def _sc_mesh():
    return plsc.VectorSubcoreMesh(core_axis_name="core", subcore_axis_name="subcore")


def _sc_worker(rows_total):
    rows = rows_total // _SC_WORKERS
    wid = lax.axis_index("core") * SC_SUBCORES + lax.axis_index("subcore")
    return wid * rows, rows


def _dispatch(dest0, dest1, xn, n_slots):
    T, D = xn.shape
    W = SC_WINDOW
    rows = T // _SC_WORKERS

    @pl.kernel(out_type=jax.ShapeDtypeStruct((n_slots, D), xn.dtype), mesh=_sc_mesh(),
               scratch_types=[pltpu.VMEM((rows,), jnp.int32), pltpu.VMEM((rows,), jnp.int32),
                              pltpu.VMEM((2, W, D), xn.dtype),
                              pltpu.SemaphoreType.DMA((2,)), pltpu.SemaphoreType.DMA((2,))])
    def scatter_rows(x_hbm, d0_hbm, d1_hbm, xs_hbm, d0_v, d1_v, buf, in_sem, out_sem):
        base, _ = _sc_worker(T)
        pltpu.sync_copy(d0_hbm.at[pl.ds(base, rows)], d0_v)
        pltpu.sync_copy(d1_hbm.at[pl.ds(base, rows)], d1_v)

        def load(w, slot):
            return pltpu.make_async_copy(x_hbm.at[pl.ds(base + w * W, W)], buf.at[slot], in_sem.at[slot])

        def store(w, slot, d_v):
            return pltpu.make_async_copy(buf.at[slot], xs_hbm.at[d_v.at[pl.ds(w * W, W)]], out_sem.at[slot])

        def step(w, slot):
            load(w, slot).wait()
            store(w, slot, d0_v).start()
            store(w, slot, d1_v).start()
            store(w, slot, d0_v).wait()
            store(w, slot, d1_v).wait()

        n = rows // W
        load(0, 0).start()

        @pl.loop(0, n, step=2)
        def _(w):
            load(w + 1, 1).start()
            step(w, 0)

            @pl.when(w + 2 < n)
            def _():
                load(w + 2, 0).start()

            step(w + 1, 1)

    return scatter_rows(xn, dest0, dest1)


def _gather_rows(src, idx):
    M = idx.shape[0]
    D = src.shape[1]
    W = SC_WINDOW
    rows = M // _SC_WORKERS

    @pl.kernel(out_type=jax.ShapeDtypeStruct((M, D), src.dtype), mesh=_sc_mesh(),
               scratch_types=[pltpu.VMEM((rows,), jnp.int32), pltpu.VMEM((2, W, D), src.dtype),
                              pltpu.SemaphoreType.DMA((2,)), pltpu.SemaphoreType.DMA((2,))])
    def gather_rows(src_hbm, i_hbm, o_hbm, i_v, buf, in_sem, out_sem):
        base, _ = _sc_worker(M)
        pltpu.sync_copy(i_hbm.at[pl.ds(base, rows)], i_v)

        def load(w, slot):
            return pltpu.make_async_copy(src_hbm.at[i_v.at[pl.ds(w * W, W)]], buf.at[slot], in_sem.at[slot])

        def store(w, slot):
            return pltpu.make_async_copy(buf.at[slot], o_hbm.at[pl.ds(base + w * W, W)], out_sem.at[slot])

        n = rows // W
        load(0, 0).start()

        @pl.loop(0, n, step=2)
        def _(w):
            @pl.when(w > 0)
            def _():
                store(w - 1, 1).wait()

            load(w + 1, 1).start()
            load(w, 0).wait()
            store(w, 0).start()
            store(w, 0).wait()

            @pl.when(w + 2 < n)
            def _():
                load(w + 2, 0).start()

            load(w + 1, 1).wait()
            store(w + 1, 1).start()

        store(n - 1, 1).wait()

    return gather_rows(src, idx)


def _expert_kernel(be_ref, nb_ref, x_ref, wg_f32, wu_f32, wd_f32, y_ref, wg_ref, wu_ref, wd_ref):
    i = pl.program_id(0)
    used = i < nb_ref[0]
    new_expert = (i == 0) | (be_ref[i] != be_ref[jnp.maximum(i - 1, 0)])

    @pl.when(used & new_expert)
    def _():
        wg_ref[0] = wg_f32[0].astype(BF16)
        wu_ref[0] = wu_f32[0].astype(BF16)
        wd_ref[0] = wd_f32[0].astype(BF16)

    @pl.when(used)
    def _():
        for c in range(EXPERT_ROWS // EXPERT_CHAIN_ROWS):
            rs = slice(c * EXPERT_CHAIN_ROWS, (c + 1) * EXPERT_CHAIN_ROWS)
            x = _unpack_pairs(x_ref[rs]).astype(BF16)
            g = _dot(x, wg_ref[0])
            u = _dot(x, wu_ref[0])
            hid = (g * _sigmoid(g) * u).astype(BF16)
            y_ref[rs] = _pack_pairs(_dot(hid, wd_ref[0]))

    @pl.when(jnp.logical_not(used))
    def _():
        y_ref[...] = jnp.zeros_like(y_ref)


def _experts(block_expert, n_used, xs, w_gate, w_up, w_down):
    n_slots, half = xs.shape
    D = 2 * half
    n_blocks = n_slots // EXPERT_ROWS
    row_blk = lambda i, be, nb: (jnp.minimum(i, nb[0] - 1), 0)
    w_blk = lambda i, be, nb: (be[i], 0, 0)
    grid_spec = pltpu.PrefetchScalarGridSpec(
        num_scalar_prefetch=2,
        grid=(n_blocks,),
        in_specs=[
            pl.BlockSpec((EXPERT_ROWS, half), row_blk),
            pl.BlockSpec((1, D, D_EXPERT), w_blk),
            pl.BlockSpec((1, D, D_EXPERT), w_blk),
            pl.BlockSpec((1, D_EXPERT, D), w_blk),
        ],
        out_specs=pl.BlockSpec((EXPERT_ROWS, half), lambda i, be, nb: (i, 0)),
        scratch_shapes=[pltpu.VMEM((1, D, D_EXPERT), BF16), pltpu.VMEM((1, D, D_EXPERT), BF16),
                        pltpu.VMEM((1, D_EXPERT, D), BF16)],
    )
    return pl.pallas_call(
        _expert_kernel,
        grid_spec=grid_spec,
        out_shape=jax.ShapeDtypeStruct((n_slots, half), jnp.uint32),
        compiler_params=pltpu.CompilerParams(
            dimension_semantics=("arbitrary",), vmem_limit_bytes=VMEM_LIMIT),
        name="experts",
    )(block_expert, n_used, xs, w_gate, w_up, w_down)


def _combine_kernel(y0_ref, y1_ref, h_ref, gates_ref, nfin_ref, o_ref):
    gates = gates_ref[...]
    h = (h_ref[...] + gates[:, 0:1] * _unpack_pairs(y0_ref[...])
         + gates[:, 1:2] * _unpack_pairs(y1_ref[...]))
    o_ref[...] = _rms(h, nfin_ref[...])


def _combine(y01, h2, gates_t, norm_final):
    T, D = h2.shape
    n_t = T // MOVE_ROWS
    return pl.pallas_call(
        _combine_kernel,
        grid=(n_t,),
        in_specs=[
            pl.BlockSpec((MOVE_ROWS, D // 2), lambda i: (i, 0)),
            pl.BlockSpec((MOVE_ROWS, D // 2), lambda i: (i + n_t, 0)),
            pl.BlockSpec((MOVE_ROWS, D), lambda i: (i, 0)),
            pl.BlockSpec((MOVE_ROWS, 2), lambda i: (i, 0)),
            pl.BlockSpec((1, D), lambda i: (0, 0)),
        ],
        out_specs=pl.BlockSpec((MOVE_ROWS, D), lambda i: (i, 0)),
        out_shape=jax.ShapeDtypeStruct((T, D), F32),
        compiler_params=pltpu.CompilerParams(
            dimension_semantics=("arbitrary",), vmem_limit_bytes=VMEM_LIMIT),
        name="combine",
    )(y01, y01, h2, gates_t, norm_final)


def kernel(x, mem, norm_mix, w_in, gmlp_ln, gmlp_w_spatial, gmlp_b_spatial, gmlp_beta, hgrn_lb_logits, hgrn_out_gain, w_out, norm_xattn, norm_mem, w_xq, w_xkv, w_xo, norm_ffn, w_router_group, b_router_group, w_router_expert, b_router_expert, w_expert_gate, w_expert_up, w_expert_down, norm_final):
    B, S, D = x.shape
    T = B * S
    depth = w_in.shape[0]
    assert depth == 1 and hgrn_lb_logits.shape[0] == 2
    l = 0
    row = lambda p: p.reshape(1, -1)

    h1 = _mixer(x, row(norm_mix[l]), w_in[l], row(gmlp_ln[l]), gmlp_w_spatial[l],
                gmlp_b_spatial[l].T, row(gmlp_beta[l]), hgrn_lb_logits, row(hgrn_out_gain[l]),
                w_out[l])
    k_mem, v_mem = _memkv(mem, row(norm_mem[l]), w_xkv[l])

    w_router = jnp.concatenate([w_router_group[l].T, w_router_expert[l].T], axis=0)
    w_router = jnp.pad(w_router, ((0, ROUTER_ROWS - w_router.shape[0]), (0, 0)))
    w_router = w_router.astype(BF16)
    h2, xn, logits = _xattn(h1, row(norm_xattn[l]), w_xq[l], k_mem, v_mem,
                            w_xo[l], row(norm_ffn[l]), w_router)

    bias = jnp.concatenate([b_router_group[l], b_router_expert[l]]).astype(F32)
    ids, gates, rank, counts = _route(bias, logits.reshape(ROUTER_ROWS, T // LANES, LANES))

    counts = counts[::ROUTE_SUB, 0].astype(jnp.int32)
    padded = (counts + EXPERT_ROWS - 1) // EXPERT_ROWS * EXPERT_ROWS
    seg_end = jnp.cumsum(padded)
    seg_start = seg_end - padded
    n_blocks = (2 * T) // EXPERT_ROWS + N_EXPERTS
    block_first_row = jnp.arange(n_blocks, dtype=jnp.int32) * EXPERT_ROWS
    block_expert = jnp.minimum(
        jnp.sum(block_first_row[:, None] >= seg_end[None, :], axis=1), N_EXPERTS - 1).astype(jnp.int32)
    n_used = (seg_end[-1:] // EXPERT_ROWS).astype(jnp.int32)

    dest = _dest(seg_start, ids, rank).reshape(2, T)

    xs = _dispatch(dest[0], dest[1], xn, n_blocks * EXPERT_ROWS)
    yb = _experts(block_expert, n_used, xs, w_expert_gate[l], w_expert_up[l], w_expert_down[l])
    gates_t = gates.reshape(2, T).T
    y01 = _gather_rows(yb, dest.reshape(2 * T))
    out = _combine(y01, h2.reshape(T, D), gates_t, row(norm_final))
    return out.reshape(B, S, D)
```

```python
import functools

import jax
import jax.numpy as jnp
from jax import lax
from jax.experimental import pallas as pl
from jax.experimental.pallas import tpu as pltpu
from jax.experimental.pallas import tpu_sc as plsc

F32 = jnp.float32
BF16 = jnp.bfloat16
EPS = 1e-6

D_MODEL = 1024
D_GMLP = 512
GMLP_GROUPS = 4
GMLP_CHUNK = 128
D_HGRN = 512
HGRN_HEADS = 4
HGRN_DK = 128
HGRN_CHUNK = 64
D_IN_PROJ = 2 * D_GMLP + 4 * D_HGRN
N_MEM = 256
XATTN_HEADS = 4
XATTN_HEAD_DIM = D_MODEL // XATTN_HEADS
N_GROUPS = 4
EXPERTS_PER_GROUP = 8
N_EXPERTS = N_GROUPS * EXPERTS_PER_GROUP
D_EXPERT = 512

LANES = 128
MIX_ROWS = 512
MIX_CHAIN_ROWS = 256
ATT_ROWS = 1024
ATT_CHAIN_ROWS = 1024
WEIGHT_STAGE_ROWS = 128
ROUTER_ROWS = 40
ROUTE_SUB = 16
EXPERT_ROWS = 512
EXPERT_CHAIN_ROWS = 256
MOVE_ROWS = 1024
TOKEN_PARTS = 2
SC_WINDOW = 64
SC_CORES = 2
SC_SUBCORES = 16
_SC_WORKERS = SC_CORES * SC_SUBCORES
VMEM_LIMIT = 48 * 1024 * 1024


def _rms(x, gain):
    return x * lax.rsqrt(jnp.mean(x * x, axis=-1, keepdims=True) + EPS) * gain


def _dot(a, b):
    return jnp.dot(a, b, preferred_element_type=F32)


def _dot_nt(a, b):
    return lax.dot_general(a, b, (((1,), (1,)), ((), ())), preferred_element_type=F32)


def _dot_tn(a, b):
    return lax.dot_general(a, b, (((0,), (0,)), ((), ())), preferred_element_type=F32)


def _gelu(x):
    return 0.5 * x * (1.0 + jnp.tanh(0.7978845608028654 * (x + 0.044715 * (x * x * x))))


def _sigmoid(x):
    return 1.0 / (1.0 + jnp.exp(-x))


def _block_id(idx, size):
    assert size & (size - 1) == 0
    return lax.shift_right_logical(idx, size.bit_length() - 1)


def _stage_weight(w_hbm, w_bf16, stage_ref, sem):
    rows = stage_ref.shape[0]
    for k in range(w_hbm.shape[0] // rows):
        copy = pltpu.make_async_copy(w_hbm.at[pl.ds(k * rows, rows)], stage_ref, sem)
        copy.start()
        copy.wait()
        w_bf16[k * rows:(k + 1) * rows, :] = stage_ref[...].astype(BF16)


def _weight_scratch(k, n):
    return [pltpu.VMEM((k, n), BF16), pltpu.VMEM((WEIGHT_STAGE_ROWS, n), F32)]


_HIGH_HALF = 0xFFFF0000


def _pack_pairs(x):
    c = x.shape[1] // 2
    bits = lax.bitcast_convert_type(x.astype(BF16).astype(F32), jnp.uint32)
    return (bits[:, c:] & jnp.uint32(_HIGH_HALF)) | lax.shift_right_logical(bits[:, :c], jnp.uint32(16))


def _unpack_pairs(w):
    lo = lax.bitcast_convert_type(lax.shift_left(w, jnp.uint32(16)), F32)
    hi = lax.bitcast_convert_type(w & jnp.uint32(_HIGH_HALF), F32)
    return jnp.concatenate([lo, hi], axis=1)


def _mixer_kernel(x_ref, nmix_ref, win_hbm, gln_ref, ws_ref, bst_ref, beta_ref, lbl_ref, og_ref,
                  wout_hbm, o_ref, proj_ref, ycat_ref, state_ref, win_ref, win_stage, wout_ref, wout_stage,
                  sem):
    @pl.when((pl.program_id(0) == 0) & (pl.program_id(1) == 0))
    def _():
        _stage_weight(win_hbm, win_ref, win_stage, sem)
        _stage_weight(wout_hbm, wout_ref, wout_stage, sem)

    @pl.when(pl.program_id(1) == 0)
    def _():
        state_ref[...] = jnp.zeros_like(state_ref)

    n = MIX_CHAIN_ROWS
    r_i = lax.broadcasted_iota(jnp.int32, (GMLP_CHUNK, GMLP_CHUNK), 0)
    c_i = lax.broadcasted_iota(jnp.int32, (GMLP_CHUNK, GMLP_CHUNK), 1)
    causal = c_i <= r_i
    w_tril = [jnp.where(causal, ws_ref[g], 0.0).astype(BF16) for g in range(GMLP_GROUPS)]
    lbl = lbl_ref[...]
    e_lb = jnp.exp(lbl - jnp.max(lbl, axis=0, keepdims=True))
    lb = e_lb[0:1] / jnp.sum(e_lb, axis=0, keepdims=True)
    rr = lax.broadcasted_iota(jnp.int32, (n, n), 0)
    cc = lax.broadcasted_iota(jnp.int32, (n, n), 1)
    tri = jnp.where((_block_id(rr, HGRN_CHUNK) == _block_id(cc, HGRN_CHUNK)) & (cc <= rr),
                    1.0, 0.0).astype(BF16)
    r64 = lax.broadcasted_iota(jnp.int32, (HGRN_CHUNK, HGRN_CHUNK), 0)
    c64 = lax.broadcasted_iota(jnp.int32, (HGRN_CHUNK, HGRN_CHUNK), 1)
    causal64 = c64 <= r64
    base = 2 * D_GMLP

    for chain in range(x_ref.shape[1] // n):
        r0 = chain * n
        x = x_ref[0, r0:r0 + n]
        a = _rms(x, nmix_ref[...]).astype(BF16)
        proj_ref[r0:r0 + n] = _dot(a, win_ref[...])

        u = _gelu(proj_ref[r0:r0 + n, 0:D_GMLP])
        v = _gelu(proj_ref[r0:r0 + n, D_GMLP:2 * D_GMLP])
        vc = v - jnp.mean(v, axis=-1, keepdims=True)
        vn = (vc * lax.rsqrt(jnp.mean(vc * vc, axis=-1, keepdims=True) + EPS) * gln_ref[...]).astype(BF16)
        z_rows = []
        for c in range(n // GMLP_CHUNK):
            z_cols = []
            for g in range(GMLP_GROUPS):
                vg = vn[c * GMLP_CHUNK:(c + 1) * GMLP_CHUNK, g * LANES:(g + 1) * LANES]
                z_cols.append(_dot(w_tril[g], vg) + bst_ref[:, g:g + 1])
            z_rows.append(jnp.concatenate(z_cols, axis=1))
        z = jnp.concatenate(z_rows, axis=0)
        ycat_ref[r0:r0 + n, 0:D_GMLP] = _rms(u * z, beta_ref[...]).astype(BF16)

        f = lb + (1.0 - lb) * _sigmoid(proj_ref[r0:r0 + n, base + D_HGRN:base + 2 * D_HGRN])
        log_f = jnp.log(f)
        lf_hi = log_f.astype(BF16)
        lf_lo = (log_f - lf_hi.astype(F32)).astype(BF16)
        b_all = _dot(tri, lf_hi) + _dot(tri, lf_lo)
        for c in range(n // HGRN_CHUNK):
            rs = slice(c * HGRN_CHUNK, (c + 1) * HGRN_CHUNK)
            ps = slice(r0 + c * HGRN_CHUNK, r0 + (c + 1) * HGRN_CHUNK)
            bc = b_all[rs]
            bl = bc[HGRN_CHUNK - 1:HGRN_CHUNK]
            q_c = proj_ref[ps, base:base + D_HGRN]
            q_c = q_c * _sigmoid(q_c)
            k_c = 1.0 - f[rs]
            qd = (q_c * jnp.exp(bc)).astype(BF16)
            ki = (k_c * jnp.exp(-bc)).astype(BF16)
            kte = (k_c * jnp.exp(bl - bc)).astype(BF16)
            decay = jnp.exp(bl)
            v_c = proj_ref[ps, base + 2 * D_HGRN:base + 3 * D_HGRN].astype(BF16)
            g_c = proj_ref[ps, base + 3 * D_HGRN:base + 4 * D_HGRN]
            gate = og_ref[...] * (g_c * _sigmoid(g_c))
            for h in range(HGRN_HEADS):
                cs = slice(h * HGRN_DK, (h + 1) * HGRN_DK)
                scores = jnp.where(causal64, _dot_nt(qd[:, cs], ki[:, cs]), 0.0).astype(BF16)
                st = state_ref[h]
                o = _dot(scores, v_c[:, cs]) + _dot_nt(qd[:, cs], st.astype(BF16))
                state_ref[h] = st * decay[:, cs] + _dot_tn(v_c[:, cs], kte[:, cs])
                o = o * lax.rsqrt(jnp.mean(o * o, axis=-1, keepdims=True) + EPS)
                ycat_ref[ps, D_GMLP + h * HGRN_DK:D_GMLP + (h + 1) * HGRN_DK] = (o * gate[:, cs]).astype(BF16)

        o_ref[0, r0:r0 + n] = x + _dot(ycat_ref[r0:r0 + n], wout_ref[...])


def _mixer(x, norm_mix, w_in, gmlp_ln, w_s, b_s_t, beta, lb_logits, out_gain, w_out):
    B, S, D = x.shape
    const2 = lambda b, s: (0, 0)
    return pl.pallas_call(
        _mixer_kernel,
        grid=(B, S // MIX_ROWS),
        in_specs=[
            pl.BlockSpec((1, MIX_ROWS, D), lambda b, s: (b, s, 0)),
            pl.BlockSpec((1, D), const2),
            pl.BlockSpec(memory_space=pl.ANY),
            pl.BlockSpec((1, D_GMLP), const2),
            pl.BlockSpec((GMLP_GROUPS, GMLP_CHUNK, GMLP_CHUNK), lambda b, s: (0, 0, 0)),
            pl.BlockSpec((GMLP_CHUNK, GMLP_GROUPS), const2),
            pl.BlockSpec((1, D_GMLP), const2),
            pl.BlockSpec(lb_logits.shape, const2),
            pl.BlockSpec((1, D_HGRN), const2),
            pl.BlockSpec(memory_space=pl.ANY),
        ],
        out_specs=pl.BlockSpec((1, MIX_ROWS, D), lambda b, s: (b, s, 0)),
        out_shape=jax.ShapeDtypeStruct((B, S, D), F32),
        scratch_shapes=[
            pltpu.VMEM((MIX_ROWS, D_IN_PROJ), F32),
            pltpu.VMEM((MIX_ROWS, D), BF16),
            pltpu.VMEM((HGRN_HEADS, HGRN_DK, HGRN_DK), F32),
        ] + _weight_scratch(D, D_IN_PROJ) + _weight_scratch(D, D) + [pltpu.SemaphoreType.DMA],
        compiler_params=pltpu.CompilerParams(
            dimension_semantics=("arbitrary", "arbitrary"), vmem_limit_bytes=VMEM_LIMIT),
        name="mixer",
    )(x, norm_mix, w_in, gmlp_ln, w_s, b_s_t, beta, lb_logits, out_gain, w_out)


def _memkv_kernel(mem_ref, nm_ref, wkv_hbm, k_ref, v_ref, wkv_ref, wkv_stage, sem):
    @pl.when(pl.program_id(0) == 0)
    def _():
        _stage_weight(wkv_hbm, wkv_ref, wkv_stage, sem)

    m = _rms(mem_ref[0], nm_ref[...]).astype(BF16)
    kv = _dot(m, wkv_ref[...])
    k_ref[0] = kv[:, :D_MODEL].astype(BF16)
    v_ref[0] = kv[:, D_MODEL:].astype(BF16)


def _memkv(mem, norm_mem, w_kv):
    B, M, D = mem.shape
    out = jax.ShapeDtypeStruct((B, M, D), BF16)
    return pl.pallas_call(
        _memkv_kernel,
        grid=(B,),
        in_specs=[
            pl.BlockSpec((1, M, D), lambda b: (b, 0, 0)),
            pl.BlockSpec((1, D), lambda b: (0, 0)),
            pl.BlockSpec(memory_space=pl.ANY),
        ],
        out_specs=[pl.BlockSpec((1, M, D), lambda b: (b, 0, 0))] * 2,
        out_shape=[out, out],
        scratch_shapes=_weight_scratch(D, 2 * D) + [pltpu.SemaphoreType.DMA],
        compiler_params=pltpu.CompilerParams(
            dimension_semantics=("arbitrary",), vmem_limit_bytes=VMEM_LIMIT),
        name="memkv",
    )(mem, norm_mem, w_kv)


def _xattn_kernel(h_ref, nx_ref, wq_hbm, k_ref, v_ref, wo_hbm, nf_ref, wr_ref,
                  h2_ref, xn_ref, lg_ref, att_ref, wq_ref, wq_stage, wo_ref, wo_stage, sem):
    @pl.when((pl.program_id(0) == 0) & (pl.program_id(1) == 0))
    def _():
        _stage_weight(wq_hbm, wq_ref, wq_stage, sem)
        _stage_weight(wo_hbm, wo_ref, wo_stage, sem)

    for c in range(ATT_ROWS // ATT_CHAIN_ROWS):
        rs = slice(c * ATT_CHAIN_ROWS, (c + 1) * ATT_CHAIN_ROWS)
        h = h_ref[0, rs]
        hn = _rms(h, nx_ref[...]).astype(BF16)
        q = (_dot(hn, wq_ref[...]) * (XATTN_HEAD_DIM ** -0.5)).astype(BF16)
        for hd in range(XATTN_HEADS):
            cs = slice(hd * XATTN_HEAD_DIM, (hd + 1) * XATTN_HEAD_DIM)
            s = _dot_nt(q[:, cs], k_ref[0, :, cs])
            p = jnp.exp(s - jnp.max(s, axis=-1, keepdims=True))
            p = (p / jnp.sum(p, axis=-1, keepdims=True)).astype(BF16)
            att_ref[rs, cs] = _dot(p, v_ref[0, :, cs]).astype(BF16)
        h2 = h + _dot(att_ref[rs], wo_ref[...])
        h2_ref[0, rs] = h2
        xn = _rms(h2, nf_ref[...])
        xn_ref[rs] = _pack_pairs(xn)
        lg_ref[:, rs] = _dot_nt(wr_ref[...], xn.astype(BF16))


def _xattn(h1, norm_x, w_q, k_mem, v_mem, w_o, norm_ffn, w_router, batch0, batches):
    _, S, D = h1.shape
    n_s = S // ATT_ROWS
    const2 = lambda b, s: (0, 0)
    return pl.pallas_call(
        _xattn_kernel,
        grid=(batches, n_s),
        in_specs=[
            pl.BlockSpec((1, ATT_ROWS, D), lambda b, s: (b + batch0, s, 0)),
            pl.BlockSpec((1, D), const2),
            pl.BlockSpec(memory_space=pl.ANY),
            pl.BlockSpec((1, N_MEM, D), lambda b, s: (b + batch0, 0, 0)),
            pl.BlockSpec((1, N_MEM, D), lambda b, s: (b + batch0, 0, 0)),
            pl.BlockSpec(memory_space=pl.ANY),
            pl.BlockSpec((1, D), const2),
            pl.BlockSpec((ROUTER_ROWS, D), const2),
        ],
        out_specs=[
            pl.BlockSpec((1, ATT_ROWS, D), lambda b, s: (b, s, 0)),
            pl.BlockSpec((ATT_ROWS, D // 2), lambda b, s: (b * n_s + s, 0)),
            pl.BlockSpec((ROUTER_ROWS, ATT_ROWS), lambda b, s: (0, b * n_s + s)),
        ],
        out_shape=[
            jax.ShapeDtypeStruct((batches, S, D), F32),
            jax.ShapeDtypeStruct((batches * S, D // 2), jnp.uint32),
            jax.ShapeDtypeStruct((ROUTER_ROWS, batches * S), F32),
        ],
        scratch_shapes=([pltpu.VMEM((ATT_ROWS, D), BF16)] + _weight_scratch(D, D) + _weight_scratch(D, D)
                        + [pltpu.SemaphoreType.DMA]),
        compiler_params=pltpu.CompilerParams(
            dimension_semantics=("arbitrary", "arbitrary"), vmem_limit_bytes=VMEM_LIMIT),
        name="xattn",
    )(h1, norm_x, w_q, k_mem, v_mem, w_o, norm_ffn, w_router)


def _route_kernel(bias_ref, lg_ref, ids_ref, gates_ref, rank_ref, cnt_ref, base_ref):
    sub = lg_ref.shape[1]

    @pl.when(pl.program_id(0) == 0)
    def _():
        base_ref[...] = jnp.zeros_like(base_ref)

    best = lg_ref[0] + bias_ref[0]
    gl = [best]
    sel = jnp.zeros(best.shape, jnp.int32)
    for g in range(1, N_GROUPS):
        cur = lg_ref[g] + bias_ref[g]
        gl.append(cur)
        better = cur > best
        best = jnp.where(better, cur, best)
        sel = jnp.where(better, g, sel)
    denom = jnp.exp(gl[0] - best)
    for g in range(1, N_GROUPS):
        denom = denom + jnp.exp(gl[g] - best)
    g_gate = 1.0 / denom

    ev = []
    for j in range(EXPERTS_PER_GROUP):
        val = lg_ref[N_GROUPS + j] + bias_ref[N_GROUPS + j]
        for g in range(1, N_GROUPS):
            e = g * EXPERTS_PER_GROUP + j
            val = jnp.where(sel == g, lg_ref[N_GROUPS + e] + bias_ref[N_GROUPS + e], val)
        ev.append(val)
    v1, i1 = ev[0], jnp.zeros(best.shape, jnp.int32)
    for j in range(1, EXPERTS_PER_GROUP):
        better = ev[j] > v1
        v1 = jnp.where(better, ev[j], v1)
        i1 = jnp.where(better, j, i1)
    rest = [jnp.where(i1 == j, -jnp.inf, ev[j]) for j in range(EXPERTS_PER_GROUP)]
    v2, i2 = rest[0], jnp.zeros(best.shape, jnp.int32)
    for j in range(1, EXPERTS_PER_GROUP):
        better = rest[j] > v2
        v2 = jnp.where(better, rest[j], v2)
        i2 = jnp.where(better, j, i2)
    e2 = jnp.exp(v2 - v1)
    inv = 1.0 / (1.0 + e2)
    id1 = sel * EXPERTS_PER_GROUP + i1
    id2 = sel * EXPERTS_PER_GROUP + i2
    ids_ref[0] = id1
    ids_ref[1] = id2
    gates_ref[0] = inv * g_gate
    gates_ref[1] = e2 * inv * g_gate

    member = jnp.concatenate(
        [jnp.where((id1 == e) | (id2 == e), 1.0, 0.0) for e in range(N_EXPERTS)], axis=0).astype(BF16)
    n = N_EXPERTS * sub
    li = lax.broadcasted_iota(jnp.int32, (LANES, LANES), 0)
    lj = lax.broadcasted_iota(jnp.int32, (LANES, LANES), 1)
    before_lane = jnp.where(li < lj, 1.0, 0.0).astype(BF16)
    ones = jnp.ones((LANES, LANES), BF16)
    ri = lax.broadcasted_iota(jnp.int32, (n, n), 0)
    rj = lax.broadcasted_iota(jnp.int32, (n, n), 1)
    same = _block_id(ri, sub) == _block_id(rj, sub)
    before_row = jnp.where(same & (rj < ri), 1.0, 0.0).astype(BF16)
    all_row = jnp.where(same, 1.0, 0.0).astype(BF16)
    in_row = _dot(member, before_lane)
    prev_rows = _dot(_dot(before_row, member).astype(BF16), ones)
    total = _dot(_dot(all_row, member).astype(BF16), ones)
    base = base_ref[...]
    pos = base + prev_rows + in_row
    r1 = jnp.zeros(best.shape, F32)
    r2 = jnp.zeros(best.shape, F32)
    for e in range(N_EXPERTS):
        pe = pos[e * sub:(e + 1) * sub]
        r1 = jnp.where(id1 == e, pe, r1)
        r2 = jnp.where(id2 == e, pe, r2)
    rank_ref[0] = r1.astype(jnp.int32)
    rank_ref[1] = r2.astype(jnp.int32)
    base_ref[...] = base + total
    cnt_ref[...] = base + total


def _route(bias, logits3):
    rp, n_sub, _ = logits3.shape
    blk = lambda i: (0, i, 0)
    pair_i = jax.ShapeDtypeStruct((2, n_sub, LANES), jnp.int32)
    return pl.pallas_call(
        _route_kernel,
        grid=(n_sub // ROUTE_SUB,),
        in_specs=[
            pl.BlockSpec(memory_space=pltpu.SMEM),
            pl.BlockSpec((rp, ROUTE_SUB, LANES), blk),
        ],
        out_specs=[
            pl.BlockSpec((2, ROUTE_SUB, LANES), blk),
            pl.BlockSpec((2, ROUTE_SUB, LANES), blk),
            pl.BlockSpec((2, ROUTE_SUB, LANES), blk),
            pl.BlockSpec((N_EXPERTS * ROUTE_SUB, LANES), lambda i: (0, 0)),
        ],
        out_shape=[
            pair_i,
            jax.ShapeDtypeStruct((2, n_sub, LANES), F32),
            pair_i,
            jax.ShapeDtypeStruct((N_EXPERTS * ROUTE_SUB, LANES), F32),
        ],
        scratch_shapes=[pltpu.VMEM((N_EXPERTS * ROUTE_SUB, LANES), F32)],
        compiler_params=pltpu.CompilerParams(
            dimension_semantics=("arbitrary",), vmem_limit_bytes=VMEM_LIMIT),
        name="route",
    )(bias, logits3)


def _dest_kernel(start_ref, ids_ref, rank_ref, dest_ref):
    ids = ids_ref[...]
    off = jnp.zeros(ids.shape, jnp.int32)
    for e in range(N_EXPERTS):
        off = jnp.where(ids == e, start_ref[e], off)
    dest_ref[...] = rank_ref[...] + off


def _dest(seg_start, ids, rank):
    _, n_sub, _ = ids.shape
    blk = pl.BlockSpec((2, ROUTE_SUB, LANES), lambda i: (0, i, 0))
    return pl.pallas_call(
        _dest_kernel,
        grid=(n_sub // ROUTE_SUB,),
        in_specs=[pl.BlockSpec(memory_space=pltpu.SMEM), blk, blk],
        out_specs=blk,
        out_shape=jax.ShapeDtypeStruct(ids.shape, jnp.int32),
        name="dest",
    )(seg_start, ids, rank)


def _sc_mesh():
    return plsc.VectorSubcoreMesh(core_axis_name="core", subcore_axis_name="subcore")


def _sc_worker(rows_total):
    rows = rows_total // _SC_WORKERS
    wid = lax.axis_index("core") * SC_SUBCORES + lax.axis_index("subcore")
    return wid * rows, rows


def _dispatch(dest0, dest1, xn, n_slots):
    T, D = xn.shape
    W = SC_WINDOW
    rows = T // _SC_WORKERS

    @pl.kernel(out_type=jax.ShapeDtypeStruct((n_slots, D), xn.dtype), mesh=_sc_mesh(),
               scratch_types=[pltpu.VMEM((rows,), jnp.int32), pltpu.VMEM((rows,), jnp.int32),
                              pltpu.VMEM((2, W, D), xn.dtype),
                              pltpu.SemaphoreType.DMA((2,)), pltpu.SemaphoreType.DMA((2,))])
    def scatter_rows(x_hbm, d0_hbm, d1_hbm, xs_hbm, d0_v, d1_v, buf, in_sem, out_sem):
        base, _ = _sc_worker(T)
        pltpu.sync_copy(d0_hbm.at[pl.ds(base, rows)], d0_v)
        pltpu.sync_copy(d1_hbm.at[pl.ds(base, rows)], d1_v)

        def load(w, slot):
            return pltpu.make_async_copy(x_hbm.at[pl.ds(base + w * W, W)], buf.at[slot], in_sem.at[slot])

        def store(w, slot, d_v):
            return pltpu.make_async_copy(buf.at[slot], xs_hbm.at[d_v.at[pl.ds(w * W, W)]], out_sem.at[slot])

        def step(w, slot):
            load(w, slot).wait()
            store(w, slot, d0_v).start()
            store(w, slot, d1_v).start()
            store(w, slot, d0_v).wait()
            store(w, slot, d1_v).wait()

        n = rows // W
        load(0, 0).start()

        @pl.loop(0, n, step=2)
        def _(w):
            load(w + 1, 1).start()
            step(w, 0)

            @pl.when(w + 2 < n)
            def _():
                load(w + 2, 0).start()

            step(w + 1, 1)

    return scatter_rows(xn, dest0, dest1)


def _gather_rows(src, idx):
    M = idx.shape[0]
    D = src.shape[1]
    W = SC_WINDOW
    rows = M // _SC_WORKERS

    @pl.kernel(out_type=jax.ShapeDtypeStruct((M, D), src.dtype), mesh=_sc_mesh(),
               scratch_types=[pltpu.VMEM((rows,), jnp.int32), pltpu.VMEM((2, W, D), src.dtype),
                              pltpu.SemaphoreType.DMA((2,)), pltpu.SemaphoreType.DMA((2,))])
    def gather_rows(src_hbm, i_hbm, o_hbm, i_v, buf, in_sem, out_sem):
        base, _ = _sc_worker(M)
        pltpu.sync_copy(i_hbm.at[pl.ds(base, rows)], i_v)

        def load(w, slot):
            return pltpu.make_async_copy(src_hbm.at[i_v.at[pl.ds(w * W, W)]], buf.at[slot], in_sem.at[slot])

        def store(w, slot):
            return pltpu.make_async_copy(buf.at[slot], o_hbm.at[pl.ds(base + w * W, W)], out_sem.at[slot])

        n = rows // W
        load(0, 0).start()

        @pl.loop(0, n, step=2)
        def _(w):
            @pl.when(w > 0)
            def _():
                store(w - 1, 1).wait()

            load(w + 1, 1).start()
            load(w, 0).wait()
            store(w, 0).start()
            store(w, 0).wait()

            @pl.when(w + 2 < n)
            def _():
                load(w + 2, 0).start()

            load(w + 1, 1).wait()
            store(w + 1, 1).start()

        store(n - 1, 1).wait()

    return gather_rows(src, idx)


def _expert_kernel(be_ref, nb_ref, x_ref, wg_f32, wu_f32, wd_f32, y_ref, wg_ref, wu_ref, wd_ref):
    i = pl.program_id(0)
    used = i < nb_ref[0]
    new_expert = (i == 0) | (be_ref[i] != be_ref[jnp.maximum(i - 1, 0)])

    @pl.when(used & new_expert)
    def _():
        wg_ref[0] = wg_f32[0].astype(BF16)
        wu_ref[0] = wu_f32[0].astype(BF16)
        wd_ref[0] = wd_f32[0].astype(BF16)

    @pl.when(used)
    def _():
        for c in range(EXPERT_ROWS // EXPERT_CHAIN_ROWS):
            rs = slice(c * EXPERT_CHAIN_ROWS, (c + 1) * EXPERT_CHAIN_ROWS)
            x = _unpack_pairs(x_ref[rs]).astype(BF16)
            g = _dot(x, wg_ref[0])
            u = _dot(x, wu_ref[0])
            hid = (g * _sigmoid(g) * u).astype(BF16)
            y_ref[rs] = _pack_pairs(_dot(hid, wd_ref[0]))

    @pl.when(jnp.logical_not(used))
    def _():
        y_ref[...] = jnp.zeros_like(y_ref)


def _experts(block_expert, n_used, xs, w_gate, w_up, w_down):
    n_slots, half = xs.shape
    D = 2 * half
    n_blocks = n_slots // EXPERT_ROWS
    row_blk = lambda i, be, nb: (jnp.minimum(i, nb[0] - 1), 0)
    w_blk = lambda i, be, nb: (be[i], 0, 0)
    grid_spec = pltpu.PrefetchScalarGridSpec(
        num_scalar_prefetch=2,
        grid=(n_blocks,),
        in_specs=[
            pl.BlockSpec((EXPERT_ROWS, half), row_blk),
            pl.BlockSpec((1, D, D_EXPERT), w_blk),
            pl.BlockSpec((1, D, D_EXPERT), w_blk),
            pl.BlockSpec((1, D_EXPERT, D), w_blk),
        ],
        out_specs=pl.BlockSpec((EXPERT_ROWS, half), lambda i, be, nb: (i, 0)),
        scratch_shapes=[pltpu.VMEM((1, D, D_EXPERT), BF16), pltpu.VMEM((1, D, D_EXPERT), BF16),
                        pltpu.VMEM((1, D_EXPERT, D), BF16)],
    )
    return pl.pallas_call(
        _expert_kernel,
        grid_spec=grid_spec,
        out_shape=jax.ShapeDtypeStruct((n_slots, half), jnp.uint32),
        compiler_params=pltpu.CompilerParams(
            dimension_semantics=("arbitrary",), vmem_limit_bytes=VMEM_LIMIT),
        name="experts",
    )(block_expert, n_used, xs, w_gate, w_up, w_down)


def _combine_kernel(y0_ref, y1_ref, h_ref, gates_ref, nfin_ref, *rest):
    o_ref = rest[-1]
    gates = gates_ref[...]
    h = (h_ref[...] + gates[:, 0:1] * _unpack_pairs(y0_ref[...])
         + gates[:, 1:2] * _unpack_pairs(y1_ref[...]))
    o_ref[...] = _rms(h, nfin_ref[...])


def _combine(y01, h2, gates_t, norm_final, out_prev, part, parts):
    T, D = h2.shape
    n_t = T // MOVE_ROWS
    in_specs = [
        pl.BlockSpec((MOVE_ROWS, D // 2), lambda i: (i, 0)),
        pl.BlockSpec((MOVE_ROWS, D // 2), lambda i: (i + n_t, 0)),
        pl.BlockSpec((MOVE_ROWS, D), lambda i: (i, 0)),
        pl.BlockSpec((MOVE_ROWS, 2), lambda i: (i, 0)),
        pl.BlockSpec((1, D), lambda i: (0, 0)),
    ]
    args = [y01, y01, h2, gates_t, norm_final]
    aliases = {}
    if out_prev is not None:
        in_specs.append(pl.BlockSpec(memory_space=pl.ANY))
        args.append(out_prev)
        aliases = {len(args) - 1: 0}
    return pl.pallas_call(
        _combine_kernel,
        grid=(n_t,),
        in_specs=in_specs,
        out_specs=pl.BlockSpec((MOVE_ROWS, D), lambda i: (i + part * n_t, 0)),
        out_shape=jax.ShapeDtypeStruct((parts * T, D), F32),
        input_output_aliases=aliases,
        compiler_params=pltpu.CompilerParams(
            dimension_semantics=("arbitrary",), vmem_limit_bytes=VMEM_LIMIT),
        name="combine",
    )(*args)


def kernel(x, mem, norm_mix, w_in, gmlp_ln, gmlp_w_spatial, gmlp_b_spatial, gmlp_beta, hgrn_lb_logits, hgrn_out_gain, w_out, norm_xattn, norm_mem, w_xq, w_xkv, w_xo, norm_ffn, w_router_group, b_router_group, w_router_expert, b_router_expert, w_expert_gate, w_expert_up, w_expert_down, norm_final):
    B, S, D = x.shape
    T = B * S
    depth = w_in.shape[0]
    assert depth == 1 and hgrn_lb_logits.shape[0] == 2
    l = 0
    row = lambda p: p.reshape(1, -1)

    h1 = _mixer(x, row(norm_mix[l]), w_in[l], row(gmlp_ln[l]), gmlp_w_spatial[l],
                gmlp_b_spatial[l].T, row(gmlp_beta[l]), hgrn_lb_logits, row(hgrn_out_gain[l]),
                w_out[l])
    k_mem, v_mem = _memkv(mem, row(norm_mem[l]), w_xkv[l])

    w_router = jnp.concatenate([w_router_group[l].T, w_router_expert[l].T], axis=0)
    w_router = jnp.pad(w_router, ((0, ROUTER_ROWS - w_router.shape[0]), (0, 0)))
    w_router = w_router.astype(BF16)
    bias = jnp.concatenate([b_router_group[l], b_router_expert[l]]).astype(F32)

    b_part = B // TOKEN_PARTS
    t_part = b_part * S
    n_blocks = (2 * t_part) // EXPERT_ROWS + N_EXPERTS
    out = None
    for part in range(TOKEN_PARTS):
        h2, xn, logits = _xattn(h1, row(norm_xattn[l]), w_xq[l], k_mem, v_mem, w_xo[l], row(norm_ffn[l]),
                                w_router, part * b_part, b_part)
        ids, gates, rank, counts = _route(bias, logits.reshape(ROUTER_ROWS, t_part // LANES, LANES))

        counts = counts[::ROUTE_SUB, 0].astype(jnp.int32)
        padded = (counts + EXPERT_ROWS - 1) // EXPERT_ROWS * EXPERT_ROWS
        seg_end = jnp.cumsum(padded)
        seg_start = seg_end - padded
        block_first_row = jnp.arange(n_blocks, dtype=jnp.int32) * EXPERT_ROWS
        block_expert = jnp.minimum(
            jnp.sum(block_first_row[:, None] >= seg_end[None, :], axis=1), N_EXPERTS - 1).astype(jnp.int32)
        n_used = (seg_end[-1:] // EXPERT_ROWS).astype(jnp.int32)

        dest = _dest(seg_start, ids, rank).reshape(2, t_part)
        xs = _dispatch(dest[0], dest[1], xn, n_blocks * EXPERT_ROWS)
        yb = _experts(block_expert, n_used, xs, w_expert_gate[l], w_expert_up[l], w_expert_down[l])
        y01 = _gather_rows(yb, dest.reshape(2 * t_part))
        out = _combine(y01, h2.reshape(t_part, D), gates.reshape(2, t_part).T, row(norm_final),
                       out, part, TOKEN_PARTS)
    return out.reshape(B, S, D)
```

```python
import functools

import jax
import jax.numpy as jnp
from jax import lax
from jax.experimental import pallas as pl
from jax.experimental.pallas import tpu as pltpu
from jax.experimental.pallas import tpu_sc as plsc

F32 = jnp.float32
BF16 = jnp.bfloat16
EPS = 1e-6

D_MODEL = 1024
D_GMLP = 512
GMLP_GROUPS = 4
GMLP_CHUNK = 128
D_HGRN = 512
HGRN_HEADS = 4
HGRN_DK = 128
HGRN_CHUNK = 64
D_IN_PROJ = 2 * D_GMLP + 4 * D_HGRN
N_MEM = 256
XATTN_HEADS = 4
XATTN_HEAD_DIM = D_MODEL // XATTN_HEADS
N_GROUPS = 4
EXPERTS_PER_GROUP = 8
N_EXPERTS = N_GROUPS * EXPERTS_PER_GROUP
D_EXPERT = 512

LANES = 128
MIX_ROWS = 512
MIX_CHAIN_ROWS = 256
ATT_ROWS = 1024
ATT_CHAIN_ROWS = 1024
WEIGHT_STAGE_ROWS = 128
ROUTER_ROWS = 40
ROUTE_SUB = 16
EXPERT_ROWS = 512
EXPERT_CHAIN_ROWS = 256
MOVE_ROWS = 1024
TOKEN_PARTS = 2
SC_WINDOW = 64
SC_CORES = 2
SC_SUBCORES = 16
_SC_WORKERS = SC_CORES * SC_SUBCORES
VMEM_LIMIT = 48 * 1024 * 1024


def _rms(x, gain):
    return x * lax.rsqrt(jnp.mean(x * x, axis=-1, keepdims=True) + EPS) * gain


def _dot(a, b):
    return jnp.dot(a, b, preferred_element_type=F32)


def _dot_nt(a, b):
    return lax.dot_general(a, b, (((1,), (1,)), ((), ())), preferred_element_type=F32)


def _dot_tn(a, b):
    return lax.dot_general(a, b, (((0,), (0,)), ((), ())), preferred_element_type=F32)


def _gelu(x):
    return 0.5 * x * (1.0 + jnp.tanh(0.7978845608028654 * (x + 0.044715 * (x * x * x))))


def _sigmoid(x):
    return 1.0 / (1.0 + jnp.exp(-x))


def _block_id(idx, size):
    assert size & (size - 1) == 0
    return lax.shift_right_logical(idx, size.bit_length() - 1)


def _stage_weight(w_hbm, w_bf16, stage_ref, sem):
    rows = stage_ref.shape[0]
    for k in range(w_hbm.shape[0] // rows):
        copy = pltpu.make_async_copy(w_hbm.at[pl.ds(k * rows, rows)], stage_ref, sem)
        copy.start()
        copy.wait()
        w_bf16[k * rows:(k + 1) * rows, :] = stage_ref[...].astype(BF16)


def _weight_scratch(k, n):
    return [pltpu.VMEM((k, n), BF16), pltpu.VMEM((WEIGHT_STAGE_ROWS, n), F32)]


_HIGH_HALF = 0xFFFF0000


def _pack_pairs(x):
    c = x.shape[1] // 2
    bits = lax.bitcast_convert_type(x.astype(BF16).astype(F32), jnp.uint32)
    return (bits[:, c:] & jnp.uint32(_HIGH_HALF)) | lax.shift_right_logical(bits[:, :c], jnp.uint32(16))


def _unpack_pairs(w):
    lo = lax.bitcast_convert_type(lax.shift_left(w, jnp.uint32(16)), F32)
    hi = lax.bitcast_convert_type(w & jnp.uint32(_HIGH_HALF), F32)
    return jnp.concatenate([lo, hi], axis=1)


def _mixer_kernel(x_ref, nmix_ref, win_hbm, gln_ref, ws_ref, bst_ref, beta_ref, lbl_ref, og_ref,
                  wout_hbm, o_ref, proj_ref, ycat_ref, state_ref, win_ref, win_stage, wout_ref, wout_stage,
                  sem):
    @pl.when((pl.program_id(0) == 0) & (pl.program_id(1) == 0))
    def _():
        _stage_weight(win_hbm, win_ref, win_stage, sem)
        _stage_weight(wout_hbm, wout_ref, wout_stage, sem)

    @pl.when(pl.program_id(1) == 0)
    def _():
        state_ref[...] = jnp.zeros_like(state_ref)

    n = MIX_CHAIN_ROWS
    r_i = lax.broadcasted_iota(jnp.int32, (GMLP_CHUNK, GMLP_CHUNK), 0)
    c_i = lax.broadcasted_iota(jnp.int32, (GMLP_CHUNK, GMLP_CHUNK), 1)
    causal = c_i <= r_i
    w_tril = [jnp.where(causal, ws_ref[g], 0.0).astype(BF16) for g in range(GMLP_GROUPS)]
    lbl = lbl_ref[...]
    e_lb = jnp.exp(lbl - jnp.max(lbl, axis=0, keepdims=True))
    lb = e_lb[0:1] / jnp.sum(e_lb, axis=0, keepdims=True)
    rr = lax.broadcasted_iota(jnp.int32, (n, n), 0)
    cc = lax.broadcasted_iota(jnp.int32, (n, n), 1)
    tri = jnp.where((_block_id(rr, HGRN_CHUNK) == _block_id(cc, HGRN_CHUNK)) & (cc <= rr),
                    1.0, 0.0).astype(BF16)
    r64 = lax.broadcasted_iota(jnp.int32, (HGRN_CHUNK, HGRN_CHUNK), 0)
    c64 = lax.broadcasted_iota(jnp.int32, (HGRN_CHUNK, HGRN_CHUNK), 1)
    causal64 = c64 <= r64
    base = 2 * D_GMLP

    for chain in range(x_ref.shape[1] // n):
        r0 = chain * n
        x = x_ref[0, r0:r0 + n]
        a = _rms(x, nmix_ref[...]).astype(BF16)
        proj_ref[r0:r0 + n] = _dot(a, win_ref[...])

        u = _gelu(proj_ref[r0:r0 + n, 0:D_GMLP])
        v = _gelu(proj_ref[r0:r0 + n, D_GMLP:2 * D_GMLP])
        vc = v - jnp.mean(v, axis=-1, keepdims=True)
        vn = (vc * lax.rsqrt(jnp.mean(vc * vc, axis=-1, keepdims=True) + EPS) * gln_ref[...]).astype(BF16)
        z_rows = []
        for c in range(n // GMLP_CHUNK):
            z_cols = []
            for g in range(GMLP_GROUPS):
                vg = vn[c * GMLP_CHUNK:(c + 1) * GMLP_CHUNK, g * LANES:(g + 1) * LANES]
                z_cols.append(_dot(w_tril[g], vg) + bst_ref[:, g:g + 1])
            z_rows.append(jnp.concatenate(z_cols, axis=1))
        z = jnp.concatenate(z_rows, axis=0)
        ycat_ref[r0:r0 + n, 0:D_GMLP] = _rms(u * z, beta_ref[...]).astype(BF16)

        f = lb + (1.0 - lb) * _sigmoid(proj_ref[r0:r0 + n, base + D_HGRN:base + 2 * D_HGRN])
        log_f = jnp.log(f)
        lf_hi = log_f.astype(BF16)
        lf_lo = (log_f - lf_hi.astype(F32)).astype(BF16)
        b_all = _dot(tri, lf_hi) + _dot(tri, lf_lo)
        for c in range(n // HGRN_CHUNK):
            rs = slice(c * HGRN_CHUNK, (c + 1) * HGRN_CHUNK)
            ps = slice(r0 + c * HGRN_CHUNK, r0 + (c + 1) * HGRN_CHUNK)
            bc = b_all[rs]
            bl = bc[HGRN_CHUNK - 1:HGRN_CHUNK]
            q_c = proj_ref[ps, base:base + D_HGRN]
            q_c = q_c * _sigmoid(q_c)
            k_c = 1.0 - f[rs]
            qd = (q_c * jnp.exp(bc)).astype(BF16)
            ki = (k_c * jnp.exp(-bc)).astype(BF16)
            kte = (k_c * jnp.exp(bl - bc)).astype(BF16)
            decay = jnp.exp(bl)
            v_c = proj_ref[ps, base + 2 * D_HGRN:base + 3 * D_HGRN].astype(BF16)
            g_c = proj_ref[ps, base + 3 * D_HGRN:base + 4 * D_HGRN]
            gate = og_ref[...] * (g_c * _sigmoid(g_c))
            for h in range(HGRN_HEADS):
                cs = slice(h * HGRN_DK, (h + 1) * HGRN_DK)
                scores = jnp.where(causal64, _dot_nt(qd[:, cs], ki[:, cs]), 0.0).astype(BF16)
                st = state_ref[h]
                o = _dot(scores, v_c[:, cs]) + _dot_nt(qd[:, cs], st.astype(BF16))
                state_ref[h] = st * decay[:, cs] + _dot_tn(v_c[:, cs], kte[:, cs])
                o = o * lax.rsqrt(jnp.mean(o * o, axis=-1, keepdims=True) + EPS)
                ycat_ref[ps, D_GMLP + h * HGRN_DK:D_GMLP + (h + 1) * HGRN_DK] = (o * gate[:, cs]).astype(BF16)

        o_ref[0, r0:r0 + n] = x + _dot(ycat_ref[r0:r0 + n], wout_ref[...])


def _mixer(x, norm_mix, w_in, gmlp_ln, w_s, b_s_t, beta, lb_logits, out_gain, w_out):
    B, S, D = x.shape
    const2 = lambda b, s: (0, 0)
    return pl.pallas_call(
        _mixer_kernel,
        grid=(B, S // MIX_ROWS),
        in_specs=[
            pl.BlockSpec((1, MIX_ROWS, D), lambda b, s: (b, s, 0)),
            pl.BlockSpec((1, D), const2),
            pl.BlockSpec(memory_space=pl.ANY),
            pl.BlockSpec((1, D_GMLP), const2),
            pl.BlockSpec((GMLP_GROUPS, GMLP_CHUNK, GMLP_CHUNK), lambda b, s: (0, 0, 0)),
            pl.BlockSpec((GMLP_CHUNK, GMLP_GROUPS), const2),
            pl.BlockSpec((1, D_GMLP), const2),
            pl.BlockSpec(lb_logits.shape, const2),
            pl.BlockSpec((1, D_HGRN), const2),
            pl.BlockSpec(memory_space=pl.ANY),
        ],
        out_specs=pl.BlockSpec((1, MIX_ROWS, D), lambda b, s: (b, s, 0)),
        out_shape=jax.ShapeDtypeStruct((B, S, D), F32),
        scratch_shapes=[
            pltpu.VMEM((MIX_ROWS, D_IN_PROJ), F32),
            pltpu.VMEM((MIX_ROWS, D), BF16),
            pltpu.VMEM((HGRN_HEADS, HGRN_DK, HGRN_DK), F32),
        ] + _weight_scratch(D, D_IN_PROJ) + _weight_scratch(D, D) + [pltpu.SemaphoreType.DMA],
        compiler_params=pltpu.CompilerParams(
            dimension_semantics=("arbitrary", "arbitrary"), vmem_limit_bytes=VMEM_LIMIT),
        name="mixer",
    )(x, norm_mix, w_in, gmlp_ln, w_s, b_s_t, beta, lb_logits, out_gain, w_out)


def _memkv_kernel(mem_ref, nm_ref, wkv_hbm, k_ref, v_ref, wkv_ref, wkv_stage, sem):
    @pl.when(pl.program_id(0) == 0)
    def _():
        _stage_weight(wkv_hbm, wkv_ref, wkv_stage, sem)

    m = _rms(mem_ref[0], nm_ref[...]).astype(BF16)
    kv = _dot(m, wkv_ref[...])
    k_ref[0] = kv[:, :D_MODEL].astype(BF16)
    v_ref[0] = kv[:, D_MODEL:].astype(BF16)


def _memkv(mem, norm_mem, w_kv):
    B, M, D = mem.shape
    out = jax.ShapeDtypeStruct((B, M, D), BF16)
    return pl.pallas_call(
        _memkv_kernel,
        grid=(B,),
        in_specs=[
            pl.BlockSpec((1, M, D), lambda b: (b, 0, 0)),
            pl.BlockSpec((1, D), lambda b: (0, 0)),
            pl.BlockSpec(memory_space=pl.ANY),
        ],
        out_specs=[pl.BlockSpec((1, M, D), lambda b: (b, 0, 0))] * 2,
        out_shape=[out, out],
        scratch_shapes=_weight_scratch(D, 2 * D) + [pltpu.SemaphoreType.DMA],
        compiler_params=pltpu.CompilerParams(
            dimension_semantics=("arbitrary",), vmem_limit_bytes=VMEM_LIMIT),
        name="memkv",
    )(mem, norm_mem, w_kv)


def _xattn_kernel(h_ref, nx_ref, wq_hbm, k_ref, v_ref, wo_hbm, nf_ref, wr_ref,
                  h2_ref, xn_ref, lg_ref, att_ref, wq_ref, wq_stage, wo_ref, wo_stage, sem):
    @pl.when((pl.program_id(0) == 0) & (pl.program_id(1) == 0))
    def _():
        _stage_weight(wq_hbm, wq_ref, wq_stage, sem)
        _stage_weight(wo_hbm, wo_ref, wo_stage, sem)

    for c in range(ATT_ROWS // ATT_CHAIN_ROWS):
        rs = slice(c * ATT_CHAIN_ROWS, (c + 1) * ATT_CHAIN_ROWS)
        h = h_ref[0, rs]
        hn = _rms(h, nx_ref[...]).astype(BF16)
        q = (_dot(hn, wq_ref[...]) * (XATTN_HEAD_DIM ** -0.5)).astype(BF16)
        for hd in range(XATTN_HEADS):
            cs = slice(hd * XATTN_HEAD_DIM, (hd + 1) * XATTN_HEAD_DIM)
            s = _dot_nt(q[:, cs], k_ref[0, :, cs])
            p = jnp.exp(s - jnp.max(s, axis=-1, keepdims=True))
            p = (p / jnp.sum(p, axis=-1, keepdims=True)).astype(BF16)
            att_ref[rs, cs] = _dot(p, v_ref[0, :, cs]).astype(BF16)
        h2 = h + _dot(att_ref[rs], wo_ref[...])
        h2_ref[0, rs] = h2
        xn = _rms(h2, nf_ref[...])
        xn_ref[rs] = _pack_pairs(xn)
        lg_ref[:, rs] = _dot_nt(wr_ref[...], xn.astype(BF16))


def _xattn(h1, norm_x, w_q, k_mem, v_mem, w_o, norm_ffn, w_router, batch0, batches):
    _, S, D = h1.shape
    n_s = S // ATT_ROWS
    const2 = lambda b, s: (0, 0)
    return pl.pallas_call(
        _xattn_kernel,
        grid=(batches, n_s),
        in_specs=[
            pl.BlockSpec((1, ATT_ROWS, D), lambda b, s: (b + batch0, s, 0)),
            pl.BlockSpec((1, D), const2),
            pl.BlockSpec(memory_space=pl.ANY),
            pl.BlockSpec((1, N_MEM, D), lambda b, s: (b + batch0, 0, 0)),
            pl.BlockSpec((1, N_MEM, D), lambda b, s: (b + batch0, 0, 0)),
            pl.BlockSpec(memory_space=pl.ANY),
            pl.BlockSpec((1, D), const2),
            pl.BlockSpec((ROUTER_ROWS, D), const2),
        ],
        out_specs=[
            pl.BlockSpec((1, ATT_ROWS, D), lambda b, s: (b, s, 0)),
            pl.BlockSpec((ATT_ROWS, D // 2), lambda b, s: (b * n_s + s, 0)),
            pl.BlockSpec((ROUTER_ROWS, ATT_ROWS), lambda b, s: (0, b * n_s + s)),
        ],
        out_shape=[
            jax.ShapeDtypeStruct((batches, S, D), F32),
            jax.ShapeDtypeStruct((batches * S, D // 2), jnp.uint32),
            jax.ShapeDtypeStruct((ROUTER_ROWS, batches * S), F32),
        ],
        scratch_shapes=([pltpu.VMEM((ATT_ROWS, D), BF16)] + _weight_scratch(D, D) + _weight_scratch(D, D)
                        + [pltpu.SemaphoreType.DMA]),
        compiler_params=pltpu.CompilerParams(
            dimension_semantics=("arbitrary", "arbitrary"), vmem_limit_bytes=VMEM_LIMIT),
        name="xattn",
    )(h1, norm_x, w_q, k_mem, v_mem, w_o, norm_ffn, w_router)


def _route_kernel(bias_ref, lg_ref, ids_ref, gates_ref, rank_ref, cnt_ref, base_ref):
    sub = lg_ref.shape[1]

    @pl.when(pl.program_id(0) == 0)
    def _():
        base_ref[...] = jnp.zeros_like(base_ref)

    best = lg_ref[0] + bias_ref[0]
    gl = [best]
    sel = jnp.zeros(best.shape, jnp.int32)
    for g in range(1, N_GROUPS):
        cur = lg_ref[g] + bias_ref[g]
        gl.append(cur)
        better = cur > best
        best = jnp.where(better, cur, best)
        sel = jnp.where(better, g, sel)
    denom = jnp.exp(gl[0] - best)
    for g in range(1, N_GROUPS):
        denom = denom + jnp.exp(gl[g] - best)
    g_gate = 1.0 / denom

    ev = []
    for j in range(EXPERTS_PER_GROUP):
        val = lg_ref[N_GROUPS + j] + bias_ref[N_GROUPS + j]
        for g in range(1, N_GROUPS):
            e = g * EXPERTS_PER_GROUP + j
            val = jnp.where(sel == g, lg_ref[N_GROUPS + e] + bias_ref[N_GROUPS + e], val)
        ev.append(val)
    v1, i1 = ev[0], jnp.zeros(best.shape, jnp.int32)
    for j in range(1, EXPERTS_PER_GROUP):
        better = ev[j] > v1
        v1 = jnp.where(better, ev[j], v1)
        i1 = jnp.where(better, j, i1)
    rest = [jnp.where(i1 == j, -jnp.inf, ev[j]) for j in range(EXPERTS_PER_GROUP)]
    v2, i2 = rest[0], jnp.zeros(best.shape, jnp.int32)
    for j in range(1, EXPERTS_PER_GROUP):
        better = rest[j] > v2
        v2 = jnp.where(better, rest[j], v2)
        i2 = jnp.where(better, j, i2)
    e2 = jnp.exp(v2 - v1)
    inv = 1.0 / (1.0 + e2)
    id1 = sel * EXPERTS_PER_GROUP + i1
    id2 = sel * EXPERTS_PER_GROUP + i2
    ids_ref[0] = id1
    ids_ref[1] = id2
    gates_ref[0] = inv * g_gate
    gates_ref[1] = e2 * inv * g_gate

    member = jnp.concatenate(
        [jnp.where((id1 == e) | (id2 == e), 1.0, 0.0) for e in range(N_EXPERTS)], axis=0).astype(BF16)
    n = N_EXPERTS * sub
    li = lax.broadcasted_iota(jnp.int32, (LANES, LANES), 0)
    lj = lax.broadcasted_iota(jnp.int32, (LANES, LANES), 1)
    before_lane = jnp.where(li < lj, 1.0, 0.0).astype(BF16)
    ones = jnp.ones((LANES, LANES), BF16)
    ri = lax.broadcasted_iota(jnp.int32, (n, n), 0)
    rj = lax.broadcasted_iota(jnp.int32, (n, n), 1)
    same = _block_id(ri, sub) == _block_id(rj, sub)
    before_row = jnp.where(same & (rj < ri), 1.0, 0.0).astype(BF16)
    all_row = jnp.where(same, 1.0, 0.0).astype(BF16)
    in_row = _dot(member, before_lane)
    prev_rows = _dot(_dot(before_row, member).astype(BF16), ones)
    total = _dot(_dot(all_row, member).astype(BF16), ones)
    base = base_ref[...]
    pos = base + prev_rows + in_row
    r1 = jnp.zeros(best.shape, F32)
    r2 = jnp.zeros(best.shape, F32)
    for e in range(N_EXPERTS):
        pe = pos[e * sub:(e + 1) * sub]
        r1 = jnp.where(id1 == e, pe, r1)
        r2 = jnp.where(id2 == e, pe, r2)
    rank_ref[0] = r1.astype(jnp.int32)
    rank_ref[1] = r2.astype(jnp.int32)
    base_ref[...] = base + total
    cnt_ref[...] = base + total


def _route(bias, logits3):
    rp, n_sub, _ = logits3.shape
    blk = lambda i: (0, i, 0)
    pair_i = jax.ShapeDtypeStruct((2, n_sub, LANES), jnp.int32)
    return pl.pallas_call(
        _route_kernel,
        grid=(n_sub // ROUTE_SUB,),
        in_specs=[
            pl.BlockSpec(memory_space=pltpu.SMEM),
            pl.BlockSpec((rp, ROUTE_SUB, LANES), blk),
        ],
        out_specs=[
            pl.BlockSpec((2, ROUTE_SUB, LANES), blk),
            pl.BlockSpec((2, ROUTE_SUB, LANES), blk),
            pl.BlockSpec((2, ROUTE_SUB, LANES), blk),
            pl.BlockSpec((N_EXPERTS * ROUTE_SUB, LANES), lambda i: (0, 0)),
        ],
        out_shape=[
            pair_i,
            jax.ShapeDtypeStruct((2, n_sub, LANES), F32),
            pair_i,
            jax.ShapeDtypeStruct((N_EXPERTS * ROUTE_SUB, LANES), F32),
        ],
        scratch_shapes=[pltpu.VMEM((N_EXPERTS * ROUTE_SUB, LANES), F32)],
        compiler_params=pltpu.CompilerParams(
            dimension_semantics=("arbitrary",), vmem_limit_bytes=VMEM_LIMIT),
        name="route",
    )(bias, logits3)


def _dest_kernel(start_ref, ids_ref, rank_ref, dest_ref):
    ids = ids_ref[...]
    off = jnp.zeros(ids.shape, jnp.int32)
    for e in range(N_EXPERTS):
        off = jnp.where(ids == e, start_ref[e], off)
    dest_ref[...] = rank_ref[...] + off


def _dest(seg_start, ids, rank):
    _, n_sub, _ = ids.shape
    blk = pl.BlockSpec((2, ROUTE_SUB, LANES), lambda i: (0, i, 0))
    return pl.pallas_call(
        _dest_kernel,
        grid=(n_sub // ROUTE_SUB,),
        in_specs=[pl.BlockSpec(memory_space=pltpu.SMEM), blk, blk],
        out_specs=blk,
        out_shape=jax.ShapeDtypeStruct(ids.shape, jnp.int32),
        name="dest",
    )(seg_start, ids, rank)


def _sc_mesh():
    return plsc.VectorSubcoreMesh(core_axis_name="core", subcore_axis_name="subcore")


def _sc_worker(rows_total):
    rows = rows_total // _SC_WORKERS
    wid = lax.axis_index("core") * SC_SUBCORES + lax.axis_index("subcore")
    return wid * rows, rows


def _dispatch(dest0, dest1, xn, n_slots):
    T, D = xn.shape
    W = SC_WINDOW
    rows = T // _SC_WORKERS

    @pl.kernel(out_type=jax.ShapeDtypeStruct((n_slots, D), xn.dtype), mesh=_sc_mesh(),
               scratch_types=[pltpu.VMEM((rows,), jnp.int32), pltpu.VMEM((rows,), jnp.int32),
                              pltpu.VMEM((2, W, D), xn.dtype),
                              pltpu.SemaphoreType.DMA((2,)), pltpu.SemaphoreType.DMA((2,))])
    def scatter_rows(x_hbm, d0_hbm, d1_hbm, xs_hbm, d0_v, d1_v, buf, in_sem, out_sem):
        base, _ = _sc_worker(T)
        pltpu.sync_copy(d0_hbm.at[pl.ds(base, rows)], d0_v)
        pltpu.sync_copy(d1_hbm.at[pl.ds(base, rows)], d1_v)

        def load(w, slot):
            return pltpu.make_async_copy(x_hbm.at[pl.ds(base + w * W, W)], buf.at[slot], in_sem.at[slot])

        def store(w, slot, d_v):
            return pltpu.make_async_copy(buf.at[slot], xs_hbm.at[d_v.at[pl.ds(w * W, W)]], out_sem.at[slot])

        def step(w, slot):
            load(w, slot).wait()
            store(w, slot, d0_v).start()
            store(w, slot, d1_v).start()
            store(w, slot, d0_v).wait()
            store(w, slot, d1_v).wait()

        n = rows // W
        load(0, 0).start()

        @pl.loop(0, n, step=2)
        def _(w):
            load(w + 1, 1).start()
            step(w, 0)

            @pl.when(w + 2 < n)
            def _():
                load(w + 2, 0).start()

            step(w + 1, 1)

    return scatter_rows(xn, dest0, dest1)


def _gather_rows(src, idx):
    M = idx.shape[0]
    D = src.shape[1]
    W = SC_WINDOW
    rows = M // _SC_WORKERS

    @pl.kernel(out_type=jax.ShapeDtypeStruct((M, D), src.dtype), mesh=_sc_mesh(),
               scratch_types=[pltpu.VMEM((rows,), jnp.int32), pltpu.VMEM((2, W, D), src.dtype),
                              pltpu.SemaphoreType.DMA((2,)), pltpu.SemaphoreType.DMA((2,))])
    def gather_rows(src_hbm, i_hbm, o_hbm, i_v, buf, in_sem, out_sem):
        base, _ = _sc_worker(M)
        pltpu.sync_copy(i_hbm.at[pl.ds(base, rows)], i_v)

        def load(w, slot):
            return pltpu.make_async_copy(src_hbm.at[i_v.at[pl.ds(w * W, W)]], buf.at[slot], in_sem.at[slot])

        def store(w, slot):
            return pltpu.make_async_copy(buf.at[slot], o_hbm.at[pl.ds(base + w * W, W)], out_sem.at[slot])

        n = rows // W
        load(0, 0).start()

        @pl.loop(0, n, step=2)
        def _(w):
            @pl.when(w > 0)
            def _():
                store(w - 1, 1).wait()

            load(w + 1, 1).start()
            load(w, 0).wait()
            store(w, 0).start()
            store(w, 0).wait()

            @pl.when(w + 2 < n)
            def _():
                load(w + 2, 0).start()

            load(w + 1, 1).wait()
            store(w + 1, 1).start()

        store(n - 1, 1).wait()

    return gather_rows(src, idx)


def _expert_kernel(be_ref, nb_ref, slot_ref, next_ref, x_ref, wg_hbm, wu_hbm, wd_hbm, y_ref,
                   wg_f32, wu_f32, wd_f32, wg_ref, wu_ref, wd_ref, sem):
    i = pl.program_id(0)
    used = i < nb_ref[0]
    new_expert = (i == 0) | (be_ref[i] != be_ref[jnp.maximum(i - 1, 0)])

    def fetch(e, slot):
        pairs = ((wg_hbm, wg_f32), (wu_hbm, wu_f32), (wd_hbm, wd_f32))
        return [pltpu.make_async_copy(w_hbm.at[e], w_f32.at[slot], sem.at[slot, k])
                for k, (w_hbm, w_f32) in enumerate(pairs)]

    @pl.when(used & (i == 0))
    def _():
        for copy in fetch(be_ref[0], slot_ref[0]):
            copy.start()

    @pl.when(used & new_expert)
    def _():
        slot = slot_ref[i]
        for copy in fetch(be_ref[i], slot):
            copy.wait()

        @pl.when(next_ref[i] >= 0)
        def _():
            for copy in fetch(next_ref[i], 1 - slot):
                copy.start()

        wg_ref[...] = wg_f32[slot].astype(BF16)
        wu_ref[...] = wu_f32[slot].astype(BF16)
        wd_ref[...] = wd_f32[slot].astype(BF16)

    @pl.when(used)
    def _():
        for c in range(EXPERT_ROWS // EXPERT_CHAIN_ROWS):
            rs = slice(c * EXPERT_CHAIN_ROWS, (c + 1) * EXPERT_CHAIN_ROWS)
            x = _unpack_pairs(x_ref[rs]).astype(BF16)
            g = _dot(x, wg_ref[...])
            u = _dot(x, wu_ref[...])
            hid = (g * _sigmoid(g) * u).astype(BF16)
            y_ref[rs] = _pack_pairs(_dot(hid, wd_ref[...]))

    @pl.when(jnp.logical_not(used))
    def _():
        y_ref[...] = jnp.zeros_like(y_ref)


def _experts(block_expert, n_used, block_slot, block_next, xs, w_gate, w_up, w_down):
    n_slots, half = xs.shape
    D = 2 * half
    n_blocks = n_slots // EXPERT_ROWS
    grid_spec = pltpu.PrefetchScalarGridSpec(
        num_scalar_prefetch=4,
        grid=(n_blocks,),
        in_specs=[
            pl.BlockSpec((EXPERT_ROWS, half), lambda i, be, nb, sl, nx: (jnp.minimum(i, nb[0] - 1), 0)),
            pl.BlockSpec(memory_space=pl.ANY),
            pl.BlockSpec(memory_space=pl.ANY),
            pl.BlockSpec(memory_space=pl.ANY),
        ],
        out_specs=pl.BlockSpec((EXPERT_ROWS, half), lambda i, be, nb, sl, nx: (i, 0)),
        scratch_shapes=[
            pltpu.VMEM((2, D, D_EXPERT), F32), pltpu.VMEM((2, D, D_EXPERT), F32),
            pltpu.VMEM((2, D_EXPERT, D), F32),
            pltpu.VMEM((D, D_EXPERT), BF16), pltpu.VMEM((D, D_EXPERT), BF16), pltpu.VMEM((D_EXPERT, D), BF16),
            pltpu.SemaphoreType.DMA((2, 3)),
        ],
    )
    return pl.pallas_call(
        _expert_kernel,
        grid_spec=grid_spec,
        out_shape=jax.ShapeDtypeStruct((n_slots, half), jnp.uint32),
        compiler_params=pltpu.CompilerParams(
            dimension_semantics=("arbitrary",), vmem_limit_bytes=VMEM_LIMIT),
        name="experts",
    )(block_expert, n_used, block_slot, block_next, xs, w_gate, w_up, w_down)


def _combine_kernel(y0_ref, y1_ref, h_ref, gates_ref, nfin_ref, *rest):
    o_ref = rest[-1]
    gates = gates_ref[...]
    h = (h_ref[...] + gates[:, 0:1] * _unpack_pairs(y0_ref[...])
         + gates[:, 1:2] * _unpack_pairs(y1_ref[...]))
    o_ref[...] = _rms(h, nfin_ref[...])


def _combine(y01, h2, gates_t, norm_final, out_prev, part, parts):
    T, D = h2.shape
    n_t = T // MOVE_ROWS
    in_specs = [
        pl.BlockSpec((MOVE_ROWS, D // 2), lambda i: (i, 0)),
        pl.BlockSpec((MOVE_ROWS, D // 2), lambda i: (i + n_t, 0)),
        pl.BlockSpec((MOVE_ROWS, D), lambda i: (i, 0)),
        pl.BlockSpec((MOVE_ROWS, 2), lambda i: (i, 0)),
        pl.BlockSpec((1, D), lambda i: (0, 0)),
    ]
    args = [y01, y01, h2, gates_t, norm_final]
    aliases = {}
    if out_prev is not None:
        in_specs.append(pl.BlockSpec(memory_space=pl.ANY))
        args.append(out_prev)
        aliases = {len(args) - 1: 0}
    return pl.pallas_call(
        _combine_kernel,
        grid=(n_t,),
        in_specs=in_specs,
        out_specs=pl.BlockSpec((MOVE_ROWS, D), lambda i: (i + part * n_t, 0)),
        out_shape=jax.ShapeDtypeStruct((parts * T, D), F32),
        input_output_aliases=aliases,
        compiler_params=pltpu.CompilerParams(
            dimension_semantics=("arbitrary",), vmem_limit_bytes=VMEM_LIMIT),
        name="combine",
    )(*args)


def kernel(x, mem, norm_mix, w_in, gmlp_ln, gmlp_w_spatial, gmlp_b_spatial, gmlp_beta, hgrn_lb_logits, hgrn_out_gain, w_out, norm_xattn, norm_mem, w_xq, w_xkv, w_xo, norm_ffn, w_router_group, b_router_group, w_router_expert, b_router_expert, w_expert_gate, w_expert_up, w_expert_down, norm_final):
    B, S, D = x.shape
    T = B * S
    depth = w_in.shape[0]
    assert depth == 1 and hgrn_lb_logits.shape[0] == 2
    l = 0
    row = lambda p: p.reshape(1, -1)

    h1 = _mixer(x, row(norm_mix[l]), w_in[l], row(gmlp_ln[l]), gmlp_w_spatial[l],
                gmlp_b_spatial[l].T, row(gmlp_beta[l]), hgrn_lb_logits, row(hgrn_out_gain[l]),
                w_out[l])
    k_mem, v_mem = _memkv(mem, row(norm_mem[l]), w_xkv[l])

    w_router = jnp.concatenate([w_router_group[l].T, w_router_expert[l].T], axis=0)
    w_router = jnp.pad(w_router, ((0, ROUTER_ROWS - w_router.shape[0]), (0, 0)))
    w_router = w_router.astype(BF16)
    bias = jnp.concatenate([b_router_group[l], b_router_expert[l]]).astype(F32)

    b_part = B // TOKEN_PARTS
    t_part = b_part * S
    n_blocks = (2 * t_part) // EXPERT_ROWS + N_EXPERTS
    out = None
    for part in range(TOKEN_PARTS):
        h2, xn, logits = _xattn(h1, row(norm_xattn[l]), w_xq[l], k_mem, v_mem, w_xo[l], row(norm_ffn[l]),
                                w_router, part * b_part, b_part)
        ids, gates, rank, counts = _route(bias, logits.reshape(ROUTER_ROWS, t_part // LANES, LANES))

        counts = counts[::ROUTE_SUB, 0].astype(jnp.int32)
        padded = (counts + EXPERT_ROWS - 1) // EXPERT_ROWS * EXPERT_ROWS
        seg_end = jnp.cumsum(padded)
        seg_start = seg_end - padded
        block_first_row = jnp.arange(n_blocks, dtype=jnp.int32) * EXPERT_ROWS
        block_expert = jnp.minimum(
            jnp.sum(block_first_row[:, None] >= seg_end[None, :], axis=1), N_EXPERTS - 1).astype(jnp.int32)
        n_used = (seg_end[-1:] // EXPERT_ROWS).astype(jnp.int32)
        present = counts > 0
        expert_ids = jnp.arange(N_EXPERTS, dtype=jnp.int32)
        expert_slot = (jnp.cumsum(present.astype(jnp.int32)) - 1) % 2
        later = jnp.where(present[None, :] & (expert_ids[None, :] > expert_ids[:, None]),
                          expert_ids[None, :], N_EXPERTS)
        expert_next = jnp.min(later, axis=1)
        expert_next = jnp.where(expert_next == N_EXPERTS, -1, expert_next).astype(jnp.int32)
        block_slot = expert_slot[block_expert].astype(jnp.int32)
        block_next = expert_next[block_expert]

        dest = _dest(seg_start, ids, rank).reshape(2, t_part)
        xs = _dispatch(dest[0], dest[1], xn, n_blocks * EXPERT_ROWS)
        yb = _experts(block_expert, n_used, block_slot, block_next, xs,
                      w_expert_gate[l], w_expert_up[l], w_expert_down[l])
        y01 = _gather_rows(yb, dest.reshape(2 * t_part))
        out = _combine(y01, h2.reshape(t_part, D), gates.reshape(2, t_part).T, row(norm_final),
                       out, part, TOKEN_PARTS)
    return out.reshape(B, S, D)
```

```python
import functools

import jax
import jax.numpy as jnp
from jax import lax
from jax.experimental import pallas as pl
from jax.experimental.pallas import tpu as pltpu
from jax.experimental.pallas import tpu_sc as plsc

F32 = jnp.float32
BF16 = jnp.bfloat16
EPS = 1e-6

D_MODEL = 1024
D_GMLP = 512
GMLP_GROUPS = 4
GMLP_CHUNK = 128
D_HGRN = 512
HGRN_HEADS = 4
HGRN_DK = 128
HGRN_CHUNK = 64
D_IN_PROJ = 2 * D_GMLP + 4 * D_HGRN
N_MEM = 256
XATTN_HEADS = 4
XATTN_HEAD_DIM = D_MODEL // XATTN_HEADS
N_GROUPS = 4
EXPERTS_PER_GROUP = 8
N_EXPERTS = N_GROUPS * EXPERTS_PER_GROUP
D_EXPERT = 512

LANES = 128
MIX_ROWS = 512
MIX_CHAIN_ROWS = 256
ATT_ROWS = 1024
ATT_CHAIN_ROWS = 1024
WEIGHT_STAGE_ROWS = 128
ROUTER_ROWS = 40
ROUTE_SUB = 16
EXPERT_ROWS = 512
EXPERT_CHAIN_ROWS = 256
MOVE_ROWS = 1024
TOKEN_PARTS = 2
SC_WINDOW = 64
SC_CORES = 2
SC_SUBCORES = 16
_SC_WORKERS = SC_CORES * SC_SUBCORES
VMEM_LIMIT = 48 * 1024 * 1024


def _rms(x, gain):
    return x * lax.rsqrt(jnp.mean(x * x, axis=-1, keepdims=True) + EPS) * gain


def _dot(a, b):
    return jnp.dot(a, b, preferred_element_type=F32)


def _dot_nt(a, b):
    return lax.dot_general(a, b, (((1,), (1,)), ((), ())), preferred_element_type=F32)


def _dot_tn(a, b):
    return lax.dot_general(a, b, (((0,), (0,)), ((), ())), preferred_element_type=F32)


def _gelu(x):
    return 0.5 * x * (1.0 + jnp.tanh(0.7978845608028654 * (x + 0.044715 * (x * x * x))))


def _sigmoid(x):
    return 1.0 / (1.0 + jnp.exp(-x))


def _block_id(idx, size):
    assert size & (size - 1) == 0
    return lax.shift_right_logical(idx, size.bit_length() - 1)


def _stage_weight(w_hbm, w_bf16, stage_ref, sem):
    rows = stage_ref.shape[0]
    for k in range(w_hbm.shape[0] // rows):
        copy = pltpu.make_async_copy(w_hbm.at[pl.ds(k * rows, rows)], stage_ref, sem)
        copy.start()
        copy.wait()
        w_bf16[k * rows:(k + 1) * rows, :] = stage_ref[...].astype(BF16)


def _weight_scratch(k, n):
    return [pltpu.VMEM((k, n), BF16), pltpu.VMEM((WEIGHT_STAGE_ROWS, n), F32)]


_HIGH_HALF = 0xFFFF0000


def _pack_pairs(x):
    c = x.shape[1] // 2
    bits = lax.bitcast_convert_type(x.astype(BF16).astype(F32), jnp.uint32)
    return (bits[:, c:] & jnp.uint32(_HIGH_HALF)) | lax.shift_right_logical(bits[:, :c], jnp.uint32(16))


def _unpack_pairs(w):
    lo = lax.bitcast_convert_type(lax.shift_left(w, jnp.uint32(16)), F32)
    hi = lax.bitcast_convert_type(w & jnp.uint32(_HIGH_HALF), F32)
    return jnp.concatenate([lo, hi], axis=1)


def _mixer_kernel(x_ref, nmix_ref, win_hbm, gln_ref, ws_ref, bst_ref, beta_ref, lbl_ref, og_ref,
                  wout_hbm, o_ref, proj_ref, ycat_ref, state_ref, win_ref, win_stage, wout_ref, wout_stage,
                  sem):
    @pl.when((pl.program_id(0) == 0) & (pl.program_id(1) == 0))
    def _():
        _stage_weight(win_hbm, win_ref, win_stage, sem)
        _stage_weight(wout_hbm, wout_ref, wout_stage, sem)

    @pl.when(pl.program_id(1) == 0)
    def _():
        state_ref[...] = jnp.zeros_like(state_ref)

    n = MIX_CHAIN_ROWS
    r_i = lax.broadcasted_iota(jnp.int32, (GMLP_CHUNK, GMLP_CHUNK), 0)
    c_i = lax.broadcasted_iota(jnp.int32, (GMLP_CHUNK, GMLP_CHUNK), 1)
    causal = c_i <= r_i
    w_tril = [jnp.where(causal, ws_ref[g], 0.0).astype(BF16) for g in range(GMLP_GROUPS)]
    lbl = lbl_ref[...]
    e_lb = jnp.exp(lbl - jnp.max(lbl, axis=0, keepdims=True))
    lb = e_lb[0:1] / jnp.sum(e_lb, axis=0, keepdims=True)
    rr = lax.broadcasted_iota(jnp.int32, (n, n), 0)
    cc = lax.broadcasted_iota(jnp.int32, (n, n), 1)
    tri = jnp.where((_block_id(rr, HGRN_CHUNK) == _block_id(cc, HGRN_CHUNK)) & (cc <= rr),
                    1.0, 0.0).astype(BF16)
    r64 = lax.broadcasted_iota(jnp.int32, (HGRN_CHUNK, HGRN_CHUNK), 0)
    c64 = lax.broadcasted_iota(jnp.int32, (HGRN_CHUNK, HGRN_CHUNK), 1)
    causal64 = c64 <= r64
    base = 2 * D_GMLP

    for chain in range(x_ref.shape[1] // n):
        r0 = chain * n
        x = x_ref[0, r0:r0 + n]
        a = _rms(x, nmix_ref[...]).astype(BF16)
        proj_ref[r0:r0 + n] = _dot(a, win_ref[...])

        u = _gelu(proj_ref[r0:r0 + n, 0:D_GMLP])
        v = _gelu(proj_ref[r0:r0 + n, D_GMLP:2 * D_GMLP])
        vc = v - jnp.mean(v, axis=-1, keepdims=True)
        vn = (vc * lax.rsqrt(jnp.mean(vc * vc, axis=-1, keepdims=True) + EPS) * gln_ref[...]).astype(BF16)
        z_rows = []
        for c in range(n // GMLP_CHUNK):
            z_cols = []
            for g in range(GMLP_GROUPS):
                vg = vn[c * GMLP_CHUNK:(c + 1) * GMLP_CHUNK, g * LANES:(g + 1) * LANES]
                z_cols.append(_dot(w_tril[g], vg) + bst_ref[:, g:g + 1])
            z_rows.append(jnp.concatenate(z_cols, axis=1))
        z = jnp.concatenate(z_rows, axis=0)
        ycat_ref[r0:r0 + n, 0:D_GMLP] = _rms(u * z, beta_ref[...]).astype(BF16)

        f = lb + (1.0 - lb) * _sigmoid(proj_ref[r0:r0 + n, base + D_HGRN:base + 2 * D_HGRN])
        log_f = jnp.log(f)
        lf_hi = log_f.astype(BF16)
        lf_lo = (log_f - lf_hi.astype(F32)).astype(BF16)
        b_all = _dot(tri, lf_hi) + _dot(tri, lf_lo)
        for c in range(n // HGRN_CHUNK):
            rs = slice(c * HGRN_CHUNK, (c + 1) * HGRN_CHUNK)
            ps = slice(r0 + c * HGRN_CHUNK, r0 + (c + 1) * HGRN_CHUNK)
            bc = b_all[rs]
            bl = bc[HGRN_CHUNK - 1:HGRN_CHUNK]
            q_c = proj_ref[ps, base:base + D_HGRN]
            q_c = q_c * _sigmoid(q_c)
            k_c = 1.0 - f[rs]
            qd = (q_c * jnp.exp(bc)).astype(BF16)
            ki = (k_c * jnp.exp(-bc)).astype(BF16)
            kte = (k_c * jnp.exp(bl - bc)).astype(BF16)
            decay = jnp.exp(bl)
            v_c = proj_ref[ps, base + 2 * D_HGRN:base + 3 * D_HGRN].astype(BF16)
            g_c = proj_ref[ps, base + 3 * D_HGRN:base + 4 * D_HGRN]
            gate = og_ref[...] * (g_c * _sigmoid(g_c))
            for h in range(HGRN_HEADS):
                cs = slice(h * HGRN_DK, (h + 1) * HGRN_DK)
                scores = jnp.where(causal64, _dot_nt(qd[:, cs], ki[:, cs]), 0.0).astype(BF16)
                st = state_ref[h]
                o = _dot(scores, v_c[:, cs]) + _dot_nt(qd[:, cs], st.astype(BF16))
                state_ref[h] = st * decay[:, cs] + _dot_tn(v_c[:, cs], kte[:, cs])
                o = o * lax.rsqrt(jnp.mean(o * o, axis=-1, keepdims=True) + EPS)
                ycat_ref[ps, D_GMLP + h * HGRN_DK:D_GMLP + (h + 1) * HGRN_DK] = (o * gate[:, cs]).astype(BF16)

        o_ref[0, r0:r0 + n] = x + _dot(ycat_ref[r0:r0 + n], wout_ref[...])


def _mixer(x, norm_mix, w_in, gmlp_ln, w_s, b_s_t, beta, lb_logits, out_gain, w_out):
    B, S, D = x.shape
    const2 = lambda b, s: (0, 0)
    return pl.pallas_call(
        _mixer_kernel,
        grid=(B, S // MIX_ROWS),
        in_specs=[
            pl.BlockSpec((1, MIX_ROWS, D), lambda b, s: (b, s, 0)),
            pl.BlockSpec((1, D), const2),
            pl.BlockSpec(memory_space=pl.ANY),
            pl.BlockSpec((1, D_GMLP), const2),
            pl.BlockSpec((GMLP_GROUPS, GMLP_CHUNK, GMLP_CHUNK), lambda b, s: (0, 0, 0)),
            pl.BlockSpec((GMLP_CHUNK, GMLP_GROUPS), const2),
            pl.BlockSpec((1, D_GMLP), const2),
            pl.BlockSpec(lb_logits.shape, const2),
            pl.BlockSpec((1, D_HGRN), const2),
            pl.BlockSpec(memory_space=pl.ANY),
        ],
        out_specs=pl.BlockSpec((1, MIX_ROWS, D), lambda b, s: (b, s, 0)),
        out_shape=jax.ShapeDtypeStruct((B, S, D), F32),
        scratch_shapes=[
            pltpu.VMEM((MIX_ROWS, D_IN_PROJ), F32),
            pltpu.VMEM((MIX_ROWS, D), BF16),
            pltpu.VMEM((HGRN_HEADS, HGRN_DK, HGRN_DK), F32),
        ] + _weight_scratch(D, D_IN_PROJ) + _weight_scratch(D, D) + [pltpu.SemaphoreType.DMA],
        compiler_params=pltpu.CompilerParams(
            dimension_semantics=("arbitrary", "arbitrary"), vmem_limit_bytes=VMEM_LIMIT),
        name="mixer",
    )(x, norm_mix, w_in, gmlp_ln, w_s, b_s_t, beta, lb_logits, out_gain, w_out)


def _memkv_kernel(mem_ref, nm_ref, wkv_hbm, k_ref, v_ref, wkv_ref, wkv_stage, sem):
    @pl.when(pl.program_id(0) == 0)
    def _():
        _stage_weight(wkv_hbm, wkv_ref, wkv_stage, sem)

    m = _rms(mem_ref[0], nm_ref[...]).astype(BF16)
    kv = _dot(m, wkv_ref[...])
    k_ref[0] = kv[:, :D_MODEL].astype(BF16)
    v_ref[0] = kv[:, D_MODEL:].astype(BF16)


def _memkv(mem, norm_mem, w_kv):
    B, M, D = mem.shape
    out = jax.ShapeDtypeStruct((B, M, D), BF16)
    return pl.pallas_call(
        _memkv_kernel,
        grid=(B,),
        in_specs=[
            pl.BlockSpec((1, M, D), lambda b: (b, 0, 0)),
            pl.BlockSpec((1, D), lambda b: (0, 0)),
            pl.BlockSpec(memory_space=pl.ANY),
        ],
        out_specs=[pl.BlockSpec((1, M, D), lambda b: (b, 0, 0))] * 2,
        out_shape=[out, out],
        scratch_shapes=_weight_scratch(D, 2 * D) + [pltpu.SemaphoreType.DMA],
        compiler_params=pltpu.CompilerParams(
            dimension_semantics=("arbitrary",), vmem_limit_bytes=VMEM_LIMIT),
        name="memkv",
    )(mem, norm_mem, w_kv)


def _xattn_kernel(h_ref, nx_ref, wq_hbm, k_ref, v_ref, wo_hbm, nf_ref, wr_ref,
                  h2_ref, xn_ref, lg_ref, att_ref, wq_ref, wq_stage, wo_ref, wo_stage, sem):
    @pl.when((pl.program_id(0) == 0) & (pl.program_id(1) == 0))
    def _():
        _stage_weight(wq_hbm, wq_ref, wq_stage, sem)
        _stage_weight(wo_hbm, wo_ref, wo_stage, sem)

    for c in range(ATT_ROWS // ATT_CHAIN_ROWS):
        rs = slice(c * ATT_CHAIN_ROWS, (c + 1) * ATT_CHAIN_ROWS)
        h = h_ref[0, rs]
        hn = _rms(h, nx_ref[...]).astype(BF16)
        q = (_dot(hn, wq_ref[...]) * (XATTN_HEAD_DIM ** -0.5)).astype(BF16)
        for hd in range(XATTN_HEADS):
            cs = slice(hd * XATTN_HEAD_DIM, (hd + 1) * XATTN_HEAD_DIM)
            s = _dot_nt(q[:, cs], k_ref[0, :, cs])
            p = jnp.exp(s - jnp.max(s, axis=-1, keepdims=True))
            p = (p / jnp.sum(p, axis=-1, keepdims=True)).astype(BF16)
            att_ref[rs, cs] = _dot(p, v_ref[0, :, cs]).astype(BF16)
        h2 = h + _dot(att_ref[rs], wo_ref[...])
        h2_ref[0, rs] = h2
        xn = _rms(h2, nf_ref[...])
        xn_ref[rs] = _pack_pairs(xn)
        lg = _dot_nt(wr_ref[...], xn.astype(BF16))
        for j in range(ATT_CHAIN_ROWS // LANES):
            lg_ref[:, c * (ATT_CHAIN_ROWS // LANES) + j, :] = lg[:, j * LANES:(j + 1) * LANES]


def _xattn(h1, norm_x, w_q, k_mem, v_mem, w_o, norm_ffn, w_router, batch0, batches):
    _, S, D = h1.shape
    n_s = S // ATT_ROWS
    const2 = lambda b, s: (0, 0)
    return pl.pallas_call(
        _xattn_kernel,
        grid=(batches, n_s),
        in_specs=[
            pl.BlockSpec((1, ATT_ROWS, D), lambda b, s: (b + batch0, s, 0)),
            pl.BlockSpec((1, D), const2),
            pl.BlockSpec(memory_space=pl.ANY),
            pl.BlockSpec((1, N_MEM, D), lambda b, s: (b + batch0, 0, 0)),
            pl.BlockSpec((1, N_MEM, D), lambda b, s: (b + batch0, 0, 0)),
            pl.BlockSpec(memory_space=pl.ANY),
            pl.BlockSpec((1, D), const2),
            pl.BlockSpec((ROUTER_ROWS, D), const2),
        ],
        out_specs=[
            pl.BlockSpec((1, ATT_ROWS, D), lambda b, s: (b, s, 0)),
            pl.BlockSpec((ATT_ROWS, D // 2), lambda b, s: (b * n_s + s, 0)),
            pl.BlockSpec((ROUTER_ROWS, ATT_ROWS // LANES, LANES), lambda b, s: (0, b * n_s + s, 0)),
        ],
        out_shape=[
            jax.ShapeDtypeStruct((batches, S, D), F32),
            jax.ShapeDtypeStruct((batches * S, D // 2), jnp.uint32),
            jax.ShapeDtypeStruct((ROUTER_ROWS, batches * S // LANES, LANES), F32),
        ],
        scratch_shapes=([pltpu.VMEM((ATT_ROWS, D), BF16)] + _weight_scratch(D, D) + _weight_scratch(D, D)
                        + [pltpu.SemaphoreType.DMA]),
        compiler_params=pltpu.CompilerParams(
            dimension_semantics=("arbitrary", "arbitrary"), vmem_limit_bytes=VMEM_LIMIT),
        name="xattn",
    )(h1, norm_x, w_q, k_mem, v_mem, w_o, norm_ffn, w_router)


def _route_kernel(bias_ref, lg_ref, ids_ref, gates_ref, rank_ref, cnt_ref, base_ref):
    sub = lg_ref.shape[1]

    @pl.when(pl.program_id(0) == 0)
    def _():
        base_ref[...] = jnp.zeros_like(base_ref)

    best = lg_ref[0] + bias_ref[0]
    gl = [best]
    sel = jnp.zeros(best.shape, jnp.int32)
    for g in range(1, N_GROUPS):
        cur = lg_ref[g] + bias_ref[g]
        gl.append(cur)
        better = cur > best
        best = jnp.where(better, cur, best)
        sel = jnp.where(better, g, sel)
    denom = jnp.exp(gl[0] - best)
    for g in range(1, N_GROUPS):
        denom = denom + jnp.exp(gl[g] - best)
    g_gate = 1.0 / denom

    ev = []
    for j in range(EXPERTS_PER_GROUP):
        val = lg_ref[N_GROUPS + j] + bias_ref[N_GROUPS + j]
        for g in range(1, N_GROUPS):
            e = g * EXPERTS_PER_GROUP + j
            val = jnp.where(sel == g, lg_ref[N_GROUPS + e] + bias_ref[N_GROUPS + e], val)
        ev.append(val)
    v1, i1 = ev[0], jnp.zeros(best.shape, jnp.int32)
    for j in range(1, EXPERTS_PER_GROUP):
        better = ev[j] > v1
        v1 = jnp.where(better, ev[j], v1)
        i1 = jnp.where(better, j, i1)
    rest = [jnp.where(i1 == j, -jnp.inf, ev[j]) for j in range(EXPERTS_PER_GROUP)]
    v2, i2 = rest[0], jnp.zeros(best.shape, jnp.int32)
    for j in range(1, EXPERTS_PER_GROUP):
        better = rest[j] > v2
        v2 = jnp.where(better, rest[j], v2)
        i2 = jnp.where(better, j, i2)
    e2 = jnp.exp(v2 - v1)
    inv = 1.0 / (1.0 + e2)
    id1 = sel * EXPERTS_PER_GROUP + i1
    id2 = sel * EXPERTS_PER_GROUP + i2
    ids_ref[0] = id1
    ids_ref[1] = id2
    gates_ref[0] = inv * g_gate
    gates_ref[1] = e2 * inv * g_gate

    member = jnp.concatenate(
        [jnp.where((id1 == e) | (id2 == e), 1.0, 0.0) for e in range(N_EXPERTS)], axis=0).astype(BF16)
    n = N_EXPERTS * sub
    li = lax.broadcasted_iota(jnp.int32, (LANES, LANES), 0)
    lj = lax.broadcasted_iota(jnp.int32, (LANES, LANES), 1)
    before_lane = jnp.where(li < lj, 1.0, 0.0).astype(BF16)
    ones = jnp.ones((LANES, LANES), BF16)
    ri = lax.broadcasted_iota(jnp.int32, (n, n), 0)
    rj = lax.broadcasted_iota(jnp.int32, (n, n), 1)
    same = _block_id(ri, sub) == _block_id(rj, sub)
    before_row = jnp.where(same & (rj < ri), 1.0, 0.0).astype(BF16)
    all_row = jnp.where(same, 1.0, 0.0).astype(BF16)
    in_row = _dot(member, before_lane)
    prev_rows = _dot(_dot(before_row, member).astype(BF16), ones)
    total = _dot(_dot(all_row, member).astype(BF16), ones)
    base = base_ref[...]
    pos = base + prev_rows + in_row
    r1 = jnp.zeros(best.shape, F32)
    r2 = jnp.zeros(best.shape, F32)
    for e in range(N_EXPERTS):
        pe = pos[e * sub:(e + 1) * sub]
        r1 = jnp.where(id1 == e, pe, r1)
        r2 = jnp.where(id2 == e, pe, r2)
    rank_ref[0] = r1.astype(jnp.int32)
    rank_ref[1] = r2.astype(jnp.int32)
    base_ref[...] = base + total
    cnt_ref[...] = base + total


def _route(bias, logits3):
    rp, n_sub, _ = logits3.shape
    blk = lambda i: (0, i, 0)
    pair_i = jax.ShapeDtypeStruct((2, n_sub, LANES), jnp.int32)
    return pl.pallas_call(
        _route_kernel,
        grid=(n_sub // ROUTE_SUB,),
        in_specs=[
            pl.BlockSpec(memory_space=pltpu.SMEM),
            pl.BlockSpec((rp, ROUTE_SUB, LANES), blk),
        ],
        out_specs=[
            pl.BlockSpec((2, ROUTE_SUB, LANES), blk),
            pl.BlockSpec((2, ROUTE_SUB, LANES), blk),
            pl.BlockSpec((2, ROUTE_SUB, LANES), blk),
            pl.BlockSpec((N_EXPERTS * ROUTE_SUB, LANES), lambda i: (0, 0)),
        ],
        out_shape=[
            pair_i,
            jax.ShapeDtypeStruct((2, n_sub, LANES), F32),
            pair_i,
            jax.ShapeDtypeStruct((N_EXPERTS * ROUTE_SUB, LANES), F32),
        ],
        scratch_shapes=[pltpu.VMEM((N_EXPERTS * ROUTE_SUB, LANES), F32)],
        compiler_params=pltpu.CompilerParams(
            dimension_semantics=("arbitrary",), vmem_limit_bytes=VMEM_LIMIT),
        name="route",
    )(bias, logits3)


def _dest_kernel(start_ref, ids_ref, rank_ref, dest_ref):
    ids = ids_ref[...]
    off = jnp.zeros(ids.shape, jnp.int32)
    for e in range(N_EXPERTS):
        off = jnp.where(ids == e, start_ref[e], off)
    dest_ref[...] = rank_ref[...] + off


def _dest(seg_start, ids, rank):
    _, n_sub, _ = ids.shape
    blk = pl.BlockSpec((2, ROUTE_SUB, LANES), lambda i: (0, i, 0))
    return pl.pallas_call(
        _dest_kernel,
        grid=(n_sub // ROUTE_SUB,),
        in_specs=[pl.BlockSpec(memory_space=pltpu.SMEM), blk, blk],
        out_specs=blk,
        out_shape=jax.ShapeDtypeStruct(ids.shape, jnp.int32),
        name="dest",
    )(seg_start, ids, rank)


def _sc_mesh():
    return plsc.VectorSubcoreMesh(core_axis_name="core", subcore_axis_name="subcore")


def _sc_worker(rows_total):
    rows = rows_total // _SC_WORKERS
    wid = lax.axis_index("core") * SC_SUBCORES + lax.axis_index("subcore")
    return wid * rows, rows


def _dispatch(dest0, dest1, xn, n_slots):
    T, D = xn.shape
    W = SC_WINDOW
    rows = T // _SC_WORKERS

    @pl.kernel(out_type=jax.ShapeDtypeStruct((n_slots, D), xn.dtype), mesh=_sc_mesh(),
               scratch_types=[pltpu.VMEM((rows,), jnp.int32), pltpu.VMEM((rows,), jnp.int32),
                              pltpu.VMEM((2, W, D), xn.dtype),
                              pltpu.SemaphoreType.DMA((2,)), pltpu.SemaphoreType.DMA((2,))])
    def scatter_rows(x_hbm, d0_hbm, d1_hbm, xs_hbm, d0_v, d1_v, buf, in_sem, out_sem):
        base, _ = _sc_worker(T)
        pltpu.sync_copy(d0_hbm.at[pl.ds(base, rows)], d0_v)
        pltpu.sync_copy(d1_hbm.at[pl.ds(base, rows)], d1_v)

        def load(w, slot):
            return pltpu.make_async_copy(x_hbm.at[pl.ds(base + w * W, W)], buf.at[slot], in_sem.at[slot])

        def store(w, slot, d_v):
            return pltpu.make_async_copy(buf.at[slot], xs_hbm.at[d_v.at[pl.ds(w * W, W)]], out_sem.at[slot])

        def step(w, slot):
            load(w, slot).wait()
            store(w, slot, d0_v).start()
            store(w, slot, d1_v).start()
            store(w, slot, d0_v).wait()
            store(w, slot, d1_v).wait()

        n = rows // W
        load(0, 0).start()

        @pl.loop(0, n, step=2)
        def _(w):
            load(w + 1, 1).start()
            step(w, 0)

            @pl.when(w + 2 < n)
            def _():
                load(w + 2, 0).start()

            step(w + 1, 1)

    return scatter_rows(xn, dest0, dest1)


def _gather_rows(src, idx):
    M = idx.shape[0]
    D = src.shape[1]
    W = SC_WINDOW
    rows = M // _SC_WORKERS

    @pl.kernel(out_type=jax.ShapeDtypeStruct((M, D), src.dtype), mesh=_sc_mesh(),
               scratch_types=[pltpu.VMEM((rows,), jnp.int32), pltpu.VMEM((2, W, D), src.dtype),
                              pltpu.SemaphoreType.DMA((2,)), pltpu.SemaphoreType.DMA((2,))])
    def gather_rows(src_hbm, i_hbm, o_hbm, i_v, buf, in_sem, out_sem):
        base, _ = _sc_worker(M)
        pltpu.sync_copy(i_hbm.at[pl.ds(base, rows)], i_v)

        def load(w, slot):
            return pltpu.make_async_copy(src_hbm.at[i_v.at[pl.ds(w * W, W)]], buf.at[slot], in_sem.at[slot])

        def store(w, slot):
            return pltpu.make_async_copy(buf.at[slot], o_hbm.at[pl.ds(base + w * W, W)], out_sem.at[slot])

        n = rows // W
        load(0, 0).start()

        @pl.loop(0, n, step=2)
        def _(w):
            @pl.when(w > 0)
            def _():
                store(w - 1, 1).wait()

            load(w + 1, 1).start()
            load(w, 0).wait()
            store(w, 0).start()
            store(w, 0).wait()

            @pl.when(w + 2 < n)
            def _():
                load(w + 2, 0).start()

            load(w + 1, 1).wait()
            store(w + 1, 1).start()

        store(n - 1, 1).wait()

    return gather_rows(src, idx)


def _expert_kernel(be_ref, nb_ref, slot_ref, next_ref, x_ref, wg_hbm, wu_hbm, wd_hbm, y_ref,
                   wg_f32, wu_f32, wd_f32, wg_ref, wu_ref, wd_ref, sem):
    i = pl.program_id(0)
    used = i < nb_ref[0]
    new_expert = (i == 0) | (be_ref[i] != be_ref[jnp.maximum(i - 1, 0)])

    def fetch(e, slot):
        pairs = ((wg_hbm, wg_f32), (wu_hbm, wu_f32), (wd_hbm, wd_f32))
        return [pltpu.make_async_copy(w_hbm.at[e], w_f32.at[slot], sem.at[slot, k])
                for k, (w_hbm, w_f32) in enumerate(pairs)]

    @pl.when(used & (i == 0))
    def _():
        for copy in fetch(be_ref[0], slot_ref[0]):
            copy.start()

    @pl.when(used & new_expert)
    def _():
        slot = slot_ref[i]
        for copy in fetch(be_ref[i], slot):
            copy.wait()

        @pl.when(next_ref[i] >= 0)
        def _():
            for copy in fetch(next_ref[i], 1 - slot):
                copy.start()

        wg_ref[...] = wg_f32[slot].astype(BF16)
        wu_ref[...] = wu_f32[slot].astype(BF16)
        wd_ref[...] = wd_f32[slot].astype(BF16)

    @pl.when(used)
    def _():
        for c in range(EXPERT_ROWS // EXPERT_CHAIN_ROWS):
            rs = slice(c * EXPERT_CHAIN_ROWS, (c + 1) * EXPERT_CHAIN_ROWS)
            x = _unpack_pairs(x_ref[rs]).astype(BF16)
            g = _dot(x, wg_ref[...])
            u = _dot(x, wu_ref[...])
            hid = (g * _sigmoid(g) * u).astype(BF16)
            y_ref[rs] = _pack_pairs(_dot(hid, wd_ref[...]))

    @pl.when(jnp.logical_not(used))
    def _():
        y_ref[...] = jnp.zeros_like(y_ref)


def _experts(block_expert, n_used, block_slot, block_next, xs, w_gate, w_up, w_down):
    n_slots, half = xs.shape
    D = 2 * half
    n_blocks = n_slots // EXPERT_ROWS
    grid_spec = pltpu.PrefetchScalarGridSpec(
        num_scalar_prefetch=4,
        grid=(n_blocks,),
        in_specs=[
            pl.BlockSpec((EXPERT_ROWS, half), lambda i, be, nb, sl, nx: (jnp.minimum(i, nb[0] - 1), 0)),
            pl.BlockSpec(memory_space=pl.ANY),
            pl.BlockSpec(memory_space=pl.ANY),
            pl.BlockSpec(memory_space=pl.ANY),
        ],
        out_specs=pl.BlockSpec((EXPERT_ROWS, half), lambda i, be, nb, sl, nx: (i, 0)),
        scratch_shapes=[
            pltpu.VMEM((2, D, D_EXPERT), F32), pltpu.VMEM((2, D, D_EXPERT), F32),
            pltpu.VMEM((2, D_EXPERT, D), F32),
            pltpu.VMEM((D, D_EXPERT), BF16), pltpu.VMEM((D, D_EXPERT), BF16), pltpu.VMEM((D_EXPERT, D), BF16),
            pltpu.SemaphoreType.DMA((2, 3)),
        ],
    )
    return pl.pallas_call(
        _expert_kernel,
        grid_spec=grid_spec,
        out_shape=jax.ShapeDtypeStruct((n_slots, half), jnp.uint32),
        compiler_params=pltpu.CompilerParams(
            dimension_semantics=("arbitrary",), vmem_limit_bytes=VMEM_LIMIT),
        name="experts",
    )(block_expert, n_used, block_slot, block_next, xs, w_gate, w_up, w_down)


def _combine_kernel(y0_ref, y1_ref, h_ref, gates_ref, nfin_ref, *rest):
    o_ref = rest[-1]
    g0 = gates_ref[0].T
    g1 = gates_ref[1].T
    for r in range(gates_ref.shape[1]):
        rs = slice(r * LANES, (r + 1) * LANES)
        h = (h_ref[rs] + g0[:, r:r + 1] * _unpack_pairs(y0_ref[rs])
             + g1[:, r:r + 1] * _unpack_pairs(y1_ref[rs]))
        o_ref[rs] = _rms(h, nfin_ref[...])


def _combine(y01, h2, gates_t, norm_final, out_prev, part, parts):
    T, D = h2.shape
    n_t = T // MOVE_ROWS
    in_specs = [
        pl.BlockSpec((MOVE_ROWS, D // 2), lambda i: (i, 0)),
        pl.BlockSpec((MOVE_ROWS, D // 2), lambda i: (i + n_t, 0)),
        pl.BlockSpec((MOVE_ROWS, D), lambda i: (i, 0)),
        pl.BlockSpec((2, MOVE_ROWS // LANES, LANES), lambda i: (0, i, 0)),
        pl.BlockSpec((1, D), lambda i: (0, 0)),
    ]
    args = [y01, y01, h2, gates_t, norm_final]
    aliases = {}
    if out_prev is not None:
        in_specs.append(pl.BlockSpec(memory_space=pl.ANY))
        args.append(out_prev)
        aliases = {len(args) - 1: 0}
    return pl.pallas_call(
        _combine_kernel,
        grid=(n_t,),
        in_specs=in_specs,
        out_specs=pl.BlockSpec((MOVE_ROWS, D), lambda i: (i + part * n_t, 0)),
        out_shape=jax.ShapeDtypeStruct((parts * T, D), F32),
        input_output_aliases=aliases,
        compiler_params=pltpu.CompilerParams(
            dimension_semantics=("arbitrary",), vmem_limit_bytes=VMEM_LIMIT),
        name="combine",
    )(*args)


def kernel(x, mem, norm_mix, w_in, gmlp_ln, gmlp_w_spatial, gmlp_b_spatial, gmlp_beta, hgrn_lb_logits, hgrn_out_gain, w_out, norm_xattn, norm_mem, w_xq, w_xkv, w_xo, norm_ffn, w_router_group, b_router_group, w_router_expert, b_router_expert, w_expert_gate, w_expert_up, w_expert_down, norm_final):
    B, S, D = x.shape
    T = B * S
    depth = w_in.shape[0]
    assert depth == 1 and hgrn_lb_logits.shape[0] == 2
    l = 0
    row = lambda p: p.reshape(1, -1)

    h1 = _mixer(x, row(norm_mix[l]), w_in[l], row(gmlp_ln[l]), gmlp_w_spatial[l],
                gmlp_b_spatial[l].T, row(gmlp_beta[l]), hgrn_lb_logits, row(hgrn_out_gain[l]),
                w_out[l])
    k_mem, v_mem = _memkv(mem, row(norm_mem[l]), w_xkv[l])

    w_router = jnp.concatenate([w_router_group[l].T, w_router_expert[l].T], axis=0)
    w_router = jnp.pad(w_router, ((0, ROUTER_ROWS - w_router.shape[0]), (0, 0)))
    w_router = w_router.astype(BF16)
    bias = jnp.concatenate([b_router_group[l], b_router_expert[l]]).astype(F32)

    b_part = B // TOKEN_PARTS
    t_part = b_part * S
    n_blocks = (2 * t_part) // EXPERT_ROWS + N_EXPERTS
    out = None
    for part in range(TOKEN_PARTS):
        h2, xn, logits = _xattn(h1, row(norm_xattn[l]), w_xq[l], k_mem, v_mem, w_xo[l], row(norm_ffn[l]),
                                w_router, part * b_part, b_part)
        ids, gates, rank, counts = _route(bias, logits)

        counts = counts[::ROUTE_SUB, 0].astype(jnp.int32)
        padded = (counts + EXPERT_ROWS - 1) // EXPERT_ROWS * EXPERT_ROWS
        seg_end = jnp.cumsum(padded)
        seg_start = seg_end - padded
        block_first_row = jnp.arange(n_blocks, dtype=jnp.int32) * EXPERT_ROWS
        block_expert = jnp.minimum(
            jnp.sum(block_first_row[:, None] >= seg_end[None, :], axis=1), N_EXPERTS - 1).astype(jnp.int32)
        n_used = (seg_end[-1:] // EXPERT_ROWS).astype(jnp.int32)
        present = counts > 0
        expert_ids = jnp.arange(N_EXPERTS, dtype=jnp.int32)
        expert_slot = (jnp.cumsum(present.astype(jnp.int32)) - 1) % 2
        later = jnp.where(present[None, :] & (expert_ids[None, :] > expert_ids[:, None]),
                          expert_ids[None, :], N_EXPERTS)
        expert_next = jnp.min(later, axis=1)
        expert_next = jnp.where(expert_next == N_EXPERTS, -1, expert_next).astype(jnp.int32)
        of_block = (block_expert[:, None] == expert_ids[None, :]).astype(jnp.int32)
        block_slot = jnp.sum(of_block * expert_slot[None, :], axis=1).astype(jnp.int32)
        block_next = jnp.sum(of_block * expert_next[None, :], axis=1).astype(jnp.int32)

        dest = _dest(seg_start, ids, rank).reshape(2, t_part)
        xs = _dispatch(dest[0], dest[1], xn, n_blocks * EXPERT_ROWS)
        yb = _experts(block_expert, n_used, block_slot, block_next, xs,
                      w_expert_gate[l], w_expert_up[l], w_expert_down[l])
        y01 = _gather_rows(yb, dest.reshape(2 * t_part))
        out = _combine(y01, h2.reshape(t_part, D), gates, row(norm_final),
                       out, part, TOKEN_PARTS)
    return out.reshape(B, S, D)
```

```python
import functools

import jax
import jax.numpy as jnp
from jax import lax
from jax.experimental import pallas as pl
from jax.experimental.pallas import tpu as pltpu
from jax.experimental.pallas import tpu_sc as plsc

F32 = jnp.float32
BF16 = jnp.bfloat16
EPS = 1e-6

D_MODEL = 1024
D_GMLP = 512
GMLP_GROUPS = 4
GMLP_CHUNK = 128
D_HGRN = 512
HGRN_HEADS = 4
HGRN_DK = 128
HGRN_CHUNK = 64
D_IN_PROJ = 2 * D_GMLP + 4 * D_HGRN
N_MEM = 256
XATTN_HEADS = 4
XATTN_HEAD_DIM = D_MODEL // XATTN_HEADS
N_GROUPS = 4
EXPERTS_PER_GROUP = 8
N_EXPERTS = N_GROUPS * EXPERTS_PER_GROUP
D_EXPERT = 512

LANES = 128
MIX_ROWS = 512
MIX_CHAIN_ROWS = 256
MIX_SCHEDULE = (("in", 0), ("gmlp", 0), ("factors", 0), ("in", 1), ("local", 0), ("recurrence", 0),
                ("gmlp", 1), ("out", 0), ("factors", 1), ("local", 1), ("recurrence", 1), ("out", 1))
ATT_ROWS = 1024
ATT_CHAIN_ROWS = 1024
WEIGHT_STAGE_ROWS = 128
ROUTER_ROWS = 40
ROUTE_SUB = 16
EXPERT_ROWS = 512
EXPERT_CHAIN_ROWS = 512
MOVE_ROWS = 1024
TOKEN_PARTS = 2
SC_WINDOW = 64
SC_CORES = 2
SC_SUBCORES = 16
_SC_WORKERS = SC_CORES * SC_SUBCORES
VMEM_LIMIT = 48 * 1024 * 1024


def _rms(x, gain):
    return x * lax.rsqrt(jnp.mean(x * x, axis=-1, keepdims=True) + EPS) * gain


def _dot(a, b):
    return jnp.dot(a, b, preferred_element_type=F32)


def _dot_nt(a, b):
    return lax.dot_general(a, b, (((1,), (1,)), ((), ())), preferred_element_type=F32)


def _dot_tn(a, b):
    return lax.dot_general(a, b, (((0,), (0,)), ((), ())), preferred_element_type=F32)


def _gelu(x):
    return 0.5 * x * (1.0 + jnp.tanh(0.7978845608028654 * (x + 0.044715 * (x * x * x))))


def _sigmoid(x):
    return 1.0 / (1.0 + jnp.exp(-x))


def _block_id(idx, size):
    assert size & (size - 1) == 0
    return lax.shift_right_logical(idx, size.bit_length() - 1)


def _stage_weight(w_hbm, w_bf16, stage_ref, sem):
    rows = stage_ref.shape[0]
    for k in range(w_hbm.shape[0] // rows):
        copy = pltpu.make_async_copy(w_hbm.at[pl.ds(k * rows, rows)], stage_ref, sem)
        copy.start()
        copy.wait()
        w_bf16[k * rows:(k + 1) * rows, :] = stage_ref[...].astype(BF16)


def _weight_scratch(k, n):
    return [pltpu.VMEM((k, n), BF16), pltpu.VMEM((WEIGHT_STAGE_ROWS, n), F32)]


_HIGH_HALF = 0xFFFF0000


def _pack_pairs(x):
    c = x.shape[1] // 2
    bits = lax.bitcast_convert_type(x.astype(BF16).astype(F32), jnp.uint32)
    return (bits[:, c:] & jnp.uint32(_HIGH_HALF)) | lax.shift_right_logical(bits[:, :c], jnp.uint32(16))


def _unpack_pairs(w):
    lo = lax.bitcast_convert_type(lax.shift_left(w, jnp.uint32(16)), F32)
    hi = lax.bitcast_convert_type(w & jnp.uint32(_HIGH_HALF), F32)
    return jnp.concatenate([lo, hi], axis=1)


def _mixer_kernel(x_ref, nmix_ref, win_hbm, gln_ref, ws_ref, bst_ref, beta_ref, lbl_ref, og_ref,
                  wout_hbm, o_ref, proj_ref, ycat_ref, state_ref, win_ref, win_stage, wout_ref, wout_stage,
                  sem):
    @pl.when((pl.program_id(0) == 0) & (pl.program_id(1) == 0))
    def _():
        _stage_weight(win_hbm, win_ref, win_stage, sem)
        _stage_weight(wout_hbm, wout_ref, wout_stage, sem)

    @pl.when(pl.program_id(1) == 0)
    def _():
        state_ref[...] = jnp.zeros_like(state_ref)

    n = MIX_CHAIN_ROWS
    r_i = lax.broadcasted_iota(jnp.int32, (GMLP_CHUNK, GMLP_CHUNK), 0)
    c_i = lax.broadcasted_iota(jnp.int32, (GMLP_CHUNK, GMLP_CHUNK), 1)
    causal = c_i <= r_i
    w_tril = [jnp.where(causal, ws_ref[g], 0.0).astype(BF16) for g in range(GMLP_GROUPS)]
    lbl = lbl_ref[...]
    e_lb = jnp.exp(lbl - jnp.max(lbl, axis=0, keepdims=True))
    lb = e_lb[0:1] / jnp.sum(e_lb, axis=0, keepdims=True)
    rr = lax.broadcasted_iota(jnp.int32, (n, n), 0)
    cc = lax.broadcasted_iota(jnp.int32, (n, n), 1)
    tri = jnp.where((_block_id(rr, HGRN_CHUNK) == _block_id(cc, HGRN_CHUNK)) & (cc <= rr),
                    1.0, 0.0).astype(BF16)
    r64 = lax.broadcasted_iota(jnp.int32, (HGRN_CHUNK, HGRN_CHUNK), 0)
    c64 = lax.broadcasted_iota(jnp.int32, (HGRN_CHUNK, HGRN_CHUNK), 1)
    causal64 = c64 <= r64
    base = 2 * D_GMLP

    n_chunks = n // HGRN_CHUNK
    chains = range(x_ref.shape[1] // n)
    env = {ch: {} for ch in chains}

    def rows(ch):
        return slice(ch * n, (ch + 1) * n)

    def in_proj(ch):
        a = _rms(x_ref[0, rows(ch)], nmix_ref[...]).astype(BF16)
        proj_ref[rows(ch)] = _dot(a, win_ref[...])

    def gmlp(ch):
        u = _gelu(proj_ref[rows(ch), 0:D_GMLP])
        v = _gelu(proj_ref[rows(ch), D_GMLP:2 * D_GMLP])
        vc = v - jnp.mean(v, axis=-1, keepdims=True)
        vn = (vc * lax.rsqrt(jnp.mean(vc * vc, axis=-1, keepdims=True) + EPS) * gln_ref[...]).astype(BF16)
        z_rows = []
        for c in range(n // GMLP_CHUNK):
            z_cols = []
            for g in range(GMLP_GROUPS):
                vg = vn[c * GMLP_CHUNK:(c + 1) * GMLP_CHUNK, g * LANES:(g + 1) * LANES]
                z_cols.append(_dot(w_tril[g], vg) + bst_ref[:, g:g + 1])
            z_rows.append(jnp.concatenate(z_cols, axis=1))
        z = jnp.concatenate(z_rows, axis=0)
        ycat_ref[rows(ch), 0:D_GMLP] = _rms(u * z, beta_ref[...]).astype(BF16)

    def hgrn_factors(ch):
        e = env[ch]
        f = lb + (1.0 - lb) * _sigmoid(proj_ref[rows(ch), base + D_HGRN:base + 2 * D_HGRN])
        log_f = jnp.log(f)
        lf_hi = log_f.astype(BF16)
        lf_lo = (log_f - lf_hi.astype(F32)).astype(BF16)
        b_all = _dot(tri, lf_hi) + _dot(tri, lf_lo)
        bl_rows = [b_all[c * HGRN_CHUNK + HGRN_CHUNK - 1:(c + 1) * HGRN_CHUNK] for c in range(n_chunks)]
        bl_all = jnp.concatenate([jnp.broadcast_to(r, (HGRN_CHUNK, D_HGRN)) for r in bl_rows], axis=0)
        q_all = proj_ref[rows(ch), base:base + D_HGRN]
        k_all = 1.0 - f
        e["qd"] = (q_all * _sigmoid(q_all) * jnp.exp(b_all)).astype(BF16)
        e["ki"] = (k_all * jnp.exp(-b_all)).astype(BF16)
        e["kte"] = (k_all * jnp.exp(bl_all - b_all)).astype(BF16)
        e["v"] = proj_ref[rows(ch), base + 2 * D_HGRN:base + 3 * D_HGRN].astype(BF16)
        e["decay"] = [jnp.exp(r) for r in bl_rows]

    def hgrn_local(ch):
        e = env[ch]
        e["o_intra"], e["d_state"] = {}, {}
        for c in range(n_chunks):
            rs = slice(c * HGRN_CHUNK, (c + 1) * HGRN_CHUNK)
            for h in range(HGRN_HEADS):
                cs = slice(h * HGRN_DK, (h + 1) * HGRN_DK)
                scores = jnp.where(causal64, _dot_nt(e["qd"][rs, cs], e["ki"][rs, cs]), 0.0).astype(BF16)
                e["o_intra"][c, h] = _dot(scores, e["v"][rs, cs])
                e["d_state"][c, h] = _dot_tn(e["v"][rs, cs], e["kte"][rs, cs])

    def hgrn_recurrence(ch):
        e = env[ch]
        for c in range(n_chunks):
            rs = slice(c * HGRN_CHUNK, (c + 1) * HGRN_CHUNK)
            ps = slice(ch * n + c * HGRN_CHUNK, ch * n + (c + 1) * HGRN_CHUNK)
            g_c = proj_ref[ps, base + 3 * D_HGRN:base + 4 * D_HGRN]
            gate = og_ref[...] * (g_c * _sigmoid(g_c))
            for h in range(HGRN_HEADS):
                cs = slice(h * HGRN_DK, (h + 1) * HGRN_DK)
                st = state_ref[h]
                o = e["o_intra"][c, h] + _dot_nt(e["qd"][rs, cs], st.astype(BF16))
                state_ref[h] = st * e["decay"][c][:, cs] + e["d_state"][c, h]
                o = o * lax.rsqrt(jnp.mean(o * o, axis=-1, keepdims=True) + EPS)
                ycat_ref[ps, D_GMLP + h * HGRN_DK:D_GMLP + (h + 1) * HGRN_DK] = (o * gate[:, cs]).astype(BF16)

    def out_proj(ch):
        o_ref[0, rows(ch)] = x_ref[0, rows(ch)] + _dot(ycat_ref[rows(ch)], wout_ref[...])

    stages = {"in": in_proj, "gmlp": gmlp, "factors": hgrn_factors, "local": hgrn_local,
              "recurrence": hgrn_recurrence, "out": out_proj}
    for stage, ch in MIX_SCHEDULE:
        stages[stage](ch)


def _mixer(x, norm_mix, w_in, gmlp_ln, w_s, b_s_t, beta, lb_logits, out_gain, w_out):
    B, S, D = x.shape
    const2 = lambda b, s: (0, 0)
    return pl.pallas_call(
        _mixer_kernel,
        grid=(B, S // MIX_ROWS),
        in_specs=[
            pl.BlockSpec((1, MIX_ROWS, D), lambda b, s: (b, s, 0)),
            pl.BlockSpec((1, D), const2),
            pl.BlockSpec(memory_space=pl.ANY),
            pl.BlockSpec((1, D_GMLP), const2),
            pl.BlockSpec((GMLP_GROUPS, GMLP_CHUNK, GMLP_CHUNK), lambda b, s: (0, 0, 0)),
            pl.BlockSpec((GMLP_CHUNK, GMLP_GROUPS), const2),
            pl.BlockSpec((1, D_GMLP), const2),
            pl.BlockSpec(lb_logits.shape, const2),
            pl.BlockSpec((1, D_HGRN), const2),
            pl.BlockSpec(memory_space=pl.ANY),
        ],
        out_specs=pl.BlockSpec((1, MIX_ROWS, D), lambda b, s: (b, s, 0)),
        out_shape=jax.ShapeDtypeStruct((B, S, D), F32),
        scratch_shapes=[
            pltpu.VMEM((MIX_ROWS, D_IN_PROJ), F32),
            pltpu.VMEM((MIX_ROWS, D), BF16),
            pltpu.VMEM((HGRN_HEADS, HGRN_DK, HGRN_DK), F32),
        ] + _weight_scratch(D, D_IN_PROJ) + _weight_scratch(D, D) + [pltpu.SemaphoreType.DMA],
        compiler_params=pltpu.CompilerParams(
            dimension_semantics=("arbitrary", "arbitrary"), vmem_limit_bytes=VMEM_LIMIT),
        name="mixer",
    )(x, norm_mix, w_in, gmlp_ln, w_s, b_s_t, beta, lb_logits, out_gain, w_out)


def _memkv_kernel(mem_ref, nm_ref, wkv_hbm, k_ref, v_ref, wkv_ref, wkv_stage, sem):
    @pl.when(pl.program_id(0) == 0)
    def _():
        _stage_weight(wkv_hbm, wkv_ref, wkv_stage, sem)

    m = _rms(mem_ref[0], nm_ref[...]).astype(BF16)
    kv = _dot(m, wkv_ref[...])
    k_ref[0] = kv[:, :D_MODEL].astype(BF16)
    v_ref[0] = kv[:, D_MODEL:].astype(BF16)


def _memkv(mem, norm_mem, w_kv):
    B, M, D = mem.shape
    out = jax.ShapeDtypeStruct((B, M, D), BF16)
    return pl.pallas_call(
        _memkv_kernel,
        grid=(B,),
        in_specs=[
            pl.BlockSpec((1, M, D), lambda b: (b, 0, 0)),
            pl.BlockSpec((1, D), lambda b: (0, 0)),
            pl.BlockSpec(memory_space=pl.ANY),
        ],
        out_specs=[pl.BlockSpec((1, M, D), lambda b: (b, 0, 0))] * 2,
        out_shape=[out, out],
        scratch_shapes=_weight_scratch(D, 2 * D) + [pltpu.SemaphoreType.DMA],
        compiler_params=pltpu.CompilerParams(
            dimension_semantics=("arbitrary",), vmem_limit_bytes=VMEM_LIMIT),
        name="memkv",
    )(mem, norm_mem, w_kv)


def _xattn_kernel(h_ref, nx_ref, wq_hbm, k_ref, v_ref, wo_hbm, nf_ref, wr_ref,
                  h2_ref, xn_ref, lg_ref, att_ref, wq_ref, wq_stage, wo_ref, wo_stage, sem):
    @pl.when((pl.program_id(0) == 0) & (pl.program_id(1) == 0))
    def _():
        _stage_weight(wq_hbm, wq_ref, wq_stage, sem)
        _stage_weight(wo_hbm, wo_ref, wo_stage, sem)

    for c in range(ATT_ROWS // ATT_CHAIN_ROWS):
        rs = slice(c * ATT_CHAIN_ROWS, (c + 1) * ATT_CHAIN_ROWS)
        h = h_ref[0, rs]
        hn = _rms(h, nx_ref[...]).astype(BF16)
        q = (_dot(hn, wq_ref[...]) * (XATTN_HEAD_DIM ** -0.5)).astype(BF16)
        for hd in range(XATTN_HEADS):
            cs = slice(hd * XATTN_HEAD_DIM, (hd + 1) * XATTN_HEAD_DIM)
            s = _dot_nt(q[:, cs], k_ref[0, :, cs])
            p = jnp.exp(s - jnp.max(s, axis=-1, keepdims=True))
            p = (p / jnp.sum(p, axis=-1, keepdims=True)).astype(BF16)
            att_ref[rs, cs] = _dot(p, v_ref[0, :, cs]).astype(BF16)
        h2 = h + _dot(att_ref[rs], wo_ref[...])
        h2_ref[0, rs] = h2
        xn = _rms(h2, nf_ref[...])
        xn_ref[rs] = _pack_pairs(xn)
        lg = _dot_nt(wr_ref[...], xn.astype(BF16))
        for j in range(ATT_CHAIN_ROWS // LANES):
            lg_ref[:, c * (ATT_CHAIN_ROWS // LANES) + j, :] = lg[:, j * LANES:(j + 1) * LANES]


def _xattn(h1, norm_x, w_q, k_mem, v_mem, w_o, norm_ffn, w_router, batch0, batches):
    _, S, D = h1.shape
    n_s = S // ATT_ROWS
    const2 = lambda b, s: (0, 0)
    return pl.pallas_call(
        _xattn_kernel,
        grid=(batches, n_s),
        in_specs=[
            pl.BlockSpec((1, ATT_ROWS, D), lambda b, s: (b + batch0, s, 0)),
            pl.BlockSpec((1, D), const2),
            pl.BlockSpec(memory_space=pl.ANY),
            pl.BlockSpec((1, N_MEM, D), lambda b, s: (b + batch0, 0, 0)),
            pl.BlockSpec((1, N_MEM, D), lambda b, s: (b + batch0, 0, 0)),
            pl.BlockSpec(memory_space=pl.ANY),
            pl.BlockSpec((1, D), const2),
            pl.BlockSpec((ROUTER_ROWS, D), const2),
        ],
        out_specs=[
            pl.BlockSpec((1, ATT_ROWS, D), lambda b, s: (b, s, 0)),
            pl.BlockSpec((ATT_ROWS, D // 2), lambda b, s: (b * n_s + s, 0)),
            pl.BlockSpec((ROUTER_ROWS, ATT_ROWS // LANES, LANES), lambda b, s: (0, b * n_s + s, 0)),
        ],
        out_shape=[
            jax.ShapeDtypeStruct((batches, S, D), F32),
            jax.ShapeDtypeStruct((batches * S, D // 2), jnp.uint32),
            jax.ShapeDtypeStruct((ROUTER_ROWS, batches * S // LANES, LANES), F32),
        ],
        scratch_shapes=([pltpu.VMEM((ATT_ROWS, D), BF16)] + _weight_scratch(D, D) + _weight_scratch(D, D)
                        + [pltpu.SemaphoreType.DMA]),
        compiler_params=pltpu.CompilerParams(
            dimension_semantics=("arbitrary", "arbitrary"), vmem_limit_bytes=VMEM_LIMIT),
        name="xattn",
    )(h1, norm_x, w_q, k_mem, v_mem, w_o, norm_ffn, w_router)


def _route_kernel(bias_ref, lg_ref, ids_ref, gates_ref, rank_ref, cnt_ref, base_ref):
    sub = lg_ref.shape[1]

    @pl.when(pl.program_id(0) == 0)
    def _():
        base_ref[...] = jnp.zeros_like(base_ref)

    best = lg_ref[0] + bias_ref[0]
    gl = [best]
    sel = jnp.zeros(best.shape, jnp.int32)
    for g in range(1, N_GROUPS):
        cur = lg_ref[g] + bias_ref[g]
        gl.append(cur)
        better = cur > best
        best = jnp.where(better, cur, best)
        sel = jnp.where(better, g, sel)
    denom = jnp.exp(gl[0] - best)
    for g in range(1, N_GROUPS):
        denom = denom + jnp.exp(gl[g] - best)
    g_gate = 1.0 / denom

    ev = []
    for j in range(EXPERTS_PER_GROUP):
        val = lg_ref[N_GROUPS + j] + bias_ref[N_GROUPS + j]
        for g in range(1, N_GROUPS):
            e = g * EXPERTS_PER_GROUP + j
            val = jnp.where(sel == g, lg_ref[N_GROUPS + e] + bias_ref[N_GROUPS + e], val)
        ev.append(val)
    v1, i1 = ev[0], jnp.zeros(best.shape, jnp.int32)
    for j in range(1, EXPERTS_PER_GROUP):
        better = ev[j] > v1
        v1 = jnp.where(better, ev[j], v1)
        i1 = jnp.where(better, j, i1)
    rest = [jnp.where(i1 == j, -jnp.inf, ev[j]) for j in range(EXPERTS_PER_GROUP)]
    v2, i2 = rest[0], jnp.zeros(best.shape, jnp.int32)
    for j in range(1, EXPERTS_PER_GROUP):
        better = rest[j] > v2
        v2 = jnp.where(better, rest[j], v2)
        i2 = jnp.where(better, j, i2)
    e2 = jnp.exp(v2 - v1)
    inv = 1.0 / (1.0 + e2)
    id1 = sel * EXPERTS_PER_GROUP + i1
    id2 = sel * EXPERTS_PER_GROUP + i2
    ids_ref[0] = id1
    ids_ref[1] = id2
    gates_ref[0] = inv * g_gate
    gates_ref[1] = e2 * inv * g_gate

    member = jnp.concatenate(
        [jnp.where((id1 == e) | (id2 == e), 1.0, 0.0) for e in range(N_EXPERTS)], axis=0).astype(BF16)
    n = N_EXPERTS * sub
    li = lax.broadcasted_iota(jnp.int32, (LANES, LANES), 0)
    lj = lax.broadcasted_iota(jnp.int32, (LANES, LANES), 1)
    before_lane = jnp.where(li < lj, 1.0, 0.0).astype(BF16)
    ones = jnp.ones((LANES, LANES), BF16)
    ri = lax.broadcasted_iota(jnp.int32, (n, n), 0)
    rj = lax.broadcasted_iota(jnp.int32, (n, n), 1)
    same = _block_id(ri, sub) == _block_id(rj, sub)
    before_row = jnp.where(same & (rj < ri), 1.0, 0.0).astype(BF16)
    all_row = jnp.where(same, 1.0, 0.0).astype(BF16)
    in_row = _dot(member, before_lane)
    prev_rows = _dot(_dot(before_row, member).astype(BF16), ones)
    total = _dot(_dot(all_row, member).astype(BF16), ones)
    base = base_ref[...]
    pos = base + prev_rows + in_row
    r1 = jnp.zeros(best.shape, F32)
    r2 = jnp.zeros(best.shape, F32)
    for e in range(N_EXPERTS):
        pe = pos[e * sub:(e + 1) * sub]
        r1 = jnp.where(id1 == e, pe, r1)
        r2 = jnp.where(id2 == e, pe, r2)
    rank_ref[0] = r1.astype(jnp.int32)
    rank_ref[1] = r2.astype(jnp.int32)
    base_ref[...] = base + total
    cnt_ref[...] = base + total


def _route(bias, logits3):
    rp, n_sub, _ = logits3.shape
    blk = lambda i: (0, i, 0)
    pair_i = jax.ShapeDtypeStruct((2, n_sub, LANES), jnp.int32)
    return pl.pallas_call(
        _route_kernel,
        grid=(n_sub // ROUTE_SUB,),
        in_specs=[
            pl.BlockSpec(memory_space=pltpu.SMEM),
            pl.BlockSpec((rp, ROUTE_SUB, LANES), blk),
        ],
        out_specs=[
            pl.BlockSpec((2, ROUTE_SUB, LANES), blk),
            pl.BlockSpec((2, ROUTE_SUB, LANES), blk),
            pl.BlockSpec((2, ROUTE_SUB, LANES), blk),
            pl.BlockSpec((N_EXPERTS * ROUTE_SUB, LANES), lambda i: (0, 0)),
        ],
        out_shape=[
            pair_i,
            jax.ShapeDtypeStruct((2, n_sub, LANES), F32),
            pair_i,
            jax.ShapeDtypeStruct((N_EXPERTS * ROUTE_SUB, LANES), F32),
        ],
        scratch_shapes=[pltpu.VMEM((N_EXPERTS * ROUTE_SUB, LANES), F32)],
        compiler_params=pltpu.CompilerParams(
            dimension_semantics=("arbitrary",), vmem_limit_bytes=VMEM_LIMIT),
        name="route",
    )(bias, logits3)


def _dest_kernel(start_ref, ids_ref, rank_ref, dest_ref):
    ids = ids_ref[...]
    off = jnp.zeros(ids.shape, jnp.int32)
    for e in range(N_EXPERTS):
        off = jnp.where(ids == e, start_ref[e], off)
    dest_ref[...] = rank_ref[...] + off


def _dest(seg_start, ids, rank):
    _, n_sub, _ = ids.shape
    blk = pl.BlockSpec((2, ROUTE_SUB, LANES), lambda i: (0, i, 0))
    return pl.pallas_call(
        _dest_kernel,
        grid=(n_sub // ROUTE_SUB,),
        in_specs=[pl.BlockSpec(memory_space=pltpu.SMEM), blk, blk],
        out_specs=blk,
        out_shape=jax.ShapeDtypeStruct(ids.shape, jnp.int32),
        name="dest",
    )(seg_start, ids, rank)


def _sc_mesh():
    return plsc.VectorSubcoreMesh(core_axis_name="core", subcore_axis_name="subcore")


def _sc_worker(rows_total):
    rows = rows_total // _SC_WORKERS
    wid = lax.axis_index("core") * SC_SUBCORES + lax.axis_index("subcore")
    return wid * rows, rows


def _dispatch(dest0, dest1, xn, n_slots):
    T, D = xn.shape
    W = SC_WINDOW
    rows = T // _SC_WORKERS

    @pl.kernel(out_type=jax.ShapeDtypeStruct((n_slots, D), xn.dtype), mesh=_sc_mesh(),
               scratch_types=[pltpu.VMEM((rows,), jnp.int32), pltpu.VMEM((rows,), jnp.int32),
                              pltpu.VMEM((2, W, D), xn.dtype),
                              pltpu.SemaphoreType.DMA((2,)), pltpu.SemaphoreType.DMA((2,))])
    def scatter_rows(x_hbm, d0_hbm, d1_hbm, xs_hbm, d0_v, d1_v, buf, in_sem, out_sem):
        base, _ = _sc_worker(T)
        pltpu.sync_copy(d0_hbm.at[pl.ds(base, rows)], d0_v)
        pltpu.sync_copy(d1_hbm.at[pl.ds(base, rows)], d1_v)

        def load(w, slot):
            return pltpu.make_async_copy(x_hbm.at[pl.ds(base + w * W, W)], buf.at[slot], in_sem.at[slot])

        def store(w, slot, d_v):
            return pltpu.make_async_copy(buf.at[slot], xs_hbm.at[d_v.at[pl.ds(w * W, W)]], out_sem.at[slot])

        def step(w, slot):
            load(w, slot).wait()
            store(w, slot, d0_v).start()
            store(w, slot, d1_v).start()
            store(w, slot, d0_v).wait()
            store(w, slot, d1_v).wait()

        n = rows // W
        load(0, 0).start()

        @pl.loop(0, n, step=2)
        def _(w):
            load(w + 1, 1).start()
            step(w, 0)

            @pl.when(w + 2 < n)
            def _():
                load(w + 2, 0).start()

            step(w + 1, 1)

    return scatter_rows(xn, dest0, dest1)


def _gather_rows(src, idx):
    M = idx.shape[0]
    D = src.shape[1]
    W = SC_WINDOW
    rows = M // _SC_WORKERS

    @pl.kernel(out_type=jax.ShapeDtypeStruct((M, D), src.dtype), mesh=_sc_mesh(),
               scratch_types=[pltpu.VMEM((rows,), jnp.int32), pltpu.VMEM((2, W, D), src.dtype),
                              pltpu.SemaphoreType.DMA((2,)), pltpu.SemaphoreType.DMA((2,))])
    def gather_rows(src_hbm, i_hbm, o_hbm, i_v, buf, in_sem, out_sem):
        base, _ = _sc_worker(M)
        pltpu.sync_copy(i_hbm.at[pl.ds(base, rows)], i_v)

        def load(w, slot):
            return pltpu.make_async_copy(src_hbm.at[i_v.at[pl.ds(w * W, W)]], buf.at[slot], in_sem.at[slot])

        def store(w, slot):
            return pltpu.make_async_copy(buf.at[slot], o_hbm.at[pl.ds(base + w * W, W)], out_sem.at[slot])

        n = rows // W
        load(0, 0).start()

        @pl.loop(0, n, step=2)
        def _(w):
            @pl.when(w > 0)
            def _():
                store(w - 1, 1).wait()

            load(w + 1, 1).start()
            load(w, 0).wait()
            store(w, 0).start()
            store(w, 0).wait()

            @pl.when(w + 2 < n)
            def _():
                load(w + 2, 0).start()

            load(w + 1, 1).wait()
            store(w + 1, 1).start()

        store(n - 1, 1).wait()

    return gather_rows(src, idx)


def _expert_kernel(be_ref, nb_ref, slot_ref, next_ref, x_ref, wg_hbm, wu_hbm, wd_hbm, y_ref,
                   wg_f32, wu_f32, wd_f32, wg_ref, wu_ref, wd_ref, sem):
    i = pl.program_id(0)
    used = i < nb_ref[0]
    new_expert = (i == 0) | (be_ref[i] != be_ref[jnp.maximum(i - 1, 0)])

    def fetch(e, slot):
        pairs = ((wg_hbm, wg_f32), (wu_hbm, wu_f32), (wd_hbm, wd_f32))
        return [pltpu.make_async_copy(w_hbm.at[e], w_f32.at[slot], sem.at[slot, k])
                for k, (w_hbm, w_f32) in enumerate(pairs)]

    @pl.when(used & (i == 0))
    def _():
        for copy in fetch(be_ref[0], slot_ref[0]):
            copy.start()

    @pl.when(used & new_expert)
    def _():
        slot = slot_ref[i]
        for copy in fetch(be_ref[i], slot):
            copy.wait()

        @pl.when(next_ref[i] >= 0)
        def _():
            for copy in fetch(next_ref[i], 1 - slot):
                copy.start()

        wg_ref[...] = wg_f32[slot].astype(BF16)
        wu_ref[...] = wu_f32[slot].astype(BF16)
        wd_ref[...] = wd_f32[slot].astype(BF16)

    @pl.when(used)
    def _():
        for c in range(EXPERT_ROWS // EXPERT_CHAIN_ROWS):
            rs = slice(c * EXPERT_CHAIN_ROWS, (c + 1) * EXPERT_CHAIN_ROWS)
            x = _unpack_pairs(x_ref[rs]).astype(BF16)
            g = _dot(x, wg_ref[...])
            u = _dot(x, wu_ref[...])
            hid = (g * _sigmoid(g) * u).astype(BF16)
            y_ref[rs] = _pack_pairs(_dot(hid, wd_ref[...]))

    @pl.when(jnp.logical_not(used))
    def _():
        y_ref[...] = jnp.zeros_like(y_ref)


def _experts(block_expert, n_used, block_slot, block_next, xs, w_gate, w_up, w_down):
    n_slots, half = xs.shape
    D = 2 * half
    n_blocks = n_slots // EXPERT_ROWS
    grid_spec = pltpu.PrefetchScalarGridSpec(
        num_scalar_prefetch=4,
        grid=(n_blocks,),
        in_specs=[
            pl.BlockSpec((EXPERT_ROWS, half), lambda i, be, nb, sl, nx: (jnp.minimum(i, nb[0] - 1), 0)),
            pl.BlockSpec(memory_space=pl.ANY),
            pl.BlockSpec(memory_space=pl.ANY),
            pl.BlockSpec(memory_space=pl.ANY),
        ],
        out_specs=pl.BlockSpec((EXPERT_ROWS, half), lambda i, be, nb, sl, nx: (i, 0)),
        scratch_shapes=[
            pltpu.VMEM((2, D, D_EXPERT), F32), pltpu.VMEM((2, D, D_EXPERT), F32),
            pltpu.VMEM((2, D_EXPERT, D), F32),
            pltpu.VMEM((D, D_EXPERT), BF16), pltpu.VMEM((D, D_EXPERT), BF16), pltpu.VMEM((D_EXPERT, D), BF16),
            pltpu.SemaphoreType.DMA((2, 3)),
        ],
    )
    return pl.pallas_call(
        _expert_kernel,
        grid_spec=grid_spec,
        out_shape=jax.ShapeDtypeStruct((n_slots, half), jnp.uint32),
        compiler_params=pltpu.CompilerParams(
            dimension_semantics=("arbitrary",), vmem_limit_bytes=VMEM_LIMIT),
        name="experts",
    )(block_expert, n_used, block_slot, block_next, xs, w_gate, w_up, w_down)


def _combine_kernel(y0_ref, y1_ref, h_ref, gates_ref, nfin_ref, *rest):
    o_ref = rest[-1]
    g0 = gates_ref[0].T
    g1 = gates_ref[1].T
    for r in range(gates_ref.shape[1]):
        rs = slice(r * LANES, (r + 1) * LANES)
        h = (h_ref[rs] + g0[:, r:r + 1] * _unpack_pairs(y0_ref[rs])
             + g1[:, r:r + 1] * _unpack_pairs(y1_ref[rs]))
        o_ref[rs] = _rms(h, nfin_ref[...])


def _combine(y01, h2, gates_t, norm_final, out_prev, part, parts):
    T, D = h2.shape
    n_t = T // MOVE_ROWS
    in_specs = [
        pl.BlockSpec((MOVE_ROWS, D // 2), lambda i: (i, 0)),
        pl.BlockSpec((MOVE_ROWS, D // 2), lambda i: (i + n_t, 0)),
        pl.BlockSpec((MOVE_ROWS, D), lambda i: (i, 0)),
        pl.BlockSpec((2, MOVE_ROWS // LANES, LANES), lambda i: (0, i, 0)),
        pl.BlockSpec((1, D), lambda i: (0, 0)),
    ]
    args = [y01, y01, h2, gates_t, norm_final]
    aliases = {}
    if out_prev is not None:
        in_specs.append(pl.BlockSpec(memory_space=pl.ANY))
        args.append(out_prev)
        aliases = {len(args) - 1: 0}
    return pl.pallas_call(
        _combine_kernel,
        grid=(n_t,),
        in_specs=in_specs,
        out_specs=pl.BlockSpec((MOVE_ROWS, D), lambda i: (i + part * n_t, 0)),
        out_shape=jax.ShapeDtypeStruct((parts * T, D), F32),
        input_output_aliases=aliases,
        compiler_params=pltpu.CompilerParams(
            dimension_semantics=("arbitrary",), vmem_limit_bytes=VMEM_LIMIT),
        name="combine",
    )(*args)


def kernel(x, mem, norm_mix, w_in, gmlp_ln, gmlp_w_spatial, gmlp_b_spatial, gmlp_beta, hgrn_lb_logits, hgrn_out_gain, w_out, norm_xattn, norm_mem, w_xq, w_xkv, w_xo, norm_ffn, w_router_group, b_router_group, w_router_expert, b_router_expert, w_expert_gate, w_expert_up, w_expert_down, norm_final):
    B, S, D = x.shape
    T = B * S
    depth = w_in.shape[0]
    assert depth == 1 and hgrn_lb_logits.shape[0] == 2
    l = 0
    row = lambda p: p.reshape(1, -1)

    h1 = _mixer(x, row(norm_mix[l]), w_in[l], row(gmlp_ln[l]), gmlp_w_spatial[l],
                gmlp_b_spatial[l].T, row(gmlp_beta[l]), hgrn_lb_logits, row(hgrn_out_gain[l]),
                w_out[l])
    k_mem, v_mem = _memkv(mem, row(norm_mem[l]), w_xkv[l])

    w_router = jnp.concatenate([w_router_group[l].T, w_router_expert[l].T], axis=0)
    w_router = jnp.pad(w_router, ((0, ROUTER_ROWS - w_router.shape[0]), (0, 0)))
    w_router = w_router.astype(BF16)
    bias = jnp.concatenate([b_router_group[l], b_router_expert[l]]).astype(F32)

    b_part = B // TOKEN_PARTS
    t_part = b_part * S
    n_blocks = (2 * t_part) // EXPERT_ROWS + N_EXPERTS
    out = None
    for part in range(TOKEN_PARTS):
        h2, xn, logits = _xattn(h1, row(norm_xattn[l]), w_xq[l], k_mem, v_mem, w_xo[l], row(norm_ffn[l]),
                                w_router, part * b_part, b_part)
        ids, gates, rank, counts = _route(bias, logits)

        counts = counts[::ROUTE_SUB, 0].astype(jnp.int32)
        padded = (counts + EXPERT_ROWS - 1) // EXPERT_ROWS * EXPERT_ROWS
        seg_end = jnp.cumsum(padded)
        seg_start = seg_end - padded
        block_first_row = jnp.arange(n_blocks, dtype=jnp.int32) * EXPERT_ROWS
        block_expert = jnp.minimum(
            jnp.sum(block_first_row[:, None] >= seg_end[None, :], axis=1), N_EXPERTS - 1).astype(jnp.int32)
        n_used = (seg_end[-1:] // EXPERT_ROWS).astype(jnp.int32)
        present = counts > 0
        expert_ids = jnp.arange(N_EXPERTS, dtype=jnp.int32)
        expert_slot = (jnp.cumsum(present.astype(jnp.int32)) - 1) % 2
        later = jnp.where(present[None, :] & (expert_ids[None, :] > expert_ids[:, None]),
                          expert_ids[None, :], N_EXPERTS)
        expert_next = jnp.min(later, axis=1)
        expert_next = jnp.where(expert_next == N_EXPERTS, -1, expert_next).astype(jnp.int32)
        of_block = (block_expert[:, None] == expert_ids[None, :]).astype(jnp.int32)
        block_slot = jnp.sum(of_block * expert_slot[None, :], axis=1).astype(jnp.int32)
        block_next = jnp.sum(of_block * expert_next[None, :], axis=1).astype(jnp.int32)

        dest = _dest(seg_start, ids, rank).reshape(2, t_part)
        xs = _dispatch(dest[0], dest[1], xn, n_blocks * EXPERT_ROWS)
        yb = _experts(block_expert, n_used, block_slot, block_next, xs,
                      w_expert_gate[l], w_expert_up[l], w_expert_down[l])
        y01 = _gather_rows(yb, dest.reshape(2 * t_part))
        out = _combine(y01, h2.reshape(t_part, D), gates, row(norm_final),
                       out, part, TOKEN_PARTS)
    return out.reshape(B, S, D)
```

```python
import functools

import jax
import jax.numpy as jnp
from jax import lax
from jax.experimental import pallas as pl
from jax.experimental.pallas import tpu as pltpu
from jax.experimental.pallas import tpu_sc as plsc

F32 = jnp.float32
BF16 = jnp.bfloat16
EPS = 1e-6

D_MODEL = 1024
D_GMLP = 512
GMLP_GROUPS = 4
GMLP_CHUNK = 128
D_HGRN = 512
HGRN_HEADS = 4
HGRN_DK = 128
HGRN_CHUNK = 64
D_IN_PROJ = 2 * D_GMLP + 4 * D_HGRN
N_MEM = 256
XATTN_HEADS = 4
XATTN_HEAD_DIM = D_MODEL // XATTN_HEADS
N_GROUPS = 4
EXPERTS_PER_GROUP = 8
N_EXPERTS = N_GROUPS * EXPERTS_PER_GROUP
D_EXPERT = 512

LANES = 128
MIX_ROWS = 1024
MIX_CHAIN_ROWS = 256
def _mix_schedule(n_chains):
    order = [("in", 0), ("gmlp", 0), ("factors", 0)]
    for k in range(n_chains):
        more = k + 1 < n_chains
        order += [("in", k + 1)] * more + [("local", k), ("recurrence", k)] + [("gmlp", k + 1)] * more
        order += [("out", k)] + [("factors", k + 1)] * more
    return tuple(order)


MIX_SCHEDULE = _mix_schedule(MIX_ROWS // MIX_CHAIN_ROWS)
ATT_ROWS = 1024
ATT_CHAIN_ROWS = 1024
WEIGHT_STAGE_ROWS = 128
ROUTER_ROWS = 40
ROUTE_SUB = 16
EXPERT_ROWS = 512
EXPERT_CHAIN_ROWS = 512
MOVE_ROWS = 1024
TOKEN_PARTS = 2
SC_WINDOW = 64
SC_CORES = 2
SC_SUBCORES = 16
_SC_WORKERS = SC_CORES * SC_SUBCORES
VMEM_LIMIT = 48 * 1024 * 1024


def _rms(x, gain):
    return x * lax.rsqrt(jnp.mean(x * x, axis=-1, keepdims=True) + EPS) * gain


def _dot(a, b):
    return jnp.dot(a, b, preferred_element_type=F32)


def _dot_nt(a, b):
    return lax.dot_general(a, b, (((1,), (1,)), ((), ())), preferred_element_type=F32)


def _dot_tn(a, b):
    return lax.dot_general(a, b, (((0,), (0,)), ((), ())), preferred_element_type=F32)


def _gelu(x):
    return 0.5 * x * (1.0 + jnp.tanh(0.7978845608028654 * (x + 0.044715 * (x * x * x))))


def _sigmoid(x):
    return 1.0 / (1.0 + jnp.exp(-x))


def _block_id(idx, size):
    assert size & (size - 1) == 0
    return lax.shift_right_logical(idx, size.bit_length() - 1)


def _stage_weight(w_hbm, w_bf16, stage_ref, sem):
    rows = stage_ref.shape[0]
    for k in range(w_hbm.shape[0] // rows):
        copy = pltpu.make_async_copy(w_hbm.at[pl.ds(k * rows, rows)], stage_ref, sem)
        copy.start()
        copy.wait()
        w_bf16[k * rows:(k + 1) * rows, :] = stage_ref[...].astype(BF16)


def _weight_scratch(k, n):
    return [pltpu.VMEM((k, n), BF16), pltpu.VMEM((WEIGHT_STAGE_ROWS, n), F32)]


_HIGH_HALF = 0xFFFF0000


def _pack_pairs(x):
    c = x.shape[1] // 2
    bits = lax.bitcast_convert_type(x.astype(BF16).astype(F32), jnp.uint32)
    return (bits[:, c:] & jnp.uint32(_HIGH_HALF)) | lax.shift_right_logical(bits[:, :c], jnp.uint32(16))


def _unpack_pairs(w):
    lo = lax.bitcast_convert_type(lax.shift_left(w, jnp.uint32(16)), F32)
    hi = lax.bitcast_convert_type(w & jnp.uint32(_HIGH_HALF), F32)
    return jnp.concatenate([lo, hi], axis=1)


def _mixer_kernel(x_ref, nmix_ref, win_hbm, gln_ref, ws_ref, bst_ref, beta_ref, lbl_ref, og_ref,
                  wout_hbm, o_ref, proj_ref, ycat_ref, state_ref, win_ref, win_stage, wout_ref, wout_stage,
                  sem):
    @pl.when((pl.program_id(0) == 0) & (pl.program_id(1) == 0))
    def _():
        _stage_weight(win_hbm, win_ref, win_stage, sem)
        _stage_weight(wout_hbm, wout_ref, wout_stage, sem)

    @pl.when(pl.program_id(1) == 0)
    def _():
        state_ref[...] = jnp.zeros_like(state_ref)

    n = MIX_CHAIN_ROWS
    r_i = lax.broadcasted_iota(jnp.int32, (GMLP_CHUNK, GMLP_CHUNK), 0)
    c_i = lax.broadcasted_iota(jnp.int32, (GMLP_CHUNK, GMLP_CHUNK), 1)
    causal = c_i <= r_i
    w_tril = [jnp.where(causal, ws_ref[g], 0.0).astype(BF16) for g in range(GMLP_GROUPS)]
    lbl = lbl_ref[...]
    e_lb = jnp.exp(lbl - jnp.max(lbl, axis=0, keepdims=True))
    lb = e_lb[0:1] / jnp.sum(e_lb, axis=0, keepdims=True)
    rr = lax.broadcasted_iota(jnp.int32, (n, n), 0)
    cc = lax.broadcasted_iota(jnp.int32, (n, n), 1)
    tri = jnp.where((_block_id(rr, HGRN_CHUNK) == _block_id(cc, HGRN_CHUNK)) & (cc <= rr),
                    1.0, 0.0).astype(BF16)
    r64 = lax.broadcasted_iota(jnp.int32, (HGRN_CHUNK, HGRN_CHUNK), 0)
    c64 = lax.broadcasted_iota(jnp.int32, (HGRN_CHUNK, HGRN_CHUNK), 1)
    causal64 = c64 <= r64
    base = 2 * D_GMLP

    n_chunks = n // HGRN_CHUNK
    chains = range(x_ref.shape[1] // n)
    env = {ch: {} for ch in chains}

    def rows(ch):
        return slice(ch * n, (ch + 1) * n)

    def in_proj(ch):
        a = _rms(x_ref[0, rows(ch)], nmix_ref[...]).astype(BF16)
        proj_ref[rows(ch)] = _dot(a, win_ref[...])

    def gmlp(ch):
        u = _gelu(proj_ref[rows(ch), 0:D_GMLP])
        v = _gelu(proj_ref[rows(ch), D_GMLP:2 * D_GMLP])
        vc = v - jnp.mean(v, axis=-1, keepdims=True)
        vn = (vc * lax.rsqrt(jnp.mean(vc * vc, axis=-1, keepdims=True) + EPS) * gln_ref[...]).astype(BF16)
        z_rows = []
        for c in range(n // GMLP_CHUNK):
            z_cols = []
            for g in range(GMLP_GROUPS):
                vg = vn[c * GMLP_CHUNK:(c + 1) * GMLP_CHUNK, g * LANES:(g + 1) * LANES]
                z_cols.append(_dot(w_tril[g], vg) + bst_ref[:, g:g + 1])
            z_rows.append(jnp.concatenate(z_cols, axis=1))
        z = jnp.concatenate(z_rows, axis=0)
        ycat_ref[rows(ch), 0:D_GMLP] = _rms(u * z, beta_ref[...]).astype(BF16)

    def hgrn_factors(ch):
        e = env[ch]
        f = lb + (1.0 - lb) * _sigmoid(proj_ref[rows(ch), base + D_HGRN:base + 2 * D_HGRN])
        log_f = jnp.log(f)
        lf_hi = log_f.astype(BF16)
        lf_lo = (log_f - lf_hi.astype(F32)).astype(BF16)
        b_all = _dot(tri, lf_hi) + _dot(tri, lf_lo)
        bl_rows = [b_all[c * HGRN_CHUNK + HGRN_CHUNK - 1:(c + 1) * HGRN_CHUNK] for c in range(n_chunks)]
        bl_all = jnp.concatenate([jnp.broadcast_to(r, (HGRN_CHUNK, D_HGRN)) for r in bl_rows], axis=0)
        q_all = proj_ref[rows(ch), base:base + D_HGRN]
        k_all = 1.0 - f
        e["qd"] = (q_all * _sigmoid(q_all) * jnp.exp(b_all)).astype(BF16)
        e["ki"] = (k_all * jnp.exp(-b_all)).astype(BF16)
        e["kte"] = (k_all * jnp.exp(bl_all - b_all)).astype(BF16)
        e["v"] = proj_ref[rows(ch), base + 2 * D_HGRN:base + 3 * D_HGRN].astype(BF16)
        e["decay"] = [jnp.exp(r) for r in bl_rows]

    def hgrn_local(ch):
        e = env[ch]
        e["o_intra"], e["d_state"] = {}, {}
        for c in range(n_chunks):
            rs = slice(c * HGRN_CHUNK, (c + 1) * HGRN_CHUNK)
            for h in range(HGRN_HEADS):
                cs = slice(h * HGRN_DK, (h + 1) * HGRN_DK)
                scores = jnp.where(causal64, _dot_nt(e["qd"][rs, cs], e["ki"][rs, cs]), 0.0).astype(BF16)
                e["o_intra"][c, h] = _dot(scores, e["v"][rs, cs])
                e["d_state"][c, h] = _dot_tn(e["v"][rs, cs], e["kte"][rs, cs])

    def hgrn_recurrence(ch):
        e = env[ch]
        for c in range(n_chunks):
            rs = slice(c * HGRN_CHUNK, (c + 1) * HGRN_CHUNK)
            ps = slice(ch * n + c * HGRN_CHUNK, ch * n + (c + 1) * HGRN_CHUNK)
            g_c = proj_ref[ps, base + 3 * D_HGRN:base + 4 * D_HGRN]
            gate = og_ref[...] * (g_c * _sigmoid(g_c))
            for h in range(HGRN_HEADS):
                cs = slice(h * HGRN_DK, (h + 1) * HGRN_DK)
                st = state_ref[h]
                o = e["o_intra"][c, h] + _dot_nt(e["qd"][rs, cs], st.astype(BF16))
                state_ref[h] = st * e["decay"][c][:, cs] + e["d_state"][c, h]
                o = o * lax.rsqrt(jnp.mean(o * o, axis=-1, keepdims=True) + EPS)
                ycat_ref[ps, D_GMLP + h * HGRN_DK:D_GMLP + (h + 1) * HGRN_DK] = (o * gate[:, cs]).astype(BF16)

    def out_proj(ch):
        o_ref[0, rows(ch)] = x_ref[0, rows(ch)] + _dot(ycat_ref[rows(ch)], wout_ref[...])

    stages = {"in": in_proj, "gmlp": gmlp, "factors": hgrn_factors, "local": hgrn_local,
              "recurrence": hgrn_recurrence, "out": out_proj}
    for stage, ch in MIX_SCHEDULE:
        stages[stage](ch)


def _mixer(x, norm_mix, w_in, gmlp_ln, w_s, b_s_t, beta, lb_logits, out_gain, w_out):
    B, S, D = x.shape
    const2 = lambda b, s: (0, 0)
    return pl.pallas_call(
        _mixer_kernel,
        grid=(B, S // MIX_ROWS),
        in_specs=[
            pl.BlockSpec((1, MIX_ROWS, D), lambda b, s: (b, s, 0)),
            pl.BlockSpec((1, D), const2),
            pl.BlockSpec(memory_space=pl.ANY),
            pl.BlockSpec((1, D_GMLP), const2),
            pl.BlockSpec((GMLP_GROUPS, GMLP_CHUNK, GMLP_CHUNK), lambda b, s: (0, 0, 0)),
            pl.BlockSpec((GMLP_CHUNK, GMLP_GROUPS), const2),
            pl.BlockSpec((1, D_GMLP), const2),
            pl.BlockSpec(lb_logits.shape, const2),
            pl.BlockSpec((1, D_HGRN), const2),
            pl.BlockSpec(memory_space=pl.ANY),
        ],
        out_specs=pl.BlockSpec((1, MIX_ROWS, D), lambda b, s: (b, s, 0)),
        out_shape=jax.ShapeDtypeStruct((B, S, D), F32),
        scratch_shapes=[
            pltpu.VMEM((MIX_ROWS, D_IN_PROJ), F32),
            pltpu.VMEM((MIX_ROWS, D), BF16),
            pltpu.VMEM((HGRN_HEADS, HGRN_DK, HGRN_DK), F32),
        ] + _weight_scratch(D, D_IN_PROJ) + _weight_scratch(D, D) + [pltpu.SemaphoreType.DMA],
        compiler_params=pltpu.CompilerParams(
            dimension_semantics=("arbitrary", "arbitrary"), vmem_limit_bytes=VMEM_LIMIT),
        name="mixer",
    )(x, norm_mix, w_in, gmlp_ln, w_s, b_s_t, beta, lb_logits, out_gain, w_out)


def _memkv_kernel(mem_ref, nm_ref, wkv_hbm, k_ref, v_ref, wkv_ref, wkv_stage, sem):
    @pl.when(pl.program_id(0) == 0)
    def _():
        _stage_weight(wkv_hbm, wkv_ref, wkv_stage, sem)

    m = _rms(mem_ref[0], nm_ref[...]).astype(BF16)
    kv = _dot(m, wkv_ref[...])
    k_ref[0] = kv[:, :D_MODEL].astype(BF16)
    v_ref[0] = kv[:, D_MODEL:].astype(BF16)


def _memkv(mem, norm_mem, w_kv):
    B, M, D = mem.shape
    out = jax.ShapeDtypeStruct((B, M, D), BF16)
    return pl.pallas_call(
        _memkv_kernel,
        grid=(B,),
        in_specs=[
            pl.BlockSpec((1, M, D), lambda b: (b, 0, 0)),
            pl.BlockSpec((1, D), lambda b: (0, 0)),
            pl.BlockSpec(memory_space=pl.ANY),
        ],
        out_specs=[pl.BlockSpec((1, M, D), lambda b: (b, 0, 0))] * 2,
        out_shape=[out, out],
        scratch_shapes=_weight_scratch(D, 2 * D) + [pltpu.SemaphoreType.DMA],
        compiler_params=pltpu.CompilerParams(
            dimension_semantics=("arbitrary",), vmem_limit_bytes=VMEM_LIMIT),
        name="memkv",
    )(mem, norm_mem, w_kv)


def _xattn_kernel(h_ref, nx_ref, wq_hbm, k_ref, v_ref, wo_hbm, nf_ref, wr_ref,
                  h2_ref, xn_ref, lg_ref, att_ref, wq_ref, wq_stage, wo_ref, wo_stage, sem):
    @pl.when((pl.program_id(0) == 0) & (pl.program_id(1) == 0))
    def _():
        _stage_weight(wq_hbm, wq_ref, wq_stage, sem)
        _stage_weight(wo_hbm, wo_ref, wo_stage, sem)

    for c in range(ATT_ROWS // ATT_CHAIN_ROWS):
        rs = slice(c * ATT_CHAIN_ROWS, (c + 1) * ATT_CHAIN_ROWS)
        h = h_ref[0, rs]
        hn = _rms(h, nx_ref[...]).astype(BF16)
        q = (_dot(hn, wq_ref[...]) * (XATTN_HEAD_DIM ** -0.5)).astype(BF16)
        heads = [slice(hd * XATTN_HEAD_DIM, (hd + 1) * XATTN_HEAD_DIM) for hd in range(XATTN_HEADS)]
        scores = [_dot_nt(q[:, cs], k_ref[0, :, cs]) for cs in heads]
        probs = []
        for s in scores:
            p = jnp.exp(s - jnp.max(s, axis=-1, keepdims=True))
            probs.append((p / jnp.sum(p, axis=-1, keepdims=True)).astype(BF16))
        for cs, p in zip(heads, probs):
            att_ref[rs, cs] = _dot(p, v_ref[0, :, cs]).astype(BF16)
        h2 = h + _dot(att_ref[rs], wo_ref[...])
        h2_ref[0, rs] = h2
        xn = _rms(h2, nf_ref[...])
        xn_ref[rs] = _pack_pairs(xn)
        lg = _dot_nt(wr_ref[...], xn.astype(BF16))
        for j in range(ATT_CHAIN_ROWS // LANES):
            lg_ref[:, c * (ATT_CHAIN_ROWS // LANES) + j, :] = lg[:, j * LANES:(j + 1) * LANES]


def _xattn(h1, norm_x, w_q, k_mem, v_mem, w_o, norm_ffn, w_router, batch0, batches):
    _, S, D = h1.shape
    n_s = S // ATT_ROWS
    const2 = lambda b, s: (0, 0)
    return pl.pallas_call(
        _xattn_kernel,
        grid=(batches, n_s),
        in_specs=[
            pl.BlockSpec((1, ATT_ROWS, D), lambda b, s: (b + batch0, s, 0)),
            pl.BlockSpec((1, D), const2),
            pl.BlockSpec(memory_space=pl.ANY),
            pl.BlockSpec((1, N_MEM, D), lambda b, s: (b + batch0, 0, 0)),
            pl.BlockSpec((1, N_MEM, D), lambda b, s: (b + batch0, 0, 0)),
            pl.BlockSpec(memory_space=pl.ANY),
            pl.BlockSpec((1, D), const2),
            pl.BlockSpec((ROUTER_ROWS, D), const2),
        ],
        out_specs=[
            pl.BlockSpec((1, ATT_ROWS, D), lambda b, s: (b, s, 0)),
            pl.BlockSpec((ATT_ROWS, D // 2), lambda b, s: (b * n_s + s, 0)),
            pl.BlockSpec((ROUTER_ROWS, ATT_ROWS // LANES, LANES), lambda b, s: (0, b * n_s + s, 0)),
        ],
        out_shape=[
            jax.ShapeDtypeStruct((batches, S, D), F32),
            jax.ShapeDtypeStruct((batches * S, D // 2), jnp.uint32),
            jax.ShapeDtypeStruct((ROUTER_ROWS, batches * S // LANES, LANES), F32),
        ],
        scratch_shapes=([pltpu.VMEM((ATT_ROWS, D), BF16)] + _weight_scratch(D, D) + _weight_scratch(D, D)
                        + [pltpu.SemaphoreType.DMA]),
        compiler_params=pltpu.CompilerParams(
            dimension_semantics=("arbitrary", "arbitrary"), vmem_limit_bytes=VMEM_LIMIT),
        name="xattn",
    )(h1, norm_x, w_q, k_mem, v_mem, w_o, norm_ffn, w_router)


def _route_kernel(bias_ref, lg_ref, ids_ref, gates_ref, rank_ref, cnt_ref, base_ref):
    sub = lg_ref.shape[1]

    @pl.when(pl.program_id(0) == 0)
    def _():
        base_ref[...] = jnp.zeros_like(base_ref)

    best = lg_ref[0] + bias_ref[0]
    gl = [best]
    sel = jnp.zeros(best.shape, jnp.int32)
    for g in range(1, N_GROUPS):
        cur = lg_ref[g] + bias_ref[g]
        gl.append(cur)
        better = cur > best
        best = jnp.where(better, cur, best)
        sel = jnp.where(better, g, sel)
    denom = jnp.exp(gl[0] - best)
    for g in range(1, N_GROUPS):
        denom = denom + jnp.exp(gl[g] - best)
    g_gate = 1.0 / denom

    ev = []
    for j in range(EXPERTS_PER_GROUP):
        val = lg_ref[N_GROUPS + j] + bias_ref[N_GROUPS + j]
        for g in range(1, N_GROUPS):
            e = g * EXPERTS_PER_GROUP + j
            val = jnp.where(sel == g, lg_ref[N_GROUPS + e] + bias_ref[N_GROUPS + e], val)
        ev.append(val)
    v1, i1 = ev[0], jnp.zeros(best.shape, jnp.int32)
    for j in range(1, EXPERTS_PER_GROUP):
        better = ev[j] > v1
        v1 = jnp.where(better, ev[j], v1)
        i1 = jnp.where(better, j, i1)
    rest = [jnp.where(i1 == j, -jnp.inf, ev[j]) for j in range(EXPERTS_PER_GROUP)]
    v2, i2 = rest[0], jnp.zeros(best.shape, jnp.int32)
    for j in range(1, EXPERTS_PER_GROUP):
        better = rest[j] > v2
        v2 = jnp.where(better, rest[j], v2)
        i2 = jnp.where(better, j, i2)
    e2 = jnp.exp(v2 - v1)
    inv = 1.0 / (1.0 + e2)
    id1 = sel * EXPERTS_PER_GROUP + i1
    id2 = sel * EXPERTS_PER_GROUP + i2
    ids_ref[0] = id1
    ids_ref[1] = id2
    gates_ref[0] = inv * g_gate
    gates_ref[1] = e2 * inv * g_gate

    member = jnp.concatenate(
        [jnp.where((id1 == e) | (id2 == e), 1.0, 0.0) for e in range(N_EXPERTS)], axis=0).astype(BF16)
    n = N_EXPERTS * sub
    li = lax.broadcasted_iota(jnp.int32, (LANES, LANES), 0)
    lj = lax.broadcasted_iota(jnp.int32, (LANES, LANES), 1)
    before_lane = jnp.where(li < lj, 1.0, 0.0).astype(BF16)
    ones = jnp.ones((LANES, LANES), BF16)
    ri = lax.broadcasted_iota(jnp.int32, (n, n), 0)
    rj = lax.broadcasted_iota(jnp.int32, (n, n), 1)
    same = _block_id(ri, sub) == _block_id(rj, sub)
    before_row = jnp.where(same & (rj < ri), 1.0, 0.0).astype(BF16)
    all_row = jnp.where(same, 1.0, 0.0).astype(BF16)
    in_row = _dot(member, before_lane)
    prev_rows = _dot(_dot(before_row, member).astype(BF16), ones)
    total = _dot(_dot(all_row, member).astype(BF16), ones)
    base = base_ref[...]
    pos = base + prev_rows + in_row
    r1 = jnp.zeros(best.shape, F32)
    r2 = jnp.zeros(best.shape, F32)
    for e in range(N_EXPERTS):
        pe = pos[e * sub:(e + 1) * sub]
        r1 = jnp.where(id1 == e, pe, r1)
        r2 = jnp.where(id2 == e, pe, r2)
    rank_ref[0] = r1.astype(jnp.int32)
    rank_ref[1] = r2.astype(jnp.int32)
    base_ref[...] = base + total
    cnt_ref[...] = base + total


def _route(bias, logits3):
    rp, n_sub, _ = logits3.shape
    blk = lambda i: (0, i, 0)
    pair_i = jax.ShapeDtypeStruct((2, n_sub, LANES), jnp.int32)
    return pl.pallas_call(
        _route_kernel,
        grid=(n_sub // ROUTE_SUB,),
        in_specs=[
            pl.BlockSpec(memory_space=pltpu.SMEM),
            pl.BlockSpec((rp, ROUTE_SUB, LANES), blk),
        ],
        out_specs=[
            pl.BlockSpec((2, ROUTE_SUB, LANES), blk),
            pl.BlockSpec((2, ROUTE_SUB, LANES), blk),
            pl.BlockSpec((2, ROUTE_SUB, LANES), blk),
            pl.BlockSpec((N_EXPERTS * ROUTE_SUB, LANES), lambda i: (0, 0)),
        ],
        out_shape=[
            pair_i,
            jax.ShapeDtypeStruct((2, n_sub, LANES), F32),
            pair_i,
            jax.ShapeDtypeStruct((N_EXPERTS * ROUTE_SUB, LANES), F32),
        ],
        scratch_shapes=[pltpu.VMEM((N_EXPERTS * ROUTE_SUB, LANES), F32)],
        compiler_params=pltpu.CompilerParams(
            dimension_semantics=("arbitrary",), vmem_limit_bytes=VMEM_LIMIT),
        name="route",
    )(bias, logits3)


def _dest_kernel(start_ref, ids_ref, rank_ref, dest_ref):
    ids = ids_ref[...]
    off = jnp.zeros(ids.shape, jnp.int32)
    for e in range(N_EXPERTS):
        off = jnp.where(ids == e, start_ref[e], off)
    dest_ref[...] = rank_ref[...] + off


def _dest(seg_start, ids, rank):
    _, n_sub, _ = ids.shape
    blk = pl.BlockSpec((2, ROUTE_SUB, LANES), lambda i: (0, i, 0))
    return pl.pallas_call(
        _dest_kernel,
        grid=(n_sub // ROUTE_SUB,),
        in_specs=[pl.BlockSpec(memory_space=pltpu.SMEM), blk, blk],
        out_specs=blk,
        out_shape=jax.ShapeDtypeStruct(ids.shape, jnp.int32),
        name="dest",
    )(seg_start, ids, rank)


def _sc_mesh():
    return plsc.VectorSubcoreMesh(core_axis_name="core", subcore_axis_name="subcore")


def _sc_worker(rows_total):
    rows = rows_total // _SC_WORKERS
    wid = lax.axis_index("core") * SC_SUBCORES + lax.axis_index("subcore")
    return wid * rows, rows


def _dispatch(dest0, dest1, xn, n_slots):
    T, D = xn.shape
    W = SC_WINDOW
    rows = T // _SC_WORKERS

    @pl.kernel(out_type=jax.ShapeDtypeStruct((n_slots, D), xn.dtype), mesh=_sc_mesh(),
               scratch_types=[pltpu.VMEM((rows,), jnp.int32), pltpu.VMEM((rows,), jnp.int32),
                              pltpu.VMEM((2, W, D), xn.dtype),
                              pltpu.SemaphoreType.DMA((2,)), pltpu.SemaphoreType.DMA((2,))])
    def scatter_rows(x_hbm, d0_hbm, d1_hbm, xs_hbm, d0_v, d1_v, buf, in_sem, out_sem):
        base, _ = _sc_worker(T)
        pltpu.sync_copy(d0_hbm.at[pl.ds(base, rows)], d0_v)
        pltpu.sync_copy(d1_hbm.at[pl.ds(base, rows)], d1_v)

        def load(w, slot):
            return pltpu.make_async_copy(x_hbm.at[pl.ds(base + w * W, W)], buf.at[slot], in_sem.at[slot])

        def store(w, slot, d_v):
            return pltpu.make_async_copy(buf.at[slot], xs_hbm.at[d_v.at[pl.ds(w * W, W)]], out_sem.at[slot])

        def step(w, slot):
            load(w, slot).wait()
            store(w, slot, d0_v).start()
            store(w, slot, d1_v).start()
            store(w, slot, d0_v).wait()
            store(w, slot, d1_v).wait()

        n = rows // W
        load(0, 0).start()

        @pl.loop(0, n, step=2)
        def _(w):
            load(w + 1, 1).start()
            step(w, 0)

            @pl.when(w + 2 < n)
            def _():
                load(w + 2, 0).start()

            step(w + 1, 1)

    return scatter_rows(xn, dest0, dest1)


def _gather_rows(src, idx):
    M = idx.shape[0]
    D = src.shape[1]
    W = SC_WINDOW
    rows = M // _SC_WORKERS

    @pl.kernel(out_type=jax.ShapeDtypeStruct((M, D), src.dtype), mesh=_sc_mesh(),
               scratch_types=[pltpu.VMEM((rows,), jnp.int32), pltpu.VMEM((2, W, D), src.dtype),
                              pltpu.SemaphoreType.DMA((2,)), pltpu.SemaphoreType.DMA((2,))])
    def gather_rows(src_hbm, i_hbm, o_hbm, i_v, buf, in_sem, out_sem):
        base, _ = _sc_worker(M)
        pltpu.sync_copy(i_hbm.at[pl.ds(base, rows)], i_v)

        def load(w, slot):
            return pltpu.make_async_copy(src_hbm.at[i_v.at[pl.ds(w * W, W)]], buf.at[slot], in_sem.at[slot])

        def store(w, slot):
            return pltpu.make_async_copy(buf.at[slot], o_hbm.at[pl.ds(base + w * W, W)], out_sem.at[slot])

        n = rows // W
        load(0, 0).start()

        @pl.loop(0, n, step=2)
        def _(w):
            @pl.when(w > 0)
            def _():
                store(w - 1, 1).wait()

            load(w + 1, 1).start()
            load(w, 0).wait()
            store(w, 0).start()
            store(w, 0).wait()

            @pl.when(w + 2 < n)
            def _():
                load(w + 2, 0).start()

            load(w + 1, 1).wait()
            store(w + 1, 1).start()

        store(n - 1, 1).wait()

    return gather_rows(src, idx)


def _expert_kernel(be_ref, nb_ref, slot_ref, next_ref, x_ref, wg_hbm, wu_hbm, wd_hbm, y_ref,
                   wg_f32, wu_f32, wd_f32, wg_ref, wu_ref, wd_ref, sem):
    i = pl.program_id(0)
    used = i < nb_ref[0]
    new_expert = (i == 0) | (be_ref[i] != be_ref[jnp.maximum(i - 1, 0)])

    def fetch(e, slot):
        pairs = ((wg_hbm, wg_f32), (wu_hbm, wu_f32), (wd_hbm, wd_f32))
        return [pltpu.make_async_copy(w_hbm.at[e], w_f32.at[slot], sem.at[slot, k])
                for k, (w_hbm, w_f32) in enumerate(pairs)]

    @pl.when(used & (i == 0))
    def _():
        for copy in fetch(be_ref[0], slot_ref[0]):
            copy.start()

    @pl.when(used & new_expert)
    def _():
        slot = slot_ref[i]
        for copy in fetch(be_ref[i], slot):
            copy.wait()

        @pl.when(next_ref[i] >= 0)
        def _():
            for copy in fetch(next_ref[i], 1 - slot):
                copy.start()

        wg_ref[...] = wg_f32[slot].astype(BF16)
        wu_ref[...] = wu_f32[slot].astype(BF16)
        wd_ref[...] = wd_f32[slot].astype(BF16)

    @pl.when(used)
    def _():
        for c in range(EXPERT_ROWS // EXPERT_CHAIN_ROWS):
            rs = slice(c * EXPERT_CHAIN_ROWS, (c + 1) * EXPERT_CHAIN_ROWS)
            x = _unpack_pairs(x_ref[rs]).astype(BF16)
            g = _dot(x, wg_ref[...])
            u = _dot(x, wu_ref[...])
            hid = (g * _sigmoid(g) * u).astype(BF16)
            y_ref[rs] = _pack_pairs(_dot(hid, wd_ref[...]))

    @pl.when(jnp.logical_not(used))
    def _():
        y_ref[...] = jnp.zeros_like(y_ref)


def _experts(block_expert, n_used, block_slot, block_next, xs, w_gate, w_up, w_down):
    n_slots, half = xs.shape
    D = 2 * half
    n_blocks = n_slots // EXPERT_ROWS
    grid_spec = pltpu.PrefetchScalarGridSpec(
        num_scalar_prefetch=4,
        grid=(n_blocks,),
        in_specs=[
            pl.BlockSpec((EXPERT_ROWS, half), lambda i, be, nb, sl, nx: (jnp.minimum(i, nb[0] - 1), 0)),
            pl.BlockSpec(memory_space=pl.ANY),
            pl.BlockSpec(memory_space=pl.ANY),
            pl.BlockSpec(memory_space=pl.ANY),
        ],
        out_specs=pl.BlockSpec((EXPERT_ROWS, half), lambda i, be, nb, sl, nx: (i, 0)),
        scratch_shapes=[
            pltpu.VMEM((2, D, D_EXPERT), F32), pltpu.VMEM((2, D, D_EXPERT), F32),
            pltpu.VMEM((2, D_EXPERT, D), F32),
            pltpu.VMEM((D, D_EXPERT), BF16), pltpu.VMEM((D, D_EXPERT), BF16), pltpu.VMEM((D_EXPERT, D), BF16),
            pltpu.SemaphoreType.DMA((2, 3)),
        ],
    )
    return pl.pallas_call(
        _expert_kernel,
        grid_spec=grid_spec,
        out_shape=jax.ShapeDtypeStruct((n_slots, half), jnp.uint32),
        compiler_params=pltpu.CompilerParams(
            dimension_semantics=("arbitrary",), vmem_limit_bytes=VMEM_LIMIT),
        name="experts",
    )(block_expert, n_used, block_slot, block_next, xs, w_gate, w_up, w_down)


def _combine_kernel(y0_ref, y1_ref, h_ref, gates_ref, nfin_ref, *rest):
    o_ref = rest[-1]
    g0 = gates_ref[0].T
    g1 = gates_ref[1].T
    for r in range(gates_ref.shape[1]):
        rs = slice(r * LANES, (r + 1) * LANES)
        h = (h_ref[rs] + g0[:, r:r + 1] * _unpack_pairs(y0_ref[rs])
             + g1[:, r:r + 1] * _unpack_pairs(y1_ref[rs]))
        o_ref[rs] = _rms(h, nfin_ref[...])


def _combine(y01, h2, gates_t, norm_final, out_prev, part, parts):
    T, D = h2.shape
    n_t = T // MOVE_ROWS
    in_specs = [
        pl.BlockSpec((MOVE_ROWS, D // 2), lambda i: (i, 0)),
        pl.BlockSpec((MOVE_ROWS, D // 2), lambda i: (i + n_t, 0)),
        pl.BlockSpec((MOVE_ROWS, D), lambda i: (i, 0)),
        pl.BlockSpec((2, MOVE_ROWS // LANES, LANES), lambda i: (0, i, 0)),
        pl.BlockSpec((1, D), lambda i: (0, 0)),
    ]
    args = [y01, y01, h2, gates_t, norm_final]
    aliases = {}
    if out_prev is not None:
        in_specs.append(pl.BlockSpec(memory_space=pl.ANY))
        args.append(out_prev)
        aliases = {len(args) - 1: 0}
    return pl.pallas_call(
        _combine_kernel,
        grid=(n_t,),
        in_specs=in_specs,
        out_specs=pl.BlockSpec((MOVE_ROWS, D), lambda i: (i + part * n_t, 0)),
        out_shape=jax.ShapeDtypeStruct((parts * T, D), F32),
        input_output_aliases=aliases,
        compiler_params=pltpu.CompilerParams(
            dimension_semantics=("arbitrary",), vmem_limit_bytes=VMEM_LIMIT),
        name="combine",
    )(*args)


def kernel(x, mem, norm_mix, w_in, gmlp_ln, gmlp_w_spatial, gmlp_b_spatial, gmlp_beta, hgrn_lb_logits, hgrn_out_gain, w_out, norm_xattn, norm_mem, w_xq, w_xkv, w_xo, norm_ffn, w_router_group, b_router_group, w_router_expert, b_router_expert, w_expert_gate, w_expert_up, w_expert_down, norm_final):
    B, S, D = x.shape
    T = B * S
    depth = w_in.shape[0]
    assert depth == 1 and hgrn_lb_logits.shape[0] == 2
    l = 0
    row = lambda p: p.reshape(1, -1)

    h1 = _mixer(x, row(norm_mix[l]), w_in[l], row(gmlp_ln[l]), gmlp_w_spatial[l],
                gmlp_b_spatial[l].T, row(gmlp_beta[l]), hgrn_lb_logits, row(hgrn_out_gain[l]),
                w_out[l])
    k_mem, v_mem = _memkv(mem, row(norm_mem[l]), w_xkv[l])

    w_router = jnp.concatenate([w_router_group[l].T, w_router_expert[l].T], axis=0)
    w_router = jnp.pad(w_router, ((0, ROUTER_ROWS - w_router.shape[0]), (0, 0)))
    w_router = w_router.astype(BF16)
    bias = jnp.concatenate([b_router_group[l], b_router_expert[l]]).astype(F32)

    b_part = B // TOKEN_PARTS
    t_part = b_part * S
    n_blocks = (2 * t_part) // EXPERT_ROWS + N_EXPERTS
    out = None
    for part in range(TOKEN_PARTS):
        h2, xn, logits = _xattn(h1, row(norm_xattn[l]), w_xq[l], k_mem, v_mem, w_xo[l], row(norm_ffn[l]),
                                w_router, part * b_part, b_part)
        ids, gates, rank, counts = _route(bias, logits)

        counts = counts[::ROUTE_SUB, 0].astype(jnp.int32)
        padded = (counts + EXPERT_ROWS - 1) // EXPERT_ROWS * EXPERT_ROWS
        seg_end = jnp.cumsum(padded)
        seg_start = seg_end - padded
        block_first_row = jnp.arange(n_blocks, dtype=jnp.int32) * EXPERT_ROWS
        block_expert = jnp.minimum(
            jnp.sum(block_first_row[:, None] >= seg_end[None, :], axis=1), N_EXPERTS - 1).astype(jnp.int32)
        n_used = (seg_end[-1:] // EXPERT_ROWS).astype(jnp.int32)
        present = counts > 0
        expert_ids = jnp.arange(N_EXPERTS, dtype=jnp.int32)
        expert_slot = (jnp.cumsum(present.astype(jnp.int32)) - 1) % 2
        later = jnp.where(present[None, :] & (expert_ids[None, :] > expert_ids[:, None]),
                          expert_ids[None, :], N_EXPERTS)
        expert_next = jnp.min(later, axis=1)
        expert_next = jnp.where(expert_next == N_EXPERTS, -1, expert_next).astype(jnp.int32)
        of_block = (block_expert[:, None] == expert_ids[None, :]).astype(jnp.int32)
        block_slot = jnp.sum(of_block * expert_slot[None, :], axis=1).astype(jnp.int32)
        block_next = jnp.sum(of_block * expert_next[None, :], axis=1).astype(jnp.int32)

        dest = _dest(seg_start, ids, rank).reshape(2, t_part)
        xs = _dispatch(dest[0], dest[1], xn, n_blocks * EXPERT_ROWS)
        yb = _experts(block_expert, n_used, block_slot, block_next, xs,
                      w_expert_gate[l], w_expert_up[l], w_expert_down[l])
        y01 = _gather_rows(yb, dest.reshape(2 * t_part))
        out = _combine(y01, h2.reshape(t_part, D), gates, row(norm_final),
                       out, part, TOKEN_PARTS)
    return out.reshape(B, S, D)
```

```python
import functools

import jax
import jax.numpy as jnp
from jax import lax
from jax.experimental import pallas as pl
from jax.experimental.pallas import tpu as pltpu
from jax.experimental.pallas import tpu_sc as plsc

F32 = jnp.float32
BF16 = jnp.bfloat16
EPS = 1e-6

D_MODEL = 1024
D_GMLP = 512
GMLP_GROUPS = 4
GMLP_CHUNK = 128
D_HGRN = 512
HGRN_HEADS = 4
HGRN_DK = 128
HGRN_CHUNK = 64
D_IN_PROJ = 2 * D_GMLP + 4 * D_HGRN
N_MEM = 256
XATTN_HEADS = 4
XATTN_HEAD_DIM = D_MODEL // XATTN_HEADS
N_GROUPS = 4
EXPERTS_PER_GROUP = 8
N_EXPERTS = N_GROUPS * EXPERTS_PER_GROUP
D_EXPERT = 512

LANES = 128
MIX_ROWS = 1024
MIX_CHAIN_ROWS = 256
def _mix_schedule(n_chains):
    order = [("in", 0), ("gmlp", 0), ("factors", 0)]
    for k in range(n_chains):
        more = k + 1 < n_chains
        order += [("in", k + 1)] * more + [("local", k), ("recurrence", k)] + [("gmlp", k + 1)] * more
        order += [("out", k)] + [("factors", k + 1)] * more
    return tuple(order)


MIX_SCHEDULE = _mix_schedule(MIX_ROWS // MIX_CHAIN_ROWS)
MEMKV_BATCHES = 4
ATT_ROWS = 1024
ATT_CHAIN_ROWS = 1024
WEIGHT_STAGE_ROWS = 128
ROUTER_ROWS = 40
ROUTE_SUB = 16
EXPERT_ROWS = 512
EXPERT_CHAIN_ROWS = 512
MOVE_ROWS = 1024
TOKEN_PARTS = 2
SC_WINDOW = 64
SC_CORES = 2
SC_SUBCORES = 16
_SC_WORKERS = SC_CORES * SC_SUBCORES
VMEM_LIMIT = 48 * 1024 * 1024


def _rms(x, gain):
    return x * lax.rsqrt(jnp.mean(x * x, axis=-1, keepdims=True) + EPS) * gain


def _dot(a, b):
    return jnp.dot(a, b, preferred_element_type=F32)


def _dot_nt(a, b):
    return lax.dot_general(a, b, (((1,), (1,)), ((), ())), preferred_element_type=F32)


def _dot_tn(a, b):
    return lax.dot_general(a, b, (((0,), (0,)), ((), ())), preferred_element_type=F32)


def _gelu(x):
    return 0.5 * x * (1.0 + jnp.tanh(0.7978845608028654 * (x + 0.044715 * (x * x * x))))


def _sigmoid(x):
    return 1.0 / (1.0 + jnp.exp(-x))


def _block_id(idx, size):
    assert size & (size - 1) == 0
    return lax.shift_right_logical(idx, size.bit_length() - 1)


def _stage_weight(w_hbm, w_bf16, stage_ref, sem):
    rows = stage_ref.shape[0]
    for k in range(w_hbm.shape[0] // rows):
        copy = pltpu.make_async_copy(w_hbm.at[pl.ds(k * rows, rows)], stage_ref, sem)
        copy.start()
        copy.wait()
        w_bf16[k * rows:(k + 1) * rows, :] = stage_ref[...].astype(BF16)


def _weight_scratch(k, n):
    return [pltpu.VMEM((k, n), BF16), pltpu.VMEM((WEIGHT_STAGE_ROWS, n), F32)]


_HIGH_HALF = 0xFFFF0000


def _pack_pairs(x):
    c = x.shape[1] // 2
    bits = lax.bitcast_convert_type(x.astype(BF16).astype(F32), jnp.uint32)
    return (bits[:, c:] & jnp.uint32(_HIGH_HALF)) | lax.shift_right_logical(bits[:, :c], jnp.uint32(16))


def _unpack_pairs(w):
    lo = lax.bitcast_convert_type(lax.shift_left(w, jnp.uint32(16)), F32)
    hi = lax.bitcast_convert_type(w & jnp.uint32(_HIGH_HALF), F32)
    return jnp.concatenate([lo, hi], axis=1)


def _mixer_kernel(x_ref, nmix_ref, win_hbm, gln_ref, ws_ref, bst_ref, beta_ref, lbl_ref, og_ref,
                  wout_hbm, o_ref, proj_ref, ycat_ref, state_ref, win_ref, win_stage, wout_ref, wout_stage,
                  sem):
    @pl.when((pl.program_id(0) == 0) & (pl.program_id(1) == 0))
    def _():
        _stage_weight(win_hbm, win_ref, win_stage, sem)
        _stage_weight(wout_hbm, wout_ref, wout_stage, sem)

    @pl.when(pl.program_id(1) == 0)
    def _():
        state_ref[...] = jnp.zeros_like(state_ref)

    n = MIX_CHAIN_ROWS
    r_i = lax.broadcasted_iota(jnp.int32, (GMLP_CHUNK, GMLP_CHUNK), 0)
    c_i = lax.broadcasted_iota(jnp.int32, (GMLP_CHUNK, GMLP_CHUNK), 1)
    causal = c_i <= r_i
    w_tril = [jnp.where(causal, ws_ref[g], 0.0).astype(BF16) for g in range(GMLP_GROUPS)]
    lbl = lbl_ref[...]
    e_lb = jnp.exp(lbl - jnp.max(lbl, axis=0, keepdims=True))
    lb = e_lb[0:1] / jnp.sum(e_lb, axis=0, keepdims=True)
    rr = lax.broadcasted_iota(jnp.int32, (n, n), 0)
    cc = lax.broadcasted_iota(jnp.int32, (n, n), 1)
    tri = jnp.where((_block_id(rr, HGRN_CHUNK) == _block_id(cc, HGRN_CHUNK)) & (cc <= rr),
                    1.0, 0.0).astype(BF16)
    r64 = lax.broadcasted_iota(jnp.int32, (HGRN_CHUNK, HGRN_CHUNK), 0)
    c64 = lax.broadcasted_iota(jnp.int32, (HGRN_CHUNK, HGRN_CHUNK), 1)
    causal64 = c64 <= r64
    base = 2 * D_GMLP

    n_chunks = n // HGRN_CHUNK
    chains = range(x_ref.shape[1] // n)
    env = {ch: {} for ch in chains}

    def rows(ch):
        return slice(ch * n, (ch + 1) * n)

    def in_proj(ch):
        a = _rms(x_ref[0, rows(ch)], nmix_ref[...]).astype(BF16)
        proj_ref[rows(ch)] = _dot(a, win_ref[...])

    def gmlp(ch):
        u = _gelu(proj_ref[rows(ch), 0:D_GMLP])
        v = _gelu(proj_ref[rows(ch), D_GMLP:2 * D_GMLP])
        vc = v - jnp.mean(v, axis=-1, keepdims=True)
        vn = (vc * lax.rsqrt(jnp.mean(vc * vc, axis=-1, keepdims=True) + EPS) * gln_ref[...]).astype(BF16)
        z_rows = []
        for c in range(n // GMLP_CHUNK):
            z_cols = []
            for g in range(GMLP_GROUPS):
                vg = vn[c * GMLP_CHUNK:(c + 1) * GMLP_CHUNK, g * LANES:(g + 1) * LANES]
                z_cols.append(_dot(w_tril[g], vg) + bst_ref[:, g:g + 1])
            z_rows.append(jnp.concatenate(z_cols, axis=1))
        z = jnp.concatenate(z_rows, axis=0)
        ycat_ref[rows(ch), 0:D_GMLP] = _rms(u * z, beta_ref[...]).astype(BF16)

    def hgrn_factors(ch):
        e = env[ch]
        f = lb + (1.0 - lb) * _sigmoid(proj_ref[rows(ch), base + D_HGRN:base + 2 * D_HGRN])
        log_f = jnp.log(f)
        lf_hi = log_f.astype(BF16)
        lf_lo = (log_f - lf_hi.astype(F32)).astype(BF16)
        b_all = _dot(tri, lf_hi) + _dot(tri, lf_lo)
        bl_rows = [b_all[c * HGRN_CHUNK + HGRN_CHUNK - 1:(c + 1) * HGRN_CHUNK] for c in range(n_chunks)]
        bl_all = jnp.concatenate([jnp.broadcast_to(r, (HGRN_CHUNK, D_HGRN)) for r in bl_rows], axis=0)
        q_all = proj_ref[rows(ch), base:base + D_HGRN]
        k_all = 1.0 - f
        e["qd"] = (q_all * _sigmoid(q_all) * jnp.exp(b_all)).astype(BF16)
        e["ki"] = (k_all * jnp.exp(-b_all)).astype(BF16)
        e["kte"] = (k_all * jnp.exp(bl_all - b_all)).astype(BF16)
        e["v"] = proj_ref[rows(ch), base + 2 * D_HGRN:base + 3 * D_HGRN].astype(BF16)
        e["decay"] = [jnp.exp(r) for r in bl_rows]

    def hgrn_local(ch):
        e = env[ch]
        e["o_intra"], e["d_state"] = {}, {}
        for c in range(n_chunks):
            rs = slice(c * HGRN_CHUNK, (c + 1) * HGRN_CHUNK)
            for h in range(HGRN_HEADS):
                cs = slice(h * HGRN_DK, (h + 1) * HGRN_DK)
                scores = jnp.where(causal64, _dot_nt(e["qd"][rs, cs], e["ki"][rs, cs]), 0.0).astype(BF16)
                e["o_intra"][c, h] = _dot(scores, e["v"][rs, cs])
                e["d_state"][c, h] = _dot_tn(e["v"][rs, cs], e["kte"][rs, cs])

    def hgrn_recurrence(ch):
        e = env[ch]
        for c in range(n_chunks):
            rs = slice(c * HGRN_CHUNK, (c + 1) * HGRN_CHUNK)
            ps = slice(ch * n + c * HGRN_CHUNK, ch * n + (c + 1) * HGRN_CHUNK)
            g_c = proj_ref[ps, base + 3 * D_HGRN:base + 4 * D_HGRN]
            gate = og_ref[...] * (g_c * _sigmoid(g_c))
            for h in range(HGRN_HEADS):
                cs = slice(h * HGRN_DK, (h + 1) * HGRN_DK)
                st = state_ref[h]
                o = e["o_intra"][c, h] + _dot_nt(e["qd"][rs, cs], st.astype(BF16))
                state_ref[h] = st * e["decay"][c][:, cs] + e["d_state"][c, h]
                o = o * lax.rsqrt(jnp.mean(o * o, axis=-1, keepdims=True) + EPS)
                ycat_ref[ps, D_GMLP + h * HGRN_DK:D_GMLP + (h + 1) * HGRN_DK] = (o * gate[:, cs]).astype(BF16)

    def out_proj(ch):
        o_ref[0, rows(ch)] = x_ref[0, rows(ch)] + _dot(ycat_ref[rows(ch)], wout_ref[...])

    stages = {"in": in_proj, "gmlp": gmlp, "factors": hgrn_factors, "local": hgrn_local,
              "recurrence": hgrn_recurrence, "out": out_proj}
    for stage, ch in MIX_SCHEDULE:
        stages[stage](ch)


def _mixer(x, norm_mix, w_in, gmlp_ln, w_s, b_s_t, beta, lb_logits, out_gain, w_out):
    B, S, D = x.shape
    const2 = lambda b, s: (0, 0)
    return pl.pallas_call(
        _mixer_kernel,
        grid=(B, S // MIX_ROWS),
        in_specs=[
            pl.BlockSpec((1, MIX_ROWS, D), lambda b, s: (b, s, 0)),
            pl.BlockSpec((1, D), const2),
            pl.BlockSpec(memory_space=pl.ANY),
            pl.BlockSpec((1, D_GMLP), const2),
            pl.BlockSpec((GMLP_GROUPS, GMLP_CHUNK, GMLP_CHUNK), lambda b, s: (0, 0, 0)),
            pl.BlockSpec((GMLP_CHUNK, GMLP_GROUPS), const2),
            pl.BlockSpec((1, D_GMLP), const2),
            pl.BlockSpec(lb_logits.shape, const2),
            pl.BlockSpec((1, D_HGRN), const2),
            pl.BlockSpec(memory_space=pl.ANY),
        ],
        out_specs=pl.BlockSpec((1, MIX_ROWS, D), lambda b, s: (b, s, 0)),
        out_shape=jax.ShapeDtypeStruct((B, S, D), F32),
        scratch_shapes=[
            pltpu.VMEM((MIX_ROWS, D_IN_PROJ), F32),
            pltpu.VMEM((MIX_ROWS, D), BF16),
            pltpu.VMEM((HGRN_HEADS, HGRN_DK, HGRN_DK), F32),
        ] + _weight_scratch(D, D_IN_PROJ) + _weight_scratch(D, D) + [pltpu.SemaphoreType.DMA],
        compiler_params=pltpu.CompilerParams(
            dimension_semantics=("arbitrary", "arbitrary"), vmem_limit_bytes=VMEM_LIMIT),
        name="mixer",
    )(x, norm_mix, w_in, gmlp_ln, w_s, b_s_t, beta, lb_logits, out_gain, w_out)


def _memkv_kernel(mem_ref, nm_ref, wkv_hbm, k_ref, v_ref, wkv_ref, wkv_stage, sem):
    @pl.when(pl.program_id(0) == 0)
    def _():
        _stage_weight(wkv_hbm, wkv_ref, wkv_stage, sem)

    nb, M, D = mem_ref.shape
    m = _rms(mem_ref[...].reshape(nb * M, D), nm_ref[...]).astype(BF16)
    kv = _dot(m, wkv_ref[...])
    k_ref[...] = kv[:, :D_MODEL].astype(BF16).reshape(nb, M, D)
    v_ref[...] = kv[:, D_MODEL:].astype(BF16).reshape(nb, M, D)


def _memkv(mem, norm_mem, w_kv):
    B, M, D = mem.shape
    out = jax.ShapeDtypeStruct((B, M, D), BF16)
    return pl.pallas_call(
        _memkv_kernel,
        grid=(B // MEMKV_BATCHES,),
        in_specs=[
            pl.BlockSpec((MEMKV_BATCHES, M, D), lambda b: (b, 0, 0)),
            pl.BlockSpec((1, D), lambda b: (0, 0)),
            pl.BlockSpec(memory_space=pl.ANY),
        ],
        out_specs=[pl.BlockSpec((MEMKV_BATCHES, M, D), lambda b: (b, 0, 0))] * 2,
        out_shape=[out, out],
        scratch_shapes=_weight_scratch(D, 2 * D) + [pltpu.SemaphoreType.DMA],
        compiler_params=pltpu.CompilerParams(
            dimension_semantics=("arbitrary",), vmem_limit_bytes=VMEM_LIMIT),
        name="memkv",
    )(mem, norm_mem, w_kv)


def _xattn_kernel(h_ref, nx_ref, wq_hbm, k_ref, v_ref, wo_hbm, nf_ref, wr_ref,
                  h2_ref, xn_ref, lg_ref, att_ref, wq_ref, wq_stage, wo_ref, wo_stage, sem):
    @pl.when((pl.program_id(0) == 0) & (pl.program_id(1) == 0))
    def _():
        _stage_weight(wq_hbm, wq_ref, wq_stage, sem)
        _stage_weight(wo_hbm, wo_ref, wo_stage, sem)

    for c in range(ATT_ROWS // ATT_CHAIN_ROWS):
        rs = slice(c * ATT_CHAIN_ROWS, (c + 1) * ATT_CHAIN_ROWS)
        h = h_ref[0, rs]
        hn = _rms(h, nx_ref[...]).astype(BF16)
        q = (_dot(hn, wq_ref[...]) * (XATTN_HEAD_DIM ** -0.5)).astype(BF16)
        heads = [slice(hd * XATTN_HEAD_DIM, (hd + 1) * XATTN_HEAD_DIM) for hd in range(XATTN_HEADS)]
        scores = [_dot_nt(q[:, cs], k_ref[0, :, cs]) for cs in heads]
        probs = []
        for s in scores:
            p = jnp.exp(s - jnp.max(s, axis=-1, keepdims=True))
            probs.append((p / jnp.sum(p, axis=-1, keepdims=True)).astype(BF16))
        for cs, p in zip(heads, probs):
            att_ref[rs, cs] = _dot(p, v_ref[0, :, cs]).astype(BF16)
        h2 = h + _dot(att_ref[rs], wo_ref[...])
        h2_ref[0, rs] = h2.astype(h2_ref.dtype)
        xn = _rms(h2, nf_ref[...])
        xn_ref[rs] = _pack_pairs(xn)
        lg = _dot_nt(wr_ref[...], xn.astype(BF16))
        for j in range(ATT_CHAIN_ROWS // LANES):
            lg_ref[:, c * (ATT_CHAIN_ROWS // LANES) + j, :] = lg[:, j * LANES:(j + 1) * LANES]


def _xattn(h1, norm_x, w_q, k_mem, v_mem, w_o, norm_ffn, w_router, batch0, batches):
    _, S, D = h1.shape
    n_s = S // ATT_ROWS
    const2 = lambda b, s: (0, 0)
    return pl.pallas_call(
        _xattn_kernel,
        grid=(batches, n_s),
        in_specs=[
            pl.BlockSpec((1, ATT_ROWS, D), lambda b, s: (b + batch0, s, 0)),
            pl.BlockSpec((1, D), const2),
            pl.BlockSpec(memory_space=pl.ANY),
            pl.BlockSpec((1, N_MEM, D), lambda b, s: (b + batch0, 0, 0)),
            pl.BlockSpec((1, N_MEM, D), lambda b, s: (b + batch0, 0, 0)),
            pl.BlockSpec(memory_space=pl.ANY),
            pl.BlockSpec((1, D), const2),
            pl.BlockSpec((ROUTER_ROWS, D), const2),
        ],
        out_specs=[
            pl.BlockSpec((1, ATT_ROWS, D), lambda b, s: (b, s, 0)),
            pl.BlockSpec((ATT_ROWS, D // 2), lambda b, s: (b * n_s + s, 0)),
            pl.BlockSpec((ROUTER_ROWS, ATT_ROWS // LANES, LANES), lambda b, s: (0, b * n_s + s, 0)),
        ],
        out_shape=[
            jax.ShapeDtypeStruct((batches, S, D), BF16),
            jax.ShapeDtypeStruct((batches * S, D // 2), jnp.uint32),
            jax.ShapeDtypeStruct((ROUTER_ROWS, batches * S // LANES, LANES), F32),
        ],
        scratch_shapes=([pltpu.VMEM((ATT_ROWS, D), BF16)] + _weight_scratch(D, D) + _weight_scratch(D, D)
                        + [pltpu.SemaphoreType.DMA]),
        compiler_params=pltpu.CompilerParams(
            dimension_semantics=("arbitrary", "arbitrary"), vmem_limit_bytes=VMEM_LIMIT),
        name="xattn",
    )(h1, norm_x, w_q, k_mem, v_mem, w_o, norm_ffn, w_router)


def _route_kernel(bias_ref, lg_ref, ids_ref, gates_ref, rank_ref, cnt_ref, base_ref):
    sub = lg_ref.shape[1]

    @pl.when(pl.program_id(0) == 0)
    def _():
        base_ref[...] = jnp.zeros_like(base_ref)

    best = lg_ref[0] + bias_ref[0]
    gl = [best]
    sel = jnp.zeros(best.shape, jnp.int32)
    for g in range(1, N_GROUPS):
        cur = lg_ref[g] + bias_ref[g]
        gl.append(cur)
        better = cur > best
        best = jnp.where(better, cur, best)
        sel = jnp.where(better, g, sel)
    denom = jnp.exp(gl[0] - best)
    for g in range(1, N_GROUPS):
        denom = denom + jnp.exp(gl[g] - best)
    g_gate = 1.0 / denom

    ev = []
    for j in range(EXPERTS_PER_GROUP):
        val = lg_ref[N_GROUPS + j] + bias_ref[N_GROUPS + j]
        for g in range(1, N_GROUPS):
            e = g * EXPERTS_PER_GROUP + j
            val = jnp.where(sel == g, lg_ref[N_GROUPS + e] + bias_ref[N_GROUPS + e], val)
        ev.append(val)
    v1, i1 = ev[0], jnp.zeros(best.shape, jnp.int32)
    for j in range(1, EXPERTS_PER_GROUP):
        better = ev[j] > v1
        v1 = jnp.where(better, ev[j], v1)
        i1 = jnp.where(better, j, i1)
    rest = [jnp.where(i1 == j, -jnp.inf, ev[j]) for j in range(EXPERTS_PER_GROUP)]
    v2, i2 = rest[0], jnp.zeros(best.shape, jnp.int32)
    for j in range(1, EXPERTS_PER_GROUP):
        better = rest[j] > v2
        v2 = jnp.where(better, rest[j], v2)
        i2 = jnp.where(better, j, i2)
    e2 = jnp.exp(v2 - v1)
    inv = 1.0 / (1.0 + e2)
    id1 = sel * EXPERTS_PER_GROUP + i1
    id2 = sel * EXPERTS_PER_GROUP + i2
    ids_ref[0] = id1
    ids_ref[1] = id2
    gates_ref[0] = inv * g_gate
    gates_ref[1] = e2 * inv * g_gate

    member = jnp.concatenate(
        [jnp.where((id1 == e) | (id2 == e), 1.0, 0.0) for e in range(N_EXPERTS)], axis=0).astype(BF16)
    n = N_EXPERTS * sub
    li = lax.broadcasted_iota(jnp.int32, (LANES, LANES), 0)
    lj = lax.broadcasted_iota(jnp.int32, (LANES, LANES), 1)
    before_lane = jnp.where(li < lj, 1.0, 0.0).astype(BF16)
    ones = jnp.ones((LANES, LANES), BF16)
    ri = lax.broadcasted_iota(jnp.int32, (n, n), 0)
    rj = lax.broadcasted_iota(jnp.int32, (n, n), 1)
    same = _block_id(ri, sub) == _block_id(rj, sub)
    before_row = jnp.where(same & (rj < ri), 1.0, 0.0).astype(BF16)
    all_row = jnp.where(same, 1.0, 0.0).astype(BF16)
    in_row = _dot(member, before_lane)
    prev_rows = _dot(_dot(before_row, member).astype(BF16), ones)
    total = _dot(_dot(all_row, member).astype(BF16), ones)
    base = base_ref[...]
    pos = base + prev_rows + in_row
    r1 = jnp.zeros(best.shape, F32)
    r2 = jnp.zeros(best.shape, F32)
    for e in range(N_EXPERTS):
        pe = pos[e * sub:(e + 1) * sub]
        r1 = jnp.where(id1 == e, pe, r1)
        r2 = jnp.where(id2 == e, pe, r2)
    rank_ref[0] = r1.astype(jnp.int32)
    rank_ref[1] = r2.astype(jnp.int32)
    base_ref[...] = base + total
    cnt_ref[...] = base + total


def _route(bias, logits3):
    rp, n_sub, _ = logits3.shape
    blk = lambda i: (0, i, 0)
    pair_i = jax.ShapeDtypeStruct((2, n_sub, LANES), jnp.int32)
    return pl.pallas_call(
        _route_kernel,
        grid=(n_sub // ROUTE_SUB,),
        in_specs=[
            pl.BlockSpec(memory_space=pltpu.SMEM),
            pl.BlockSpec((rp, ROUTE_SUB, LANES), blk),
        ],
        out_specs=[
            pl.BlockSpec((2, ROUTE_SUB, LANES), blk),
            pl.BlockSpec((2, ROUTE_SUB, LANES), blk),
            pl.BlockSpec((2, ROUTE_SUB, LANES), blk),
            pl.BlockSpec((N_EXPERTS * ROUTE_SUB, LANES), lambda i: (0, 0)),
        ],
        out_shape=[
            pair_i,
            jax.ShapeDtypeStruct((2, n_sub, LANES), F32),
            pair_i,
            jax.ShapeDtypeStruct((N_EXPERTS * ROUTE_SUB, LANES), F32),
        ],
        scratch_shapes=[pltpu.VMEM((N_EXPERTS * ROUTE_SUB, LANES), F32)],
        compiler_params=pltpu.CompilerParams(
            dimension_semantics=("arbitrary",), vmem_limit_bytes=VMEM_LIMIT),
        name="route",
    )(bias, logits3)


def _dest_kernel(start_ref, ids_ref, rank_ref, dest_ref):
    ids = ids_ref[...]
    off = jnp.zeros(ids.shape, jnp.int32)
    for e in range(N_EXPERTS):
        off = jnp.where(ids == e, start_ref[e], off)
    dest_ref[...] = rank_ref[...] + off


def _dest(seg_start, ids, rank):
    _, n_sub, _ = ids.shape
    blk = pl.BlockSpec((2, ROUTE_SUB, LANES), lambda i: (0, i, 0))
    return pl.pallas_call(
        _dest_kernel,
        grid=(n_sub // ROUTE_SUB,),
        in_specs=[pl.BlockSpec(memory_space=pltpu.SMEM), blk, blk],
        out_specs=blk,
        out_shape=jax.ShapeDtypeStruct(ids.shape, jnp.int32),
        name="dest",
    )(seg_start, ids, rank)


def _sc_mesh():
    return plsc.VectorSubcoreMesh(core_axis_name="core", subcore_axis_name="subcore")


def _sc_worker(rows_total):
    rows = rows_total // _SC_WORKERS
    wid = lax.axis_index("core") * SC_SUBCORES + lax.axis_index("subcore")
    return wid * rows, rows


def _dispatch(dest0, dest1, xn, n_slots):
    T, D = xn.shape
    W = SC_WINDOW
    rows = T // _SC_WORKERS

    @pl.kernel(out_type=jax.ShapeDtypeStruct((n_slots, D), xn.dtype), mesh=_sc_mesh(),
               scratch_types=[pltpu.VMEM((rows,), jnp.int32), pltpu.VMEM((rows,), jnp.int32),
                              pltpu.VMEM((2, W, D), xn.dtype),
                              pltpu.SemaphoreType.DMA((2,)), pltpu.SemaphoreType.DMA((2,))])
    def scatter_rows(x_hbm, d0_hbm, d1_hbm, xs_hbm, d0_v, d1_v, buf, in_sem, out_sem):
        base, _ = _sc_worker(T)
        pltpu.sync_copy(d0_hbm.at[pl.ds(base, rows)], d0_v)
        pltpu.sync_copy(d1_hbm.at[pl.ds(base, rows)], d1_v)

        def load(w, slot):
            return pltpu.make_async_copy(x_hbm.at[pl.ds(base + w * W, W)], buf.at[slot], in_sem.at[slot])

        def store(w, slot, d_v):
            return pltpu.make_async_copy(buf.at[slot], xs_hbm.at[d_v.at[pl.ds(w * W, W)]], out_sem.at[slot])

        def step(w, slot):
            load(w, slot).wait()
            store(w, slot, d0_v).start()
            store(w, slot, d1_v).start()
            store(w, slot, d0_v).wait()
            store(w, slot, d1_v).wait()

        n = rows // W
        load(0, 0).start()

        @pl.loop(0, n, step=2)
        def _(w):
            load(w + 1, 1).start()
            step(w, 0)

            @pl.when(w + 2 < n)
            def _():
                load(w + 2, 0).start()

            step(w + 1, 1)

    return scatter_rows(xn, dest0, dest1)


def _gather_rows(src, idx):
    M = idx.shape[0]
    D = src.shape[1]
    W = SC_WINDOW
    rows = M // _SC_WORKERS

    @pl.kernel(out_type=jax.ShapeDtypeStruct((M, D), src.dtype), mesh=_sc_mesh(),
               scratch_types=[pltpu.VMEM((rows,), jnp.int32), pltpu.VMEM((2, W, D), src.dtype),
                              pltpu.SemaphoreType.DMA((2,)), pltpu.SemaphoreType.DMA((2,))])
    def gather_rows(src_hbm, i_hbm, o_hbm, i_v, buf, in_sem, out_sem):
        base, _ = _sc_worker(M)
        pltpu.sync_copy(i_hbm.at[pl.ds(base, rows)], i_v)

        def load(w, slot):
            return pltpu.make_async_copy(src_hbm.at[i_v.at[pl.ds(w * W, W)]], buf.at[slot], in_sem.at[slot])

        def store(w, slot):
            return pltpu.make_async_copy(buf.at[slot], o_hbm.at[pl.ds(base + w * W, W)], out_sem.at[slot])

        n = rows // W
        load(0, 0).start()

        @pl.loop(0, n, step=2)
        def _(w):
            @pl.when(w > 0)
            def _():
                store(w - 1, 1).wait()

            load(w + 1, 1).start()
            load(w, 0).wait()
            store(w, 0).start()
            store(w, 0).wait()

            @pl.when(w + 2 < n)
            def _():
                load(w + 2, 0).start()

            load(w + 1, 1).wait()
            store(w + 1, 1).start()

        store(n - 1, 1).wait()

    return gather_rows(src, idx)


def _expert_kernel(be_ref, nb_ref, slot_ref, next_ref, x_ref, wg_hbm, wu_hbm, wd_hbm, y_ref,
                   wg_f32, wu_f32, wd_f32, wg_ref, wu_ref, wd_ref, sem):
    i = pl.program_id(0)
    used = i < nb_ref[0]
    new_expert = (i == 0) | (be_ref[i] != be_ref[jnp.maximum(i - 1, 0)])

    def fetch(e, slot):
        pairs = ((wg_hbm, wg_f32), (wu_hbm, wu_f32), (wd_hbm, wd_f32))
        return [pltpu.make_async_copy(w_hbm.at[e], w_f32.at[slot], sem.at[slot, k])
                for k, (w_hbm, w_f32) in enumerate(pairs)]

    @pl.when(used & (i == 0))
    def _():
        for copy in fetch(be_ref[0], slot_ref[0]):
            copy.start()

    @pl.when(used & new_expert)
    def _():
        slot = slot_ref[i]
        for copy in fetch(be_ref[i], slot):
            copy.wait()

        @pl.when(next_ref[i] >= 0)
        def _():
            for copy in fetch(next_ref[i], 1 - slot):
                copy.start()

        wg_ref[...] = wg_f32[slot].astype(BF16)
        wu_ref[...] = wu_f32[slot].astype(BF16)
        wd_ref[...] = wd_f32[slot].astype(BF16)

    @pl.when(used)
    def _():
        for c in range(EXPERT_ROWS // EXPERT_CHAIN_ROWS):
            rs = slice(c * EXPERT_CHAIN_ROWS, (c + 1) * EXPERT_CHAIN_ROWS)
            x = _unpack_pairs(x_ref[rs]).astype(BF16)
            g = _dot(x, wg_ref[...])
            u = _dot(x, wu_ref[...])
            hid = (g * _sigmoid(g) * u).astype(BF16)
            y_ref[rs] = _pack_pairs(_dot(hid, wd_ref[...]))

    @pl.when(jnp.logical_not(used))
    def _():
        y_ref[...] = jnp.zeros_like(y_ref)


def _experts(block_expert, n_used, block_slot, block_next, xs, w_gate, w_up, w_down):
    n_slots, half = xs.shape
    D = 2 * half
    n_blocks = n_slots // EXPERT_ROWS
    grid_spec = pltpu.PrefetchScalarGridSpec(
        num_scalar_prefetch=4,
        grid=(n_blocks,),
        in_specs=[
            pl.BlockSpec((EXPERT_ROWS, half), lambda i, be, nb, sl, nx: (jnp.minimum(i, nb[0] - 1), 0)),
            pl.BlockSpec(memory_space=pl.ANY),
            pl.BlockSpec(memory_space=pl.ANY),
            pl.BlockSpec(memory_space=pl.ANY),
        ],
        out_specs=pl.BlockSpec((EXPERT_ROWS, half), lambda i, be, nb, sl, nx: (i, 0)),
        scratch_shapes=[
            pltpu.VMEM((2, D, D_EXPERT), F32), pltpu.VMEM((2, D, D_EXPERT), F32),
            pltpu.VMEM((2, D_EXPERT, D), F32),
            pltpu.VMEM((D, D_EXPERT), BF16), pltpu.VMEM((D, D_EXPERT), BF16), pltpu.VMEM((D_EXPERT, D), BF16),
            pltpu.SemaphoreType.DMA((2, 3)),
        ],
    )
    return pl.pallas_call(
        _expert_kernel,
        grid_spec=grid_spec,
        out_shape=jax.ShapeDtypeStruct((n_slots, half), jnp.uint32),
        compiler_params=pltpu.CompilerParams(
            dimension_semantics=("arbitrary",), vmem_limit_bytes=VMEM_LIMIT),
        name="experts",
    )(block_expert, n_used, block_slot, block_next, xs, w_gate, w_up, w_down)


def _combine_kernel(y0_ref, y1_ref, h_ref, gates_ref, nfin_ref, *rest):
    o_ref = rest[-1]
    g0 = gates_ref[0].T
    g1 = gates_ref[1].T
    for r in range(gates_ref.shape[1]):
        rs = slice(r * LANES, (r + 1) * LANES)
        h = (h_ref[rs].astype(F32) + g0[:, r:r + 1] * _unpack_pairs(y0_ref[rs])
             + g1[:, r:r + 1] * _unpack_pairs(y1_ref[rs]))
        o_ref[rs] = _rms(h, nfin_ref[...])


def _combine(y01, h2, gates_t, norm_final, out_prev, part, parts):
    T, D = h2.shape
    n_t = T // MOVE_ROWS
    in_specs = [
        pl.BlockSpec((MOVE_ROWS, D // 2), lambda i: (i, 0)),
        pl.BlockSpec((MOVE_ROWS, D // 2), lambda i: (i + n_t, 0)),
        pl.BlockSpec((MOVE_ROWS, D), lambda i: (i, 0)),
        pl.BlockSpec((2, MOVE_ROWS // LANES, LANES), lambda i: (0, i, 0)),
        pl.BlockSpec((1, D), lambda i: (0, 0)),
    ]
    args = [y01, y01, h2, gates_t, norm_final]
    aliases = {}
    if out_prev is not None:
        in_specs.append(pl.BlockSpec(memory_space=pl.ANY))
        args.append(out_prev)
        aliases = {len(args) - 1: 0}
    return pl.pallas_call(
        _combine_kernel,
        grid=(n_t,),
        in_specs=in_specs,
        out_specs=pl.BlockSpec((MOVE_ROWS, D), lambda i: (i + part * n_t, 0)),
        out_shape=jax.ShapeDtypeStruct((parts * T, D), F32),
        input_output_aliases=aliases,
        compiler_params=pltpu.CompilerParams(
            dimension_semantics=("arbitrary",), vmem_limit_bytes=VMEM_LIMIT),
        name="combine",
    )(*args)


def kernel(x, mem, norm_mix, w_in, gmlp_ln, gmlp_w_spatial, gmlp_b_spatial, gmlp_beta, hgrn_lb_logits, hgrn_out_gain, w_out, norm_xattn, norm_mem, w_xq, w_xkv, w_xo, norm_ffn, w_router_group, b_router_group, w_router_expert, b_router_expert, w_expert_gate, w_expert_up, w_expert_down, norm_final):
    B, S, D = x.shape
    T = B * S
    depth = w_in.shape[0]
    assert depth == 1 and hgrn_lb_logits.shape[0] == 2
    l = 0
    row = lambda p: p.reshape(1, -1)

    h1 = _mixer(x, row(norm_mix[l]), w_in[l], row(gmlp_ln[l]), gmlp_w_spatial[l],
                gmlp_b_spatial[l].T, row(gmlp_beta[l]), hgrn_lb_logits, row(hgrn_out_gain[l]),
                w_out[l])
    k_mem, v_mem = _memkv(mem, row(norm_mem[l]), w_xkv[l])

    w_router = jnp.concatenate([w_router_group[l].T, w_router_expert[l].T], axis=0)
    w_router = jnp.pad(w_router, ((0, ROUTER_ROWS - w_router.shape[0]), (0, 0)))
    w_router = w_router.astype(BF16)
    bias = jnp.concatenate([b_router_group[l], b_router_expert[l]]).astype(F32)

    b_part = B // TOKEN_PARTS
    t_part = b_part * S
    n_blocks = (2 * t_part) // EXPERT_ROWS + N_EXPERTS
    out = None
    for part in range(TOKEN_PARTS):
        h2, xn, logits = _xattn(h1, row(norm_xattn[l]), w_xq[l], k_mem, v_mem, w_xo[l], row(norm_ffn[l]),
                                w_router, part * b_part, b_part)
        ids, gates, rank, counts = _route(bias, logits)

        counts = counts[::ROUTE_SUB, 0].astype(jnp.int32)
        padded = (counts + EXPERT_ROWS - 1) // EXPERT_ROWS * EXPERT_ROWS
        seg_end = jnp.cumsum(padded)
        seg_start = seg_end - padded
        block_first_row = jnp.arange(n_blocks, dtype=jnp.int32) * EXPERT_ROWS
        block_expert = jnp.minimum(
            jnp.sum(block_first_row[:, None] >= seg_end[None, :], axis=1), N_EXPERTS - 1).astype(jnp.int32)
        n_used = (seg_end[-1:] // EXPERT_ROWS).astype(jnp.int32)
        present = counts > 0
        expert_ids = jnp.arange(N_EXPERTS, dtype=jnp.int32)
        expert_slot = (jnp.cumsum(present.astype(jnp.int32)) - 1) % 2
        later = jnp.where(present[None, :] & (expert_ids[None, :] > expert_ids[:, None]),
                          expert_ids[None, :], N_EXPERTS)
        expert_next = jnp.min(later, axis=1)
        expert_next = jnp.where(expert_next == N_EXPERTS, -1, expert_next).astype(jnp.int32)
        of_block = (block_expert[:, None] == expert_ids[None, :]).astype(jnp.int32)
        block_slot = jnp.sum(of_block * expert_slot[None, :], axis=1).astype(jnp.int32)
        block_next = jnp.sum(of_block * expert_next[None, :], axis=1).astype(jnp.int32)

        dest = _dest(seg_start, ids, rank).reshape(2, t_part)
        xs = _dispatch(dest[0], dest[1], xn, n_blocks * EXPERT_ROWS)
        yb = _experts(block_expert, n_used, block_slot, block_next, xs,
                      w_expert_gate[l], w_expert_up[l], w_expert_down[l])
        y01 = _gather_rows(yb, dest.reshape(2 * t_part))
        out = _combine(y01, h2.reshape(t_part, D), gates, row(norm_final),
                       out, part, TOKEN_PARTS)
    return out.reshape(B, S, D)
```

```python
import functools

import jax
import jax.numpy as jnp
from jax import lax
from jax.experimental import pallas as pl
from jax.experimental.pallas import tpu as pltpu
from jax.experimental.pallas import tpu_sc as plsc

F32 = jnp.float32
BF16 = jnp.bfloat16
EPS = 1e-6

D_MODEL = 1024
D_GMLP = 512
GMLP_GROUPS = 4
GMLP_CHUNK = 128
D_HGRN = 512
HGRN_HEADS = 4
HGRN_DK = 128
HGRN_CHUNK = 64
D_IN_PROJ = 2 * D_GMLP + 4 * D_HGRN
N_MEM = 256
XATTN_HEADS = 4
XATTN_HEAD_DIM = D_MODEL // XATTN_HEADS
N_GROUPS = 4
EXPERTS_PER_GROUP = 8
N_EXPERTS = N_GROUPS * EXPERTS_PER_GROUP
D_EXPERT = 512

LANES = 128
MIX_ROWS = 1024
MIX_CHAIN_ROWS = 256
def _mix_schedule(n_chains):
    order = [("in", 0), ("gmlp", 0), ("factors", 0)]
    for k in range(n_chains):
        more = k + 1 < n_chains
        order += [("in", k + 1)] * more + [("local", k), ("recurrence", k)] + [("gmlp", k + 1)] * more
        order += [("out", k)] + [("factors", k + 1)] * more
    return tuple(order)


MIX_SCHEDULE = _mix_schedule(MIX_ROWS // MIX_CHAIN_ROWS)
MEMKV_BATCHES = 4
ATT_ROWS = 1024
ATT_CHAIN_ROWS = 1024
WEIGHT_STAGE_ROWS = 128
ROUTER_ROWS = 40
ROUTE_SUB = 16
EXPERT_ROWS = 512
EXPERT_CHAIN_ROWS = 512
MOVE_ROWS = 1024
PART_BATCHES = (24, 8)
SC_WINDOW = 64
SC_CORES = 2
SC_SUBCORES = 16
_SC_WORKERS = SC_CORES * SC_SUBCORES
VMEM_LIMIT = 48 * 1024 * 1024


def _rms(x, gain):
    return x * lax.rsqrt(jnp.mean(x * x, axis=-1, keepdims=True) + EPS) * gain


def _dot(a, b):
    return jnp.dot(a, b, preferred_element_type=F32)


def _dot_nt(a, b):
    return lax.dot_general(a, b, (((1,), (1,)), ((), ())), preferred_element_type=F32)


def _dot_tn(a, b):
    return lax.dot_general(a, b, (((0,), (0,)), ((), ())), preferred_element_type=F32)


def _gelu(x):
    return 0.5 * x * (1.0 + jnp.tanh(0.7978845608028654 * (x + 0.044715 * (x * x * x))))


def _sigmoid(x):
    return 1.0 / (1.0 + jnp.exp(-x))


def _block_id(idx, size):
    assert size & (size - 1) == 0
    return lax.shift_right_logical(idx, size.bit_length() - 1)


def _stage_weight(w_hbm, w_bf16, stage_ref, sem):
    rows = stage_ref.shape[0]
    for k in range(w_hbm.shape[0] // rows):
        copy = pltpu.make_async_copy(w_hbm.at[pl.ds(k * rows, rows)], stage_ref, sem)
        copy.start()
        copy.wait()
        w_bf16[k * rows:(k + 1) * rows, :] = stage_ref[...].astype(BF16)


def _weight_scratch(k, n):
    return [pltpu.VMEM((k, n), BF16), pltpu.VMEM((WEIGHT_STAGE_ROWS, n), F32)]


_HIGH_HALF = 0xFFFF0000


def _pack_pairs(x):
    c = x.shape[1] // 2
    bits = lax.bitcast_convert_type(x.astype(BF16).astype(F32), jnp.uint32)
    return (bits[:, c:] & jnp.uint32(_HIGH_HALF)) | lax.shift_right_logical(bits[:, :c], jnp.uint32(16))


def _unpack_pairs(w):
    lo = lax.bitcast_convert_type(lax.shift_left(w, jnp.uint32(16)), F32)
    hi = lax.bitcast_convert_type(w & jnp.uint32(_HIGH_HALF), F32)
    return jnp.concatenate([lo, hi], axis=1)


def _mixer_kernel(x_ref, nmix_ref, win_hbm, gln_ref, ws_ref, bst_ref, beta_ref, lbl_ref, og_ref,
                  wout_hbm, o_ref, proj_ref, ycat_ref, state_ref, win_ref, win_stage, wout_ref, wout_stage,
                  sem):
    @pl.when((pl.program_id(0) == 0) & (pl.program_id(1) == 0))
    def _():
        _stage_weight(win_hbm, win_ref, win_stage, sem)
        _stage_weight(wout_hbm, wout_ref, wout_stage, sem)

    @pl.when(pl.program_id(1) == 0)
    def _():
        state_ref[...] = jnp.zeros_like(state_ref)

    n = MIX_CHAIN_ROWS
    r_i = lax.broadcasted_iota(jnp.int32, (GMLP_CHUNK, GMLP_CHUNK), 0)
    c_i = lax.broadcasted_iota(jnp.int32, (GMLP_CHUNK, GMLP_CHUNK), 1)
    causal = c_i <= r_i
    w_tril = [jnp.where(causal, ws_ref[g], 0.0).astype(BF16) for g in range(GMLP_GROUPS)]
    lbl = lbl_ref[...]
    e_lb = jnp.exp(lbl - jnp.max(lbl, axis=0, keepdims=True))
    lb = e_lb[0:1] / jnp.sum(e_lb, axis=0, keepdims=True)
    rr = lax.broadcasted_iota(jnp.int32, (n, n), 0)
    cc = lax.broadcasted_iota(jnp.int32, (n, n), 1)
    tri = jnp.where((_block_id(rr, HGRN_CHUNK) == _block_id(cc, HGRN_CHUNK)) & (cc <= rr),
                    1.0, 0.0).astype(BF16)
    r64 = lax.broadcasted_iota(jnp.int32, (HGRN_CHUNK, HGRN_CHUNK), 0)
    c64 = lax.broadcasted_iota(jnp.int32, (HGRN_CHUNK, HGRN_CHUNK), 1)
    causal64 = c64 <= r64
    base = 2 * D_GMLP

    n_chunks = n // HGRN_CHUNK
    chains = range(x_ref.shape[1] // n)
    env = {ch: {} for ch in chains}

    def rows(ch):
        return slice(ch * n, (ch + 1) * n)

    def in_proj(ch):
        a = _rms(x_ref[0, rows(ch)], nmix_ref[...]).astype(BF16)
        proj_ref[rows(ch)] = _dot(a, win_ref[...])

    def gmlp(ch):
        u = _gelu(proj_ref[rows(ch), 0:D_GMLP])
        v = _gelu(proj_ref[rows(ch), D_GMLP:2 * D_GMLP])
        vc = v - jnp.mean(v, axis=-1, keepdims=True)
        vn = (vc * lax.rsqrt(jnp.mean(vc * vc, axis=-1, keepdims=True) + EPS) * gln_ref[...]).astype(BF16)
        z_rows = []
        for c in range(n // GMLP_CHUNK):
            z_cols = []
            for g in range(GMLP_GROUPS):
                vg = vn[c * GMLP_CHUNK:(c + 1) * GMLP_CHUNK, g * LANES:(g + 1) * LANES]
                z_cols.append(_dot(w_tril[g], vg) + bst_ref[:, g:g + 1])
            z_rows.append(jnp.concatenate(z_cols, axis=1))
        z = jnp.concatenate(z_rows, axis=0)
        ycat_ref[rows(ch), 0:D_GMLP] = _rms(u * z, beta_ref[...]).astype(BF16)

    def hgrn_factors(ch):
        e = env[ch]
        f = lb + (1.0 - lb) * _sigmoid(proj_ref[rows(ch), base + D_HGRN:base + 2 * D_HGRN])
        log_f = jnp.log(f)
        lf_hi = log_f.astype(BF16)
        lf_lo = (log_f - lf_hi.astype(F32)).astype(BF16)
        b_all = _dot(tri, lf_hi) + _dot(tri, lf_lo)
        bl_rows = [b_all[c * HGRN_CHUNK + HGRN_CHUNK - 1:(c + 1) * HGRN_CHUNK] for c in range(n_chunks)]
        bl_all = jnp.concatenate([jnp.broadcast_to(r, (HGRN_CHUNK, D_HGRN)) for r in bl_rows], axis=0)
        q_all = proj_ref[rows(ch), base:base + D_HGRN]
        k_all = 1.0 - f
        e["qd"] = (q_all * _sigmoid(q_all) * jnp.exp(b_all)).astype(BF16)
        e["ki"] = (k_all * jnp.exp(-b_all)).astype(BF16)
        e["kte"] = (k_all * jnp.exp(bl_all - b_all)).astype(BF16)
        e["v"] = proj_ref[rows(ch), base + 2 * D_HGRN:base + 3 * D_HGRN].astype(BF16)
        e["decay"] = [jnp.exp(r) for r in bl_rows]

    def hgrn_local(ch):
        e = env[ch]
        e["o_intra"], e["d_state"] = {}, {}
        for c in range(n_chunks):
            rs = slice(c * HGRN_CHUNK, (c + 1) * HGRN_CHUNK)
            for h in range(HGRN_HEADS):
                cs = slice(h * HGRN_DK, (h + 1) * HGRN_DK)
                scores = jnp.where(causal64, _dot_nt(e["qd"][rs, cs], e["ki"][rs, cs]), 0.0).astype(BF16)
                e["o_intra"][c, h] = _dot(scores, e["v"][rs, cs])
                e["d_state"][c, h] = _dot_tn(e["v"][rs, cs], e["kte"][rs, cs])

    def hgrn_recurrence(ch):
        e = env[ch]
        for c in range(n_chunks):
            rs = slice(c * HGRN_CHUNK, (c + 1) * HGRN_CHUNK)
            ps = slice(ch * n + c * HGRN_CHUNK, ch * n + (c + 1) * HGRN_CHUNK)
            g_c = proj_ref[ps, base + 3 * D_HGRN:base + 4 * D_HGRN]
            gate = og_ref[...] * (g_c * _sigmoid(g_c))
            for h in range(HGRN_HEADS):
                cs = slice(h * HGRN_DK, (h + 1) * HGRN_DK)
                st = state_ref[h]
                o = e["o_intra"][c, h] + _dot_nt(e["qd"][rs, cs], st.astype(BF16))
                state_ref[h] = st * e["decay"][c][:, cs] + e["d_state"][c, h]
                o = o * lax.rsqrt(jnp.mean(o * o, axis=-1, keepdims=True) + EPS)
                ycat_ref[ps, D_GMLP + h * HGRN_DK:D_GMLP + (h + 1) * HGRN_DK] = (o * gate[:, cs]).astype(BF16)

    def out_proj(ch):
        o_ref[0, rows(ch)] = x_ref[0, rows(ch)] + _dot(ycat_ref[rows(ch)], wout_ref[...])

    stages = {"in": in_proj, "gmlp": gmlp, "factors": hgrn_factors, "local": hgrn_local,
              "recurrence": hgrn_recurrence, "out": out_proj}
    for stage, ch in MIX_SCHEDULE:
        stages[stage](ch)


def _mixer(x, norm_mix, w_in, gmlp_ln, w_s, b_s_t, beta, lb_logits, out_gain, w_out):
    B, S, D = x.shape
    const2 = lambda b, s: (0, 0)
    return pl.pallas_call(
        _mixer_kernel,
        grid=(B, S // MIX_ROWS),
        in_specs=[
            pl.BlockSpec((1, MIX_ROWS, D), lambda b, s: (b, s, 0)),
            pl.BlockSpec((1, D), const2),
            pl.BlockSpec(memory_space=pl.ANY),
            pl.BlockSpec((1, D_GMLP), const2),
            pl.BlockSpec((GMLP_GROUPS, GMLP_CHUNK, GMLP_CHUNK), lambda b, s: (0, 0, 0)),
            pl.BlockSpec((GMLP_CHUNK, GMLP_GROUPS), const2),
            pl.BlockSpec((1, D_GMLP), const2),
            pl.BlockSpec(lb_logits.shape, const2),
            pl.BlockSpec((1, D_HGRN), const2),
            pl.BlockSpec(memory_space=pl.ANY),
        ],
        out_specs=pl.BlockSpec((1, MIX_ROWS, D), lambda b, s: (b, s, 0)),
        out_shape=jax.ShapeDtypeStruct((B, S, D), F32),
        scratch_shapes=[
            pltpu.VMEM((MIX_ROWS, D_IN_PROJ), F32),
            pltpu.VMEM((MIX_ROWS, D), BF16),
            pltpu.VMEM((HGRN_HEADS, HGRN_DK, HGRN_DK), F32),
        ] + _weight_scratch(D, D_IN_PROJ) + _weight_scratch(D, D) + [pltpu.SemaphoreType.DMA],
        compiler_params=pltpu.CompilerParams(
            dimension_semantics=("arbitrary", "arbitrary"), vmem_limit_bytes=VMEM_LIMIT),
        name="mixer",
    )(x, norm_mix, w_in, gmlp_ln, w_s, b_s_t, beta, lb_logits, out_gain, w_out)


def _memkv_kernel(mem_ref, nm_ref, wkv_hbm, k_ref, v_ref, wkv_ref, wkv_stage, sem):
    @pl.when(pl.program_id(0) == 0)
    def _():
        _stage_weight(wkv_hbm, wkv_ref, wkv_stage, sem)

    nb, M, D = mem_ref.shape
    m = _rms(mem_ref[...].reshape(nb * M, D), nm_ref[...]).astype(BF16)
    kv = _dot(m, wkv_ref[...])
    k_ref[...] = kv[:, :D_MODEL].astype(BF16).reshape(nb, M, D)
    v_ref[...] = kv[:, D_MODEL:].astype(BF16).reshape(nb, M, D)


def _memkv(mem, norm_mem, w_kv):
    B, M, D = mem.shape
    out = jax.ShapeDtypeStruct((B, M, D), BF16)
    return pl.pallas_call(
        _memkv_kernel,
        grid=(B // MEMKV_BATCHES,),
        in_specs=[
            pl.BlockSpec((MEMKV_BATCHES, M, D), lambda b: (b, 0, 0)),
            pl.BlockSpec((1, D), lambda b: (0, 0)),
            pl.BlockSpec(memory_space=pl.ANY),
        ],
        out_specs=[pl.BlockSpec((MEMKV_BATCHES, M, D), lambda b: (b, 0, 0))] * 2,
        out_shape=[out, out],
        scratch_shapes=_weight_scratch(D, 2 * D) + [pltpu.SemaphoreType.DMA],
        compiler_params=pltpu.CompilerParams(
            dimension_semantics=("arbitrary",), vmem_limit_bytes=VMEM_LIMIT),
        name="memkv",
    )(mem, norm_mem, w_kv)


def _xattn_kernel(h_ref, nx_ref, wq_hbm, k_ref, v_ref, wo_hbm, nf_ref, wr_ref,
                  h2_ref, xn_ref, lg_ref, att_ref, wq_ref, wq_stage, wo_ref, wo_stage, sem):
    @pl.when((pl.program_id(0) == 0) & (pl.program_id(1) == 0))
    def _():
        _stage_weight(wq_hbm, wq_ref, wq_stage, sem)
        _stage_weight(wo_hbm, wo_ref, wo_stage, sem)

    for c in range(ATT_ROWS // ATT_CHAIN_ROWS):
        rs = slice(c * ATT_CHAIN_ROWS, (c + 1) * ATT_CHAIN_ROWS)
        h = h_ref[0, rs]
        hn = _rms(h, nx_ref[...]).astype(BF16)
        q = (_dot(hn, wq_ref[...]) * (XATTN_HEAD_DIM ** -0.5)).astype(BF16)
        heads = [slice(hd * XATTN_HEAD_DIM, (hd + 1) * XATTN_HEAD_DIM) for hd in range(XATTN_HEADS)]
        scores = [_dot_nt(q[:, cs], k_ref[0, :, cs]) for cs in heads]
        probs = []
        for s in scores:
            p = jnp.exp(s - jnp.max(s, axis=-1, keepdims=True))
            probs.append((p / jnp.sum(p, axis=-1, keepdims=True)).astype(BF16))
        for cs, p in zip(heads, probs):
            att_ref[rs, cs] = _dot(p, v_ref[0, :, cs]).astype(BF16)
        h2 = h + _dot(att_ref[rs], wo_ref[...])
        h2_ref[0, rs] = h2.astype(h2_ref.dtype)
        xn = _rms(h2, nf_ref[...])
        xn_ref[rs] = _pack_pairs(xn)
        lg = _dot_nt(wr_ref[...], xn.astype(BF16))
        for j in range(ATT_CHAIN_ROWS // LANES):
            lg_ref[:, c * (ATT_CHAIN_ROWS // LANES) + j, :] = lg[:, j * LANES:(j + 1) * LANES]


def _xattn(h1, norm_x, w_q, k_mem, v_mem, w_o, norm_ffn, w_router, batch0, batches):
    _, S, D = h1.shape
    n_s = S // ATT_ROWS
    const2 = lambda b, s: (0, 0)
    return pl.pallas_call(
        _xattn_kernel,
        grid=(batches, n_s),
        in_specs=[
            pl.BlockSpec((1, ATT_ROWS, D), lambda b, s: (b + batch0, s, 0)),
            pl.BlockSpec((1, D), const2),
            pl.BlockSpec(memory_space=pl.ANY),
            pl.BlockSpec((1, N_MEM, D), lambda b, s: (b + batch0, 0, 0)),
            pl.BlockSpec((1, N_MEM, D), lambda b, s: (b + batch0, 0, 0)),
            pl.BlockSpec(memory_space=pl.ANY),
            pl.BlockSpec((1, D), const2),
            pl.BlockSpec((ROUTER_ROWS, D), const2),
        ],
        out_specs=[
            pl.BlockSpec((1, ATT_ROWS, D), lambda b, s: (b, s, 0)),
            pl.BlockSpec((ATT_ROWS, D // 2), lambda b, s: (b * n_s + s, 0)),
            pl.BlockSpec((ROUTER_ROWS, ATT_ROWS // LANES, LANES), lambda b, s: (0, b * n_s + s, 0)),
        ],
        out_shape=[
            jax.ShapeDtypeStruct((batches, S, D), BF16),
            jax.ShapeDtypeStruct((batches * S, D // 2), jnp.uint32),
            jax.ShapeDtypeStruct((ROUTER_ROWS, batches * S // LANES, LANES), F32),
        ],
        scratch_shapes=([pltpu.VMEM((ATT_ROWS, D), BF16)] + _weight_scratch(D, D) + _weight_scratch(D, D)
                        + [pltpu.SemaphoreType.DMA]),
        compiler_params=pltpu.CompilerParams(
            dimension_semantics=("arbitrary", "arbitrary"), vmem_limit_bytes=VMEM_LIMIT),
        name="xattn",
    )(h1, norm_x, w_q, k_mem, v_mem, w_o, norm_ffn, w_router)


def _route_kernel(bias_ref, lg_ref, ids_ref, gates_ref, rank_ref, cnt_ref, base_ref):
    sub = lg_ref.shape[1]

    @pl.when(pl.program_id(0) == 0)
    def _():
        base_ref[...] = jnp.zeros_like(base_ref)

    best = lg_ref[0] + bias_ref[0]
    gl = [best]
    sel = jnp.zeros(best.shape, jnp.int32)
    for g in range(1, N_GROUPS):
        cur = lg_ref[g] + bias_ref[g]
        gl.append(cur)
        better = cur > best
        best = jnp.where(better, cur, best)
        sel = jnp.where(better, g, sel)
    denom = jnp.exp(gl[0] - best)
    for g in range(1, N_GROUPS):
        denom = denom + jnp.exp(gl[g] - best)
    g_gate = 1.0 / denom

    ev = []
    for j in range(EXPERTS_PER_GROUP):
        val = lg_ref[N_GROUPS + j] + bias_ref[N_GROUPS + j]
        for g in range(1, N_GROUPS):
            e = g * EXPERTS_PER_GROUP + j
            val = jnp.where(sel == g, lg_ref[N_GROUPS + e] + bias_ref[N_GROUPS + e], val)
        ev.append(val)
    v1, i1 = ev[0], jnp.zeros(best.shape, jnp.int32)
    for j in range(1, EXPERTS_PER_GROUP):
        better = ev[j] > v1
        v1 = jnp.where(better, ev[j], v1)
        i1 = jnp.where(better, j, i1)
    rest = [jnp.where(i1 == j, -jnp.inf, ev[j]) for j in range(EXPERTS_PER_GROUP)]
    v2, i2 = rest[0], jnp.zeros(best.shape, jnp.int32)
    for j in range(1, EXPERTS_PER_GROUP):
        better = rest[j] > v2
        v2 = jnp.where(better, rest[j], v2)
        i2 = jnp.where(better, j, i2)
    e2 = jnp.exp(v2 - v1)
    inv = 1.0 / (1.0 + e2)
    id1 = sel * EXPERTS_PER_GROUP + i1
    id2 = sel * EXPERTS_PER_GROUP + i2
    ids_ref[0] = id1
    ids_ref[1] = id2
    gates_ref[0] = inv * g_gate
    gates_ref[1] = e2 * inv * g_gate

    member = jnp.concatenate(
        [jnp.where((id1 == e) | (id2 == e), 1.0, 0.0) for e in range(N_EXPERTS)], axis=0).astype(BF16)
    n = N_EXPERTS * sub
    li = lax.broadcasted_iota(jnp.int32, (LANES, LANES), 0)
    lj = lax.broadcasted_iota(jnp.int32, (LANES, LANES), 1)
    before_lane = jnp.where(li < lj, 1.0, 0.0).astype(BF16)
    ones = jnp.ones((LANES, LANES), BF16)
    ri = lax.broadcasted_iota(jnp.int32, (n, n), 0)
    rj = lax.broadcasted_iota(jnp.int32, (n, n), 1)
    same = _block_id(ri, sub) == _block_id(rj, sub)
    before_row = jnp.where(same & (rj < ri), 1.0, 0.0).astype(BF16)
    all_row = jnp.where(same, 1.0, 0.0).astype(BF16)
    in_row = _dot(member, before_lane)
    prev_rows = _dot(_dot(before_row, member).astype(BF16), ones)
    total = _dot(_dot(all_row, member).astype(BF16), ones)
    base = base_ref[...]
    pos = base + prev_rows + in_row
    r1 = jnp.zeros(best.shape, F32)
    r2 = jnp.zeros(best.shape, F32)
    for e in range(N_EXPERTS):
        pe = pos[e * sub:(e + 1) * sub]
        r1 = jnp.where(id1 == e, pe, r1)
        r2 = jnp.where(id2 == e, pe, r2)
    rank_ref[0] = r1.astype(jnp.int32)
    rank_ref[1] = r2.astype(jnp.int32)
    base_ref[...] = base + total
    cnt_ref[...] = base + total


def _route(bias, logits3):
    rp, n_sub, _ = logits3.shape
    blk = lambda i: (0, i, 0)
    pair_i = jax.ShapeDtypeStruct((2, n_sub, LANES), jnp.int32)
    return pl.pallas_call(
        _route_kernel,
        grid=(n_sub // ROUTE_SUB,),
        in_specs=[
            pl.BlockSpec(memory_space=pltpu.SMEM),
            pl.BlockSpec((rp, ROUTE_SUB, LANES), blk),
        ],
        out_specs=[
            pl.BlockSpec((2, ROUTE_SUB, LANES), blk),
            pl.BlockSpec((2, ROUTE_SUB, LANES), blk),
            pl.BlockSpec((2, ROUTE_SUB, LANES), blk),
            pl.BlockSpec((N_EXPERTS * ROUTE_SUB, LANES), lambda i: (0, 0)),
        ],
        out_shape=[
            pair_i,
            jax.ShapeDtypeStruct((2, n_sub, LANES), F32),
            pair_i,
            jax.ShapeDtypeStruct((N_EXPERTS * ROUTE_SUB, LANES), F32),
        ],
        scratch_shapes=[pltpu.VMEM((N_EXPERTS * ROUTE_SUB, LANES), F32)],
        compiler_params=pltpu.CompilerParams(
            dimension_semantics=("arbitrary",), vmem_limit_bytes=VMEM_LIMIT),
        name="route",
    )(bias, logits3)


def _dest_kernel(start_ref, ids_ref, rank_ref, dest_ref):
    ids = ids_ref[...]
    off = jnp.zeros(ids.shape, jnp.int32)
    for e in range(N_EXPERTS):
        off = jnp.where(ids == e, start_ref[e], off)
    dest_ref[...] = rank_ref[...] + off


def _dest(seg_start, ids, rank):
    _, n_sub, _ = ids.shape
    blk = pl.BlockSpec((2, ROUTE_SUB, LANES), lambda i: (0, i, 0))
    return pl.pallas_call(
        _dest_kernel,
        grid=(n_sub // ROUTE_SUB,),
        in_specs=[pl.BlockSpec(memory_space=pltpu.SMEM), blk, blk],
        out_specs=blk,
        out_shape=jax.ShapeDtypeStruct(ids.shape, jnp.int32),
        name="dest",
    )(seg_start, ids, rank)


def _sc_mesh():
    return plsc.VectorSubcoreMesh(core_axis_name="core", subcore_axis_name="subcore")


def _sc_worker(rows_total):
    rows = rows_total // _SC_WORKERS
    wid = lax.axis_index("core") * SC_SUBCORES + lax.axis_index("subcore")
    return wid * rows, rows


def _dispatch(dest0, dest1, xn, n_slots):
    T, D = xn.shape
    W = SC_WINDOW
    rows = T // _SC_WORKERS

    @pl.kernel(out_type=jax.ShapeDtypeStruct((n_slots, D), xn.dtype), mesh=_sc_mesh(),
               scratch_types=[pltpu.VMEM((rows,), jnp.int32), pltpu.VMEM((rows,), jnp.int32),
                              pltpu.VMEM((2, W, D), xn.dtype),
                              pltpu.SemaphoreType.DMA((2,)), pltpu.SemaphoreType.DMA((2,))])
    def scatter_rows(x_hbm, d0_hbm, d1_hbm, xs_hbm, d0_v, d1_v, buf, in_sem, out_sem):
        base, _ = _sc_worker(T)
        pltpu.sync_copy(d0_hbm.at[pl.ds(base, rows)], d0_v)
        pltpu.sync_copy(d1_hbm.at[pl.ds(base, rows)], d1_v)

        def load(w, slot):
            return pltpu.make_async_copy(x_hbm.at[pl.ds(base + w * W, W)], buf.at[slot], in_sem.at[slot])

        def store(w, slot, d_v):
            return pltpu.make_async_copy(buf.at[slot], xs_hbm.at[d_v.at[pl.ds(w * W, W)]], out_sem.at[slot])

        def step(w, slot):
            load(w, slot).wait()
            store(w, slot, d0_v).start()
            store(w, slot, d1_v).start()
            store(w, slot, d0_v).wait()
            store(w, slot, d1_v).wait()

        n = rows // W
        load(0, 0).start()

        @pl.loop(0, n, step=2)
        def _(w):
            load(w + 1, 1).start()
            step(w, 0)

            @pl.when(w + 2 < n)
            def _():
                load(w + 2, 0).start()

            step(w + 1, 1)

    return scatter_rows(xn, dest0, dest1)


def _gather_rows(src, idx):
    M = idx.shape[0]
    D = src.shape[1]
    W = SC_WINDOW
    rows = M // _SC_WORKERS

    @pl.kernel(out_type=jax.ShapeDtypeStruct((M, D), src.dtype), mesh=_sc_mesh(),
               scratch_types=[pltpu.VMEM((rows,), jnp.int32), pltpu.VMEM((2, W, D), src.dtype),
                              pltpu.SemaphoreType.DMA((2,)), pltpu.SemaphoreType.DMA((2,))])
    def gather_rows(src_hbm, i_hbm, o_hbm, i_v, buf, in_sem, out_sem):
        base, _ = _sc_worker(M)
        pltpu.sync_copy(i_hbm.at[pl.ds(base, rows)], i_v)

        def load(w, slot):
            return pltpu.make_async_copy(src_hbm.at[i_v.at[pl.ds(w * W, W)]], buf.at[slot], in_sem.at[slot])

        def store(w, slot):
            return pltpu.make_async_copy(buf.at[slot], o_hbm.at[pl.ds(base + w * W, W)], out_sem.at[slot])

        n = rows // W
        load(0, 0).start()

        @pl.loop(0, n, step=2)
        def _(w):
            @pl.when(w > 0)
            def _():
                store(w - 1, 1).wait()

            load(w + 1, 1).start()
            load(w, 0).wait()
            store(w, 0).start()
            store(w, 0).wait()

            @pl.when(w + 2 < n)
            def _():
                load(w + 2, 0).start()

            load(w + 1, 1).wait()
            store(w + 1, 1).start()

        store(n - 1, 1).wait()

    return gather_rows(src, idx)


def _expert_kernel(be_ref, nb_ref, slot_ref, next_ref, x_ref, wg_hbm, wu_hbm, wd_hbm, y_ref,
                   wg_f32, wu_f32, wd_f32, wg_ref, wu_ref, wd_ref, sem):
    i = pl.program_id(0)
    used = i < nb_ref[0]
    new_expert = (i == 0) | (be_ref[i] != be_ref[jnp.maximum(i - 1, 0)])

    def fetch(e, slot):
        pairs = ((wg_hbm, wg_f32), (wu_hbm, wu_f32), (wd_hbm, wd_f32))
        return [pltpu.make_async_copy(w_hbm.at[e], w_f32.at[slot], sem.at[slot, k])
                for k, (w_hbm, w_f32) in enumerate(pairs)]

    @pl.when(used & (i == 0))
    def _():
        for copy in fetch(be_ref[0], slot_ref[0]):
            copy.start()

    @pl.when(used & new_expert)
    def _():
        slot = slot_ref[i]
        for copy in fetch(be_ref[i], slot):
            copy.wait()

        @pl.when(next_ref[i] >= 0)
        def _():
            for copy in fetch(next_ref[i], 1 - slot):
                copy.start()

        wg_ref[...] = wg_f32[slot].astype(BF16)
        wu_ref[...] = wu_f32[slot].astype(BF16)
        wd_ref[...] = wd_f32[slot].astype(BF16)

    @pl.when(used)
    def _():
        for c in range(EXPERT_ROWS // EXPERT_CHAIN_ROWS):
            rs = slice(c * EXPERT_CHAIN_ROWS, (c + 1) * EXPERT_CHAIN_ROWS)
            x = _unpack_pairs(x_ref[rs]).astype(BF16)
            g = _dot(x, wg_ref[...])
            u = _dot(x, wu_ref[...])
            hid = (g * _sigmoid(g) * u).astype(BF16)
            y_ref[rs] = _pack_pairs(_dot(hid, wd_ref[...]))

    @pl.when(jnp.logical_not(used))
    def _():
        y_ref[...] = jnp.zeros_like(y_ref)


def _experts(block_expert, n_used, block_slot, block_next, xs, w_gate, w_up, w_down):
    n_slots, half = xs.shape
    D = 2 * half
    n_blocks = n_slots // EXPERT_ROWS
    grid_spec = pltpu.PrefetchScalarGridSpec(
        num_scalar_prefetch=4,
        grid=(n_blocks,),
        in_specs=[
            pl.BlockSpec((EXPERT_ROWS, half), lambda i, be, nb, sl, nx: (jnp.minimum(i, nb[0] - 1), 0)),
            pl.BlockSpec(memory_space=pl.ANY),
            pl.BlockSpec(memory_space=pl.ANY),
            pl.BlockSpec(memory_space=pl.ANY),
        ],
        out_specs=pl.BlockSpec((EXPERT_ROWS, half), lambda i, be, nb, sl, nx: (i, 0)),
        scratch_shapes=[
            pltpu.VMEM((2, D, D_EXPERT), F32), pltpu.VMEM((2, D, D_EXPERT), F32),
            pltpu.VMEM((2, D_EXPERT, D), F32),
            pltpu.VMEM((D, D_EXPERT), BF16), pltpu.VMEM((D, D_EXPERT), BF16), pltpu.VMEM((D_EXPERT, D), BF16),
            pltpu.SemaphoreType.DMA((2, 3)),
        ],
    )
    return pl.pallas_call(
        _expert_kernel,
        grid_spec=grid_spec,
        out_shape=jax.ShapeDtypeStruct((n_slots, half), jnp.uint32),
        compiler_params=pltpu.CompilerParams(
            dimension_semantics=("arbitrary",), vmem_limit_bytes=VMEM_LIMIT),
        name="experts",
    )(block_expert, n_used, block_slot, block_next, xs, w_gate, w_up, w_down)


def _combine_kernel(y0_ref, y1_ref, h_ref, gates_ref, nfin_ref, *rest):
    o_ref = rest[-1]
    g0 = gates_ref[0].T
    g1 = gates_ref[1].T
    for r in range(gates_ref.shape[1]):
        rs = slice(r * LANES, (r + 1) * LANES)
        h = (h_ref[rs].astype(F32) + g0[:, r:r + 1] * _unpack_pairs(y0_ref[rs])
             + g1[:, r:r + 1] * _unpack_pairs(y1_ref[rs]))
        o_ref[rs] = _rms(h, nfin_ref[...])


def _combine(y01, h2, gates, norm_final, out_prev, row0, total_rows):
    T, D = h2.shape
    n_t = T // MOVE_ROWS
    in_specs = [
        pl.BlockSpec((MOVE_ROWS, D // 2), lambda i: (i, 0)),
        pl.BlockSpec((MOVE_ROWS, D // 2), lambda i: (i + n_t, 0)),
        pl.BlockSpec((MOVE_ROWS, D), lambda i: (i, 0)),
        pl.BlockSpec((2, MOVE_ROWS // LANES, LANES), lambda i: (0, i, 0)),
        pl.BlockSpec((1, D), lambda i: (0, 0)),
    ]
    args = [y01, y01, h2, gates, norm_final]
    aliases = {}
    if out_prev is not None:
        in_specs.append(pl.BlockSpec(memory_space=pl.ANY))
        args.append(out_prev)
        aliases = {len(args) - 1: 0}
    return pl.pallas_call(
        _combine_kernel,
        grid=(n_t,),
        in_specs=in_specs,
        out_specs=pl.BlockSpec((MOVE_ROWS, D), lambda i: (i + row0 // MOVE_ROWS, 0)),
        out_shape=jax.ShapeDtypeStruct((total_rows, D), F32),
        input_output_aliases=aliases,
        compiler_params=pltpu.CompilerParams(
            dimension_semantics=("arbitrary",), vmem_limit_bytes=VMEM_LIMIT),
        name="combine",
    )(*args)


def kernel(x, mem, norm_mix, w_in, gmlp_ln, gmlp_w_spatial, gmlp_b_spatial, gmlp_beta, hgrn_lb_logits, hgrn_out_gain, w_out, norm_xattn, norm_mem, w_xq, w_xkv, w_xo, norm_ffn, w_router_group, b_router_group, w_router_expert, b_router_expert, w_expert_gate, w_expert_up, w_expert_down, norm_final):
    B, S, D = x.shape
    T = B * S
    depth = w_in.shape[0]
    assert depth == 1 and hgrn_lb_logits.shape[0] == 2
    l = 0
    row = lambda p: p.reshape(1, -1)

    h1 = _mixer(x, row(norm_mix[l]), w_in[l], row(gmlp_ln[l]), gmlp_w_spatial[l],
                gmlp_b_spatial[l].T, row(gmlp_beta[l]), hgrn_lb_logits, row(hgrn_out_gain[l]),
                w_out[l])
    k_mem, v_mem = _memkv(mem, row(norm_mem[l]), w_xkv[l])

    w_router = jnp.concatenate([w_router_group[l].T, w_router_expert[l].T], axis=0)
    w_router = jnp.pad(w_router, ((0, ROUTER_ROWS - w_router.shape[0]), (0, 0)))
    w_router = w_router.astype(BF16)
    bias = jnp.concatenate([b_router_group[l], b_router_expert[l]]).astype(F32)

    assert sum(PART_BATCHES) == B
    out = None
    b0 = 0
    for b_part in PART_BATCHES:
        t_part = b_part * S
        n_blocks = (2 * t_part) // EXPERT_ROWS + N_EXPERTS
        h2, xn, logits = _xattn(h1, row(norm_xattn[l]), w_xq[l], k_mem, v_mem, w_xo[l], row(norm_ffn[l]),
                                w_router, b0, b_part)
        ids, gates, rank, counts = _route(bias, logits)

        counts = counts[::ROUTE_SUB, 0].astype(jnp.int32)
        padded = (counts + EXPERT_ROWS - 1) // EXPERT_ROWS * EXPERT_ROWS
        seg_end = jnp.cumsum(padded)
        seg_start = seg_end - padded
        block_first_row = jnp.arange(n_blocks, dtype=jnp.int32) * EXPERT_ROWS
        block_expert = jnp.minimum(
            jnp.sum(block_first_row[:, None] >= seg_end[None, :], axis=1), N_EXPERTS - 1).astype(jnp.int32)
        n_used = (seg_end[-1:] // EXPERT_ROWS).astype(jnp.int32)
        present = counts > 0
        expert_ids = jnp.arange(N_EXPERTS, dtype=jnp.int32)
        expert_slot = (jnp.cumsum(present.astype(jnp.int32)) - 1) % 2
        later = jnp.where(present[None, :] & (expert_ids[None, :] > expert_ids[:, None]),
                          expert_ids[None, :], N_EXPERTS)
        expert_next = jnp.min(later, axis=1)
        expert_next = jnp.where(expert_next == N_EXPERTS, -1, expert_next).astype(jnp.int32)
        of_block = (block_expert[:, None] == expert_ids[None, :]).astype(jnp.int32)
        block_slot = jnp.sum(of_block * expert_slot[None, :], axis=1).astype(jnp.int32)
        block_next = jnp.sum(of_block * expert_next[None, :], axis=1).astype(jnp.int32)

        dest = _dest(seg_start, ids, rank).reshape(2, t_part)
        xs = _dispatch(dest[0], dest[1], xn, n_blocks * EXPERT_ROWS)
        yb = _experts(block_expert, n_used, block_slot, block_next, xs,
                      w_expert_gate[l], w_expert_up[l], w_expert_down[l])
        y01 = _gather_rows(yb, dest.reshape(2 * t_part))
        out = _combine(y01, h2.reshape(t_part, D), gates, row(norm_final), out, b0 * S, T)
        b0 += b_part
    return out.reshape(B, S, D)
```

```python
import jax
import jax.numpy as jnp
from jax import lax
from jax.experimental import pallas as pl
from jax.experimental.pallas import tpu as pltpu
from jax.experimental.pallas import tpu_sc as plsc

F32 = jnp.float32
BF16 = jnp.bfloat16
EPS = 1e-6

D_MODEL = 1024
D_GMLP = 512
GMLP_GROUPS = 4
GMLP_CHUNK = 128
D_HGRN = 512
HGRN_HEADS = 4
HGRN_DK = 128
HGRN_CHUNK = 64
D_IN_PROJ = 2 * D_GMLP + 4 * D_HGRN
N_MEM = 256
XATTN_HEADS = 4
XATTN_HEAD_DIM = D_MODEL // XATTN_HEADS
N_GROUPS = 4
EXPERTS_PER_GROUP = 8
N_EXPERTS = N_GROUPS * EXPERTS_PER_GROUP
D_EXPERT = 512

LANES = 128
MIX_ROWS = 1024
MIX_CHAIN_ROWS = 256


def _mix_schedule(n_chains):
    order = [("in", 0), ("gmlp", 0), ("factors", 0)]
    for k in range(n_chains):
        more = k + 1 < n_chains
        order += [("in", k + 1)] * more + [("local", k), ("recurrence", k)] + [("gmlp", k + 1)] * more
        order += [("out", k)] + [("factors", k + 1)] * more
    return tuple(order)


MIX_SCHEDULE = _mix_schedule(MIX_ROWS // MIX_CHAIN_ROWS)
MEMKV_BATCHES = 4
ATT_ROWS = 1024
WEIGHT_STAGE_ROWS = 128
ROUTER_ROWS = 40
ROUTE_SUB = 16
EXPERT_ROWS = 512
MOVE_ROWS = 1024
PART_BATCHES = (16, 16)
SC_WINDOW = 64
SC_CORES = 2
SC_SUBCORES = 16
_SC_WORKERS = SC_CORES * SC_SUBCORES
VMEM_LIMIT = 48 * 1024 * 1024


def _rms(x, gain):
    return x * lax.rsqrt(jnp.mean(x * x, axis=-1, keepdims=True) + EPS) * gain


def _dot(a, b):
    return jnp.dot(a, b, preferred_element_type=F32)


def _dot_nt(a, b):
    return lax.dot_general(a, b, (((1,), (1,)), ((), ())), preferred_element_type=F32)


def _dot_tn(a, b):
    return lax.dot_general(a, b, (((0,), (0,)), ((), ())), preferred_element_type=F32)


def _gelu(x):
    return 0.5 * x * (1.0 + jnp.tanh(0.7978845608028654 * (x + 0.044715 * (x * x * x))))


def _sigmoid(x):
    return 1.0 / (1.0 + jnp.exp(-x))


def _block_id(idx, size):
    assert size & (size - 1) == 0
    return lax.shift_right_logical(idx, size.bit_length() - 1)


def _stage_weight(w_hbm, w_bf16, stage_ref, sem):
    rows = stage_ref.shape[0]
    for k in range(w_hbm.shape[0] // rows):
        copy = pltpu.make_async_copy(w_hbm.at[pl.ds(k * rows, rows)], stage_ref, sem)
        copy.start()
        copy.wait()
        w_bf16[k * rows:(k + 1) * rows, :] = stage_ref[...].astype(BF16)


def _weight_scratch(k, n):
    return [pltpu.VMEM((k, n), BF16), pltpu.VMEM((WEIGHT_STAGE_ROWS, n), F32)]


_HIGH_HALF = 0xFFFF0000


def _pack_pairs(x):
    c = x.shape[1] // 2
    bits = lax.bitcast_convert_type(x.astype(BF16).astype(F32), jnp.uint32)
    return (bits[:, c:] & jnp.uint32(_HIGH_HALF)) | lax.shift_right_logical(bits[:, :c], jnp.uint32(16))


def _unpack_pairs(w):
    lo = lax.bitcast_convert_type(lax.shift_left(w, jnp.uint32(16)), F32)
    hi = lax.bitcast_convert_type(w & jnp.uint32(_HIGH_HALF), F32)
    return jnp.concatenate([lo, hi], axis=1)


def _mixer_kernel(x_ref, nmix_ref, win_hbm, gln_ref, ws_ref, bst_ref, beta_ref, lbl_ref, og_ref,
                  wout_hbm, o_ref, proj_ref, ycat_ref, state_ref, win_ref, win_stage, wout_ref, wout_stage,
                  sem):
    @pl.when((pl.program_id(0) == 0) & (pl.program_id(1) == 0))
    def _():
        _stage_weight(win_hbm, win_ref, win_stage, sem)
        _stage_weight(wout_hbm, wout_ref, wout_stage, sem)

    @pl.when(pl.program_id(1) == 0)
    def _():
        state_ref[...] = jnp.zeros_like(state_ref)

    n = MIX_CHAIN_ROWS
    r_i = lax.broadcasted_iota(jnp.int32, (GMLP_CHUNK, GMLP_CHUNK), 0)
    c_i = lax.broadcasted_iota(jnp.int32, (GMLP_CHUNK, GMLP_CHUNK), 1)
    causal = c_i <= r_i
    w_tril = [jnp.where(causal, ws_ref[g], 0.0).astype(BF16) for g in range(GMLP_GROUPS)]
    lbl = lbl_ref[...]
    e_lb = jnp.exp(lbl - jnp.max(lbl, axis=0, keepdims=True))
    lb = e_lb[0:1] / jnp.sum(e_lb, axis=0, keepdims=True)
    rr = lax.broadcasted_iota(jnp.int32, (n, n), 0)
    cc = lax.broadcasted_iota(jnp.int32, (n, n), 1)
    tri = jnp.where((_block_id(rr, HGRN_CHUNK) == _block_id(cc, HGRN_CHUNK)) & (cc <= rr),
                    1.0, 0.0).astype(BF16)
    r64 = lax.broadcasted_iota(jnp.int32, (HGRN_CHUNK, HGRN_CHUNK), 0)
    c64 = lax.broadcasted_iota(jnp.int32, (HGRN_CHUNK, HGRN_CHUNK), 1)
    causal64 = c64 <= r64
    base = 2 * D_GMLP

    n_chunks = n // HGRN_CHUNK
    chains = range(x_ref.shape[1] // n)
    env = {ch: {} for ch in chains}

    def rows(ch):
        return slice(ch * n, (ch + 1) * n)

    def in_proj(ch):
        a = _rms(x_ref[0, rows(ch)], nmix_ref[...]).astype(BF16)
        proj_ref[rows(ch)] = _dot(a, win_ref[...])

    def gmlp(ch):
        u = _gelu(proj_ref[rows(ch), 0:D_GMLP])
        v = _gelu(proj_ref[rows(ch), D_GMLP:2 * D_GMLP])
        vc = v - jnp.mean(v, axis=-1, keepdims=True)
        vn = (vc * lax.rsqrt(jnp.mean(vc * vc, axis=-1, keepdims=True) + EPS) * gln_ref[...]).astype(BF16)
        z_rows = []
        for c in range(n // GMLP_CHUNK):
            z_cols = []
            for g in range(GMLP_GROUPS):
                vg = vn[c * GMLP_CHUNK:(c + 1) * GMLP_CHUNK, g * LANES:(g + 1) * LANES]
                z_cols.append(_dot(w_tril[g], vg) + bst_ref[:, g:g + 1])
            z_rows.append(jnp.concatenate(z_cols, axis=1))
        z = jnp.concatenate(z_rows, axis=0)
        ycat_ref[rows(ch), 0:D_GMLP] = _rms(u * z, beta_ref[...]).astype(BF16)

    def hgrn_factors(ch):
        e = env[ch]
        f = lb + (1.0 - lb) * _sigmoid(proj_ref[rows(ch), base + D_HGRN:base + 2 * D_HGRN])
        log_f = jnp.log(f)
        lf_hi = log_f.astype(BF16)
        lf_lo = (log_f - lf_hi.astype(F32)).astype(BF16)
        b_all = _dot(tri, lf_hi) + _dot(tri, lf_lo)
        bl_rows = [b_all[c * HGRN_CHUNK + HGRN_CHUNK - 1:(c + 1) * HGRN_CHUNK] for c in range(n_chunks)]
        bl_all = jnp.concatenate([jnp.broadcast_to(r, (HGRN_CHUNK, D_HGRN)) for r in bl_rows], axis=0)
        q_all = proj_ref[rows(ch), base:base + D_HGRN]
        k_all = 1.0 - f
        e["qd"] = (q_all * _sigmoid(q_all) * jnp.exp(b_all)).astype(BF16)
        e["ki"] = (k_all * jnp.exp(-b_all)).astype(BF16)
        e["kte"] = (k_all * jnp.exp(bl_all - b_all)).astype(BF16)
        e["v"] = proj_ref[rows(ch), base + 2 * D_HGRN:base + 3 * D_HGRN].astype(BF16)
        e["decay"] = [jnp.exp(r) for r in bl_rows]

    def hgrn_local(ch):
        e = env[ch]
        e["o_intra"], e["d_state"] = {}, {}
        for c in range(n_chunks):
            rs = slice(c * HGRN_CHUNK, (c + 1) * HGRN_CHUNK)
            for h in range(HGRN_HEADS):
                cs = slice(h * HGRN_DK, (h + 1) * HGRN_DK)
                scores = jnp.where(causal64, _dot_nt(e["qd"][rs, cs], e["ki"][rs, cs]), 0.0).astype(BF16)
                e["o_intra"][c, h] = _dot(scores, e["v"][rs, cs])
                e["d_state"][c, h] = _dot_tn(e["v"][rs, cs], e["kte"][rs, cs])

    def hgrn_recurrence(ch):
        e = env[ch]
        for c in range(n_chunks):
            rs = slice(c * HGRN_CHUNK, (c + 1) * HGRN_CHUNK)
            ps = slice(ch * n + c * HGRN_CHUNK, ch * n + (c + 1) * HGRN_CHUNK)
            g_c = proj_ref[ps, base + 3 * D_HGRN:base + 4 * D_HGRN]
            gate = og_ref[...] * (g_c * _sigmoid(g_c))
            for h in range(HGRN_HEADS):
                cs = slice(h * HGRN_DK, (h + 1) * HGRN_DK)
                st = state_ref[h]
                o = e["o_intra"][c, h] + _dot_nt(e["qd"][rs, cs], st.astype(BF16))
                state_ref[h] = st * e["decay"][c][:, cs] + e["d_state"][c, h]
                o = o * lax.rsqrt(jnp.mean(o * o, axis=-1, keepdims=True) + EPS)
                ycat_ref[ps, D_GMLP + h * HGRN_DK:D_GMLP + (h + 1) * HGRN_DK] = (o * gate[:, cs]).astype(BF16)

    def out_proj(ch):
        o_ref[0, rows(ch)] = x_ref[0, rows(ch)] + _dot(ycat_ref[rows(ch)], wout_ref[...])

    stages = {"in": in_proj, "gmlp": gmlp, "factors": hgrn_factors, "local": hgrn_local,
              "recurrence": hgrn_recurrence, "out": out_proj}
    for stage, ch in MIX_SCHEDULE:
        stages[stage](ch)


def _mixer(x, norm_mix, w_in, gmlp_ln, w_s, b_s_t, beta, lb_logits, out_gain, w_out):
    B, S, D = x.shape
    const2 = lambda b, s: (0, 0)
    return pl.pallas_call(
        _mixer_kernel,
        grid=(B, S // MIX_ROWS),
        in_specs=[
            pl.BlockSpec((1, MIX_ROWS, D), lambda b, s: (b, s, 0)),
            pl.BlockSpec((1, D), const2),
            pl.BlockSpec(memory_space=pl.ANY),
            pl.BlockSpec((1, D_GMLP), const2),
            pl.BlockSpec((GMLP_GROUPS, GMLP_CHUNK, GMLP_CHUNK), lambda b, s: (0, 0, 0)),
            pl.BlockSpec((GMLP_CHUNK, GMLP_GROUPS), const2),
            pl.BlockSpec((1, D_GMLP), const2),
            pl.BlockSpec(lb_logits.shape, const2),
            pl.BlockSpec((1, D_HGRN), const2),
            pl.BlockSpec(memory_space=pl.ANY),
        ],
        out_specs=pl.BlockSpec((1, MIX_ROWS, D), lambda b, s: (b, s, 0)),
        out_shape=jax.ShapeDtypeStruct((B, S, D), F32),
        scratch_shapes=[
            pltpu.VMEM((MIX_ROWS, D_IN_PROJ), F32),
            pltpu.VMEM((MIX_ROWS, D), BF16),
            pltpu.VMEM((HGRN_HEADS, HGRN_DK, HGRN_DK), F32),
        ] + _weight_scratch(D, D_IN_PROJ) + _weight_scratch(D, D) + [pltpu.SemaphoreType.DMA],
        compiler_params=pltpu.CompilerParams(
            dimension_semantics=("arbitrary", "arbitrary"), vmem_limit_bytes=VMEM_LIMIT),
        name="mixer",
    )(x, norm_mix, w_in, gmlp_ln, w_s, b_s_t, beta, lb_logits, out_gain, w_out)


def _memkv_kernel(mem_ref, nm_ref, wkv_hbm, k_ref, v_ref, wkv_ref, wkv_stage, sem):
    @pl.when(pl.program_id(0) == 0)
    def _():
        _stage_weight(wkv_hbm, wkv_ref, wkv_stage, sem)

    nb, M, D = mem_ref.shape
    m = _rms(mem_ref[...].reshape(nb * M, D), nm_ref[...]).astype(BF16)
    kv = _dot(m, wkv_ref[...])
    k_ref[...] = kv[:, :D_MODEL].astype(BF16).reshape(nb, M, D)
    v_ref[...] = kv[:, D_MODEL:].astype(BF16).reshape(nb, M, D)


def _memkv(mem, norm_mem, w_kv):
    B, M, D = mem.shape
    out = jax.ShapeDtypeStruct((B, M, D), BF16)
    return pl.pallas_call(
        _memkv_kernel,
        grid=(B // MEMKV_BATCHES,),
        in_specs=[
            pl.BlockSpec((MEMKV_BATCHES, M, D), lambda b: (b, 0, 0)),
            pl.BlockSpec((1, D), lambda b: (0, 0)),
            pl.BlockSpec(memory_space=pl.ANY),
        ],
        out_specs=[pl.BlockSpec((MEMKV_BATCHES, M, D), lambda b: (b, 0, 0))] * 2,
        out_shape=[out, out],
        scratch_shapes=_weight_scratch(D, 2 * D) + [pltpu.SemaphoreType.DMA],
        compiler_params=pltpu.CompilerParams(
            dimension_semantics=("arbitrary",), vmem_limit_bytes=VMEM_LIMIT),
        name="memkv",
    )(mem, norm_mem, w_kv)


def _xattn_kernel(h_ref, nx_ref, wq_hbm, k_ref, v_ref, wo_hbm, nf_ref, wr_ref,
                  h2_ref, xn_ref, lg_ref, att_ref, wq_ref, wq_stage, wo_ref, wo_stage, sem):
    @pl.when((pl.program_id(0) == 0) & (pl.program_id(1) == 0))
    def _():
        _stage_weight(wq_hbm, wq_ref, wq_stage, sem)
        _stage_weight(wo_hbm, wo_ref, wo_stage, sem)

    h = h_ref[0]
    hn = _rms(h, nx_ref[...]).astype(BF16)
    q = (_dot(hn, wq_ref[...]) * (XATTN_HEAD_DIM ** -0.5)).astype(BF16)
    heads = [slice(hd * XATTN_HEAD_DIM, (hd + 1) * XATTN_HEAD_DIM) for hd in range(XATTN_HEADS)]
    scores = [_dot_nt(q[:, cs], k_ref[0, :, cs]) for cs in heads]
    probs = []
    for s in scores:
        p = jnp.exp(s - jnp.max(s, axis=-1, keepdims=True))
        probs.append((p / jnp.sum(p, axis=-1, keepdims=True)).astype(BF16))
    for cs, p in zip(heads, probs):
        att_ref[:, cs] = _dot(p, v_ref[0, :, cs]).astype(BF16)
    h2 = h + _dot(att_ref[...], wo_ref[...])
    h2_ref[0] = h2.astype(h2_ref.dtype)
    xn = _rms(h2, nf_ref[...])
    xn_ref[...] = _pack_pairs(xn)
    lg = _dot_nt(wr_ref[...], xn.astype(BF16))
    for j in range(ATT_ROWS // LANES):
        lg_ref[:, j, :] = lg[:, j * LANES:(j + 1) * LANES]


def _xattn(h1, norm_x, w_q, k_mem, v_mem, w_o, norm_ffn, w_router, batch0, batches):
    _, S, D = h1.shape
    n_s = S // ATT_ROWS
    const2 = lambda b, s: (0, 0)
    return pl.pallas_call(
        _xattn_kernel,
        grid=(batches, n_s),
        in_specs=[
            pl.BlockSpec((1, ATT_ROWS, D), lambda b, s: (b + batch0, s, 0)),
            pl.BlockSpec((1, D), const2),
            pl.BlockSpec(memory_space=pl.ANY),
            pl.BlockSpec((1, N_MEM, D), lambda b, s: (b + batch0, 0, 0)),
            pl.BlockSpec((1, N_MEM, D), lambda b, s: (b + batch0, 0, 0)),
            pl.BlockSpec(memory_space=pl.ANY),
            pl.BlockSpec((1, D), const2),
            pl.BlockSpec((ROUTER_ROWS, D), const2),
        ],
        out_specs=[
            pl.BlockSpec((1, ATT_ROWS, D), lambda b, s: (b, s, 0)),
            pl.BlockSpec((ATT_ROWS, D // 2), lambda b, s: (b * n_s + s, 0)),
            pl.BlockSpec((ROUTER_ROWS, ATT_ROWS // LANES, LANES), lambda b, s: (0, b * n_s + s, 0)),
        ],
        out_shape=[
            jax.ShapeDtypeStruct((batches, S, D), BF16),
            jax.ShapeDtypeStruct((batches * S, D // 2), jnp.uint32),
            jax.ShapeDtypeStruct((ROUTER_ROWS, batches * S // LANES, LANES), F32),
        ],
        scratch_shapes=([pltpu.VMEM((ATT_ROWS, D), BF16)] + _weight_scratch(D, D) + _weight_scratch(D, D)
                        + [pltpu.SemaphoreType.DMA]),
        compiler_params=pltpu.CompilerParams(
            dimension_semantics=("arbitrary", "arbitrary"), vmem_limit_bytes=VMEM_LIMIT),
        name="xattn",
    )(h1, norm_x, w_q, k_mem, v_mem, w_o, norm_ffn, w_router)


def _route_kernel(bias_ref, lg_ref, ids_ref, gates_ref, rank_ref, cnt_ref, base_ref):
    sub = lg_ref.shape[1]

    @pl.when(pl.program_id(0) == 0)
    def _():
        base_ref[...] = jnp.zeros_like(base_ref)

    best = lg_ref[0] + bias_ref[0]
    gl = [best]
    sel = jnp.zeros(best.shape, jnp.int32)
    for g in range(1, N_GROUPS):
        cur = lg_ref[g] + bias_ref[g]
        gl.append(cur)
        better = cur > best
        best = jnp.where(better, cur, best)
        sel = jnp.where(better, g, sel)
    denom = jnp.exp(gl[0] - best)
    for g in range(1, N_GROUPS):
        denom = denom + jnp.exp(gl[g] - best)
    g_gate = 1.0 / denom

    ev = []
    for j in range(EXPERTS_PER_GROUP):
        val = lg_ref[N_GROUPS + j] + bias_ref[N_GROUPS + j]
        for g in range(1, N_GROUPS):
            e = g * EXPERTS_PER_GROUP + j
            val = jnp.where(sel == g, lg_ref[N_GROUPS + e] + bias_ref[N_GROUPS + e], val)
        ev.append(val)
    v1, i1 = ev[0], jnp.zeros(best.shape, jnp.int32)
    for j in range(1, EXPERTS_PER_GROUP):
        better = ev[j] > v1
        v1 = jnp.where(better, ev[j], v1)
        i1 = jnp.where(better, j, i1)
    rest = [jnp.where(i1 == j, -jnp.inf, ev[j]) for j in range(EXPERTS_PER_GROUP)]
    v2, i2 = rest[0], jnp.zeros(best.shape, jnp.int32)
    for j in range(1, EXPERTS_PER_GROUP):
        better = rest[j] > v2
        v2 = jnp.where(better, rest[j], v2)
        i2 = jnp.where(better, j, i2)
    e2 = jnp.exp(v2 - v1)
    inv = 1.0 / (1.0 + e2)
    id1 = sel * EXPERTS_PER_GROUP + i1
    id2 = sel * EXPERTS_PER_GROUP + i2
    ids_ref[0] = id1
    ids_ref[1] = id2
    gates_ref[0] = inv * g_gate
    gates_ref[1] = e2 * inv * g_gate

    member = jnp.concatenate(
        [jnp.where((id1 == e) | (id2 == e), 1.0, 0.0) for e in range(N_EXPERTS)], axis=0).astype(BF16)
    n = N_EXPERTS * sub
    li = lax.broadcasted_iota(jnp.int32, (LANES, LANES), 0)
    lj = lax.broadcasted_iota(jnp.int32, (LANES, LANES), 1)
    before_lane = jnp.where(li < lj, 1.0, 0.0).astype(BF16)
    ones = jnp.ones((LANES, LANES), BF16)
    ri = lax.broadcasted_iota(jnp.int32, (n, n), 0)
    rj = lax.broadcasted_iota(jnp.int32, (n, n), 1)
    same = _block_id(ri, sub) == _block_id(rj, sub)
    before_row = jnp.where(same & (rj < ri), 1.0, 0.0).astype(BF16)
    all_row = jnp.where(same, 1.0, 0.0).astype(BF16)
    in_row = _dot(member, before_lane)
    prev_rows = _dot(_dot(before_row, member).astype(BF16), ones)
    total = _dot(_dot(all_row, member).astype(BF16), ones)
    base = base_ref[...]
    pos = base + prev_rows + in_row
    r1 = jnp.zeros(best.shape, F32)
    r2 = jnp.zeros(best.shape, F32)
    for e in range(N_EXPERTS):
        pe = pos[e * sub:(e + 1) * sub]
        r1 = jnp.where(id1 == e, pe, r1)
        r2 = jnp.where(id2 == e, pe, r2)
    rank_ref[0] = r1.astype(jnp.int32)
    rank_ref[1] = r2.astype(jnp.int32)
    base_ref[...] = base + total
    cnt_ref[...] = base + total


def _route(bias, logits3):
    rp, n_sub, _ = logits3.shape
    blk = lambda i: (0, i, 0)
    pair_i = jax.ShapeDtypeStruct((2, n_sub, LANES), jnp.int32)
    return pl.pallas_call(
        _route_kernel,
        grid=(n_sub // ROUTE_SUB,),
        in_specs=[
            pl.BlockSpec(memory_space=pltpu.SMEM),
            pl.BlockSpec((rp, ROUTE_SUB, LANES), blk),
        ],
        out_specs=[
            pl.BlockSpec((2, ROUTE_SUB, LANES), blk),
            pl.BlockSpec((2, ROUTE_SUB, LANES), blk),
            pl.BlockSpec((2, ROUTE_SUB, LANES), blk),
            pl.BlockSpec((N_EXPERTS * ROUTE_SUB, LANES), lambda i: (0, 0)),
        ],
        out_shape=[
            pair_i,
            jax.ShapeDtypeStruct((2, n_sub, LANES), F32),
            pair_i,
            jax.ShapeDtypeStruct((N_EXPERTS * ROUTE_SUB, LANES), F32),
        ],
        scratch_shapes=[pltpu.VMEM((N_EXPERTS * ROUTE_SUB, LANES), F32)],
        compiler_params=pltpu.CompilerParams(
            dimension_semantics=("arbitrary",), vmem_limit_bytes=VMEM_LIMIT),
        name="route",
    )(bias, logits3)


def _dest_kernel(start_ref, ids_ref, rank_ref, dest_ref):
    ids = ids_ref[...]
    off = jnp.zeros(ids.shape, jnp.int32)
    for e in range(N_EXPERTS):
        off = jnp.where(ids == e, start_ref[e], off)
    dest_ref[...] = rank_ref[...] + off


def _dest(seg_start, ids, rank):
    _, n_sub, _ = ids.shape
    blk = pl.BlockSpec((2, ROUTE_SUB, LANES), lambda i: (0, i, 0))
    return pl.pallas_call(
        _dest_kernel,
        grid=(n_sub // ROUTE_SUB,),
        in_specs=[pl.BlockSpec(memory_space=pltpu.SMEM), blk, blk],
        out_specs=blk,
        out_shape=jax.ShapeDtypeStruct(ids.shape, jnp.int32),
        name="dest",
    )(seg_start, ids, rank)


def _sc_mesh():
    return plsc.VectorSubcoreMesh(core_axis_name="core", subcore_axis_name="subcore")


def _sc_worker(rows_total):
    rows = rows_total // _SC_WORKERS
    wid = lax.axis_index("core") * SC_SUBCORES + lax.axis_index("subcore")
    return wid * rows, rows


def _dispatch(dest0, dest1, xn, n_slots):
    T, D = xn.shape
    W = SC_WINDOW
    rows = T // _SC_WORKERS
    assert T % (_SC_WORKERS * 2 * W) == 0

    @pl.kernel(out_type=jax.ShapeDtypeStruct((n_slots, D), xn.dtype), mesh=_sc_mesh(),
               scratch_types=[pltpu.VMEM((rows,), jnp.int32), pltpu.VMEM((rows,), jnp.int32),
                              pltpu.VMEM((2, W, D), xn.dtype),
                              pltpu.SemaphoreType.DMA((2,)), pltpu.SemaphoreType.DMA((2,))])
    def scatter_rows(x_hbm, d0_hbm, d1_hbm, xs_hbm, d0_v, d1_v, buf, in_sem, out_sem):
        base, _ = _sc_worker(T)
        pltpu.sync_copy(d0_hbm.at[pl.ds(base, rows)], d0_v)
        pltpu.sync_copy(d1_hbm.at[pl.ds(base, rows)], d1_v)

        def load(w, slot):
            return pltpu.make_async_copy(x_hbm.at[pl.ds(base + w * W, W)], buf.at[slot], in_sem.at[slot])

        def store(w, slot, d_v):
            return pltpu.make_async_copy(buf.at[slot], xs_hbm.at[d_v.at[pl.ds(w * W, W)]], out_sem.at[slot])

        def step(w, slot):
            load(w, slot).wait()
            store(w, slot, d0_v).start()
            store(w, slot, d1_v).start()
            store(w, slot, d0_v).wait()
            store(w, slot, d1_v).wait()

        n = rows // W
        load(0, 0).start()

        @pl.loop(0, n, step=2)
        def _(w):
            load(w + 1, 1).start()
            step(w, 0)

            @pl.when(w + 2 < n)
            def _():
                load(w + 2, 0).start()

            step(w + 1, 1)

    return scatter_rows(xn, dest0, dest1)


def _gather_rows(src, idx):
    M = idx.shape[0]
    D = src.shape[1]
    W = SC_WINDOW
    rows = M // _SC_WORKERS
    assert M % (_SC_WORKERS * 2 * W) == 0

    @pl.kernel(out_type=jax.ShapeDtypeStruct((M, D), src.dtype), mesh=_sc_mesh(),
               scratch_types=[pltpu.VMEM((rows,), jnp.int32), pltpu.VMEM((2, W, D), src.dtype),
                              pltpu.SemaphoreType.DMA((2,)), pltpu.SemaphoreType.DMA((2,))])
    def gather_rows(src_hbm, i_hbm, o_hbm, i_v, buf, in_sem, out_sem):
        base, _ = _sc_worker(M)
        pltpu.sync_copy(i_hbm.at[pl.ds(base, rows)], i_v)

        def load(w, slot):
            return pltpu.make_async_copy(src_hbm.at[i_v.at[pl.ds(w * W, W)]], buf.at[slot], in_sem.at[slot])

        def store(w, slot):
            return pltpu.make_async_copy(buf.at[slot], o_hbm.at[pl.ds(base + w * W, W)], out_sem.at[slot])

        n = rows // W
        load(0, 0).start()

        @pl.loop(0, n, step=2)
        def _(w):
            @pl.when(w > 0)
            def _():
                store(w - 1, 1).wait()

            load(w + 1, 1).start()
            load(w, 0).wait()
            store(w, 0).start()
            store(w, 0).wait()

            @pl.when(w + 2 < n)
            def _():
                load(w + 2, 0).start()

            load(w + 1, 1).wait()
            store(w + 1, 1).start()

        store(n - 1, 1).wait()

    return gather_rows(src, idx)


def _expert_kernel(be_ref, nb_ref, slot_ref, next_ref, x_ref, wg_hbm, wu_hbm, wd_hbm, y_ref,
                   wg_f32, wu_f32, wd_f32, wg_ref, wu_ref, wd_ref, sem):
    i = pl.program_id(0)
    used = i < nb_ref[0]
    new_expert = (i == 0) | (be_ref[i] != be_ref[jnp.maximum(i - 1, 0)])

    def fetch(e, slot):
        pairs = ((wg_hbm, wg_f32), (wu_hbm, wu_f32), (wd_hbm, wd_f32))
        return [pltpu.make_async_copy(w_hbm.at[e], w_f32.at[slot], sem.at[slot, k])
                for k, (w_hbm, w_f32) in enumerate(pairs)]

    @pl.when(used & (i == 0))
    def _():
        for copy in fetch(be_ref[0], slot_ref[0]):
            copy.start()

    @pl.when(used & new_expert)
    def _():
        slot = slot_ref[i]
        for copy in fetch(be_ref[i], slot):
            copy.wait()

        @pl.when(next_ref[i] >= 0)
        def _():
            for copy in fetch(next_ref[i], 1 - slot):
                copy.start()

        wg_ref[...] = wg_f32[slot].astype(BF16)
        wu_ref[...] = wu_f32[slot].astype(BF16)
        wd_ref[...] = wd_f32[slot].astype(BF16)

    @pl.when(used)
    def _():
        x = _unpack_pairs(x_ref[...]).astype(BF16)
        g = _dot(x, wg_ref[...])
        u = _dot(x, wu_ref[...])
        hid = (g * _sigmoid(g) * u).astype(BF16)
        y_ref[...] = _pack_pairs(_dot(hid, wd_ref[...]))

    @pl.when(jnp.logical_not(used))
    def _():
        y_ref[...] = jnp.zeros_like(y_ref)


def _experts(block_expert, n_used, block_slot, block_next, xs, w_gate, w_up, w_down):
    n_slots, half = xs.shape
    D = 2 * half
    n_blocks = n_slots // EXPERT_ROWS
    grid_spec = pltpu.PrefetchScalarGridSpec(
        num_scalar_prefetch=4,
        grid=(n_blocks,),
        in_specs=[
            pl.BlockSpec((EXPERT_ROWS, half), lambda i, be, nb, sl, nx: (jnp.minimum(i, nb[0] - 1), 0)),
            pl.BlockSpec(memory_space=pl.ANY),
            pl.BlockSpec(memory_space=pl.ANY),
            pl.BlockSpec(memory_space=pl.ANY),
        ],
        out_specs=pl.BlockSpec((EXPERT_ROWS, half), lambda i, be, nb, sl, nx: (i, 0)),
        scratch_shapes=[
            pltpu.VMEM((2, D, D_EXPERT), F32), pltpu.VMEM((2, D, D_EXPERT), F32),
            pltpu.VMEM((2, D_EXPERT, D), F32),
            pltpu.VMEM((D, D_EXPERT), BF16), pltpu.VMEM((D, D_EXPERT), BF16), pltpu.VMEM((D_EXPERT, D), BF16),
            pltpu.SemaphoreType.DMA((2, 3)),
        ],
    )
    return pl.pallas_call(
        _expert_kernel,
        grid_spec=grid_spec,
        out_shape=jax.ShapeDtypeStruct((n_slots, half), jnp.uint32),
        compiler_params=pltpu.CompilerParams(
            dimension_semantics=("arbitrary",), vmem_limit_bytes=VMEM_LIMIT),
        name="experts",
    )(block_expert, n_used, block_slot, block_next, xs, w_gate, w_up, w_down)


def _combine_kernel(y0_ref, y1_ref, h_ref, gates_ref, nfin_ref, *rest):
    o_ref = rest[-1]
    g0 = gates_ref[0].T
    g1 = gates_ref[1].T
    for r in range(gates_ref.shape[1]):
        rs = slice(r * LANES, (r + 1) * LANES)
        h = (h_ref[rs].astype(F32) + g0[:, r:r + 1] * _unpack_pairs(y0_ref[rs])
             + g1[:, r:r + 1] * _unpack_pairs(y1_ref[rs]))
        o_ref[rs] = _rms(h, nfin_ref[...])


def _combine(y01, h2, gates, norm_final, out_prev, row0, total_rows):
    T, D = h2.shape
    n_t = T // MOVE_ROWS
    in_specs = [
        pl.BlockSpec((MOVE_ROWS, D // 2), lambda i: (i, 0)),
        pl.BlockSpec((MOVE_ROWS, D // 2), lambda i: (i + n_t, 0)),
        pl.BlockSpec((MOVE_ROWS, D), lambda i: (i, 0)),
        pl.BlockSpec((2, MOVE_ROWS // LANES, LANES), lambda i: (0, i, 0)),
        pl.BlockSpec((1, D), lambda i: (0, 0)),
    ]
    args = [y01, y01, h2, gates, norm_final]
    aliases = {}
    if out_prev is not None:
        in_specs.append(pl.BlockSpec(memory_space=pl.ANY))
        args.append(out_prev)
        aliases = {len(args) - 1: 0}
    return pl.pallas_call(
        _combine_kernel,
        grid=(n_t,),
        in_specs=in_specs,
        out_specs=pl.BlockSpec((MOVE_ROWS, D), lambda i: (i + row0 // MOVE_ROWS, 0)),
        out_shape=jax.ShapeDtypeStruct((total_rows, D), F32),
        input_output_aliases=aliases,
        compiler_params=pltpu.CompilerParams(
            dimension_semantics=("arbitrary",), vmem_limit_bytes=VMEM_LIMIT),
        name="combine",
    )(*args)


def kernel(x, mem, norm_mix, w_in, gmlp_ln, gmlp_w_spatial, gmlp_b_spatial, gmlp_beta, hgrn_lb_logits, hgrn_out_gain, w_out, norm_xattn, norm_mem, w_xq, w_xkv, w_xo, norm_ffn, w_router_group, b_router_group, w_router_expert, b_router_expert, w_expert_gate, w_expert_up, w_expert_down, norm_final):
    B, S, D = x.shape
    T = B * S
    depth = w_in.shape[0]
    assert depth == 1 and hgrn_lb_logits.shape[0] == 2
    assert D == D_MODEL and mem.shape[1] == N_MEM and w_in.shape[2] == D_IN_PROJ
    assert S % MIX_ROWS == 0 and S % ATT_ROWS == 0 and B % MEMKV_BATCHES == 0
    l = 0
    row = lambda p: p.reshape(1, -1)

    h1 = _mixer(x, row(norm_mix[l]), w_in[l], row(gmlp_ln[l]), gmlp_w_spatial[l],
                gmlp_b_spatial[l].T, row(gmlp_beta[l]), hgrn_lb_logits, row(hgrn_out_gain[l]),
                w_out[l])
    k_mem, v_mem = _memkv(mem, row(norm_mem[l]), w_xkv[l])

    w_router = jnp.concatenate([w_router_group[l].T, w_router_expert[l].T], axis=0)
    w_router = jnp.pad(w_router, ((0, ROUTER_ROWS - w_router.shape[0]), (0, 0)))
    w_router = w_router.astype(BF16)
    bias = jnp.concatenate([b_router_group[l], b_router_expert[l]]).astype(F32)

    assert sum(PART_BATCHES) == B
    out = None
    b0 = 0
    for b_part in PART_BATCHES:
        t_part = b_part * S
        n_blocks = (2 * t_part) // EXPERT_ROWS + N_EXPERTS
        h2, xn, logits = _xattn(h1, row(norm_xattn[l]), w_xq[l], k_mem, v_mem, w_xo[l], row(norm_ffn[l]),
                                w_router, b0, b_part)
        ids, gates, rank, counts = _route(bias, logits)

        counts = counts[::ROUTE_SUB, 0].astype(jnp.int32)
        padded = (counts + EXPERT_ROWS - 1) // EXPERT_ROWS * EXPERT_ROWS
        seg_end = jnp.cumsum(padded)
        seg_start = seg_end - padded
        block_first_row = jnp.arange(n_blocks, dtype=jnp.int32) * EXPERT_ROWS
        block_expert = jnp.minimum(
            jnp.sum(block_first_row[:, None] >= seg_end[None, :], axis=1), N_EXPERTS - 1).astype(jnp.int32)
        n_used = (seg_end[-1:] // EXPERT_ROWS).astype(jnp.int32)
        present = counts > 0
        expert_ids = jnp.arange(N_EXPERTS, dtype=jnp.int32)
        expert_slot = (jnp.cumsum(present.astype(jnp.int32)) - 1) % 2
        later = jnp.where(present[None, :] & (expert_ids[None, :] > expert_ids[:, None]),
                          expert_ids[None, :], N_EXPERTS)
        expert_next = jnp.min(later, axis=1)
        expert_next = jnp.where(expert_next == N_EXPERTS, -1, expert_next).astype(jnp.int32)
        of_block = (block_expert[:, None] == expert_ids[None, :]).astype(jnp.int32)
        block_slot = jnp.sum(of_block * expert_slot[None, :], axis=1).astype(jnp.int32)
        block_next = jnp.sum(of_block * expert_next[None, :], axis=1).astype(jnp.int32)

        dest = _dest(seg_start, ids, rank).reshape(2, t_part)
        xs = _dispatch(dest[0], dest[1], xn, n_blocks * EXPERT_ROWS)
        yb = _experts(block_expert, n_used, block_slot, block_next, xs,
                      w_expert_gate[l], w_expert_up[l], w_expert_down[l])
        y01 = _gather_rows(yb, dest.reshape(2 * t_part))
        out = _combine(y01, h2.reshape(t_part, D), gates, row(norm_final), out, b0 * S, T)
        b0 += b_part
    return out.reshape(B, S, D)
```

```python
import jax
import jax.numpy as jnp
from jax import lax
from jax.experimental import pallas as pl
from jax.experimental.pallas import tpu as pltpu
from jax.experimental.pallas import tpu_sc as plsc

F32 = jnp.float32
BF16 = jnp.bfloat16
EPS = 1e-6

D_MODEL = 1024
D_GMLP = 512
GMLP_GROUPS = 4
GMLP_CHUNK = 128
D_HGRN = 512
HGRN_HEADS = 4
HGRN_DK = 128
HGRN_CHUNK = 64
D_IN_PROJ = 2 * D_GMLP + 4 * D_HGRN
N_MEM = 256
XATTN_HEADS = 4
XATTN_HEAD_DIM = D_MODEL // XATTN_HEADS
N_GROUPS = 4
EXPERTS_PER_GROUP = 8
N_EXPERTS = N_GROUPS * EXPERTS_PER_GROUP
D_EXPERT = 512

LANES = 128
MIX_ROWS = 1024
MIX_CHAIN_ROWS = 256


def _mix_schedule(n_chains):
    order = [("in", 0), ("gmlp", 0), ("factors", 0)]
    for k in range(n_chains):
        more = k + 1 < n_chains
        order += [("in", k + 1)] * more + [("local", k), ("recurrence", k)] + [("gmlp", k + 1)] * more
        order += [("out", k)] + [("factors", k + 1)] * more
    return tuple(order)


MIX_SCHEDULE = _mix_schedule(MIX_ROWS // MIX_CHAIN_ROWS)
MEMKV_BATCHES = 4
ATT_ROWS = 1024
WEIGHT_STAGE_ROWS = 128
ROUTER_ROWS = 40
ROUTE_SUB = 16
EXPERT_ROWS = 512
EXPERT_SUB_ROWS = 256
MOVE_ROWS = 1024
PART_BATCHES = (16, 16)
SC_WINDOW = 64
SC_CORES = 2
SC_SUBCORES = 16
_SC_WORKERS = SC_CORES * SC_SUBCORES
VMEM_LIMIT = 48 * 1024 * 1024


def _rms(x, gain):
    return x * lax.rsqrt(jnp.mean(x * x, axis=-1, keepdims=True) + EPS) * gain


def _dot(a, b):
    return jnp.dot(a, b, preferred_element_type=F32)


def _dot_nt(a, b):
    return lax.dot_general(a, b, (((1,), (1,)), ((), ())), preferred_element_type=F32)


def _dot_tn(a, b):
    return lax.dot_general(a, b, (((0,), (0,)), ((), ())), preferred_element_type=F32)


def _gelu(x):
    return 0.5 * x * (1.0 + jnp.tanh(0.7978845608028654 * (x + 0.044715 * (x * x * x))))


def _sigmoid(x):
    return 1.0 / (1.0 + jnp.exp(-x))


def _block_id(idx, size):
    assert size & (size - 1) == 0
    return lax.shift_right_logical(idx, size.bit_length() - 1)


def _stage_weight(w_hbm, w_bf16, stage_ref, sem):
    rows = stage_ref.shape[0]
    for k in range(w_hbm.shape[0] // rows):
        copy = pltpu.make_async_copy(w_hbm.at[pl.ds(k * rows, rows)], stage_ref, sem)
        copy.start()
        copy.wait()
        w_bf16[k * rows:(k + 1) * rows, :] = stage_ref[...].astype(BF16)


def _weight_scratch(k, n):
    return [pltpu.VMEM((k, n), BF16), pltpu.VMEM((WEIGHT_STAGE_ROWS, n), F32)]


_HIGH_HALF = 0xFFFF0000


def _pack_pairs(x):
    c = x.shape[1] // 2
    bits = lax.bitcast_convert_type(x.astype(BF16).astype(F32), jnp.uint32)
    return (bits[:, c:] & jnp.uint32(_HIGH_HALF)) | lax.shift_right_logical(bits[:, :c], jnp.uint32(16))


def _unpack_pairs(w):
    lo = lax.bitcast_convert_type(lax.shift_left(w, jnp.uint32(16)), F32)
    hi = lax.bitcast_convert_type(w & jnp.uint32(_HIGH_HALF), F32)
    return jnp.concatenate([lo, hi], axis=1)


def _mixer_kernel(x_ref, nmix_ref, win_hbm, gln_ref, ws_ref, bst_ref, beta_ref, lbl_ref, og_ref,
                  wout_hbm, o_ref, proj_ref, ycat_ref, state_ref, win_ref, win_stage, wout_ref, wout_stage,
                  sem):
    @pl.when((pl.program_id(0) == 0) & (pl.program_id(1) == 0))
    def _():
        _stage_weight(win_hbm, win_ref, win_stage, sem)
        _stage_weight(wout_hbm, wout_ref, wout_stage, sem)

    @pl.when(pl.program_id(1) == 0)
    def _():
        state_ref[...] = jnp.zeros_like(state_ref)

    n = MIX_CHAIN_ROWS
    r_i = lax.broadcasted_iota(jnp.int32, (GMLP_CHUNK, GMLP_CHUNK), 0)
    c_i = lax.broadcasted_iota(jnp.int32, (GMLP_CHUNK, GMLP_CHUNK), 1)
    causal = c_i <= r_i
    w_tril = [jnp.where(causal, ws_ref[g], 0.0).astype(BF16) for g in range(GMLP_GROUPS)]
    lbl = lbl_ref[...]
    e_lb = jnp.exp(lbl - jnp.max(lbl, axis=0, keepdims=True))
    lb = e_lb[0:1] / jnp.sum(e_lb, axis=0, keepdims=True)
    rr = lax.broadcasted_iota(jnp.int32, (n, n), 0)
    cc = lax.broadcasted_iota(jnp.int32, (n, n), 1)
    tri = jnp.where((_block_id(rr, HGRN_CHUNK) == _block_id(cc, HGRN_CHUNK)) & (cc <= rr),
                    1.0, 0.0).astype(BF16)
    r64 = lax.broadcasted_iota(jnp.int32, (HGRN_CHUNK, HGRN_CHUNK), 0)
    c64 = lax.broadcasted_iota(jnp.int32, (HGRN_CHUNK, HGRN_CHUNK), 1)
    causal64 = c64 <= r64
    base = 2 * D_GMLP

    n_chunks = n // HGRN_CHUNK
    chains = range(x_ref.shape[1] // n)
    env = {ch: {} for ch in chains}

    def rows(ch):
        return slice(ch * n, (ch + 1) * n)

    def in_proj(ch):
        a = _rms(x_ref[0, rows(ch)], nmix_ref[...]).astype(BF16)
        proj_ref[rows(ch)] = _dot(a, win_ref[...])

    def gmlp(ch):
        u = _gelu(proj_ref[rows(ch), 0:D_GMLP])
        v = _gelu(proj_ref[rows(ch), D_GMLP:2 * D_GMLP])
        vc = v - jnp.mean(v, axis=-1, keepdims=True)
        vn = (vc * lax.rsqrt(jnp.mean(vc * vc, axis=-1, keepdims=True) + EPS) * gln_ref[...]).astype(BF16)
        z_rows = []
        for c in range(n // GMLP_CHUNK):
            z_cols = []
            for g in range(GMLP_GROUPS):
                vg = vn[c * GMLP_CHUNK:(c + 1) * GMLP_CHUNK, g * LANES:(g + 1) * LANES]
                z_cols.append(_dot(w_tril[g], vg) + bst_ref[:, g:g + 1])
            z_rows.append(jnp.concatenate(z_cols, axis=1))
        z = jnp.concatenate(z_rows, axis=0)
        ycat_ref[rows(ch), 0:D_GMLP] = _rms(u * z, beta_ref[...]).astype(BF16)

    def hgrn_factors(ch):
        e = env[ch]
        f = lb + (1.0 - lb) * _sigmoid(proj_ref[rows(ch), base + D_HGRN:base + 2 * D_HGRN])
        log_f = jnp.log(f)
        lf_hi = log_f.astype(BF16)
        lf_lo = (log_f - lf_hi.astype(F32)).astype(BF16)
        b_all = _dot(tri, lf_hi) + _dot(tri, lf_lo)
        bl_rows = [b_all[c * HGRN_CHUNK + HGRN_CHUNK - 1:(c + 1) * HGRN_CHUNK] for c in range(n_chunks)]
        bl_all = jnp.concatenate([jnp.broadcast_to(r, (HGRN_CHUNK, D_HGRN)) for r in bl_rows], axis=0)
        q_all = proj_ref[rows(ch), base:base + D_HGRN]
        k_all = 1.0 - f
        e["qd"] = (q_all * _sigmoid(q_all) * jnp.exp(b_all)).astype(BF16)
        e["ki"] = (k_all * jnp.exp(-b_all)).astype(BF16)
        e["kte"] = (k_all * jnp.exp(bl_all - b_all)).astype(BF16)
        e["v"] = proj_ref[rows(ch), base + 2 * D_HGRN:base + 3 * D_HGRN].astype(BF16)
        e["decay"] = [jnp.exp(r) for r in bl_rows]

    def hgrn_local(ch):
        e = env[ch]
        e["o_intra"], e["d_state"] = {}, {}
        for c in range(n_chunks):
            rs = slice(c * HGRN_CHUNK, (c + 1) * HGRN_CHUNK)
            for h in range(HGRN_HEADS):
                cs = slice(h * HGRN_DK, (h + 1) * HGRN_DK)
                scores = jnp.where(causal64, _dot_nt(e["qd"][rs, cs], e["ki"][rs, cs]), 0.0).astype(BF16)
                e["o_intra"][c, h] = _dot(scores, e["v"][rs, cs])
                e["d_state"][c, h] = _dot_tn(e["v"][rs, cs], e["kte"][rs, cs])

    def hgrn_recurrence(ch):
        e = env[ch]
        for c in range(n_chunks):
            rs = slice(c * HGRN_CHUNK, (c + 1) * HGRN_CHUNK)
            ps = slice(ch * n + c * HGRN_CHUNK, ch * n + (c + 1) * HGRN_CHUNK)
            g_c = proj_ref[ps, base + 3 * D_HGRN:base + 4 * D_HGRN]
            gate = og_ref[...] * (g_c * _sigmoid(g_c))
            for h in range(HGRN_HEADS):
                cs = slice(h * HGRN_DK, (h + 1) * HGRN_DK)
                st = state_ref[h]
                o = e["o_intra"][c, h] + _dot_nt(e["qd"][rs, cs], st.astype(BF16))
                state_ref[h] = st * e["decay"][c][:, cs] + e["d_state"][c, h]
                o = o * lax.rsqrt(jnp.mean(o * o, axis=-1, keepdims=True) + EPS)
                ycat_ref[ps, D_GMLP + h * HGRN_DK:D_GMLP + (h + 1) * HGRN_DK] = (o * gate[:, cs]).astype(BF16)

    def out_proj(ch):
        o_ref[0, rows(ch)] = x_ref[0, rows(ch)] + _dot(ycat_ref[rows(ch)], wout_ref[...])

    stages = {"in": in_proj, "gmlp": gmlp, "factors": hgrn_factors, "local": hgrn_local,
              "recurrence": hgrn_recurrence, "out": out_proj}
    for stage, ch in MIX_SCHEDULE:
        stages[stage](ch)


def _mixer(x, norm_mix, w_in, gmlp_ln, w_s, b_s_t, beta, lb_logits, out_gain, w_out):
    B, S, D = x.shape
    const2 = lambda b, s: (0, 0)
    return pl.pallas_call(
        _mixer_kernel,
        grid=(B, S // MIX_ROWS),
        in_specs=[
            pl.BlockSpec((1, MIX_ROWS, D), lambda b, s: (b, s, 0)),
            pl.BlockSpec((1, D), const2),
            pl.BlockSpec(memory_space=pl.ANY),
            pl.BlockSpec((1, D_GMLP), const2),
            pl.BlockSpec((GMLP_GROUPS, GMLP_CHUNK, GMLP_CHUNK), lambda b, s: (0, 0, 0)),
            pl.BlockSpec((GMLP_CHUNK, GMLP_GROUPS), const2),
            pl.BlockSpec((1, D_GMLP), const2),
            pl.BlockSpec(lb_logits.shape, const2),
            pl.BlockSpec((1, D_HGRN), const2),
            pl.BlockSpec(memory_space=pl.ANY),
        ],
        out_specs=pl.BlockSpec((1, MIX_ROWS, D), lambda b, s: (b, s, 0)),
        out_shape=jax.ShapeDtypeStruct((B, S, D), F32),
        scratch_shapes=[
            pltpu.VMEM((MIX_ROWS, D_IN_PROJ), F32),
            pltpu.VMEM((MIX_ROWS, D), BF16),
            pltpu.VMEM((HGRN_HEADS, HGRN_DK, HGRN_DK), F32),
        ] + _weight_scratch(D, D_IN_PROJ) + _weight_scratch(D, D) + [pltpu.SemaphoreType.DMA],
        compiler_params=pltpu.CompilerParams(
            dimension_semantics=("arbitrary", "arbitrary"), vmem_limit_bytes=VMEM_LIMIT),
        name="mixer",
    )(x, norm_mix, w_in, gmlp_ln, w_s, b_s_t, beta, lb_logits, out_gain, w_out)


def _memkv_kernel(mem_ref, nm_ref, wkv_hbm, k_ref, v_ref, wkv_ref, wkv_stage, sem):
    @pl.when(pl.program_id(0) == 0)
    def _():
        _stage_weight(wkv_hbm, wkv_ref, wkv_stage, sem)

    nb, M, D = mem_ref.shape
    m = _rms(mem_ref[...].reshape(nb * M, D), nm_ref[...]).astype(BF16)
    kv = _dot(m, wkv_ref[...])
    k_ref[...] = kv[:, :D_MODEL].astype(BF16).reshape(nb, M, D)
    v_ref[...] = kv[:, D_MODEL:].astype(BF16).reshape(nb, M, D)


def _memkv(mem, norm_mem, w_kv):
    B, M, D = mem.shape
    out = jax.ShapeDtypeStruct((B, M, D), BF16)
    return pl.pallas_call(
        _memkv_kernel,
        grid=(B // MEMKV_BATCHES,),
        in_specs=[
            pl.BlockSpec((MEMKV_BATCHES, M, D), lambda b: (b, 0, 0)),
            pl.BlockSpec((1, D), lambda b: (0, 0)),
            pl.BlockSpec(memory_space=pl.ANY),
        ],
        out_specs=[pl.BlockSpec((MEMKV_BATCHES, M, D), lambda b: (b, 0, 0))] * 2,
        out_shape=[out, out],
        scratch_shapes=_weight_scratch(D, 2 * D) + [pltpu.SemaphoreType.DMA],
        compiler_params=pltpu.CompilerParams(
            dimension_semantics=("arbitrary",), vmem_limit_bytes=VMEM_LIMIT),
        name="memkv",
    )(mem, norm_mem, w_kv)


def _xattn_kernel(h_ref, nx_ref, wq_hbm, k_ref, v_ref, wo_hbm, nf_ref, wr_ref,
                  h2_ref, xn_ref, lg_ref, att_ref, wq_ref, wq_stage, wo_ref, wo_stage, sem):
    @pl.when((pl.program_id(0) == 0) & (pl.program_id(1) == 0))
    def _():
        _stage_weight(wq_hbm, wq_ref, wq_stage, sem)
        _stage_weight(wo_hbm, wo_ref, wo_stage, sem)

    h = h_ref[0]
    hn = _rms(h, nx_ref[...]).astype(BF16)
    q = (_dot(hn, wq_ref[...]) * (XATTN_HEAD_DIM ** -0.5)).astype(BF16)
    heads = [slice(hd * XATTN_HEAD_DIM, (hd + 1) * XATTN_HEAD_DIM) for hd in range(XATTN_HEADS)]
    scores = [_dot_nt(q[:, cs], k_ref[0, :, cs]) for cs in heads]
    probs = []
    for s in scores:
        p = jnp.exp(s - jnp.max(s, axis=-1, keepdims=True))
        probs.append((p / jnp.sum(p, axis=-1, keepdims=True)).astype(BF16))
    for cs, p in zip(heads, probs):
        att_ref[:, cs] = _dot(p, v_ref[0, :, cs]).astype(BF16)
    h2 = h + _dot(att_ref[...], wo_ref[...])
    h2_ref[0] = h2.astype(h2_ref.dtype)
    xn = _rms(h2, nf_ref[...])
    xn_ref[...] = _pack_pairs(xn)
    lg = _dot_nt(wr_ref[...], xn.astype(BF16))
    for j in range(ATT_ROWS // LANES):
        lg_ref[:, j, :] = lg[:, j * LANES:(j + 1) * LANES]


def _xattn(h1, norm_x, w_q, k_mem, v_mem, w_o, norm_ffn, w_router, batch0, batches):
    _, S, D = h1.shape
    n_s = S // ATT_ROWS
    const2 = lambda b, s: (0, 0)
    return pl.pallas_call(
        _xattn_kernel,
        grid=(batches, n_s),
        in_specs=[
            pl.BlockSpec((1, ATT_ROWS, D), lambda b, s: (b + batch0, s, 0)),
            pl.BlockSpec((1, D), const2),
            pl.BlockSpec(memory_space=pl.ANY),
            pl.BlockSpec((1, N_MEM, D), lambda b, s: (b + batch0, 0, 0)),
            pl.BlockSpec((1, N_MEM, D), lambda b, s: (b + batch0, 0, 0)),
            pl.BlockSpec(memory_space=pl.ANY),
            pl.BlockSpec((1, D), const2),
            pl.BlockSpec((ROUTER_ROWS, D), const2),
        ],
        out_specs=[
            pl.BlockSpec((1, ATT_ROWS, D), lambda b, s: (b, s, 0)),
            pl.BlockSpec((ATT_ROWS, D // 2), lambda b, s: (b * n_s + s, 0)),
            pl.BlockSpec((ROUTER_ROWS, ATT_ROWS // LANES, LANES), lambda b, s: (0, b * n_s + s, 0)),
        ],
        out_shape=[
            jax.ShapeDtypeStruct((batches, S, D), BF16),
            jax.ShapeDtypeStruct((batches * S, D // 2), jnp.uint32),
            jax.ShapeDtypeStruct((ROUTER_ROWS, batches * S // LANES, LANES), F32),
        ],
        scratch_shapes=([pltpu.VMEM((ATT_ROWS, D), BF16)] + _weight_scratch(D, D) + _weight_scratch(D, D)
                        + [pltpu.SemaphoreType.DMA]),
        compiler_params=pltpu.CompilerParams(
            dimension_semantics=("arbitrary", "arbitrary"), vmem_limit_bytes=VMEM_LIMIT),
        name="xattn",
    )(h1, norm_x, w_q, k_mem, v_mem, w_o, norm_ffn, w_router)


def _route_kernel(bias_ref, lg_ref, ids_ref, gates_ref, rank_ref, cnt_ref, base_ref):
    sub = lg_ref.shape[1]

    @pl.when(pl.program_id(0) == 0)
    def _():
        base_ref[...] = jnp.zeros_like(base_ref)

    best = lg_ref[0] + bias_ref[0]
    gl = [best]
    sel = jnp.zeros(best.shape, jnp.int32)
    for g in range(1, N_GROUPS):
        cur = lg_ref[g] + bias_ref[g]
        gl.append(cur)
        better = cur > best
        best = jnp.where(better, cur, best)
        sel = jnp.where(better, g, sel)
    denom = jnp.exp(gl[0] - best)
    for g in range(1, N_GROUPS):
        denom = denom + jnp.exp(gl[g] - best)
    g_gate = 1.0 / denom

    ev = []
    for j in range(EXPERTS_PER_GROUP):
        val = lg_ref[N_GROUPS + j] + bias_ref[N_GROUPS + j]
        for g in range(1, N_GROUPS):
            e = g * EXPERTS_PER_GROUP + j
            val = jnp.where(sel == g, lg_ref[N_GROUPS + e] + bias_ref[N_GROUPS + e], val)
        ev.append(val)
    v1, i1 = ev[0], jnp.zeros(best.shape, jnp.int32)
    for j in range(1, EXPERTS_PER_GROUP):
        better = ev[j] > v1
        v1 = jnp.where(better, ev[j], v1)
        i1 = jnp.where(better, j, i1)
    rest = [jnp.where(i1 == j, -jnp.inf, ev[j]) for j in range(EXPERTS_PER_GROUP)]
    v2, i2 = rest[0], jnp.zeros(best.shape, jnp.int32)
    for j in range(1, EXPERTS_PER_GROUP):
        better = rest[j] > v2
        v2 = jnp.where(better, rest[j], v2)
        i2 = jnp.where(better, j, i2)
    e2 = jnp.exp(v2 - v1)
    inv = 1.0 / (1.0 + e2)
    id1 = sel * EXPERTS_PER_GROUP + i1
    id2 = sel * EXPERTS_PER_GROUP + i2
    ids_ref[0] = id1
    ids_ref[1] = id2
    gates_ref[0] = inv * g_gate
    gates_ref[1] = e2 * inv * g_gate

    member = jnp.concatenate(
        [jnp.where((id1 == e) | (id2 == e), 1.0, 0.0) for e in range(N_EXPERTS)], axis=0).astype(BF16)
    n = N_EXPERTS * sub
    li = lax.broadcasted_iota(jnp.int32, (LANES, LANES), 0)
    lj = lax.broadcasted_iota(jnp.int32, (LANES, LANES), 1)
    before_lane = jnp.where(li < lj, 1.0, 0.0).astype(BF16)
    ones = jnp.ones((LANES, LANES), BF16)
    ri = lax.broadcasted_iota(jnp.int32, (n, n), 0)
    rj = lax.broadcasted_iota(jnp.int32, (n, n), 1)
    same = _block_id(ri, sub) == _block_id(rj, sub)
    before_row = jnp.where(same & (rj < ri), 1.0, 0.0).astype(BF16)
    all_row = jnp.where(same, 1.0, 0.0).astype(BF16)
    in_row = _dot(member, before_lane)
    prev_rows = _dot(_dot(before_row, member).astype(BF16), ones)
    total = _dot(_dot(all_row, member).astype(BF16), ones)
    base = base_ref[...]
    pos = base + prev_rows + in_row
    r1 = jnp.zeros(best.shape, F32)
    r2 = jnp.zeros(best.shape, F32)
    for e in range(N_EXPERTS):
        pe = pos[e * sub:(e + 1) * sub]
        r1 = jnp.where(id1 == e, pe, r1)
        r2 = jnp.where(id2 == e, pe, r2)
    rank_ref[0] = r1.astype(jnp.int32)
    rank_ref[1] = r2.astype(jnp.int32)
    base_ref[...] = base + total
    cnt_ref[...] = base + total


def _route(bias, logits3):
    rp, n_sub, _ = logits3.shape
    blk = lambda i: (0, i, 0)
    pair_i = jax.ShapeDtypeStruct((2, n_sub, LANES), jnp.int32)
    return pl.pallas_call(
        _route_kernel,
        grid=(n_sub // ROUTE_SUB,),
        in_specs=[
            pl.BlockSpec(memory_space=pltpu.SMEM),
            pl.BlockSpec((rp, ROUTE_SUB, LANES), blk),
        ],
        out_specs=[
            pl.BlockSpec((2, ROUTE_SUB, LANES), blk),
            pl.BlockSpec((2, ROUTE_SUB, LANES), blk),
            pl.BlockSpec((2, ROUTE_SUB, LANES), blk),
            pl.BlockSpec((N_EXPERTS * ROUTE_SUB, LANES), lambda i: (0, 0)),
        ],
        out_shape=[
            pair_i,
            jax.ShapeDtypeStruct((2, n_sub, LANES), F32),
            pair_i,
            jax.ShapeDtypeStruct((N_EXPERTS * ROUTE_SUB, LANES), F32),
        ],
        scratch_shapes=[pltpu.VMEM((N_EXPERTS * ROUTE_SUB, LANES), F32)],
        compiler_params=pltpu.CompilerParams(
            dimension_semantics=("arbitrary",), vmem_limit_bytes=VMEM_LIMIT),
        name="route",
    )(bias, logits3)


def _dest_kernel(start_ref, ids_ref, rank_ref, dest_ref):
    ids = ids_ref[...]
    off = jnp.zeros(ids.shape, jnp.int32)
    for e in range(N_EXPERTS):
        off = jnp.where(ids == e, start_ref[e], off)
    dest_ref[...] = rank_ref[...] + off


def _dest(seg_start, ids, rank):
    _, n_sub, _ = ids.shape
    blk = pl.BlockSpec((2, ROUTE_SUB, LANES), lambda i: (0, i, 0))
    return pl.pallas_call(
        _dest_kernel,
        grid=(n_sub // ROUTE_SUB,),
        in_specs=[pl.BlockSpec(memory_space=pltpu.SMEM), blk, blk],
        out_specs=blk,
        out_shape=jax.ShapeDtypeStruct(ids.shape, jnp.int32),
        name="dest",
    )(seg_start, ids, rank)


def _sc_mesh():
    return plsc.VectorSubcoreMesh(core_axis_name="core", subcore_axis_name="subcore")


def _sc_worker(rows_total):
    rows = rows_total // _SC_WORKERS
    wid = lax.axis_index("core") * SC_SUBCORES + lax.axis_index("subcore")
    return wid * rows, rows


def _dispatch(dest0, dest1, xn, n_slots):
    T, D = xn.shape
    W = SC_WINDOW
    rows = T // _SC_WORKERS
    assert T % (_SC_WORKERS * 2 * W) == 0

    @pl.kernel(out_type=jax.ShapeDtypeStruct((n_slots, D), xn.dtype), mesh=_sc_mesh(),
               scratch_types=[pltpu.VMEM((rows,), jnp.int32), pltpu.VMEM((rows,), jnp.int32),
                              pltpu.VMEM((2, W, D), xn.dtype),
                              pltpu.SemaphoreType.DMA((2,)), pltpu.SemaphoreType.DMA((2,))])
    def scatter_rows(x_hbm, d0_hbm, d1_hbm, xs_hbm, d0_v, d1_v, buf, in_sem, out_sem):
        base, _ = _sc_worker(T)
        pltpu.sync_copy(d0_hbm.at[pl.ds(base, rows)], d0_v)
        pltpu.sync_copy(d1_hbm.at[pl.ds(base, rows)], d1_v)

        def load(w, slot):
            return pltpu.make_async_copy(x_hbm.at[pl.ds(base + w * W, W)], buf.at[slot], in_sem.at[slot])

        def store(w, slot, d_v):
            return pltpu.make_async_copy(buf.at[slot], xs_hbm.at[d_v.at[pl.ds(w * W, W)]], out_sem.at[slot])

        def step(w, slot):
            load(w, slot).wait()
            store(w, slot, d0_v).start()
            store(w, slot, d1_v).start()
            store(w, slot, d0_v).wait()
            store(w, slot, d1_v).wait()

        n = rows // W
        load(0, 0).start()

        @pl.loop(0, n, step=2)
        def _(w):
            load(w + 1, 1).start()
            step(w, 0)

            @pl.when(w + 2 < n)
            def _():
                load(w + 2, 0).start()

            step(w + 1, 1)

    return scatter_rows(xn, dest0, dest1)


def _gather_rows(src, idx):
    M = idx.shape[0]
    D = src.shape[1]
    W = SC_WINDOW
    rows = M // _SC_WORKERS
    assert M % (_SC_WORKERS * 2 * W) == 0

    @pl.kernel(out_type=jax.ShapeDtypeStruct((M, D), src.dtype), mesh=_sc_mesh(),
               scratch_types=[pltpu.VMEM((rows,), jnp.int32), pltpu.VMEM((2, W, D), src.dtype),
                              pltpu.SemaphoreType.DMA((2,)), pltpu.SemaphoreType.DMA((2,))])
    def gather_rows(src_hbm, i_hbm, o_hbm, i_v, buf, in_sem, out_sem):
        base, _ = _sc_worker(M)
        pltpu.sync_copy(i_hbm.at[pl.ds(base, rows)], i_v)

        def load(w, slot):
            return pltpu.make_async_copy(src_hbm.at[i_v.at[pl.ds(w * W, W)]], buf.at[slot], in_sem.at[slot])

        def store(w, slot):
            return pltpu.make_async_copy(buf.at[slot], o_hbm.at[pl.ds(base + w * W, W)], out_sem.at[slot])

        n = rows // W
        load(0, 0).start()

        @pl.loop(0, n, step=2)
        def _(w):
            @pl.when(w > 0)
            def _():
                store(w - 1, 1).wait()

            load(w + 1, 1).start()
            load(w, 0).wait()
            store(w, 0).start()
            store(w, 0).wait()

            @pl.when(w + 2 < n)
            def _():
                load(w + 2, 0).start()

            load(w + 1, 1).wait()
            store(w + 1, 1).start()

        store(n - 1, 1).wait()

    return gather_rows(src, idx)


def _expert_kernel(be_ref, nb_ref, slot_ref, next_ref, rows_ref, x_ref, wg_hbm, wu_hbm, wd_hbm, y_ref,
                   wg_f32, wu_f32, wd_f32, wg_ref, wu_ref, wd_ref, sem):
    i = pl.program_id(0)
    used = i < nb_ref[0]
    new_expert = (i == 0) | (be_ref[i] != be_ref[jnp.maximum(i - 1, 0)])

    def fetch(e, slot):
        pairs = ((wg_hbm, wg_f32), (wu_hbm, wu_f32), (wd_hbm, wd_f32))
        return [pltpu.make_async_copy(w_hbm.at[e], w_f32.at[slot], sem.at[slot, k])
                for k, (w_hbm, w_f32) in enumerate(pairs)]

    @pl.when(used & (i == 0))
    def _():
        for copy in fetch(be_ref[0], slot_ref[0]):
            copy.start()

    @pl.when(used & new_expert)
    def _():
        slot = slot_ref[i]
        for copy in fetch(be_ref[i], slot):
            copy.wait()

        @pl.when(next_ref[i] >= 0)
        def _():
            for copy in fetch(next_ref[i], 1 - slot):
                copy.start()

        wg_ref[...] = wg_f32[slot].astype(BF16)
        wu_ref[...] = wu_f32[slot].astype(BF16)
        wd_ref[...] = wd_f32[slot].astype(BF16)

    def ffn(rs):
        x = _unpack_pairs(x_ref[rs]).astype(BF16)
        g = _dot(x, wg_ref[...])
        u = _dot(x, wu_ref[...])
        hid = (g * _sigmoid(g) * u).astype(BF16)
        y_ref[rs] = _pack_pairs(_dot(hid, wd_ref[...]))

    half_only = rows_ref[i] <= EXPERT_SUB_ROWS

    @pl.when(used & jnp.logical_not(half_only))
    def _():
        ffn(slice(0, EXPERT_ROWS))

    @pl.when(used & half_only)
    def _():
        ffn(slice(0, EXPERT_SUB_ROWS))
        y_ref[EXPERT_SUB_ROWS:] = jnp.zeros((EXPERT_ROWS - EXPERT_SUB_ROWS, y_ref.shape[1]), y_ref.dtype)

    @pl.when(jnp.logical_not(used))
    def _():
        y_ref[...] = jnp.zeros_like(y_ref)


def _experts(block_expert, n_used, block_slot, block_next, block_rows, xs, w_gate, w_up, w_down):
    n_slots, half = xs.shape
    D = 2 * half
    n_blocks = n_slots // EXPERT_ROWS
    grid_spec = pltpu.PrefetchScalarGridSpec(
        num_scalar_prefetch=5,
        grid=(n_blocks,),
        in_specs=[
            pl.BlockSpec((EXPERT_ROWS, half), lambda i, be, nb, sl, nx, br: (jnp.minimum(i, nb[0] - 1), 0)),
            pl.BlockSpec(memory_space=pl.ANY),
            pl.BlockSpec(memory_space=pl.ANY),
            pl.BlockSpec(memory_space=pl.ANY),
        ],
        out_specs=pl.BlockSpec((EXPERT_ROWS, half), lambda i, be, nb, sl, nx, br: (i, 0)),
        scratch_shapes=[
            pltpu.VMEM((2, D, D_EXPERT), F32), pltpu.VMEM((2, D, D_EXPERT), F32),
            pltpu.VMEM((2, D_EXPERT, D), F32),
            pltpu.VMEM((D, D_EXPERT), BF16), pltpu.VMEM((D, D_EXPERT), BF16), pltpu.VMEM((D_EXPERT, D), BF16),
            pltpu.SemaphoreType.DMA((2, 3)),
        ],
    )
    return pl.pallas_call(
        _expert_kernel,
        grid_spec=grid_spec,
        out_shape=jax.ShapeDtypeStruct((n_slots, half), jnp.uint32),
        compiler_params=pltpu.CompilerParams(
            dimension_semantics=("arbitrary",), vmem_limit_bytes=VMEM_LIMIT),
        name="experts",
    )(block_expert, n_used, block_slot, block_next, block_rows, xs, w_gate, w_up, w_down)


def _combine_kernel(y0_ref, y1_ref, h_ref, gates_ref, nfin_ref, *rest):
    o_ref = rest[-1]
    g0 = gates_ref[0].T
    g1 = gates_ref[1].T
    for r in range(gates_ref.shape[1]):
        rs = slice(r * LANES, (r + 1) * LANES)
        h = (h_ref[rs].astype(F32) + g0[:, r:r + 1] * _unpack_pairs(y0_ref[rs])
             + g1[:, r:r + 1] * _unpack_pairs(y1_ref[rs]))
        o_ref[rs] = _rms(h, nfin_ref[...])


def _combine(y01, h2, gates, norm_final, out_prev, row0, total_rows):
    T, D = h2.shape
    n_t = T // MOVE_ROWS
    in_specs = [
        pl.BlockSpec((MOVE_ROWS, D // 2), lambda i: (i, 0)),
        pl.BlockSpec((MOVE_ROWS, D // 2), lambda i: (i + n_t, 0)),
        pl.BlockSpec((MOVE_ROWS, D), lambda i: (i, 0)),
        pl.BlockSpec((2, MOVE_ROWS // LANES, LANES), lambda i: (0, i, 0)),
        pl.BlockSpec((1, D), lambda i: (0, 0)),
    ]
    args = [y01, y01, h2, gates, norm_final]
    aliases = {}
    if out_prev is not None:
        in_specs.append(pl.BlockSpec(memory_space=pl.ANY))
        args.append(out_prev)
        aliases = {len(args) - 1: 0}
    return pl.pallas_call(
        _combine_kernel,
        grid=(n_t,),
        in_specs=in_specs,
        out_specs=pl.BlockSpec((MOVE_ROWS, D), lambda i: (i + row0 // MOVE_ROWS, 0)),
        out_shape=jax.ShapeDtypeStruct((total_rows, D), F32),
        input_output_aliases=aliases,
        compiler_params=pltpu.CompilerParams(
            dimension_semantics=("arbitrary",), vmem_limit_bytes=VMEM_LIMIT),
        name="combine",
    )(*args)


def kernel(x, mem, norm_mix, w_in, gmlp_ln, gmlp_w_spatial, gmlp_b_spatial, gmlp_beta, hgrn_lb_logits, hgrn_out_gain, w_out, norm_xattn, norm_mem, w_xq, w_xkv, w_xo, norm_ffn, w_router_group, b_router_group, w_router_expert, b_router_expert, w_expert_gate, w_expert_up, w_expert_down, norm_final):
    B, S, D = x.shape
    T = B * S
    depth = w_in.shape[0]
    assert depth == 1 and hgrn_lb_logits.shape[0] == 2
    assert D == D_MODEL and mem.shape[1] == N_MEM and w_in.shape[2] == D_IN_PROJ
    assert S % MIX_ROWS == 0 and S % ATT_ROWS == 0 and B % MEMKV_BATCHES == 0
    l = 0
    row = lambda p: p.reshape(1, -1)

    h1 = _mixer(x, row(norm_mix[l]), w_in[l], row(gmlp_ln[l]), gmlp_w_spatial[l],
                gmlp_b_spatial[l].T, row(gmlp_beta[l]), hgrn_lb_logits, row(hgrn_out_gain[l]),
                w_out[l])
    k_mem, v_mem = _memkv(mem, row(norm_mem[l]), w_xkv[l])

    w_router = jnp.concatenate([w_router_group[l].T, w_router_expert[l].T], axis=0)
    w_router = jnp.pad(w_router, ((0, ROUTER_ROWS - w_router.shape[0]), (0, 0)))
    w_router = w_router.astype(BF16)
    bias = jnp.concatenate([b_router_group[l], b_router_expert[l]]).astype(F32)

    assert sum(PART_BATCHES) == B
    out = None
    b0 = 0
    for b_part in PART_BATCHES:
        t_part = b_part * S
        n_blocks = (2 * t_part) // EXPERT_ROWS + N_EXPERTS
        h2, xn, logits = _xattn(h1, row(norm_xattn[l]), w_xq[l], k_mem, v_mem, w_xo[l], row(norm_ffn[l]),
                                w_router, b0, b_part)
        ids, gates, rank, counts = _route(bias, logits)

        counts = counts[::ROUTE_SUB, 0].astype(jnp.int32)
        padded = (counts + EXPERT_ROWS - 1) // EXPERT_ROWS * EXPERT_ROWS
        seg_end = jnp.cumsum(padded)
        seg_start = seg_end - padded
        block_first_row = jnp.arange(n_blocks, dtype=jnp.int32) * EXPERT_ROWS
        block_expert = jnp.minimum(
            jnp.sum(block_first_row[:, None] >= seg_end[None, :], axis=1), N_EXPERTS - 1).astype(jnp.int32)
        n_used = (seg_end[-1:] // EXPERT_ROWS).astype(jnp.int32)
        present = counts > 0
        expert_ids = jnp.arange(N_EXPERTS, dtype=jnp.int32)
        expert_slot = (jnp.cumsum(present.astype(jnp.int32)) - 1) % 2
        later = jnp.where(present[None, :] & (expert_ids[None, :] > expert_ids[:, None]),
                          expert_ids[None, :], N_EXPERTS)
        expert_next = jnp.min(later, axis=1)
        expert_next = jnp.where(expert_next == N_EXPERTS, -1, expert_next).astype(jnp.int32)
        of_block = (block_expert[:, None] == expert_ids[None, :]).astype(jnp.int32)
        block_slot = jnp.sum(of_block * expert_slot[None, :], axis=1).astype(jnp.int32)
        block_next = jnp.sum(of_block * expert_next[None, :], axis=1).astype(jnp.int32)
        block_last_row = jnp.sum(of_block * (seg_start + counts)[None, :], axis=1)
        block_rows = jnp.clip(block_last_row - block_first_row, 0, EXPERT_ROWS).astype(jnp.int32)

        dest = _dest(seg_start, ids, rank).reshape(2, t_part)
        xs = _dispatch(dest[0], dest[1], xn, n_blocks * EXPERT_ROWS)
        yb = _experts(block_expert, n_used, block_slot, block_next, block_rows, xs,
                      w_expert_gate[l], w_expert_up[l], w_expert_down[l])
        y01 = _gather_rows(yb, dest.reshape(2 * t_part))
        out = _combine(y01, h2.reshape(t_part, D), gates, row(norm_final), out, b0 * S, T)
        b0 += b_part
    return out.reshape(B, S, D)
```

```python
import jax
import jax.numpy as jnp
from jax import lax
from jax.experimental import pallas as pl
from jax.experimental.pallas import tpu as pltpu
from jax.experimental.pallas import tpu_sc as plsc

F32 = jnp.float32
BF16 = jnp.bfloat16
EPS = 1e-6

D_MODEL = 1024
D_GMLP = 512
GMLP_GROUPS = 4
GMLP_CHUNK = 128
D_HGRN = 512
HGRN_HEADS = 4
HGRN_DK = 128
HGRN_CHUNK = 64
D_IN_PROJ = 2 * D_GMLP + 4 * D_HGRN
N_MEM = 256
XATTN_HEADS = 4
XATTN_HEAD_DIM = D_MODEL // XATTN_HEADS
N_GROUPS = 4
EXPERTS_PER_GROUP = 8
N_EXPERTS = N_GROUPS * EXPERTS_PER_GROUP
D_EXPERT = 512

LANES = 128
MIX_ROWS = 1024
MIX_CHAIN_ROWS = 256


def _mix_schedule(n_chains):
    order = [("in", 0), ("gmlp", 0), ("factors", 0)]
    for k in range(n_chains):
        more = k + 1 < n_chains
        order += [("in", k + 1)] * more + [("local", k), ("recurrence", k)] + [("gmlp", k + 1)] * more
        order += [("out", k)] + [("factors", k + 1)] * more
    return tuple(order)


MIX_SCHEDULE = _mix_schedule(MIX_ROWS // MIX_CHAIN_ROWS)
MEMKV_BATCHES = 2
ATT_ROWS = 1024
WEIGHT_STAGE_ROWS = 128
ROUTER_ROWS = 40
ROUTE_SUB = 16
EXPERT_ROWS = 512
MOVE_ROWS = 1024
PART_BATCHES = (16, 16)
SC_WINDOW = 64
SC_CORES = 2
SC_SUBCORES = 16
_SC_WORKERS = SC_CORES * SC_SUBCORES
VMEM_LIMIT = 48 * 1024 * 1024


def _rms(x, gain):
    return x * lax.rsqrt(jnp.mean(x * x, axis=-1, keepdims=True) + EPS) * gain


def _dot(a, b):
    return jnp.dot(a, b, preferred_element_type=F32)


def _dot_nt(a, b):
    return lax.dot_general(a, b, (((1,), (1,)), ((), ())), preferred_element_type=F32)


def _dot_tn(a, b):
    return lax.dot_general(a, b, (((0,), (0,)), ((), ())), preferred_element_type=F32)


def _gelu(x):
    return 0.5 * x * (1.0 + jnp.tanh(0.7978845608028654 * (x + 0.044715 * (x * x * x))))


def _sigmoid(x):
    return 1.0 / (1.0 + jnp.exp(-x))


def _block_id(idx, size):
    assert size & (size - 1) == 0
    return lax.shift_right_logical(idx, size.bit_length() - 1)


def _stage_weight(w_hbm, w_bf16, stage_ref, sem):
    rows = stage_ref.shape[0]
    for k in range(w_hbm.shape[0] // rows):
        copy = pltpu.make_async_copy(w_hbm.at[pl.ds(k * rows, rows)], stage_ref, sem)
        copy.start()
        copy.wait()
        w_bf16[k * rows:(k + 1) * rows, :] = stage_ref[...].astype(BF16)


def _weight_scratch(k, n):
    return [pltpu.VMEM((k, n), BF16), pltpu.VMEM((WEIGHT_STAGE_ROWS, n), F32)]


_HIGH_HALF = 0xFFFF0000


def _pack_pairs(x):
    c = x.shape[1] // 2
    bits = lax.bitcast_convert_type(x.astype(BF16).astype(F32), jnp.uint32)
    return (bits[:, c:] & jnp.uint32(_HIGH_HALF)) | lax.shift_right_logical(bits[:, :c], jnp.uint32(16))


def _unpack_pairs(w):
    lo = lax.bitcast_convert_type(lax.shift_left(w, jnp.uint32(16)), F32)
    hi = lax.bitcast_convert_type(w & jnp.uint32(_HIGH_HALF), F32)
    return jnp.concatenate([lo, hi], axis=1)


def _mixer_kernel(x_ref, nmix_ref, win_hbm, gln_ref, ws_ref, bst_ref, beta_ref, lbl_ref, og_ref,
                  wout_hbm, o_ref, proj_ref, ycat_ref, state_ref, win_ref, win_stage, wout_ref, wout_stage,
                  sem):
    @pl.when((pl.program_id(0) == 0) & (pl.program_id(1) == 0))
    def _():
        _stage_weight(win_hbm, win_ref, win_stage, sem)
        _stage_weight(wout_hbm, wout_ref, wout_stage, sem)

    @pl.when(pl.program_id(1) == 0)
    def _():
        state_ref[...] = jnp.zeros_like(state_ref)

    n = MIX_CHAIN_ROWS
    r_i = lax.broadcasted_iota(jnp.int32, (GMLP_CHUNK, GMLP_CHUNK), 0)
    c_i = lax.broadcasted_iota(jnp.int32, (GMLP_CHUNK, GMLP_CHUNK), 1)
    causal = c_i <= r_i
    w_tril = [jnp.where(causal, ws_ref[g], 0.0).astype(BF16) for g in range(GMLP_GROUPS)]
    lbl = lbl_ref[...]
    e_lb = jnp.exp(lbl - jnp.max(lbl, axis=0, keepdims=True))
    lb = e_lb[0:1] / jnp.sum(e_lb, axis=0, keepdims=True)
    rr = lax.broadcasted_iota(jnp.int32, (n, n), 0)
    cc = lax.broadcasted_iota(jnp.int32, (n, n), 1)
    tri = jnp.where((_block_id(rr, HGRN_CHUNK) == _block_id(cc, HGRN_CHUNK)) & (cc <= rr),
                    1.0, 0.0).astype(BF16)
    r64 = lax.broadcasted_iota(jnp.int32, (HGRN_CHUNK, HGRN_CHUNK), 0)
    c64 = lax.broadcasted_iota(jnp.int32, (HGRN_CHUNK, HGRN_CHUNK), 1)
    causal64 = c64 <= r64
    base = 2 * D_GMLP

    n_chunks = n // HGRN_CHUNK
    chains = range(x_ref.shape[1] // n)
    env = {ch: {} for ch in chains}

    def rows(ch):
        return slice(ch * n, (ch + 1) * n)

    def in_proj(ch):
        a = _rms(x_ref[0, rows(ch)], nmix_ref[...]).astype(BF16)
        proj_ref[rows(ch)] = _dot(a, win_ref[...])

    def gmlp(ch):
        u = _gelu(proj_ref[rows(ch), 0:D_GMLP])
        v = _gelu(proj_ref[rows(ch), D_GMLP:2 * D_GMLP])
        vc = v - jnp.mean(v, axis=-1, keepdims=True)
        vn = (vc * lax.rsqrt(jnp.mean(vc * vc, axis=-1, keepdims=True) + EPS) * gln_ref[...]).astype(BF16)
        z_rows = []
        for c in range(n // GMLP_CHUNK):
            z_cols = []
            for g in range(GMLP_GROUPS):
                vg = vn[c * GMLP_CHUNK:(c + 1) * GMLP_CHUNK, g * LANES:(g + 1) * LANES]
                z_cols.append(_dot(w_tril[g], vg) + bst_ref[:, g:g + 1])
            z_rows.append(jnp.concatenate(z_cols, axis=1))
        z = jnp.concatenate(z_rows, axis=0)
        ycat_ref[rows(ch), 0:D_GMLP] = _rms(u * z, beta_ref[...]).astype(BF16)

    def hgrn_factors(ch):
        e = env[ch]
        f = lb + (1.0 - lb) * _sigmoid(proj_ref[rows(ch), base + D_HGRN:base + 2 * D_HGRN])
        log_f = jnp.log(f)
        lf_hi = log_f.astype(BF16)
        lf_lo = (log_f - lf_hi.astype(F32)).astype(BF16)
        b_all = _dot(tri, lf_hi) + _dot(tri, lf_lo)
        bl_rows = [b_all[c * HGRN_CHUNK + HGRN_CHUNK - 1:(c + 1) * HGRN_CHUNK] for c in range(n_chunks)]
        bl_all = jnp.concatenate([jnp.broadcast_to(r, (HGRN_CHUNK, D_HGRN)) for r in bl_rows], axis=0)
        q_all = proj_ref[rows(ch), base:base + D_HGRN]
        k_all = 1.0 - f
        e["qd"] = (q_all * _sigmoid(q_all) * jnp.exp(b_all)).astype(BF16)
        e["ki"] = (k_all * jnp.exp(-b_all)).astype(BF16)
        e["kte"] = (k_all * jnp.exp(bl_all - b_all)).astype(BF16)
        e["v"] = proj_ref[rows(ch), base + 2 * D_HGRN:base + 3 * D_HGRN].astype(BF16)
        e["decay"] = [jnp.exp(r) for r in bl_rows]

    def hgrn_local(ch):
        e = env[ch]
        e["o_intra"], e["d_state"] = {}, {}
        for c in range(n_chunks):
            rs = slice(c * HGRN_CHUNK, (c + 1) * HGRN_CHUNK)
            for h in range(HGRN_HEADS):
                cs = slice(h * HGRN_DK, (h + 1) * HGRN_DK)
                scores = jnp.where(causal64, _dot_nt(e["qd"][rs, cs], e["ki"][rs, cs]), 0.0).astype(BF16)
                e["o_intra"][c, h] = _dot(scores, e["v"][rs, cs])
                e["d_state"][c, h] = _dot_tn(e["v"][rs, cs], e["kte"][rs, cs])

    def hgrn_recurrence(ch):
        e = env[ch]
        for c in range(n_chunks):
            rs = slice(c * HGRN_CHUNK, (c + 1) * HGRN_CHUNK)
            ps = slice(ch * n + c * HGRN_CHUNK, ch * n + (c + 1) * HGRN_CHUNK)
            g_c = proj_ref[ps, base + 3 * D_HGRN:base + 4 * D_HGRN]
            gate = og_ref[...] * (g_c * _sigmoid(g_c))
            for h in range(HGRN_HEADS):
                cs = slice(h * HGRN_DK, (h + 1) * HGRN_DK)
                st = state_ref[h]
                o = e["o_intra"][c, h] + _dot_nt(e["qd"][rs, cs], st.astype(BF16))
                state_ref[h] = st * e["decay"][c][:, cs] + e["d_state"][c, h]
                o = o * lax.rsqrt(jnp.mean(o * o, axis=-1, keepdims=True) + EPS)
                ycat_ref[ps, D_GMLP + h * HGRN_DK:D_GMLP + (h + 1) * HGRN_DK] = (o * gate[:, cs]).astype(BF16)

    def out_proj(ch):
        o_ref[0, rows(ch)] = x_ref[0, rows(ch)] + _dot(ycat_ref[rows(ch)], wout_ref[...])

    stages = {"in": in_proj, "gmlp": gmlp, "factors": hgrn_factors, "local": hgrn_local,
              "recurrence": hgrn_recurrence, "out": out_proj}
    for stage, ch in MIX_SCHEDULE:
        stages[stage](ch)


def _mixer(x, norm_mix, w_in, gmlp_ln, w_s, b_s_t, beta, lb_logits, out_gain, w_out):
    B, S, D = x.shape
    const2 = lambda b, s: (0, 0)
    return pl.pallas_call(
        _mixer_kernel,
        grid=(B, S // MIX_ROWS),
        in_specs=[
            pl.BlockSpec((1, MIX_ROWS, D), lambda b, s: (b, s, 0)),
            pl.BlockSpec((1, D), const2),
            pl.BlockSpec(memory_space=pl.ANY),
            pl.BlockSpec((1, D_GMLP), const2),
            pl.BlockSpec((GMLP_GROUPS, GMLP_CHUNK, GMLP_CHUNK), lambda b, s: (0, 0, 0)),
            pl.BlockSpec((GMLP_CHUNK, GMLP_GROUPS), const2),
            pl.BlockSpec((1, D_GMLP), const2),
            pl.BlockSpec(lb_logits.shape, const2),
            pl.BlockSpec((1, D_HGRN), const2),
            pl.BlockSpec(memory_space=pl.ANY),
        ],
        out_specs=pl.BlockSpec((1, MIX_ROWS, D), lambda b, s: (b, s, 0)),
        out_shape=jax.ShapeDtypeStruct((B, S, D), F32),
        scratch_shapes=[
            pltpu.VMEM((MIX_ROWS, D_IN_PROJ), F32),
            pltpu.VMEM((MIX_ROWS, D), BF16),
            pltpu.VMEM((HGRN_HEADS, HGRN_DK, HGRN_DK), F32),
        ] + _weight_scratch(D, D_IN_PROJ) + _weight_scratch(D, D) + [pltpu.SemaphoreType.DMA],
        compiler_params=pltpu.CompilerParams(
            dimension_semantics=("arbitrary", "arbitrary"), vmem_limit_bytes=VMEM_LIMIT),
        name="mixer",
    )(x, norm_mix, w_in, gmlp_ln, w_s, b_s_t, beta, lb_logits, out_gain, w_out)


def _memkv_kernel(mem_ref, nm_ref, wkv_hbm, wq_hbm, wo_hbm, qk_ref, vo_ref,
                  wkv_ref, wkv_stage, wq_ref, wq_stage, wo_ref, wo_stage, sem):
    @pl.when(pl.program_id(0) == 0)
    def _():
        _stage_weight(wkv_hbm, wkv_ref, wkv_stage, sem)
        _stage_weight(wq_hbm, wq_ref, wq_stage, sem)
        _stage_weight(wo_hbm, wo_ref, wo_stage, sem)

    nb, M, D = mem_ref.shape
    m = _rms(mem_ref[...].reshape(nb * M, D), nm_ref[...]).astype(BF16)
    kv = _dot(m, wkv_ref[...])
    k = kv[:, :D_MODEL].astype(BF16)
    v = kv[:, D_MODEL:].astype(BF16)
    for b in range(nb):
        rs = slice(b * M, (b + 1) * M)
        for hd in range(XATTN_HEADS):
            cs = slice(hd * XATTN_HEAD_DIM, (hd + 1) * XATTN_HEAD_DIM)
            ms = slice(hd * M, (hd + 1) * M)
            qk_ref[b, :, ms] = (_dot_nt(wq_ref[:, cs], k[rs, cs]) * (XATTN_HEAD_DIM ** -0.5)).astype(BF16)
            vo_ref[b, ms, :] = _dot(v[rs, cs], wo_ref[cs, :]).astype(BF16)


def _memkv(mem, norm_mem, w_kv, w_q, w_o):
    B, M, D = mem.shape
    out = jax.ShapeDtypeStruct((B, D, XATTN_HEADS * M), BF16), jax.ShapeDtypeStruct((B, XATTN_HEADS * M, D), BF16)
    return pl.pallas_call(
        _memkv_kernel,
        grid=(B // MEMKV_BATCHES,),
        in_specs=[
            pl.BlockSpec((MEMKV_BATCHES, M, D), lambda b: (b, 0, 0)),
            pl.BlockSpec((1, D), lambda b: (0, 0)),
            pl.BlockSpec(memory_space=pl.ANY),
            pl.BlockSpec(memory_space=pl.ANY),
            pl.BlockSpec(memory_space=pl.ANY),
        ],
        out_specs=[pl.BlockSpec((MEMKV_BATCHES, D, XATTN_HEADS * M), lambda b: (b, 0, 0)),
                   pl.BlockSpec((MEMKV_BATCHES, XATTN_HEADS * M, D), lambda b: (b, 0, 0))],
        out_shape=list(out),
        scratch_shapes=(_weight_scratch(D, 2 * D) + _weight_scratch(D, D) + _weight_scratch(D, D)
                        + [pltpu.SemaphoreType.DMA]),
        compiler_params=pltpu.CompilerParams(
            dimension_semantics=("arbitrary",), vmem_limit_bytes=VMEM_LIMIT),
        name="memkv",
    )(mem, norm_mem, w_kv, w_q, w_o)


def _xattn_kernel(h_ref, nx_ref, qk_ref, vo_ref, nf_ref, wr_ref, h2_ref, xn_ref, lg_ref, att_ref):
    h = h_ref[0]
    hn = _rms(h, nx_ref[...]).astype(BF16)
    scores = _dot(hn, qk_ref[0])
    n_mem = qk_ref.shape[2] // XATTN_HEADS
    for hd in range(XATTN_HEADS):
        ms = slice(hd * n_mem, (hd + 1) * n_mem)
        s = scores[:, ms]
        p = jnp.exp(s - jnp.max(s, axis=-1, keepdims=True))
        att_ref[:, ms] = (p / jnp.sum(p, axis=-1, keepdims=True)).astype(BF16)
    h2 = h + _dot(att_ref[...], vo_ref[0])
    h2_ref[0] = h2.astype(h2_ref.dtype)
    xn = _rms(h2, nf_ref[...])
    xn_ref[...] = _pack_pairs(xn)
    lg = _dot_nt(wr_ref[...], xn.astype(BF16))
    for j in range(ATT_ROWS // LANES):
        lg_ref[:, j, :] = lg[:, j * LANES:(j + 1) * LANES]


def _xattn(h1, norm_x, qk_mem, vo_mem, norm_ffn, w_router, batch0, batches):
    _, S, D = h1.shape
    n_s = S // ATT_ROWS
    n_att = qk_mem.shape[2]
    const2 = lambda b, s: (0, 0)
    return pl.pallas_call(
        _xattn_kernel,
        grid=(batches, n_s),
        in_specs=[
            pl.BlockSpec((1, ATT_ROWS, D), lambda b, s: (b + batch0, s, 0)),
            pl.BlockSpec((1, D), const2),
            pl.BlockSpec((1, D, n_att), lambda b, s: (b + batch0, 0, 0)),
            pl.BlockSpec((1, n_att, D), lambda b, s: (b + batch0, 0, 0)),
            pl.BlockSpec((1, D), const2),
            pl.BlockSpec((ROUTER_ROWS, D), const2),
        ],
        out_specs=[
            pl.BlockSpec((1, ATT_ROWS, D), lambda b, s: (b, s, 0)),
            pl.BlockSpec((ATT_ROWS, D // 2), lambda b, s: (b * n_s + s, 0)),
            pl.BlockSpec((ROUTER_ROWS, ATT_ROWS // LANES, LANES), lambda b, s: (0, b * n_s + s, 0)),
        ],
        out_shape=[
            jax.ShapeDtypeStruct((batches, S, D), BF16),
            jax.ShapeDtypeStruct((batches * S, D // 2), jnp.uint32),
            jax.ShapeDtypeStruct((ROUTER_ROWS, batches * S // LANES, LANES), F32),
        ],
        scratch_shapes=[pltpu.VMEM((ATT_ROWS, n_att), BF16)],
        compiler_params=pltpu.CompilerParams(
            dimension_semantics=("arbitrary", "arbitrary"), vmem_limit_bytes=VMEM_LIMIT),
        name="xattn",
    )(h1, norm_x, qk_mem, vo_mem, norm_ffn, w_router)


def _route_kernel(bias_ref, lg_ref, ids_ref, gates_ref, rank_ref, cnt_ref, base_ref):
    sub = lg_ref.shape[1]

    @pl.when(pl.program_id(0) == 0)
    def _():
        base_ref[...] = jnp.zeros_like(base_ref)

    best = lg_ref[0] + bias_ref[0]
    gl = [best]
    sel = jnp.zeros(best.shape, jnp.int32)
    for g in range(1, N_GROUPS):
        cur = lg_ref[g] + bias_ref[g]
        gl.append(cur)
        better = cur > best
        best = jnp.where(better, cur, best)
        sel = jnp.where(better, g, sel)
    denom = jnp.exp(gl[0] - best)
    for g in range(1, N_GROUPS):
        denom = denom + jnp.exp(gl[g] - best)
    g_gate = 1.0 / denom

    ev = []
    for j in range(EXPERTS_PER_GROUP):
        val = lg_ref[N_GROUPS + j] + bias_ref[N_GROUPS + j]
        for g in range(1, N_GROUPS):
            e = g * EXPERTS_PER_GROUP + j
            val = jnp.where(sel == g, lg_ref[N_GROUPS + e] + bias_ref[N_GROUPS + e], val)
        ev.append(val)
    v1, i1 = ev[0], jnp.zeros(best.shape, jnp.int32)
    for j in range(1, EXPERTS_PER_GROUP):
        better = ev[j] > v1
        v1 = jnp.where(better, ev[j], v1)
        i1 = jnp.where(better, j, i1)
    rest = [jnp.where(i1 == j, -jnp.inf, ev[j]) for j in range(EXPERTS_PER_GROUP)]
    v2, i2 = rest[0], jnp.zeros(best.shape, jnp.int32)
    for j in range(1, EXPERTS_PER_GROUP):
        better = rest[j] > v2
        v2 = jnp.where(better, rest[j], v2)
        i2 = jnp.where(better, j, i2)
    e2 = jnp.exp(v2 - v1)
    inv = 1.0 / (1.0 + e2)
    id1 = sel * EXPERTS_PER_GROUP + i1
    id2 = sel * EXPERTS_PER_GROUP + i2
    ids_ref[0] = id1
    ids_ref[1] = id2
    gates_ref[0] = inv * g_gate
    gates_ref[1] = e2 * inv * g_gate

    member = jnp.concatenate(
        [jnp.where((id1 == e) | (id2 == e), 1.0, 0.0) for e in range(N_EXPERTS)], axis=0).astype(BF16)
    n = N_EXPERTS * sub
    li = lax.broadcasted_iota(jnp.int32, (LANES, LANES), 0)
    lj = lax.broadcasted_iota(jnp.int32, (LANES, LANES), 1)
    before_lane = jnp.where(li < lj, 1.0, 0.0).astype(BF16)
    ones = jnp.ones((LANES, LANES), BF16)
    ri = lax.broadcasted_iota(jnp.int32, (n, n), 0)
    rj = lax.broadcasted_iota(jnp.int32, (n, n), 1)
    same = _block_id(ri, sub) == _block_id(rj, sub)
    before_row = jnp.where(same & (rj < ri), 1.0, 0.0).astype(BF16)
    all_row = jnp.where(same, 1.0, 0.0).astype(BF16)
    in_row = _dot(member, before_lane)
    prev_rows = _dot(_dot(before_row, member).astype(BF16), ones)
    total = _dot(_dot(all_row, member).astype(BF16), ones)
    base = base_ref[...]
    pos = base + prev_rows + in_row
    r1 = jnp.zeros(best.shape, F32)
    r2 = jnp.zeros(best.shape, F32)
    for e in range(N_EXPERTS):
        pe = pos[e * sub:(e + 1) * sub]
        r1 = jnp.where(id1 == e, pe, r1)
        r2 = jnp.where(id2 == e, pe, r2)
    rank_ref[0] = r1.astype(jnp.int32)
    rank_ref[1] = r2.astype(jnp.int32)
    base_ref[...] = base + total
    cnt_ref[...] = base + total


def _route(bias, logits3):
    rp, n_sub, _ = logits3.shape
    blk = lambda i: (0, i, 0)
    pair_i = jax.ShapeDtypeStruct((2, n_sub, LANES), jnp.int32)
    return pl.pallas_call(
        _route_kernel,
        grid=(n_sub // ROUTE_SUB,),
        in_specs=[
            pl.BlockSpec(memory_space=pltpu.SMEM),
            pl.BlockSpec((rp, ROUTE_SUB, LANES), blk),
        ],
        out_specs=[
            pl.BlockSpec((2, ROUTE_SUB, LANES), blk),
            pl.BlockSpec((2, ROUTE_SUB, LANES), blk),
            pl.BlockSpec((2, ROUTE_SUB, LANES), blk),
            pl.BlockSpec((N_EXPERTS * ROUTE_SUB, LANES), lambda i: (0, 0)),
        ],
        out_shape=[
            pair_i,
            jax.ShapeDtypeStruct((2, n_sub, LANES), F32),
            pair_i,
            jax.ShapeDtypeStruct((N_EXPERTS * ROUTE_SUB, LANES), F32),
        ],
        scratch_shapes=[pltpu.VMEM((N_EXPERTS * ROUTE_SUB, LANES), F32)],
        compiler_params=pltpu.CompilerParams(
            dimension_semantics=("arbitrary",), vmem_limit_bytes=VMEM_LIMIT),
        name="route",
    )(bias, logits3)


def _dest_kernel(start_ref, ids_ref, rank_ref, dest_ref):
    ids = ids_ref[...]
    off = jnp.zeros(ids.shape, jnp.int32)
    for e in range(N_EXPERTS):
        off = jnp.where(ids == e, start_ref[e], off)
    dest_ref[...] = rank_ref[...] + off


def _dest(seg_start, ids, rank):
    _, n_sub, _ = ids.shape
    blk = pl.BlockSpec((2, ROUTE_SUB, LANES), lambda i: (0, i, 0))
    return pl.pallas_call(
        _dest_kernel,
        grid=(n_sub // ROUTE_SUB,),
        in_specs=[pl.BlockSpec(memory_space=pltpu.SMEM), blk, blk],
        out_specs=blk,
        out_shape=jax.ShapeDtypeStruct(ids.shape, jnp.int32),
        name="dest",
    )(seg_start, ids, rank)


def _sc_mesh():
    return plsc.VectorSubcoreMesh(core_axis_name="core", subcore_axis_name="subcore")


def _sc_worker(rows_total):
    rows = rows_total // _SC_WORKERS
    wid = lax.axis_index("core") * SC_SUBCORES + lax.axis_index("subcore")
    return wid * rows, rows


def _dispatch(dest0, dest1, xn, n_slots):
    T, D = xn.shape
    W = SC_WINDOW
    rows = T // _SC_WORKERS
    assert T % (_SC_WORKERS * 2 * W) == 0

    @pl.kernel(out_type=jax.ShapeDtypeStruct((n_slots, D), xn.dtype), mesh=_sc_mesh(),
               scratch_types=[pltpu.VMEM((rows,), jnp.int32), pltpu.VMEM((rows,), jnp.int32),
                              pltpu.VMEM((2, W, D), xn.dtype),
                              pltpu.SemaphoreType.DMA((2,)), pltpu.SemaphoreType.DMA((2,))])
    def scatter_rows(x_hbm, d0_hbm, d1_hbm, xs_hbm, d0_v, d1_v, buf, in_sem, out_sem):
        base, _ = _sc_worker(T)
        pltpu.sync_copy(d0_hbm.at[pl.ds(base, rows)], d0_v)
        pltpu.sync_copy(d1_hbm.at[pl.ds(base, rows)], d1_v)

        def load(w, slot):
            return pltpu.make_async_copy(x_hbm.at[pl.ds(base + w * W, W)], buf.at[slot], in_sem.at[slot])

        def store(w, slot, d_v):
            return pltpu.make_async_copy(buf.at[slot], xs_hbm.at[d_v.at[pl.ds(w * W, W)]], out_sem.at[slot])

        def step(w, slot):
            load(w, slot).wait()
            store(w, slot, d0_v).start()
            store(w, slot, d1_v).start()
            store(w, slot, d0_v).wait()
            store(w, slot, d1_v).wait()

        n = rows // W
        load(0, 0).start()

        @pl.loop(0, n, step=2)
        def _(w):
            load(w + 1, 1).start()
            step(w, 0)

            @pl.when(w + 2 < n)
            def _():
                load(w + 2, 0).start()

            step(w + 1, 1)

    return scatter_rows(xn, dest0, dest1)


def _gather_rows(src, idx):
    M = idx.shape[0]
    D = src.shape[1]
    W = SC_WINDOW
    rows = M // _SC_WORKERS
    assert M % (_SC_WORKERS * 2 * W) == 0

    @pl.kernel(out_type=jax.ShapeDtypeStruct((M, D), src.dtype), mesh=_sc_mesh(),
               scratch_types=[pltpu.VMEM((rows,), jnp.int32), pltpu.VMEM((2, W, D), src.dtype),
                              pltpu.SemaphoreType.DMA((2,)), pltpu.SemaphoreType.DMA((2,))])
    def gather_rows(src_hbm, i_hbm, o_hbm, i_v, buf, in_sem, out_sem):
        base, _ = _sc_worker(M)
        pltpu.sync_copy(i_hbm.at[pl.ds(base, rows)], i_v)

        def load(w, slot):
            return pltpu.make_async_copy(src_hbm.at[i_v.at[pl.ds(w * W, W)]], buf.at[slot], in_sem.at[slot])

        def store(w, slot):
            return pltpu.make_async_copy(buf.at[slot], o_hbm.at[pl.ds(base + w * W, W)], out_sem.at[slot])

        n = rows // W
        load(0, 0).start()

        @pl.loop(0, n, step=2)
        def _(w):
            @pl.when(w > 0)
            def _():
                store(w - 1, 1).wait()

            load(w + 1, 1).start()
            load(w, 0).wait()
            store(w, 0).start()
            store(w, 0).wait()

            @pl.when(w + 2 < n)
            def _():
                load(w + 2, 0).start()

            load(w + 1, 1).wait()
            store(w + 1, 1).start()

        store(n - 1, 1).wait()

    return gather_rows(src, idx)


def _expert_kernel(be_ref, nb_ref, slot_ref, next_ref, x_ref, wg_hbm, wu_hbm, wd_hbm, y_ref,
                   wg_f32, wu_f32, wd_f32, wg_ref, wu_ref, wd_ref, sem):
    i = pl.program_id(0)
    used = i < nb_ref[0]
    new_expert = (i == 0) | (be_ref[i] != be_ref[jnp.maximum(i - 1, 0)])

    def fetch(e, slot):
        pairs = ((wg_hbm, wg_f32), (wu_hbm, wu_f32), (wd_hbm, wd_f32))
        return [pltpu.make_async_copy(w_hbm.at[e], w_f32.at[slot], sem.at[slot, k])
                for k, (w_hbm, w_f32) in enumerate(pairs)]

    @pl.when(used & (i == 0))
    def _():
        for copy in fetch(be_ref[0], slot_ref[0]):
            copy.start()

    @pl.when(used & new_expert)
    def _():
        slot = slot_ref[i]
        for copy in fetch(be_ref[i], slot):
            copy.wait()

        @pl.when(next_ref[i] >= 0)
        def _():
            for copy in fetch(next_ref[i], 1 - slot):
                copy.start()

        wg_ref[...] = wg_f32[slot].astype(BF16)
        wu_ref[...] = wu_f32[slot].astype(BF16)
        wd_ref[...] = wd_f32[slot].astype(BF16)

    @pl.when(used)
    def _():
        x = _unpack_pairs(x_ref[...]).astype(BF16)
        g = _dot(x, wg_ref[...])
        u = _dot(x, wu_ref[...])
        hid = (g * _sigmoid(g) * u).astype(BF16)
        y_ref[...] = _pack_pairs(_dot(hid, wd_ref[...]))

    @pl.when(jnp.logical_not(used))
    def _():
        y_ref[...] = jnp.zeros_like(y_ref)


def _experts(block_expert, n_used, block_slot, block_next, xs, w_gate, w_up, w_down):
    n_slots, half = xs.shape
    D = 2 * half
    n_blocks = n_slots // EXPERT_ROWS
    grid_spec = pltpu.PrefetchScalarGridSpec(
        num_scalar_prefetch=4,
        grid=(n_blocks,),
        in_specs=[
            pl.BlockSpec((EXPERT_ROWS, half), lambda i, be, nb, sl, nx: (jnp.minimum(i, nb[0] - 1), 0)),
            pl.BlockSpec(memory_space=pl.ANY),
            pl.BlockSpec(memory_space=pl.ANY),
            pl.BlockSpec(memory_space=pl.ANY),
        ],
        out_specs=pl.BlockSpec((EXPERT_ROWS, half), lambda i, be, nb, sl, nx: (i, 0)),
        scratch_shapes=[
            pltpu.VMEM((2, D, D_EXPERT), F32), pltpu.VMEM((2, D, D_EXPERT), F32),
            pltpu.VMEM((2, D_EXPERT, D), F32),
            pltpu.VMEM((D, D_EXPERT), BF16), pltpu.VMEM((D, D_EXPERT), BF16), pltpu.VMEM((D_EXPERT, D), BF16),
            pltpu.SemaphoreType.DMA((2, 3)),
        ],
    )
    return pl.pallas_call(
        _expert_kernel,
        grid_spec=grid_spec,
        out_shape=jax.ShapeDtypeStruct((n_slots, half), jnp.uint32),
        compiler_params=pltpu.CompilerParams(
            dimension_semantics=("arbitrary",), vmem_limit_bytes=VMEM_LIMIT),
        name="experts",
    )(block_expert, n_used, block_slot, block_next, xs, w_gate, w_up, w_down)


def _combine_kernel(y0_ref, y1_ref, h_ref, gates_ref, nfin_ref, *rest):
    o_ref = rest[-1]
    g0 = gates_ref[0].T
    g1 = gates_ref[1].T
    for r in range(gates_ref.shape[1]):
        rs = slice(r * LANES, (r + 1) * LANES)
        h = (h_ref[rs].astype(F32) + g0[:, r:r + 1] * _unpack_pairs(y0_ref[rs])
             + g1[:, r:r + 1] * _unpack_pairs(y1_ref[rs]))
        o_ref[rs] = _rms(h, nfin_ref[...])


def _combine(y01, h2, gates, norm_final, out_prev, row0, total_rows):
    T, D = h2.shape
    n_t = T // MOVE_ROWS
    in_specs = [
        pl.BlockSpec((MOVE_ROWS, D // 2), lambda i: (i, 0)),
        pl.BlockSpec((MOVE_ROWS, D // 2), lambda i: (i + n_t, 0)),
        pl.BlockSpec((MOVE_ROWS, D), lambda i: (i, 0)),
        pl.BlockSpec((2, MOVE_ROWS // LANES, LANES), lambda i: (0, i, 0)),
        pl.BlockSpec((1, D), lambda i: (0, 0)),
    ]
    args = [y01, y01, h2, gates, norm_final]
    aliases = {}
    if out_prev is not None:
        in_specs.append(pl.BlockSpec(memory_space=pl.ANY))
        args.append(out_prev)
        aliases = {len(args) - 1: 0}
    return pl.pallas_call(
        _combine_kernel,
        grid=(n_t,),
        in_specs=in_specs,
        out_specs=pl.BlockSpec((MOVE_ROWS, D), lambda i: (i + row0 // MOVE_ROWS, 0)),
        out_shape=jax.ShapeDtypeStruct((total_rows, D), F32),
        input_output_aliases=aliases,
        compiler_params=pltpu.CompilerParams(
            dimension_semantics=("arbitrary",), vmem_limit_bytes=VMEM_LIMIT),
        name="combine",
    )(*args)


def kernel(x, mem, norm_mix, w_in, gmlp_ln, gmlp_w_spatial, gmlp_b_spatial, gmlp_beta, hgrn_lb_logits, hgrn_out_gain, w_out, norm_xattn, norm_mem, w_xq, w_xkv, w_xo, norm_ffn, w_router_group, b_router_group, w_router_expert, b_router_expert, w_expert_gate, w_expert_up, w_expert_down, norm_final):
    B, S, D = x.shape
    T = B * S
    depth = w_in.shape[0]
    assert depth == 1 and hgrn_lb_logits.shape[0] == 2
    assert D == D_MODEL and mem.shape[1] == N_MEM and w_in.shape[2] == D_IN_PROJ
    assert S % MIX_ROWS == 0 and S % ATT_ROWS == 0 and B % MEMKV_BATCHES == 0
    l = 0
    row = lambda p: p.reshape(1, -1)

    h1 = _mixer(x, row(norm_mix[l]), w_in[l], row(gmlp_ln[l]), gmlp_w_spatial[l],
                gmlp_b_spatial[l].T, row(gmlp_beta[l]), hgrn_lb_logits, row(hgrn_out_gain[l]),
                w_out[l])
    qk_mem, vo_mem = _memkv(mem, row(norm_mem[l]), w_xkv[l], w_xq[l], w_xo[l])

    w_router = jnp.concatenate([w_router_group[l].T, w_router_expert[l].T], axis=0)
    w_router = jnp.pad(w_router, ((0, ROUTER_ROWS - w_router.shape[0]), (0, 0)))
    w_router = w_router.astype(BF16)
    bias = jnp.concatenate([b_router_group[l], b_router_expert[l]]).astype(F32)

    assert sum(PART_BATCHES) == B
    out = None
    b0 = 0
    for b_part in PART_BATCHES:
        t_part = b_part * S
        n_blocks = (2 * t_part) // EXPERT_ROWS + N_EXPERTS
        h2, xn, logits = _xattn(h1, row(norm_xattn[l]), qk_mem, vo_mem, row(norm_ffn[l]), w_router, b0, b_part)
        ids, gates, rank, counts = _route(bias, logits)

        counts = counts[::ROUTE_SUB, 0].astype(jnp.int32)
        padded = (counts + EXPERT_ROWS - 1) // EXPERT_ROWS * EXPERT_ROWS
        seg_end = jnp.cumsum(padded)
        seg_start = seg_end - padded
        block_first_row = jnp.arange(n_blocks, dtype=jnp.int32) * EXPERT_ROWS
        block_expert = jnp.minimum(
            jnp.sum(block_first_row[:, None] >= seg_end[None, :], axis=1), N_EXPERTS - 1).astype(jnp.int32)
        n_used = (seg_end[-1:] // EXPERT_ROWS).astype(jnp.int32)
        present = counts > 0
        expert_ids = jnp.arange(N_EXPERTS, dtype=jnp.int32)
        expert_slot = (jnp.cumsum(present.astype(jnp.int32)) - 1) % 2
        later = jnp.where(present[None, :] & (expert_ids[None, :] > expert_ids[:, None]),
                          expert_ids[None, :], N_EXPERTS)
        expert_next = jnp.min(later, axis=1)
        expert_next = jnp.where(expert_next == N_EXPERTS, -1, expert_next).astype(jnp.int32)
        of_block = (block_expert[:, None] == expert_ids[None, :]).astype(jnp.int32)
        block_slot = jnp.sum(of_block * expert_slot[None, :], axis=1).astype(jnp.int32)
        block_next = jnp.sum(of_block * expert_next[None, :], axis=1).astype(jnp.int32)

        dest = _dest(seg_start, ids, rank).reshape(2, t_part)
        xs = _dispatch(dest[0], dest[1], xn, n_blocks * EXPERT_ROWS)
        yb = _experts(block_expert, n_used, block_slot, block_next, xs,
                      w_expert_gate[l], w_expert_up[l], w_expert_down[l])
        y01 = _gather_rows(yb, dest.reshape(2 * t_part))
        out = _combine(y01, h2.reshape(t_part, D), gates, row(norm_final), out, b0 * S, T)
        b0 += b_part
    return out.reshape(B, S, D)
```

```python
import jax
import jax.numpy as jnp
from jax import lax
from jax.experimental import pallas as pl
from jax.experimental.pallas import tpu as pltpu
from jax.experimental.pallas import tpu_sc as plsc

F32 = jnp.float32
BF16 = jnp.bfloat16
EPS = 1e-6

D_MODEL = 1024
D_GMLP = 512
GMLP_GROUPS = 4
GMLP_CHUNK = 128
D_HGRN = 512
HGRN_HEADS = 4
HGRN_DK = 128
HGRN_CHUNK = 64
D_IN_PROJ = 2 * D_GMLP + 4 * D_HGRN
N_MEM = 256
XATTN_HEADS = 4
XATTN_HEAD_DIM = D_MODEL // XATTN_HEADS
N_GROUPS = 4
EXPERTS_PER_GROUP = 8
N_EXPERTS = N_GROUPS * EXPERTS_PER_GROUP
D_EXPERT = 512

LANES = 128
MIX_ROWS = 1024
MIX_CHAIN_ROWS = 256


def _mix_schedule(n_chains):
    order = [("in", 0), ("gmlp", 0), ("factors", 0)]
    for k in range(n_chains):
        more = k + 1 < n_chains
        order += [("in", k + 1)] * more + [("local", k), ("recurrence", k)] + [("gmlp", k + 1)] * more
        order += [("out", k)] + [("factors", k + 1)] * more
    return tuple(order)


MIX_SCHEDULE = _mix_schedule(MIX_ROWS // MIX_CHAIN_ROWS)
MEMKV_BATCHES = 2
ATT_ROWS = 1024
WEIGHT_STAGE_ROWS = 128
ROUTER_ROWS = 40
ROUTE_SUB = 16
EXPERT_ROWS = 512
MOVE_ROWS = 1024
PART_BATCHES = (16, 16)
SC_WINDOW = 64
SC_CORES = 2
SC_SUBCORES = 16
_SC_WORKERS = SC_CORES * SC_SUBCORES
VMEM_LIMIT = 48 * 1024 * 1024


def _rms(x, gain):
    return x * lax.rsqrt(jnp.mean(x * x, axis=-1, keepdims=True) + EPS) * gain


def _dot(a, b):
    return jnp.dot(a, b, preferred_element_type=F32)


def _dot_nt(a, b):
    return lax.dot_general(a, b, (((1,), (1,)), ((), ())), preferred_element_type=F32)


def _dot_tn(a, b):
    return lax.dot_general(a, b, (((0,), (0,)), ((), ())), preferred_element_type=F32)


def _gelu(x):
    return 0.5 * x * (1.0 + jnp.tanh(0.7978845608028654 * (x + 0.044715 * (x * x * x))))


def _sigmoid(x):
    return 1.0 / (1.0 + jnp.exp(-x))


def _block_id(idx, size):
    assert size & (size - 1) == 0
    return lax.shift_right_logical(idx, size.bit_length() - 1)


def _stage_weight(w_hbm, w_bf16, stage_ref, sem):
    rows = stage_ref.shape[0]
    for k in range(w_hbm.shape[0] // rows):
        copy = pltpu.make_async_copy(w_hbm.at[pl.ds(k * rows, rows)], stage_ref, sem)
        copy.start()
        copy.wait()
        w_bf16[k * rows:(k + 1) * rows, :] = stage_ref[...].astype(BF16)


def _weight_scratch(k, n):
    return [pltpu.VMEM((k, n), BF16), pltpu.VMEM((WEIGHT_STAGE_ROWS, n), F32)]


_HIGH_HALF = 0xFFFF0000


def _pack_pairs(x):
    c = x.shape[1] // 2
    bits = lax.bitcast_convert_type(x.astype(BF16).astype(F32), jnp.uint32)
    return (bits[:, c:] & jnp.uint32(_HIGH_HALF)) | lax.shift_right_logical(bits[:, :c], jnp.uint32(16))


def _unpack_pairs(w):
    lo = lax.bitcast_convert_type(lax.shift_left(w, jnp.uint32(16)), F32)
    hi = lax.bitcast_convert_type(w & jnp.uint32(_HIGH_HALF), F32)
    return jnp.concatenate([lo, hi], axis=1)


def _mixer_kernel(x_ref, nmix_ref, win_hbm, gln_ref, ws_ref, bst_ref, beta_ref, lbl_ref, og_ref,
                  wout_hbm, o_ref, proj_ref, ycat_ref, state_ref, win_ref, win_stage, wout_ref, wout_stage,
                  sem):
    @pl.when((pl.program_id(0) == 0) & (pl.program_id(1) == 0))
    def _():
        _stage_weight(win_hbm, win_ref, win_stage, sem)
        _stage_weight(wout_hbm, wout_ref, wout_stage, sem)

    @pl.when(pl.program_id(1) == 0)
    def _():
        state_ref[...] = jnp.zeros_like(state_ref)

    n = MIX_CHAIN_ROWS
    r_i = lax.broadcasted_iota(jnp.int32, (GMLP_CHUNK, GMLP_CHUNK), 0)
    c_i = lax.broadcasted_iota(jnp.int32, (GMLP_CHUNK, GMLP_CHUNK), 1)
    causal = c_i <= r_i
    w_tril = [jnp.where(causal, ws_ref[g], 0.0).astype(BF16) for g in range(GMLP_GROUPS)]
    lbl = lbl_ref[...]
    e_lb = jnp.exp(lbl - jnp.max(lbl, axis=0, keepdims=True))
    lb = e_lb[0:1] / jnp.sum(e_lb, axis=0, keepdims=True)
    rr = lax.broadcasted_iota(jnp.int32, (n, n), 0)
    cc = lax.broadcasted_iota(jnp.int32, (n, n), 1)
    tri = jnp.where((_block_id(rr, HGRN_CHUNK) == _block_id(cc, HGRN_CHUNK)) & (cc <= rr),
                    1.0, 0.0).astype(BF16)
    r64 = lax.broadcasted_iota(jnp.int32, (HGRN_CHUNK, HGRN_CHUNK), 0)
    c64 = lax.broadcasted_iota(jnp.int32, (HGRN_CHUNK, HGRN_CHUNK), 1)
    causal64 = c64 <= r64
    base = 2 * D_GMLP

    n_chunks = n // HGRN_CHUNK
    chains = range(x_ref.shape[1] // n)
    env = {ch: {} for ch in chains}

    def rows(ch):
        return slice(ch * n, (ch + 1) * n)

    def in_proj(ch):
        a = _rms(x_ref[0, rows(ch)], nmix_ref[...]).astype(BF16)
        proj_ref[rows(ch)] = _dot(a, win_ref[...])

    def gmlp(ch):
        u = _gelu(proj_ref[rows(ch), 0:D_GMLP])
        v = _gelu(proj_ref[rows(ch), D_GMLP:2 * D_GMLP])
        vc = v - jnp.mean(v, axis=-1, keepdims=True)
        vn = (vc * lax.rsqrt(jnp.mean(vc * vc, axis=-1, keepdims=True) + EPS) * gln_ref[...]).astype(BF16)
        z_rows = []
        for c in range(n // GMLP_CHUNK):
            z_cols = []
            for g in range(GMLP_GROUPS):
                vg = vn[c * GMLP_CHUNK:(c + 1) * GMLP_CHUNK, g * LANES:(g + 1) * LANES]
                z_cols.append(_dot(w_tril[g], vg) + bst_ref[:, g:g + 1])
            z_rows.append(jnp.concatenate(z_cols, axis=1))
        z = jnp.concatenate(z_rows, axis=0)
        ycat_ref[rows(ch), 0:D_GMLP] = _rms(u * z, beta_ref[...]).astype(BF16)

    def hgrn_factors(ch):
        e = env[ch]
        f = lb + (1.0 - lb) * _sigmoid(proj_ref[rows(ch), base + D_HGRN:base + 2 * D_HGRN])
        log_f = jnp.log(f)
        lf_hi = log_f.astype(BF16)
        lf_lo = (log_f - lf_hi.astype(F32)).astype(BF16)
        b_all = _dot(tri, lf_hi) + _dot(tri, lf_lo)
        bl_rows = [b_all[c * HGRN_CHUNK + HGRN_CHUNK - 1:(c + 1) * HGRN_CHUNK] for c in range(n_chunks)]
        bl_all = jnp.concatenate([jnp.broadcast_to(r, (HGRN_CHUNK, D_HGRN)) for r in bl_rows], axis=0)
        q_all = proj_ref[rows(ch), base:base + D_HGRN]
        k_all = 1.0 - f
        e["qd"] = (q_all * _sigmoid(q_all) * jnp.exp(b_all)).astype(BF16)
        e["ki"] = (k_all * jnp.exp(-b_all)).astype(BF16)
        e["kte"] = (k_all * jnp.exp(bl_all - b_all)).astype(BF16)
        e["v"] = proj_ref[rows(ch), base + 2 * D_HGRN:base + 3 * D_HGRN].astype(BF16)
        e["decay"] = [jnp.exp(r) for r in bl_rows]

    def hgrn_local(ch):
        e = env[ch]
        e["o_intra"], e["d_state"] = {}, {}
        for c in range(n_chunks):
            rs = slice(c * HGRN_CHUNK, (c + 1) * HGRN_CHUNK)
            for h in range(HGRN_HEADS):
                cs = slice(h * HGRN_DK, (h + 1) * HGRN_DK)
                scores = jnp.where(causal64, _dot_nt(e["qd"][rs, cs], e["ki"][rs, cs]), 0.0).astype(BF16)
                e["o_intra"][c, h] = _dot(scores, e["v"][rs, cs])
                e["d_state"][c, h] = _dot_tn(e["v"][rs, cs], e["kte"][rs, cs])

    def hgrn_recurrence(ch):
        e = env[ch]
        for c in range(n_chunks):
            rs = slice(c * HGRN_CHUNK, (c + 1) * HGRN_CHUNK)
            ps = slice(ch * n + c * HGRN_CHUNK, ch * n + (c + 1) * HGRN_CHUNK)
            g_c = proj_ref[ps, base + 3 * D_HGRN:base + 4 * D_HGRN]
            gate = og_ref[...] * (g_c * _sigmoid(g_c))
            for h in range(HGRN_HEADS):
                cs = slice(h * HGRN_DK, (h + 1) * HGRN_DK)
                st = state_ref[h]
                o = e["o_intra"][c, h] + _dot_nt(e["qd"][rs, cs], st.astype(BF16))
                state_ref[h] = st * e["decay"][c][:, cs] + e["d_state"][c, h]
                o = o * lax.rsqrt(jnp.mean(o * o, axis=-1, keepdims=True) + EPS)
                ycat_ref[ps, D_GMLP + h * HGRN_DK:D_GMLP + (h + 1) * HGRN_DK] = (o * gate[:, cs]).astype(BF16)

    def out_proj(ch):
        o_ref[0, rows(ch)] = x_ref[0, rows(ch)] + _dot(ycat_ref[rows(ch)], wout_ref[...])

    stages = {"in": in_proj, "gmlp": gmlp, "factors": hgrn_factors, "local": hgrn_local,
              "recurrence": hgrn_recurrence, "out": out_proj}
    for stage, ch in MIX_SCHEDULE:
        stages[stage](ch)


def _mixer(x, norm_mix, w_in, gmlp_ln, w_s, b_s_t, beta, lb_logits, out_gain, w_out):
    B, S, D = x.shape
    const2 = lambda b, s: (0, 0)
    return pl.pallas_call(
        _mixer_kernel,
        grid=(B, S // MIX_ROWS),
        in_specs=[
            pl.BlockSpec((1, MIX_ROWS, D), lambda b, s: (b, s, 0)),
            pl.BlockSpec((1, D), const2),
            pl.BlockSpec(memory_space=pl.ANY),
            pl.BlockSpec((1, D_GMLP), const2),
            pl.BlockSpec((GMLP_GROUPS, GMLP_CHUNK, GMLP_CHUNK), lambda b, s: (0, 0, 0)),
            pl.BlockSpec((GMLP_CHUNK, GMLP_GROUPS), const2),
            pl.BlockSpec((1, D_GMLP), const2),
            pl.BlockSpec(lb_logits.shape, const2),
            pl.BlockSpec((1, D_HGRN), const2),
            pl.BlockSpec(memory_space=pl.ANY),
        ],
        out_specs=pl.BlockSpec((1, MIX_ROWS, D), lambda b, s: (b, s, 0)),
        out_shape=jax.ShapeDtypeStruct((B, S, D), F32),
        scratch_shapes=[
            pltpu.VMEM((MIX_ROWS, D_IN_PROJ), F32),
            pltpu.VMEM((MIX_ROWS, D), BF16),
            pltpu.VMEM((HGRN_HEADS, HGRN_DK, HGRN_DK), F32),
        ] + _weight_scratch(D, D_IN_PROJ) + _weight_scratch(D, D) + [pltpu.SemaphoreType.DMA],
        compiler_params=pltpu.CompilerParams(
            dimension_semantics=("arbitrary", "arbitrary"), vmem_limit_bytes=VMEM_LIMIT),
        name="mixer",
    )(x, norm_mix, w_in, gmlp_ln, w_s, b_s_t, beta, lb_logits, out_gain, w_out)


def _memkv_kernel(mem_ref, nm_ref, wkv_hbm, wq_hbm, wo_hbm, qk_ref, vo_ref,
                  wkv_ref, wkv_stage, wq_ref, wq_stage, wo_ref, wo_stage, sem):
    @pl.when(pl.program_id(0) == 0)
    def _():
        _stage_weight(wkv_hbm, wkv_ref, wkv_stage, sem)
        _stage_weight(wq_hbm, wq_ref, wq_stage, sem)
        _stage_weight(wo_hbm, wo_ref, wo_stage, sem)

    nb, M, D = mem_ref.shape
    m = _rms(mem_ref[...].reshape(nb * M, D), nm_ref[...]).astype(BF16)
    kv = _dot(m, wkv_ref[...])
    k = kv[:, :D_MODEL].astype(BF16)
    v = kv[:, D_MODEL:].astype(BF16)
    for b in range(nb):
        rs = slice(b * M, (b + 1) * M)
        for hd in range(XATTN_HEADS):
            cs = slice(hd * XATTN_HEAD_DIM, (hd + 1) * XATTN_HEAD_DIM)
            ms = slice(hd * M, (hd + 1) * M)
            qk_ref[b, :, ms] = (_dot_nt(wq_ref[:, cs], k[rs, cs]) * (XATTN_HEAD_DIM ** -0.5)).astype(BF16)
            vo_ref[b, ms, :] = _dot(v[rs, cs], wo_ref[cs, :]).astype(BF16)


def _memkv(mem, norm_mem, w_kv, w_q, w_o):
    B, M, D = mem.shape
    out = jax.ShapeDtypeStruct((B, D, XATTN_HEADS * M), BF16), jax.ShapeDtypeStruct((B, XATTN_HEADS * M, D), BF16)
    return pl.pallas_call(
        _memkv_kernel,
        grid=(B // MEMKV_BATCHES,),
        in_specs=[
            pl.BlockSpec((MEMKV_BATCHES, M, D), lambda b: (b, 0, 0)),
            pl.BlockSpec((1, D), lambda b: (0, 0)),
            pl.BlockSpec(memory_space=pl.ANY),
            pl.BlockSpec(memory_space=pl.ANY),
            pl.BlockSpec(memory_space=pl.ANY),
        ],
        out_specs=[pl.BlockSpec((MEMKV_BATCHES, D, XATTN_HEADS * M), lambda b: (b, 0, 0)),
                   pl.BlockSpec((MEMKV_BATCHES, XATTN_HEADS * M, D), lambda b: (b, 0, 0))],
        out_shape=list(out),
        scratch_shapes=(_weight_scratch(D, 2 * D) + _weight_scratch(D, D) + _weight_scratch(D, D)
                        + [pltpu.SemaphoreType.DMA]),
        compiler_params=pltpu.CompilerParams(
            dimension_semantics=("arbitrary",), vmem_limit_bytes=VMEM_LIMIT),
        name="memkv",
    )(mem, norm_mem, w_kv, w_q, w_o)


def _xattn_kernel(h_ref, nx_ref, qk_ref, vo_ref, nf_ref, wr_ref, h2_ref, xn_ref, lg_ref, att_ref):
    h = h_ref[0]
    hn = _rms(h, nx_ref[...]).astype(BF16)
    scores = _dot(hn, qk_ref[0])
    n_mem = qk_ref.shape[2] // XATTN_HEADS
    for hd in range(XATTN_HEADS):
        ms = slice(hd * n_mem, (hd + 1) * n_mem)
        s = scores[:, ms]
        p = jnp.exp(s - jnp.max(s, axis=-1, keepdims=True))
        att_ref[:, ms] = (p / jnp.sum(p, axis=-1, keepdims=True)).astype(BF16)
    h2 = h + _dot(att_ref[...], vo_ref[0])
    h2_ref[0] = h2.astype(h2_ref.dtype)
    xn = _rms(h2, nf_ref[...])
    xn_ref[...] = _pack_pairs(xn)
    lg = _dot_nt(wr_ref[...], xn.astype(BF16))
    for j in range(ATT_ROWS // LANES):
        lg_ref[:, j, :] = lg[:, j * LANES:(j + 1) * LANES]


def _xattn(h1, norm_x, qk_mem, vo_mem, norm_ffn, w_router, batch0, batches):
    _, S, D = h1.shape
    n_s = S // ATT_ROWS
    n_att = qk_mem.shape[2]
    const2 = lambda b, s: (0, 0)
    return pl.pallas_call(
        _xattn_kernel,
        grid=(batches, n_s),
        in_specs=[
            pl.BlockSpec((1, ATT_ROWS, D), lambda b, s: (b + batch0, s, 0)),
            pl.BlockSpec((1, D), const2),
            pl.BlockSpec((1, D, n_att), lambda b, s: (b + batch0, 0, 0)),
            pl.BlockSpec((1, n_att, D), lambda b, s: (b + batch0, 0, 0)),
            pl.BlockSpec((1, D), const2),
            pl.BlockSpec((ROUTER_ROWS, D), const2),
        ],
        out_specs=[
            pl.BlockSpec((1, ATT_ROWS, D), lambda b, s: (b, s, 0)),
            pl.BlockSpec((ATT_ROWS, D // 2), lambda b, s: (b * n_s + s, 0)),
            pl.BlockSpec((ROUTER_ROWS, ATT_ROWS // LANES, LANES), lambda b, s: (0, b * n_s + s, 0)),
        ],
        out_shape=[
            jax.ShapeDtypeStruct((batches, S, D), BF16),
            jax.ShapeDtypeStruct((batches * S, D // 2), jnp.uint32),
            jax.ShapeDtypeStruct((ROUTER_ROWS, batches * S // LANES, LANES), F32),
        ],
        scratch_shapes=[pltpu.VMEM((ATT_ROWS, n_att), BF16)],
        compiler_params=pltpu.CompilerParams(
            dimension_semantics=("arbitrary", "arbitrary"), vmem_limit_bytes=VMEM_LIMIT),
        name="xattn",
    )(h1, norm_x, qk_mem, vo_mem, norm_ffn, w_router)


def _route_kernel(bias_ref, lg_ref, ids_ref, gates_ref, rank_ref, cnt_ref, base_ref):
    sub = lg_ref.shape[1]

    @pl.when(pl.program_id(0) == 0)
    def _():
        base_ref[...] = jnp.zeros_like(base_ref)

    best = lg_ref[0] + bias_ref[0]
    gl = [best]
    sel = jnp.zeros(best.shape, jnp.int32)
    for g in range(1, N_GROUPS):
        cur = lg_ref[g] + bias_ref[g]
        gl.append(cur)
        better = cur > best
        best = jnp.where(better, cur, best)
        sel = jnp.where(better, g, sel)
    denom = jnp.exp(gl[0] - best)
    for g in range(1, N_GROUPS):
        denom = denom + jnp.exp(gl[g] - best)
    g_gate = 1.0 / denom

    ev = []
    for j in range(EXPERTS_PER_GROUP):
        val = lg_ref[N_GROUPS + j] + bias_ref[N_GROUPS + j]
        for g in range(1, N_GROUPS):
            e = g * EXPERTS_PER_GROUP + j
            val = jnp.where(sel == g, lg_ref[N_GROUPS + e] + bias_ref[N_GROUPS + e], val)
        ev.append(val)
    v1, i1 = ev[0], jnp.zeros(best.shape, jnp.int32)
    for j in range(1, EXPERTS_PER_GROUP):
        better = ev[j] > v1
        v1 = jnp.where(better, ev[j], v1)
        i1 = jnp.where(better, j, i1)
    rest = [jnp.where(i1 == j, -jnp.inf, ev[j]) for j in range(EXPERTS_PER_GROUP)]
    v2, i2 = rest[0], jnp.zeros(best.shape, jnp.int32)
    for j in range(1, EXPERTS_PER_GROUP):
        better = rest[j] > v2
        v2 = jnp.where(better, rest[j], v2)
        i2 = jnp.where(better, j, i2)
    e2 = jnp.exp(v2 - v1)
    inv = 1.0 / (1.0 + e2)
    id1 = sel * EXPERTS_PER_GROUP + i1
    id2 = sel * EXPERTS_PER_GROUP + i2
    ids_ref[0] = id1
    ids_ref[1] = id2
    gates_ref[0] = inv * g_gate
    gates_ref[1] = e2 * inv * g_gate

    member = jnp.concatenate(
        [jnp.where((id1 == e) | (id2 == e), 1.0, 0.0) for e in range(N_EXPERTS)], axis=0).astype(BF16)
    n = N_EXPERTS * sub
    li = lax.broadcasted_iota(jnp.int32, (LANES, LANES), 0)
    lj = lax.broadcasted_iota(jnp.int32, (LANES, LANES), 1)
    before_lane = jnp.where(li < lj, 1.0, 0.0).astype(BF16)
    ones = jnp.ones((LANES, LANES), BF16)
    ri = lax.broadcasted_iota(jnp.int32, (n, n), 0)
    rj = lax.broadcasted_iota(jnp.int32, (n, n), 1)
    same = _block_id(ri, sub) == _block_id(rj, sub)
    before_row = jnp.where(same & (rj < ri), 1.0, 0.0).astype(BF16)
    all_row = jnp.where(same, 1.0, 0.0).astype(BF16)
    in_row = _dot(member, before_lane)
    prev_rows = _dot(_dot(before_row, member).astype(BF16), ones)
    total = _dot(_dot(all_row, member).astype(BF16), ones)
    base = base_ref[...]
    pos = base + prev_rows + in_row
    r1 = jnp.zeros(best.shape, F32)
    r2 = jnp.zeros(best.shape, F32)
    for e in range(N_EXPERTS):
        pe = pos[e * sub:(e + 1) * sub]
        r1 = jnp.where(id1 == e, pe, r1)
        r2 = jnp.where(id2 == e, pe, r2)
    rank_ref[0] = r1.astype(jnp.int32)
    rank_ref[1] = r2.astype(jnp.int32)
    base_ref[...] = base + total
    cnt_ref[...] = base + total


def _route(bias, logits3):
    rp, n_sub, _ = logits3.shape
    blk = lambda i: (0, i, 0)
    pair_i = jax.ShapeDtypeStruct((2, n_sub, LANES), jnp.int32)
    return pl.pallas_call(
        _route_kernel,
        grid=(n_sub // ROUTE_SUB,),
        in_specs=[
            pl.BlockSpec(memory_space=pltpu.SMEM),
            pl.BlockSpec((rp, ROUTE_SUB, LANES), blk),
        ],
        out_specs=[
            pl.BlockSpec((2, ROUTE_SUB, LANES), blk),
            pl.BlockSpec((2, ROUTE_SUB, LANES), blk),
            pl.BlockSpec((2, ROUTE_SUB, LANES), blk),
            pl.BlockSpec((N_EXPERTS * ROUTE_SUB, LANES), lambda i: (0, 0)),
        ],
        out_shape=[
            pair_i,
            jax.ShapeDtypeStruct((2, n_sub, LANES), F32),
            pair_i,
            jax.ShapeDtypeStruct((N_EXPERTS * ROUTE_SUB, LANES), F32),
        ],
        scratch_shapes=[pltpu.VMEM((N_EXPERTS * ROUTE_SUB, LANES), F32)],
        compiler_params=pltpu.CompilerParams(
            dimension_semantics=("arbitrary",), vmem_limit_bytes=VMEM_LIMIT),
        name="route",
    )(bias, logits3)


def _dest_kernel(cnt_ref, ids_ref, rank_ref, dest_ref, be_ref, nb_ref, slot_ref, next_ref, start_ref, succ_ref):
    n_blocks = be_ref.shape[0]
    shift = EXPERT_ROWS.bit_length() - 1
    assert 1 << shift == EXPERT_ROWS

    @pl.when(pl.program_id(0) == 0)
    def _():
        def tail(i, carry):
            be_ref[i] = N_EXPERTS - 1
            slot_ref[i] = 0
            next_ref[i] = -1
            return carry

        lax.fori_loop(0, n_blocks, tail, 0)

        def successor(k, nxt):
            e = N_EXPERTS - 1 - k
            succ_ref[e] = nxt
            return jnp.where(cnt_ref[e] > 0, e, nxt)

        lax.fori_loop(0, N_EXPERTS, successor, jnp.int32(-1))

        def segment(e, carry):
            block0, ordinal = carry
            blocks = lax.shift_right_logical(cnt_ref[e] + (EXPERT_ROWS - 1), shift)
            start_ref[e] = lax.shift_left(block0, shift)

            def fill(j, c):
                be_ref[block0 + j] = e
                slot_ref[block0 + j] = ordinal & 1
                next_ref[block0 + j] = succ_ref[e]
                return c

            lax.fori_loop(0, blocks, fill, 0)
            return block0 + blocks, ordinal + (cnt_ref[e] > 0).astype(jnp.int32)

        used, _ = lax.fori_loop(0, N_EXPERTS, segment, (jnp.int32(0), jnp.int32(0)))
        nb_ref[0] = used

    ids = ids_ref[...]
    off = jnp.zeros(ids.shape, jnp.int32)
    for e in range(N_EXPERTS):
        off = jnp.where(ids == e, start_ref[e], off)
    dest_ref[...] = rank_ref[...] + off


def _dest(counts, ids, rank, n_blocks):
    _, n_sub, _ = ids.shape
    blk = pl.BlockSpec((2, ROUTE_SUB, LANES), lambda i: (0, i, 0))
    smem = pl.BlockSpec(memory_space=pltpu.SMEM)
    table = jax.ShapeDtypeStruct((n_blocks,), jnp.int32)
    return pl.pallas_call(
        _dest_kernel,
        grid=(n_sub // ROUTE_SUB,),
        in_specs=[smem, blk, blk],
        out_specs=[blk, smem, smem, smem, smem],
        out_shape=[jax.ShapeDtypeStruct(ids.shape, jnp.int32), table, jax.ShapeDtypeStruct((1,), jnp.int32),
                   table, table],
        scratch_shapes=[pltpu.SMEM((N_EXPERTS,), jnp.int32), pltpu.SMEM((N_EXPERTS,), jnp.int32)],
        compiler_params=pltpu.CompilerParams(dimension_semantics=("arbitrary",)),
        name="dest",
    )(counts, ids, rank)


def _sc_mesh():
    return plsc.VectorSubcoreMesh(core_axis_name="core", subcore_axis_name="subcore")


def _sc_worker(rows_total):
    rows = rows_total // _SC_WORKERS
    wid = lax.axis_index("core") * SC_SUBCORES + lax.axis_index("subcore")
    return wid * rows, rows


def _dispatch(dest0, dest1, xn, n_slots):
    T, D = xn.shape
    W = SC_WINDOW
    rows = T // _SC_WORKERS
    assert T % (_SC_WORKERS * 2 * W) == 0

    @pl.kernel(out_type=jax.ShapeDtypeStruct((n_slots, D), xn.dtype), mesh=_sc_mesh(),
               scratch_types=[pltpu.VMEM((rows,), jnp.int32), pltpu.VMEM((rows,), jnp.int32),
                              pltpu.VMEM((2, W, D), xn.dtype),
                              pltpu.SemaphoreType.DMA((2,)), pltpu.SemaphoreType.DMA((2,))])
    def scatter_rows(x_hbm, d0_hbm, d1_hbm, xs_hbm, d0_v, d1_v, buf, in_sem, out_sem):
        base, _ = _sc_worker(T)
        pltpu.sync_copy(d0_hbm.at[pl.ds(base, rows)], d0_v)
        pltpu.sync_copy(d1_hbm.at[pl.ds(base, rows)], d1_v)

        def load(w, slot):
            return pltpu.make_async_copy(x_hbm.at[pl.ds(base + w * W, W)], buf.at[slot], in_sem.at[slot])

        def store(w, slot, d_v):
            return pltpu.make_async_copy(buf.at[slot], xs_hbm.at[d_v.at[pl.ds(w * W, W)]], out_sem.at[slot])

        def step(w, slot):
            load(w, slot).wait()
            store(w, slot, d0_v).start()
            store(w, slot, d1_v).start()
            store(w, slot, d0_v).wait()
            store(w, slot, d1_v).wait()

        n = rows // W
        load(0, 0).start()

        @pl.loop(0, n, step=2)
        def _(w):
            load(w + 1, 1).start()
            step(w, 0)

            @pl.when(w + 2 < n)
            def _():
                load(w + 2, 0).start()

            step(w + 1, 1)

    return scatter_rows(xn, dest0, dest1)


def _gather_rows(src, idx):
    M = idx.shape[0]
    D = src.shape[1]
    W = SC_WINDOW
    rows = M // _SC_WORKERS
    assert M % (_SC_WORKERS * 2 * W) == 0

    @pl.kernel(out_type=jax.ShapeDtypeStruct((M, D), src.dtype), mesh=_sc_mesh(),
               scratch_types=[pltpu.VMEM((rows,), jnp.int32), pltpu.VMEM((2, W, D), src.dtype),
                              pltpu.SemaphoreType.DMA((2,)), pltpu.SemaphoreType.DMA((2,))])
    def gather_rows(src_hbm, i_hbm, o_hbm, i_v, buf, in_sem, out_sem):
        base, _ = _sc_worker(M)
        pltpu.sync_copy(i_hbm.at[pl.ds(base, rows)], i_v)

        def load(w, slot):
            return pltpu.make_async_copy(src_hbm.at[i_v.at[pl.ds(w * W, W)]], buf.at[slot], in_sem.at[slot])

        def store(w, slot):
            return pltpu.make_async_copy(buf.at[slot], o_hbm.at[pl.ds(base + w * W, W)], out_sem.at[slot])

        n = rows // W
        load(0, 0).start()

        @pl.loop(0, n, step=2)
        def _(w):
            @pl.when(w > 0)
            def _():
                store(w - 1, 1).wait()

            load(w + 1, 1).start()
            load(w, 0).wait()
            store(w, 0).start()
            store(w, 0).wait()

            @pl.when(w + 2 < n)
            def _():
                load(w + 2, 0).start()

            load(w + 1, 1).wait()
            store(w + 1, 1).start()

        store(n - 1, 1).wait()

    return gather_rows(src, idx)


def _expert_kernel(be_ref, nb_ref, slot_ref, next_ref, x_ref, wg_hbm, wu_hbm, wd_hbm, y_ref,
                   wg_f32, wu_f32, wd_f32, wg_ref, wu_ref, wd_ref, sem):
    i = pl.program_id(0)
    used = i < nb_ref[0]
    new_expert = (i == 0) | (be_ref[i] != be_ref[jnp.maximum(i - 1, 0)])

    def fetch(e, slot):
        pairs = ((wg_hbm, wg_f32), (wu_hbm, wu_f32), (wd_hbm, wd_f32))
        return [pltpu.make_async_copy(w_hbm.at[e], w_f32.at[slot], sem.at[slot, k])
                for k, (w_hbm, w_f32) in enumerate(pairs)]

    @pl.when(used & (i == 0))
    def _():
        for copy in fetch(be_ref[0], slot_ref[0]):
            copy.start()

    @pl.when(used & new_expert)
    def _():
        slot = slot_ref[i]
        for copy in fetch(be_ref[i], slot):
            copy.wait()

        @pl.when(next_ref[i] >= 0)
        def _():
            for copy in fetch(next_ref[i], 1 - slot):
                copy.start()

        wg_ref[...] = wg_f32[slot].astype(BF16)
        wu_ref[...] = wu_f32[slot].astype(BF16)
        wd_ref[...] = wd_f32[slot].astype(BF16)

    @pl.when(used)
    def _():
        x = _unpack_pairs(x_ref[...]).astype(BF16)
        g = _dot(x, wg_ref[...])
        u = _dot(x, wu_ref[...])
        hid = (g * _sigmoid(g) * u).astype(BF16)
        y_ref[...] = _pack_pairs(_dot(hid, wd_ref[...]))

    @pl.when(jnp.logical_not(used))
    def _():
        y_ref[...] = jnp.zeros_like(y_ref)


def _experts(block_expert, n_used, block_slot, block_next, xs, w_gate, w_up, w_down):
    n_slots, half = xs.shape
    D = 2 * half
    n_blocks = n_slots // EXPERT_ROWS
    grid_spec = pltpu.PrefetchScalarGridSpec(
        num_scalar_prefetch=4,
        grid=(n_blocks,),
        in_specs=[
            pl.BlockSpec((EXPERT_ROWS, half), lambda i, be, nb, sl, nx: (jnp.minimum(i, nb[0] - 1), 0)),
            pl.BlockSpec(memory_space=pl.ANY),
            pl.BlockSpec(memory_space=pl.ANY),
            pl.BlockSpec(memory_space=pl.ANY),
        ],
        out_specs=pl.BlockSpec((EXPERT_ROWS, half), lambda i, be, nb, sl, nx: (i, 0)),
        scratch_shapes=[
            pltpu.VMEM((2, D, D_EXPERT), F32), pltpu.VMEM((2, D, D_EXPERT), F32),
            pltpu.VMEM((2, D_EXPERT, D), F32),
            pltpu.VMEM((D, D_EXPERT), BF16), pltpu.VMEM((D, D_EXPERT), BF16), pltpu.VMEM((D_EXPERT, D), BF16),
            pltpu.SemaphoreType.DMA((2, 3)),
        ],
    )
    return pl.pallas_call(
        _expert_kernel,
        grid_spec=grid_spec,
        out_shape=jax.ShapeDtypeStruct((n_slots, half), jnp.uint32),
        compiler_params=pltpu.CompilerParams(
            dimension_semantics=("arbitrary",), vmem_limit_bytes=VMEM_LIMIT),
        name="experts",
    )(block_expert, n_used, block_slot, block_next, xs, w_gate, w_up, w_down)


def _combine_kernel(y0_ref, y1_ref, h_ref, gates_ref, nfin_ref, *rest):
    o_ref = rest[-1]
    g0 = gates_ref[0].T
    g1 = gates_ref[1].T
    for r in range(gates_ref.shape[1]):
        rs = slice(r * LANES, (r + 1) * LANES)
        h = (h_ref[rs].astype(F32) + g0[:, r:r + 1] * _unpack_pairs(y0_ref[rs])
             + g1[:, r:r + 1] * _unpack_pairs(y1_ref[rs]))
        o_ref[rs] = _rms(h, nfin_ref[...])


def _combine(y01, h2, gates, norm_final, out_prev, row0, total_rows):
    T, D = h2.shape
    n_t = T // MOVE_ROWS
    in_specs = [
        pl.BlockSpec((MOVE_ROWS, D // 2), lambda i: (i, 0)),
        pl.BlockSpec((MOVE_ROWS, D // 2), lambda i: (i + n_t, 0)),
        pl.BlockSpec((MOVE_ROWS, D), lambda i: (i, 0)),
        pl.BlockSpec((2, MOVE_ROWS // LANES, LANES), lambda i: (0, i, 0)),
        pl.BlockSpec((1, D), lambda i: (0, 0)),
    ]
    args = [y01, y01, h2, gates, norm_final]
    aliases = {}
    if out_prev is not None:
        in_specs.append(pl.BlockSpec(memory_space=pl.ANY))
        args.append(out_prev)
        aliases = {len(args) - 1: 0}
    return pl.pallas_call(
        _combine_kernel,
        grid=(n_t,),
        in_specs=in_specs,
        out_specs=pl.BlockSpec((MOVE_ROWS, D), lambda i: (i + row0 // MOVE_ROWS, 0)),
        out_shape=jax.ShapeDtypeStruct((total_rows, D), F32),
        input_output_aliases=aliases,
        compiler_params=pltpu.CompilerParams(
            dimension_semantics=("arbitrary",), vmem_limit_bytes=VMEM_LIMIT),
        name="combine",
    )(*args)


def kernel(x, mem, norm_mix, w_in, gmlp_ln, gmlp_w_spatial, gmlp_b_spatial, gmlp_beta, hgrn_lb_logits, hgrn_out_gain, w_out, norm_xattn, norm_mem, w_xq, w_xkv, w_xo, norm_ffn, w_router_group, b_router_group, w_router_expert, b_router_expert, w_expert_gate, w_expert_up, w_expert_down, norm_final):
    B, S, D = x.shape
    T = B * S
    depth = w_in.shape[0]
    assert depth == 1 and hgrn_lb_logits.shape[0] == 2
    assert D == D_MODEL and mem.shape[1] == N_MEM and w_in.shape[2] == D_IN_PROJ
    assert S % MIX_ROWS == 0 and S % ATT_ROWS == 0 and B % MEMKV_BATCHES == 0
    l = 0
    row = lambda p: p.reshape(1, -1)

    h1 = _mixer(x, row(norm_mix[l]), w_in[l], row(gmlp_ln[l]), gmlp_w_spatial[l],
                gmlp_b_spatial[l].T, row(gmlp_beta[l]), hgrn_lb_logits, row(hgrn_out_gain[l]),
                w_out[l])
    qk_mem, vo_mem = _memkv(mem, row(norm_mem[l]), w_xkv[l], w_xq[l], w_xo[l])

    w_router = jnp.concatenate([w_router_group[l].T, w_router_expert[l].T], axis=0)
    w_router = jnp.pad(w_router, ((0, ROUTER_ROWS - w_router.shape[0]), (0, 0)))
    w_router = w_router.astype(BF16)
    bias = jnp.concatenate([b_router_group[l], b_router_expert[l]]).astype(F32)

    assert sum(PART_BATCHES) == B
    out = None
    b0 = 0
    for b_part in PART_BATCHES:
        t_part = b_part * S
        n_blocks = (2 * t_part) // EXPERT_ROWS + N_EXPERTS
        h2, xn, logits = _xattn(h1, row(norm_xattn[l]), qk_mem, vo_mem, row(norm_ffn[l]), w_router, b0, b_part)
        ids, gates, rank, counts = _route(bias, logits)

        counts = counts[::ROUTE_SUB, 0].astype(jnp.int32)
        dest, block_expert, n_used, block_slot, block_next = _dest(counts, ids, rank, n_blocks)
        dest = dest.reshape(2, t_part)
        xs = _dispatch(dest[0], dest[1], xn, n_blocks * EXPERT_ROWS)
        yb = _experts(block_expert, n_used, block_slot, block_next, xs,
                      w_expert_gate[l], w_expert_up[l], w_expert_down[l])
        y01 = _gather_rows(yb, dest.reshape(2 * t_part))
        out = _combine(y01, h2.reshape(t_part, D), gates, row(norm_final), out, b0 * S, T)
        b0 += b_part
    return out.reshape(B, S, D)
```

```python
import jax
import jax.numpy as jnp
from jax import lax
from jax.experimental import pallas as pl
from jax.experimental.pallas import tpu as pltpu
from jax.experimental.pallas import tpu_sc as plsc

F32 = jnp.float32
BF16 = jnp.bfloat16
EPS = 1e-6

D_MODEL = 1024
D_GMLP = 512
GMLP_GROUPS = 4
GMLP_CHUNK = 128
D_HGRN = 512
HGRN_HEADS = 4
HGRN_DK = 128
HGRN_CHUNK = 64
D_IN_PROJ = 2 * D_GMLP + 4 * D_HGRN
N_MEM = 256
XATTN_HEADS = 4
XATTN_HEAD_DIM = D_MODEL // XATTN_HEADS
N_GROUPS = 4
EXPERTS_PER_GROUP = 8
N_EXPERTS = N_GROUPS * EXPERTS_PER_GROUP
D_EXPERT = 512

LANES = 128
MIX_ROWS = 1024
MIX_CHAIN_ROWS = 256


def _mix_schedule(n_chains):
    order = [("in", 0), ("gmlp", 0), ("factors", 0)]
    for k in range(n_chains):
        more = k + 1 < n_chains
        order += [("in", k + 1)] * more + [("local", k), ("recurrence", k)] + [("gmlp", k + 1)] * more
        order += [("out", k)] + [("factors", k + 1)] * more
    return tuple(order)


MIX_SCHEDULE = _mix_schedule(MIX_ROWS // MIX_CHAIN_ROWS)
MEMKV_BATCHES = 2
ATT_ROWS = 1024
WEIGHT_STAGE_ROWS = 128
ROUTER_ROWS = 40
ROUTE_SUB = 16
EXPERT_ROWS = 512
MOVE_ROWS = 1024
PART_BATCHES = (16, 16)
SC_WINDOW = 64
SC_CORES = 2
SC_SUBCORES = 16
_SC_WORKERS = SC_CORES * SC_SUBCORES
VMEM_LIMIT = 48 * 1024 * 1024


def _rms(x, gain):
    return x * lax.rsqrt(jnp.mean(x * x, axis=-1, keepdims=True) + EPS) * gain


def _dot(a, b):
    return jnp.dot(a, b, preferred_element_type=F32)


def _dot_nt(a, b):
    return lax.dot_general(a, b, (((1,), (1,)), ((), ())), preferred_element_type=F32)


def _dot_tn(a, b):
    return lax.dot_general(a, b, (((0,), (0,)), ((), ())), preferred_element_type=F32)


def _gelu(x):
    return 0.5 * x * (1.0 + jnp.tanh(0.7978845608028654 * (x + 0.044715 * (x * x * x))))


def _sigmoid(x):
    return 1.0 / (1.0 + jnp.exp(-x))


def _block_id(idx, size):
    assert size & (size - 1) == 0
    return lax.shift_right_logical(idx, size.bit_length() - 1)


def _stage_weight(w_hbm, w_bf16, stage_ref, sem):
    rows = stage_ref.shape[0]
    for k in range(w_hbm.shape[0] // rows):
        copy = pltpu.make_async_copy(w_hbm.at[pl.ds(k * rows, rows)], stage_ref, sem)
        copy.start()
        copy.wait()
        w_bf16[k * rows:(k + 1) * rows, :] = stage_ref[...].astype(BF16)


def _weight_scratch(k, n):
    return [pltpu.VMEM((k, n), BF16), pltpu.VMEM((WEIGHT_STAGE_ROWS, n), F32)]


_HIGH_HALF = 0xFFFF0000


def _pack_pairs(x):
    c = x.shape[1] // 2
    bits = lax.bitcast_convert_type(x.astype(BF16).astype(F32), jnp.uint32)
    return (bits[:, c:] & jnp.uint32(_HIGH_HALF)) | lax.shift_right_logical(bits[:, :c], jnp.uint32(16))


def _unpack_pairs(w):
    lo = lax.bitcast_convert_type(lax.shift_left(w, jnp.uint32(16)), F32)
    hi = lax.bitcast_convert_type(w & jnp.uint32(_HIGH_HALF), F32)
    return jnp.concatenate([lo, hi], axis=1)


def _mixer_kernel(x_ref, nmix_ref, win_hbm, gln_ref, ws_ref, bst_ref, beta_ref, lbl_ref, og_ref,
                  wout_hbm, o_ref, proj_ref, ycat_ref, state_ref, win_ref, win_stage, wout_ref, wout_stage,
                  sem):
    @pl.when((pl.program_id(0) == 0) & (pl.program_id(1) == 0))
    def _():
        _stage_weight(win_hbm, win_ref, win_stage, sem)
        _stage_weight(wout_hbm, wout_ref, wout_stage, sem)

    @pl.when(pl.program_id(1) == 0)
    def _():
        state_ref[...] = jnp.zeros_like(state_ref)

    n = MIX_CHAIN_ROWS
    r_i = lax.broadcasted_iota(jnp.int32, (GMLP_CHUNK, GMLP_CHUNK), 0)
    c_i = lax.broadcasted_iota(jnp.int32, (GMLP_CHUNK, GMLP_CHUNK), 1)
    causal = c_i <= r_i
    w_tril = [jnp.where(causal, ws_ref[g], 0.0).astype(BF16) for g in range(GMLP_GROUPS)]
    lbl = lbl_ref[...]
    e_lb = jnp.exp(lbl - jnp.max(lbl, axis=0, keepdims=True))
    lb = e_lb[0:1] / jnp.sum(e_lb, axis=0, keepdims=True)
    rr = lax.broadcasted_iota(jnp.int32, (n, n), 0)
    cc = lax.broadcasted_iota(jnp.int32, (n, n), 1)
    tri = jnp.where((_block_id(rr, HGRN_CHUNK) == _block_id(cc, HGRN_CHUNK)) & (cc <= rr),
                    1.0, 0.0).astype(BF16)
    r64 = lax.broadcasted_iota(jnp.int32, (HGRN_CHUNK, HGRN_CHUNK), 0)
    c64 = lax.broadcasted_iota(jnp.int32, (HGRN_CHUNK, HGRN_CHUNK), 1)
    causal64 = c64 <= r64
    base = 2 * D_GMLP

    n_chunks = n // HGRN_CHUNK
    chains = range(x_ref.shape[1] // n)
    env = {ch: {} for ch in chains}

    def rows(ch):
        return slice(ch * n, (ch + 1) * n)

    def in_proj(ch):
        a = _rms(x_ref[0, rows(ch)], nmix_ref[...]).astype(BF16)
        proj_ref[rows(ch)] = _dot(a, win_ref[...])

    def gmlp(ch):
        u = _gelu(proj_ref[rows(ch), 0:D_GMLP])
        v = _gelu(proj_ref[rows(ch), D_GMLP:2 * D_GMLP])
        vc = v - jnp.mean(v, axis=-1, keepdims=True)
        vn = (vc * lax.rsqrt(jnp.mean(vc * vc, axis=-1, keepdims=True) + EPS) * gln_ref[...]).astype(BF16)
        z_rows = []
        for c in range(n // GMLP_CHUNK):
            z_cols = []
            for g in range(GMLP_GROUPS):
                vg = vn[c * GMLP_CHUNK:(c + 1) * GMLP_CHUNK, g * LANES:(g + 1) * LANES]
                z_cols.append(_dot(w_tril[g], vg) + bst_ref[:, g:g + 1])
            z_rows.append(jnp.concatenate(z_cols, axis=1))
        z = jnp.concatenate(z_rows, axis=0)
        ycat_ref[rows(ch), 0:D_GMLP] = _rms(u * z, beta_ref[...]).astype(BF16)

    def hgrn_factors(ch):
        e = env[ch]
        f = lb + (1.0 - lb) * _sigmoid(proj_ref[rows(ch), base + D_HGRN:base + 2 * D_HGRN])
        log_f = jnp.log(f)
        lf_hi = log_f.astype(BF16)
        lf_lo = (log_f - lf_hi.astype(F32)).astype(BF16)
        b_all = _dot(tri, lf_hi) + _dot(tri, lf_lo)
        bl_rows = [b_all[c * HGRN_CHUNK + HGRN_CHUNK - 1:(c + 1) * HGRN_CHUNK] for c in range(n_chunks)]
        bl_all = jnp.concatenate([jnp.broadcast_to(r, (HGRN_CHUNK, D_HGRN)) for r in bl_rows], axis=0)
        q_all = proj_ref[rows(ch), base:base + D_HGRN]
        k_all = 1.0 - f
        e["qd"] = (q_all * _sigmoid(q_all) * jnp.exp(b_all)).astype(BF16)
        e["ki"] = (k_all * jnp.exp(-b_all)).astype(BF16)
        e["kte"] = (k_all * jnp.exp(bl_all - b_all)).astype(BF16)
        e["v"] = proj_ref[rows(ch), base + 2 * D_HGRN:base + 3 * D_HGRN].astype(BF16)
        e["decay"] = [jnp.exp(r) for r in bl_rows]

    def hgrn_local(ch):
        e = env[ch]
        e["o_intra"], e["d_state"] = {}, {}
        for c in range(n_chunks):
            rs = slice(c * HGRN_CHUNK, (c + 1) * HGRN_CHUNK)
            for h in range(HGRN_HEADS):
                cs = slice(h * HGRN_DK, (h + 1) * HGRN_DK)
                scores = jnp.where(causal64, _dot_nt(e["qd"][rs, cs], e["ki"][rs, cs]), 0.0).astype(BF16)
                e["o_intra"][c, h] = _dot(scores, e["v"][rs, cs])
                e["d_state"][c, h] = _dot_tn(e["v"][rs, cs], e["kte"][rs, cs])

    def hgrn_recurrence(ch):
        e = env[ch]
        for c in range(n_chunks):
            rs = slice(c * HGRN_CHUNK, (c + 1) * HGRN_CHUNK)
            ps = slice(ch * n + c * HGRN_CHUNK, ch * n + (c + 1) * HGRN_CHUNK)
            g_c = proj_ref[ps, base + 3 * D_HGRN:base + 4 * D_HGRN]
            gate = og_ref[...] * (g_c * _sigmoid(g_c))
            for h in range(HGRN_HEADS):
                cs = slice(h * HGRN_DK, (h + 1) * HGRN_DK)
                st = state_ref[h]
                o = e["o_intra"][c, h] + _dot_nt(e["qd"][rs, cs], st.astype(BF16))
                state_ref[h] = st * e["decay"][c][:, cs] + e["d_state"][c, h]
                o = o * lax.rsqrt(jnp.mean(o * o, axis=-1, keepdims=True) + EPS)
                ycat_ref[ps, D_GMLP + h * HGRN_DK:D_GMLP + (h + 1) * HGRN_DK] = (o * gate[:, cs]).astype(BF16)

    def out_proj(ch):
        o_ref[0, rows(ch)] = x_ref[0, rows(ch)] + _dot(ycat_ref[rows(ch)], wout_ref[...])

    stages = {"in": in_proj, "gmlp": gmlp, "factors": hgrn_factors, "local": hgrn_local,
              "recurrence": hgrn_recurrence, "out": out_proj}
    for stage, ch in MIX_SCHEDULE:
        stages[stage](ch)


def _mixer(x, norm_mix, w_in, gmlp_ln, w_s, b_s_t, beta, lb_logits, out_gain, w_out):
    B, S, D = x.shape
    const2 = lambda b, s: (0, 0)
    return pl.pallas_call(
        _mixer_kernel,
        grid=(B, S // MIX_ROWS),
        in_specs=[
            pl.BlockSpec((1, MIX_ROWS, D), lambda b, s: (b, s, 0)),
            pl.BlockSpec((1, D), const2),
            pl.BlockSpec(memory_space=pl.ANY),
            pl.BlockSpec((1, D_GMLP), const2),
            pl.BlockSpec((GMLP_GROUPS, GMLP_CHUNK, GMLP_CHUNK), lambda b, s: (0, 0, 0)),
            pl.BlockSpec((GMLP_CHUNK, GMLP_GROUPS), const2),
            pl.BlockSpec((1, D_GMLP), const2),
            pl.BlockSpec(lb_logits.shape, const2),
            pl.BlockSpec((1, D_HGRN), const2),
            pl.BlockSpec(memory_space=pl.ANY),
        ],
        out_specs=pl.BlockSpec((1, MIX_ROWS, D), lambda b, s: (b, s, 0)),
        out_shape=jax.ShapeDtypeStruct((B, S, D), F32),
        scratch_shapes=[
            pltpu.VMEM((MIX_ROWS, D_IN_PROJ), F32),
            pltpu.VMEM((MIX_ROWS, D), BF16),
            pltpu.VMEM((HGRN_HEADS, HGRN_DK, HGRN_DK), F32),
        ] + _weight_scratch(D, D_IN_PROJ) + _weight_scratch(D, D) + [pltpu.SemaphoreType.DMA],
        compiler_params=pltpu.CompilerParams(
            dimension_semantics=("arbitrary", "arbitrary"), vmem_limit_bytes=VMEM_LIMIT),
        name="mixer",
    )(x, norm_mix, w_in, gmlp_ln, w_s, b_s_t, beta, lb_logits, out_gain, w_out)


def _memkv_kernel(mem_ref, nm_ref, wkv_hbm, wq_hbm, wo_hbm, qk_ref, vo_ref,
                  wkv_ref, wkv_stage, wq_ref, wq_stage, wo_ref, wo_stage, sem):
    @pl.when(pl.program_id(0) == 0)
    def _():
        _stage_weight(wkv_hbm, wkv_ref, wkv_stage, sem)
        _stage_weight(wq_hbm, wq_ref, wq_stage, sem)
        _stage_weight(wo_hbm, wo_ref, wo_stage, sem)

    nb, M, D = mem_ref.shape
    m = _rms(mem_ref[...].reshape(nb * M, D), nm_ref[...]).astype(BF16)
    kv = _dot(m, wkv_ref[...])
    k = kv[:, :D_MODEL].astype(BF16)
    v = kv[:, D_MODEL:].astype(BF16)
    for b in range(nb):
        rs = slice(b * M, (b + 1) * M)
        for hd in range(XATTN_HEADS):
            cs = slice(hd * XATTN_HEAD_DIM, (hd + 1) * XATTN_HEAD_DIM)
            ms = slice(hd * M, (hd + 1) * M)
            qk_ref[b, :, ms] = (_dot_nt(wq_ref[:, cs], k[rs, cs]) * (XATTN_HEAD_DIM ** -0.5)).astype(BF16)
            vo_ref[b, ms, :] = _dot(v[rs, cs], wo_ref[cs, :]).astype(BF16)


def _memkv(mem, norm_mem, w_kv, w_q, w_o):
    B, M, D = mem.shape
    out = jax.ShapeDtypeStruct((B, D, XATTN_HEADS * M), BF16), jax.ShapeDtypeStruct((B, XATTN_HEADS * M, D), BF16)
    return pl.pallas_call(
        _memkv_kernel,
        grid=(B // MEMKV_BATCHES,),
        in_specs=[
            pl.BlockSpec((MEMKV_BATCHES, M, D), lambda b: (b, 0, 0)),
            pl.BlockSpec((1, D), lambda b: (0, 0)),
            pl.BlockSpec(memory_space=pl.ANY),
            pl.BlockSpec(memory_space=pl.ANY),
            pl.BlockSpec(memory_space=pl.ANY),
        ],
        out_specs=[pl.BlockSpec((MEMKV_BATCHES, D, XATTN_HEADS * M), lambda b: (b, 0, 0)),
                   pl.BlockSpec((MEMKV_BATCHES, XATTN_HEADS * M, D), lambda b: (b, 0, 0))],
        out_shape=list(out),
        scratch_shapes=(_weight_scratch(D, 2 * D) + _weight_scratch(D, D) + _weight_scratch(D, D)
                        + [pltpu.SemaphoreType.DMA]),
        compiler_params=pltpu.CompilerParams(
            dimension_semantics=("arbitrary",), vmem_limit_bytes=VMEM_LIMIT),
        name="memkv",
    )(mem, norm_mem, w_kv, w_q, w_o)


def _xattn_kernel(h_ref, nx_ref, qk_ref, vo_ref, nf_ref, wr_ref, h2_ref, xn_ref, lg_ref, att_ref):
    h = h_ref[0]
    hn = _rms(h, nx_ref[...]).astype(BF16)
    scores = _dot(hn, qk_ref[0])
    n_mem = qk_ref.shape[2] // XATTN_HEADS
    for hd in range(XATTN_HEADS):
        ms = slice(hd * n_mem, (hd + 1) * n_mem)
        s = scores[:, ms]
        p = jnp.exp(s - jnp.max(s, axis=-1, keepdims=True))
        att_ref[:, ms] = (p / jnp.sum(p, axis=-1, keepdims=True)).astype(BF16)
    h2 = h + _dot(att_ref[...], vo_ref[0])
    h2_ref[0] = h2.astype(h2_ref.dtype)
    xn = _rms(h2, nf_ref[...])
    xn_ref[...] = _pack_pairs(xn)
    lg = _dot_nt(wr_ref[...], xn.astype(BF16))
    for j in range(ATT_ROWS // LANES):
        lg_ref[:, j, :] = lg[:, j * LANES:(j + 1) * LANES]


def _xattn(h1, norm_x, qk_mem, vo_mem, norm_ffn, w_router, batch0, batches):
    _, S, D = h1.shape
    n_s = S // ATT_ROWS
    n_att = qk_mem.shape[2]
    const2 = lambda b, s: (0, 0)
    return pl.pallas_call(
        _xattn_kernel,
        grid=(batches, n_s),
        in_specs=[
            pl.BlockSpec((1, ATT_ROWS, D), lambda b, s: (b + batch0, s, 0)),
            pl.BlockSpec((1, D), const2),
            pl.BlockSpec((1, D, n_att), lambda b, s: (b + batch0, 0, 0)),
            pl.BlockSpec((1, n_att, D), lambda b, s: (b + batch0, 0, 0)),
            pl.BlockSpec((1, D), const2),
            pl.BlockSpec((ROUTER_ROWS, D), const2),
        ],
        out_specs=[
            pl.BlockSpec((1, ATT_ROWS, D), lambda b, s: (b, s, 0)),
            pl.BlockSpec((ATT_ROWS, D // 2), lambda b, s: (b * n_s + s, 0)),
            pl.BlockSpec((ROUTER_ROWS, ATT_ROWS // LANES, LANES), lambda b, s: (0, b * n_s + s, 0)),
        ],
        out_shape=[
            jax.ShapeDtypeStruct((batches, S, D), BF16),
            jax.ShapeDtypeStruct((batches * S, D // 2), jnp.uint32),
            jax.ShapeDtypeStruct((ROUTER_ROWS, batches * S // LANES, LANES), F32),
        ],
        scratch_shapes=[pltpu.VMEM((ATT_ROWS, n_att), BF16)],
        compiler_params=pltpu.CompilerParams(
            dimension_semantics=("arbitrary", "arbitrary"), vmem_limit_bytes=VMEM_LIMIT),
        name="xattn",
    )(h1, norm_x, qk_mem, vo_mem, norm_ffn, w_router)


def _route_kernel(bias_ref, lg_ref, ids_ref, gates_ref, rank_ref, cnt_ref, base_ref):
    sub = lg_ref.shape[1]

    @pl.when(pl.program_id(0) == 0)
    def _():
        base_ref[...] = jnp.zeros_like(base_ref)

    best = lg_ref[0] + bias_ref[0]
    gl = [best]
    sel = jnp.zeros(best.shape, jnp.int32)
    for g in range(1, N_GROUPS):
        cur = lg_ref[g] + bias_ref[g]
        gl.append(cur)
        better = cur > best
        best = jnp.where(better, cur, best)
        sel = jnp.where(better, g, sel)
    denom = jnp.exp(gl[0] - best)
    for g in range(1, N_GROUPS):
        denom = denom + jnp.exp(gl[g] - best)
    g_gate = 1.0 / denom

    ev = []
    for j in range(EXPERTS_PER_GROUP):
        val = lg_ref[N_GROUPS + j] + bias_ref[N_GROUPS + j]
        for g in range(1, N_GROUPS):
            e = g * EXPERTS_PER_GROUP + j
            val = jnp.where(sel == g, lg_ref[N_GROUPS + e] + bias_ref[N_GROUPS + e], val)
        ev.append(val)
    v1, i1 = ev[0], jnp.zeros(best.shape, jnp.int32)
    for j in range(1, EXPERTS_PER_GROUP):
        better = ev[j] > v1
        v1 = jnp.where(better, ev[j], v1)
        i1 = jnp.where(better, j, i1)
    rest = [jnp.where(i1 == j, -jnp.inf, ev[j]) for j in range(EXPERTS_PER_GROUP)]
    v2, i2 = rest[0], jnp.zeros(best.shape, jnp.int32)
    for j in range(1, EXPERTS_PER_GROUP):
        better = rest[j] > v2
        v2 = jnp.where(better, rest[j], v2)
        i2 = jnp.where(better, j, i2)
    e2 = jnp.exp(v2 - v1)
    inv = 1.0 / (1.0 + e2)
    id1 = sel * EXPERTS_PER_GROUP + i1
    id2 = sel * EXPERTS_PER_GROUP + i2
    ids_ref[0] = id1
    ids_ref[1] = id2
    gates_ref[0] = inv * g_gate
    gates_ref[1] = e2 * inv * g_gate

    member = jnp.concatenate(
        [jnp.where((id1 == e) | (id2 == e), 1.0, 0.0) for e in range(N_EXPERTS)], axis=0).astype(BF16)
    n = N_EXPERTS * sub
    li = lax.broadcasted_iota(jnp.int32, (LANES, LANES), 0)
    lj = lax.broadcasted_iota(jnp.int32, (LANES, LANES), 1)
    before_lane = jnp.where(li < lj, 1.0, 0.0).astype(BF16)
    ones = jnp.ones((LANES, LANES), BF16)
    ri = lax.broadcasted_iota(jnp.int32, (n, n), 0)
    rj = lax.broadcasted_iota(jnp.int32, (n, n), 1)
    same = _block_id(ri, sub) == _block_id(rj, sub)
    before_row = jnp.where(same & (rj < ri), 1.0, 0.0).astype(BF16)
    all_row = jnp.where(same, 1.0, 0.0).astype(BF16)
    in_row = _dot(member, before_lane)
    prev_rows = _dot(_dot(before_row, member).astype(BF16), ones)
    total = _dot(_dot(all_row, member).astype(BF16), ones)
    base = base_ref[...]
    pos = base + prev_rows + in_row
    r1 = jnp.zeros(best.shape, F32)
    r2 = jnp.zeros(best.shape, F32)
    for e in range(N_EXPERTS):
        pe = pos[e * sub:(e + 1) * sub]
        r1 = jnp.where(id1 == e, pe, r1)
        r2 = jnp.where(id2 == e, pe, r2)
    rank_ref[0] = r1.astype(jnp.int32)
    rank_ref[1] = r2.astype(jnp.int32)
    base_ref[...] = base + total
    cnt_ref[...] = base + total


def _route(bias, logits3):
    rp, n_sub, _ = logits3.shape
    blk = lambda i: (0, i, 0)
    pair_i = jax.ShapeDtypeStruct((2, n_sub, LANES), jnp.int32)
    return pl.pallas_call(
        _route_kernel,
        grid=(n_sub // ROUTE_SUB,),
        in_specs=[
            pl.BlockSpec(memory_space=pltpu.SMEM),
            pl.BlockSpec((rp, ROUTE_SUB, LANES), blk),
        ],
        out_specs=[
            pl.BlockSpec((2, ROUTE_SUB, LANES), blk),
            pl.BlockSpec((2, ROUTE_SUB, LANES), blk),
            pl.BlockSpec((2, ROUTE_SUB, LANES), blk),
            pl.BlockSpec((N_EXPERTS * ROUTE_SUB, LANES), lambda i: (0, 0)),
        ],
        out_shape=[
            pair_i,
            jax.ShapeDtypeStruct((2, n_sub, LANES), F32),
            pair_i,
            jax.ShapeDtypeStruct((N_EXPERTS * ROUTE_SUB, LANES), F32),
        ],
        scratch_shapes=[pltpu.VMEM((N_EXPERTS * ROUTE_SUB, LANES), F32)],
        compiler_params=pltpu.CompilerParams(
            dimension_semantics=("arbitrary",), vmem_limit_bytes=VMEM_LIMIT),
        name="route",
    )(bias, logits3)


def _dest_kernel(cnt_ref, ids_ref, rank_ref, dest_ref, first_ref, blocks_ref, start_ref):
    shift = EXPERT_ROWS.bit_length() - 1
    assert 1 << shift == EXPERT_ROWS

    @pl.when(pl.program_id(0) == 0)
    def _():
        def segment(e, block0):
            blocks = lax.shift_right_logical(cnt_ref[e] + (EXPERT_ROWS - 1), shift)
            first_ref[e] = block0
            blocks_ref[e] = blocks
            start_ref[e] = lax.shift_left(block0, shift)
            return block0 + blocks

        lax.fori_loop(0, N_EXPERTS, segment, jnp.int32(0))

    ids = ids_ref[...]
    off = jnp.zeros(ids.shape, jnp.int32)
    for e in range(N_EXPERTS):
        off = jnp.where(ids == e, start_ref[e], off)
    dest_ref[...] = rank_ref[...] + off


def _dest(counts, ids, rank):
    _, n_sub, _ = ids.shape
    blk = pl.BlockSpec((2, ROUTE_SUB, LANES), lambda i: (0, i, 0))
    smem = pl.BlockSpec(memory_space=pltpu.SMEM)
    table = jax.ShapeDtypeStruct((N_EXPERTS,), jnp.int32)
    return pl.pallas_call(
        _dest_kernel,
        grid=(n_sub // ROUTE_SUB,),
        in_specs=[smem, blk, blk],
        out_specs=[blk, smem, smem],
        out_shape=[jax.ShapeDtypeStruct(ids.shape, jnp.int32), table, table],
        scratch_shapes=[pltpu.SMEM((N_EXPERTS,), jnp.int32)],
        compiler_params=pltpu.CompilerParams(dimension_semantics=("arbitrary",)),
        name="dest",
    )(counts, ids, rank)


def _sc_mesh():
    return plsc.VectorSubcoreMesh(core_axis_name="core", subcore_axis_name="subcore")


def _sc_worker(rows_total):
    rows = rows_total // _SC_WORKERS
    wid = lax.axis_index("core") * SC_SUBCORES + lax.axis_index("subcore")
    return wid * rows, rows


def _dispatch(dest0, dest1, xn, n_slots):
    T, D = xn.shape
    W = SC_WINDOW
    rows = T // _SC_WORKERS
    assert T % (_SC_WORKERS * 2 * W) == 0

    @pl.kernel(out_type=jax.ShapeDtypeStruct((n_slots, D), xn.dtype), mesh=_sc_mesh(),
               scratch_types=[pltpu.VMEM((rows,), jnp.int32), pltpu.VMEM((rows,), jnp.int32),
                              pltpu.VMEM((2, W, D), xn.dtype),
                              pltpu.SemaphoreType.DMA((2,)), pltpu.SemaphoreType.DMA((2,))])
    def scatter_rows(x_hbm, d0_hbm, d1_hbm, xs_hbm, d0_v, d1_v, buf, in_sem, out_sem):
        base, _ = _sc_worker(T)
        pltpu.sync_copy(d0_hbm.at[pl.ds(base, rows)], d0_v)
        pltpu.sync_copy(d1_hbm.at[pl.ds(base, rows)], d1_v)

        def load(w, slot):
            return pltpu.make_async_copy(x_hbm.at[pl.ds(base + w * W, W)], buf.at[slot], in_sem.at[slot])

        def store(w, slot, d_v):
            return pltpu.make_async_copy(buf.at[slot], xs_hbm.at[d_v.at[pl.ds(w * W, W)]], out_sem.at[slot])

        def step(w, slot):
            load(w, slot).wait()
            store(w, slot, d0_v).start()
            store(w, slot, d1_v).start()
            store(w, slot, d0_v).wait()
            store(w, slot, d1_v).wait()

        n = rows // W
        load(0, 0).start()

        @pl.loop(0, n, step=2)
        def _(w):
            load(w + 1, 1).start()
            step(w, 0)

            @pl.when(w + 2 < n)
            def _():
                load(w + 2, 0).start()

            step(w + 1, 1)

    return scatter_rows(xn, dest0, dest1)


def _gather_rows(src, idx):
    M = idx.shape[0]
    D = src.shape[1]
    W = SC_WINDOW
    rows = M // _SC_WORKERS
    assert M % (_SC_WORKERS * 2 * W) == 0

    @pl.kernel(out_type=jax.ShapeDtypeStruct((M, D), src.dtype), mesh=_sc_mesh(),
               scratch_types=[pltpu.VMEM((rows,), jnp.int32), pltpu.VMEM((2, W, D), src.dtype),
                              pltpu.SemaphoreType.DMA((2,)), pltpu.SemaphoreType.DMA((2,))])
    def gather_rows(src_hbm, i_hbm, o_hbm, i_v, buf, in_sem, out_sem):
        base, _ = _sc_worker(M)
        pltpu.sync_copy(i_hbm.at[pl.ds(base, rows)], i_v)

        def load(w, slot):
            return pltpu.make_async_copy(src_hbm.at[i_v.at[pl.ds(w * W, W)]], buf.at[slot], in_sem.at[slot])

        def store(w, slot):
            return pltpu.make_async_copy(buf.at[slot], o_hbm.at[pl.ds(base + w * W, W)], out_sem.at[slot])

        n = rows // W
        load(0, 0).start()

        @pl.loop(0, n, step=2)
        def _(w):
            @pl.when(w > 0)
            def _():
                store(w - 1, 1).wait()

            load(w + 1, 1).start()
            load(w, 0).wait()
            store(w, 0).start()
            store(w, 0).wait()

            @pl.when(w + 2 < n)
            def _():
                load(w + 2, 0).start()

            load(w + 1, 1).wait()
            store(w + 1, 1).start()

        store(n - 1, 1).wait()

    return gather_rows(src, idx)


def _expert_kernel(first_ref, blocks_ref, xs_hbm, wg_f32, wu_f32, wd_f32, yb_hbm,
                   wg_ref, wu_ref, wd_ref, xbuf, ybuf, in_sem, out_sem):
    e = pl.program_id(0)
    block0 = first_ref[e]
    n_blk = blocks_ref[e]

    def load(j, slot):
        rows = pl.ds(pl.multiple_of((block0 + j) * EXPERT_ROWS, EXPERT_ROWS), EXPERT_ROWS)
        return pltpu.make_async_copy(xs_hbm.at[rows], xbuf.at[slot], in_sem.at[slot])

    def store(j, slot):
        rows = pl.ds(pl.multiple_of((block0 + j) * EXPERT_ROWS, EXPERT_ROWS), EXPERT_ROWS)
        return pltpu.make_async_copy(ybuf.at[slot], yb_hbm.at[rows], out_sem.at[slot])

    @pl.when(n_blk > 0)
    def _():
        load(0, 0).start()
        wg_ref[...] = wg_f32[0].astype(BF16)
        wu_ref[...] = wu_f32[0].astype(BF16)
        wd_ref[...] = wd_f32[0].astype(BF16)

        def block(j, carry):
            slot = j & 1
            load(j, slot).wait()

            @pl.when(j + 1 < n_blk)
            def _():
                load(j + 1, 1 - slot).start()

            @pl.when(j >= 2)
            def _():
                store(j - 2, slot).wait()

            x = _unpack_pairs(xbuf[slot]).astype(BF16)
            g = _dot(x, wg_ref[...])
            u = _dot(x, wu_ref[...])
            hid = (g * _sigmoid(g) * u).astype(BF16)
            ybuf[slot] = _pack_pairs(_dot(hid, wd_ref[...]))
            store(j, slot).start()
            return carry

        lax.fori_loop(0, n_blk, block, 0)

        @pl.when(n_blk >= 2)
        def _():
            store(n_blk - 2, n_blk & 1).wait()

        store(n_blk - 1, (n_blk - 1) & 1).wait()


def _experts(seg_first, seg_blocks, xs, w_gate, w_up, w_down):
    n_slots, half = xs.shape
    D = 2 * half
    w_blk = lambda e, first, blocks: (e, 0, 0)
    grid_spec = pltpu.PrefetchScalarGridSpec(
        num_scalar_prefetch=2,
        grid=(N_EXPERTS,),
        in_specs=[
            pl.BlockSpec(memory_space=pl.ANY),
            pl.BlockSpec((1, D, D_EXPERT), w_blk),
            pl.BlockSpec((1, D, D_EXPERT), w_blk),
            pl.BlockSpec((1, D_EXPERT, D), w_blk),
        ],
        out_specs=pl.BlockSpec(memory_space=pl.ANY),
        scratch_shapes=[
            pltpu.VMEM((D, D_EXPERT), BF16), pltpu.VMEM((D, D_EXPERT), BF16), pltpu.VMEM((D_EXPERT, D), BF16),
            pltpu.VMEM((2, EXPERT_ROWS, half), jnp.uint32), pltpu.VMEM((2, EXPERT_ROWS, half), jnp.uint32),
            pltpu.SemaphoreType.DMA((2,)), pltpu.SemaphoreType.DMA((2,)),
        ],
    )
    return pl.pallas_call(
        _expert_kernel,
        grid_spec=grid_spec,
        out_shape=jax.ShapeDtypeStruct((n_slots, half), jnp.uint32),
        compiler_params=pltpu.CompilerParams(
            dimension_semantics=("arbitrary",), vmem_limit_bytes=VMEM_LIMIT),
        name="experts",
    )(seg_first, seg_blocks, xs, w_gate, w_up, w_down)


def _combine_kernel(y0_ref, y1_ref, h_ref, gates_ref, nfin_ref, *rest):
    o_ref = rest[-1]
    g0 = gates_ref[0].T
    g1 = gates_ref[1].T
    for r in range(gates_ref.shape[1]):
        rs = slice(r * LANES, (r + 1) * LANES)
        h = (h_ref[rs].astype(F32) + g0[:, r:r + 1] * _unpack_pairs(y0_ref[rs])
             + g1[:, r:r + 1] * _unpack_pairs(y1_ref[rs]))
        o_ref[rs] = _rms(h, nfin_ref[...])


def _combine(y01, h2, gates, norm_final, out_prev, row0, total_rows):
    T, D = h2.shape
    n_t = T // MOVE_ROWS
    in_specs = [
        pl.BlockSpec((MOVE_ROWS, D // 2), lambda i: (i, 0)),
        pl.BlockSpec((MOVE_ROWS, D // 2), lambda i: (i + n_t, 0)),
        pl.BlockSpec((MOVE_ROWS, D), lambda i: (i, 0)),
        pl.BlockSpec((2, MOVE_ROWS // LANES, LANES), lambda i: (0, i, 0)),
        pl.BlockSpec((1, D), lambda i: (0, 0)),
    ]
    args = [y01, y01, h2, gates, norm_final]
    aliases = {}
    if out_prev is not None:
        in_specs.append(pl.BlockSpec(memory_space=pl.ANY))
        args.append(out_prev)
        aliases = {len(args) - 1: 0}
    return pl.pallas_call(
        _combine_kernel,
        grid=(n_t,),
        in_specs=in_specs,
        out_specs=pl.BlockSpec((MOVE_ROWS, D), lambda i: (i + row0 // MOVE_ROWS, 0)),
        out_shape=jax.ShapeDtypeStruct((total_rows, D), F32),
        input_output_aliases=aliases,
        compiler_params=pltpu.CompilerParams(
            dimension_semantics=("arbitrary",), vmem_limit_bytes=VMEM_LIMIT),
        name="combine",
    )(*args)


def kernel(x, mem, norm_mix, w_in, gmlp_ln, gmlp_w_spatial, gmlp_b_spatial, gmlp_beta, hgrn_lb_logits, hgrn_out_gain, w_out, norm_xattn, norm_mem, w_xq, w_xkv, w_xo, norm_ffn, w_router_group, b_router_group, w_router_expert, b_router_expert, w_expert_gate, w_expert_up, w_expert_down, norm_final):
    B, S, D = x.shape
    T = B * S
    depth = w_in.shape[0]
    assert depth == 1 and hgrn_lb_logits.shape[0] == 2
    assert D == D_MODEL and mem.shape[1] == N_MEM and w_in.shape[2] == D_IN_PROJ
    assert S % MIX_ROWS == 0 and S % ATT_ROWS == 0 and B % MEMKV_BATCHES == 0
    l = 0
    row = lambda p: p.reshape(1, -1)

    h1 = _mixer(x, row(norm_mix[l]), w_in[l], row(gmlp_ln[l]), gmlp_w_spatial[l],
                gmlp_b_spatial[l].T, row(gmlp_beta[l]), hgrn_lb_logits, row(hgrn_out_gain[l]),
                w_out[l])
    qk_mem, vo_mem = _memkv(mem, row(norm_mem[l]), w_xkv[l], w_xq[l], w_xo[l])

    w_router = jnp.concatenate([w_router_group[l].T, w_router_expert[l].T], axis=0)
    w_router = jnp.pad(w_router, ((0, ROUTER_ROWS - w_router.shape[0]), (0, 0)))
    w_router = w_router.astype(BF16)
    bias = jnp.concatenate([b_router_group[l], b_router_expert[l]]).astype(F32)

    assert sum(PART_BATCHES) == B
    out = None
    b0 = 0
    for b_part in PART_BATCHES:
        t_part = b_part * S
        n_blocks = (2 * t_part) // EXPERT_ROWS + N_EXPERTS
        h2, xn, logits = _xattn(h1, row(norm_xattn[l]), qk_mem, vo_mem, row(norm_ffn[l]), w_router, b0, b_part)
        ids, gates, rank, counts = _route(bias, logits)

        counts = counts[::ROUTE_SUB, 0].astype(jnp.int32)
        dest, seg_first, seg_blocks = _dest(counts, ids, rank)
        dest = dest.reshape(2, t_part)
        xs = _dispatch(dest[0], dest[1], xn, n_blocks * EXPERT_ROWS)
        yb = _experts(seg_first, seg_blocks, xs, w_expert_gate[l], w_expert_up[l], w_expert_down[l])
        y01 = _gather_rows(yb, dest.reshape(2 * t_part))
        out = _combine(y01, h2.reshape(t_part, D), gates, row(norm_final), out, b0 * S, T)
        b0 += b_part
    return out.reshape(B, S, D)
```

```python
import jax
import jax.numpy as jnp
from jax import lax
from jax.experimental import pallas as pl
from jax.experimental.pallas import tpu as pltpu
from jax.experimental.pallas import tpu_sc as plsc

F32 = jnp.float32
BF16 = jnp.bfloat16
EPS = 1e-6

D_MODEL = 1024
D_GMLP = 512
GMLP_GROUPS = 4
GMLP_CHUNK = 128
D_HGRN = 512
HGRN_HEADS = 4
HGRN_DK = 128
HGRN_CHUNK = 64
D_IN_PROJ = 2 * D_GMLP + 4 * D_HGRN
N_MEM = 256
XATTN_HEADS = 4
XATTN_HEAD_DIM = D_MODEL // XATTN_HEADS
N_GROUPS = 4
EXPERTS_PER_GROUP = 8
N_EXPERTS = N_GROUPS * EXPERTS_PER_GROUP
D_EXPERT = 512

LANES = 128
MIX_ROWS = 1024
MIX_CHAIN_ROWS = 256


def _mix_schedule(n_chains):
    order = [("in", 0), ("gmlp", 0), ("factors", 0)]
    for k in range(n_chains):
        more = k + 1 < n_chains
        order += [("in", k + 1)] * more + [("local", k), ("recurrence", k)] + [("gmlp", k + 1)] * more
        order += [("out", k)] + [("factors", k + 1)] * more
    return tuple(order)


MIX_SCHEDULE = _mix_schedule(MIX_ROWS // MIX_CHAIN_ROWS)
MEMKV_BATCHES = 2
ATT_ROWS = 1024
WEIGHT_STAGE_ROWS = 128
ROUTER_ROWS = 40
ROUTE_SUB = 16
EXPERT_ROWS = 512
MOVE_ROWS = 1024
PART_BATCHES = (20, 12)
SC_WINDOW = 64
SC_CORES = 2
SC_SUBCORES = 16
_SC_WORKERS = SC_CORES * SC_SUBCORES
VMEM_LIMIT = 48 * 1024 * 1024


def _rms(x, gain):
    return x * lax.rsqrt(jnp.mean(x * x, axis=-1, keepdims=True) + EPS) * gain


def _dot(a, b):
    return jnp.dot(a, b, preferred_element_type=F32)


def _dot_nt(a, b):
    return lax.dot_general(a, b, (((1,), (1,)), ((), ())), preferred_element_type=F32)


def _dot_tn(a, b):
    return lax.dot_general(a, b, (((0,), (0,)), ((), ())), preferred_element_type=F32)


def _gelu(x):
    return 0.5 * x * (1.0 + jnp.tanh(0.7978845608028654 * (x + 0.044715 * (x * x * x))))


def _sigmoid(x):
    return 1.0 / (1.0 + jnp.exp(-x))


def _block_id(idx, size):
    assert size & (size - 1) == 0
    return lax.shift_right_logical(idx, size.bit_length() - 1)


def _stage_weight(w_hbm, w_bf16, stage_ref, sem):
    rows = stage_ref.shape[0]
    for k in range(w_hbm.shape[0] // rows):
        copy = pltpu.make_async_copy(w_hbm.at[pl.ds(k * rows, rows)], stage_ref, sem)
        copy.start()
        copy.wait()
        w_bf16[k * rows:(k + 1) * rows, :] = stage_ref[...].astype(BF16)


def _weight_scratch(k, n):
    return [pltpu.VMEM((k, n), BF16), pltpu.VMEM((WEIGHT_STAGE_ROWS, n), F32)]


_HIGH_HALF = 0xFFFF0000


def _pack_pairs(x):
    c = x.shape[1] // 2
    bits = lax.bitcast_convert_type(x.astype(BF16).astype(F32), jnp.uint32)
    return (bits[:, c:] & jnp.uint32(_HIGH_HALF)) | lax.shift_right_logical(bits[:, :c], jnp.uint32(16))


def _unpack_pairs(w):
    lo = lax.bitcast_convert_type(lax.shift_left(w, jnp.uint32(16)), F32)
    hi = lax.bitcast_convert_type(w & jnp.uint32(_HIGH_HALF), F32)
    return jnp.concatenate([lo, hi], axis=1)


def _mixer_kernel(x_ref, nmix_ref, win_hbm, gln_ref, ws_ref, bst_ref, beta_ref, lbl_ref, og_ref,
                  wout_hbm, o_ref, proj_ref, ycat_ref, state_ref, win_ref, win_stage, wout_ref, wout_stage,
                  sem):
    @pl.when((pl.program_id(0) == 0) & (pl.program_id(1) == 0))
    def _():
        _stage_weight(win_hbm, win_ref, win_stage, sem)
        _stage_weight(wout_hbm, wout_ref, wout_stage, sem)

    @pl.when(pl.program_id(1) == 0)
    def _():
        state_ref[...] = jnp.zeros_like(state_ref)

    n = MIX_CHAIN_ROWS
    r_i = lax.broadcasted_iota(jnp.int32, (GMLP_CHUNK, GMLP_CHUNK), 0)
    c_i = lax.broadcasted_iota(jnp.int32, (GMLP_CHUNK, GMLP_CHUNK), 1)
    causal = c_i <= r_i
    w_tril = [jnp.where(causal, ws_ref[g], 0.0).astype(BF16) for g in range(GMLP_GROUPS)]
    lbl = lbl_ref[...]
    e_lb = jnp.exp(lbl - jnp.max(lbl, axis=0, keepdims=True))
    lb = e_lb[0:1] / jnp.sum(e_lb, axis=0, keepdims=True)
    rr = lax.broadcasted_iota(jnp.int32, (n, n), 0)
    cc = lax.broadcasted_iota(jnp.int32, (n, n), 1)
    tri = jnp.where((_block_id(rr, HGRN_CHUNK) == _block_id(cc, HGRN_CHUNK)) & (cc <= rr),
                    1.0, 0.0).astype(BF16)
    r64 = lax.broadcasted_iota(jnp.int32, (HGRN_CHUNK, HGRN_CHUNK), 0)
    c64 = lax.broadcasted_iota(jnp.int32, (HGRN_CHUNK, HGRN_CHUNK), 1)
    causal64 = c64 <= r64
    base = 2 * D_GMLP

    n_chunks = n // HGRN_CHUNK
    chains = range(x_ref.shape[1] // n)
    env = {ch: {} for ch in chains}

    def rows(ch):
        return slice(ch * n, (ch + 1) * n)

    def in_proj(ch):
        a = _rms(x_ref[0, rows(ch)], nmix_ref[...]).astype(BF16)
        proj_ref[rows(ch)] = _dot(a, win_ref[...])

    def gmlp(ch):
        u = _gelu(proj_ref[rows(ch), 0:D_GMLP])
        v = _gelu(proj_ref[rows(ch), D_GMLP:2 * D_GMLP])
        vc = v - jnp.mean(v, axis=-1, keepdims=True)
        vn = (vc * lax.rsqrt(jnp.mean(vc * vc, axis=-1, keepdims=True) + EPS) * gln_ref[...]).astype(BF16)
        z_rows = []
        for c in range(n // GMLP_CHUNK):
            z_cols = []
            for g in range(GMLP_GROUPS):
                vg = vn[c * GMLP_CHUNK:(c + 1) * GMLP_CHUNK, g * LANES:(g + 1) * LANES]
                z_cols.append(_dot(w_tril[g], vg) + bst_ref[:, g:g + 1])
            z_rows.append(jnp.concatenate(z_cols, axis=1))
        z = jnp.concatenate(z_rows, axis=0)
        ycat_ref[rows(ch), 0:D_GMLP] = _rms(u * z, beta_ref[...]).astype(BF16)

    def hgrn_factors(ch):
        e = env[ch]
        f = lb + (1.0 - lb) * _sigmoid(proj_ref[rows(ch), base + D_HGRN:base + 2 * D_HGRN])
        log_f = jnp.log(f)
        lf_hi = log_f.astype(BF16)
        lf_lo = (log_f - lf_hi.astype(F32)).astype(BF16)
        b_all = _dot(tri, lf_hi) + _dot(tri, lf_lo)
        bl_rows = [b_all[c * HGRN_CHUNK + HGRN_CHUNK - 1:(c + 1) * HGRN_CHUNK] for c in range(n_chunks)]
        bl_all = jnp.concatenate([jnp.broadcast_to(r, (HGRN_CHUNK, D_HGRN)) for r in bl_rows], axis=0)
        q_all = proj_ref[rows(ch), base:base + D_HGRN]
        k_all = 1.0 - f
        e["qd"] = (q_all * _sigmoid(q_all) * jnp.exp(b_all)).astype(BF16)
        e["ki"] = (k_all * jnp.exp(-b_all)).astype(BF16)
        e["kte"] = (k_all * jnp.exp(bl_all - b_all)).astype(BF16)
        e["v"] = proj_ref[rows(ch), base + 2 * D_HGRN:base + 3 * D_HGRN].astype(BF16)
        e["decay"] = [jnp.exp(r) for r in bl_rows]

    def hgrn_local(ch):
        e = env[ch]
        e["o_intra"], e["d_state"] = {}, {}
        for c in range(n_chunks):
            rs = slice(c * HGRN_CHUNK, (c + 1) * HGRN_CHUNK)
            for h in range(HGRN_HEADS):
                cs = slice(h * HGRN_DK, (h + 1) * HGRN_DK)
                scores = jnp.where(causal64, _dot_nt(e["qd"][rs, cs], e["ki"][rs, cs]), 0.0).astype(BF16)
                e["o_intra"][c, h] = _dot(scores, e["v"][rs, cs])
                e["d_state"][c, h] = _dot_tn(e["v"][rs, cs], e["kte"][rs, cs])

    def hgrn_recurrence(ch):
        e = env[ch]
        for c in range(n_chunks):
            rs = slice(c * HGRN_CHUNK, (c + 1) * HGRN_CHUNK)
            ps = slice(ch * n + c * HGRN_CHUNK, ch * n + (c + 1) * HGRN_CHUNK)
            g_c = proj_ref[ps, base + 3 * D_HGRN:base + 4 * D_HGRN]
            gate = og_ref[...] * (g_c * _sigmoid(g_c))
            for h in range(HGRN_HEADS):
                cs = slice(h * HGRN_DK, (h + 1) * HGRN_DK)
                st = state_ref[h]
                o = e["o_intra"][c, h] + _dot_nt(e["qd"][rs, cs], st.astype(BF16))
                state_ref[h] = st * e["decay"][c][:, cs] + e["d_state"][c, h]
                o = o * lax.rsqrt(jnp.mean(o * o, axis=-1, keepdims=True) + EPS)
                ycat_ref[ps, D_GMLP + h * HGRN_DK:D_GMLP + (h + 1) * HGRN_DK] = (o * gate[:, cs]).astype(BF16)

    def out_proj(ch):
        o_ref[0, rows(ch)] = x_ref[0, rows(ch)] + _dot(ycat_ref[rows(ch)], wout_ref[...])

    stages = {"in": in_proj, "gmlp": gmlp, "factors": hgrn_factors, "local": hgrn_local,
              "recurrence": hgrn_recurrence, "out": out_proj}
    for stage, ch in MIX_SCHEDULE:
        stages[stage](ch)


def _mixer(x, norm_mix, w_in, gmlp_ln, w_s, b_s_t, beta, lb_logits, out_gain, w_out):
    B, S, D = x.shape
    const2 = lambda b, s: (0, 0)
    return pl.pallas_call(
        _mixer_kernel,
        grid=(B, S // MIX_ROWS),
        in_specs=[
            pl.BlockSpec((1, MIX_ROWS, D), lambda b, s: (b, s, 0)),
            pl.BlockSpec((1, D), const2),
            pl.BlockSpec(memory_space=pl.ANY),
            pl.BlockSpec((1, D_GMLP), const2),
            pl.BlockSpec((GMLP_GROUPS, GMLP_CHUNK, GMLP_CHUNK), lambda b, s: (0, 0, 0)),
            pl.BlockSpec((GMLP_CHUNK, GMLP_GROUPS), const2),
            pl.BlockSpec((1, D_GMLP), const2),
            pl.BlockSpec(lb_logits.shape, const2),
            pl.BlockSpec((1, D_HGRN), const2),
            pl.BlockSpec(memory_space=pl.ANY),
        ],
        out_specs=pl.BlockSpec((1, MIX_ROWS, D), lambda b, s: (b, s, 0)),
        out_shape=jax.ShapeDtypeStruct((B, S, D), F32),
        scratch_shapes=[
            pltpu.VMEM((MIX_ROWS, D_IN_PROJ), F32),
            pltpu.VMEM((MIX_ROWS, D), BF16),
            pltpu.VMEM((HGRN_HEADS, HGRN_DK, HGRN_DK), F32),
        ] + _weight_scratch(D, D_IN_PROJ) + _weight_scratch(D, D) + [pltpu.SemaphoreType.DMA],
        compiler_params=pltpu.CompilerParams(
            dimension_semantics=("arbitrary", "arbitrary"), vmem_limit_bytes=VMEM_LIMIT),
        name="mixer",
    )(x, norm_mix, w_in, gmlp_ln, w_s, b_s_t, beta, lb_logits, out_gain, w_out)


def _memkv_kernel(mem_ref, nm_ref, wkv_hbm, wq_hbm, wo_hbm, qk_ref, vo_ref,
                  wkv_ref, wkv_stage, wq_ref, wq_stage, wo_ref, wo_stage, sem):
    @pl.when(pl.program_id(0) == 0)
    def _():
        _stage_weight(wkv_hbm, wkv_ref, wkv_stage, sem)
        _stage_weight(wq_hbm, wq_ref, wq_stage, sem)
        _stage_weight(wo_hbm, wo_ref, wo_stage, sem)

    nb, M, D = mem_ref.shape
    m = _rms(mem_ref[...].reshape(nb * M, D), nm_ref[...]).astype(BF16)
    kv = _dot(m, wkv_ref[...])
    k = kv[:, :D_MODEL].astype(BF16)
    v = kv[:, D_MODEL:].astype(BF16)
    for b in range(nb):
        rs = slice(b * M, (b + 1) * M)
        for hd in range(XATTN_HEADS):
            cs = slice(hd * XATTN_HEAD_DIM, (hd + 1) * XATTN_HEAD_DIM)
            ms = slice(hd * M, (hd + 1) * M)
            qk_ref[b, :, ms] = (_dot_nt(wq_ref[:, cs], k[rs, cs]) * (XATTN_HEAD_DIM ** -0.5)).astype(BF16)
            vo_ref[b, ms, :] = _dot(v[rs, cs], wo_ref[cs, :]).astype(BF16)


def _memkv(mem, norm_mem, w_kv, w_q, w_o):
    B, M, D = mem.shape
    out = jax.ShapeDtypeStruct((B, D, XATTN_HEADS * M), BF16), jax.ShapeDtypeStruct((B, XATTN_HEADS * M, D), BF16)
    return pl.pallas_call(
        _memkv_kernel,
        grid=(B // MEMKV_BATCHES,),
        in_specs=[
            pl.BlockSpec((MEMKV_BATCHES, M, D), lambda b: (b, 0, 0)),
            pl.BlockSpec((1, D), lambda b: (0, 0)),
            pl.BlockSpec(memory_space=pl.ANY),
            pl.BlockSpec(memory_space=pl.ANY),
            pl.BlockSpec(memory_space=pl.ANY),
        ],
        out_specs=[pl.BlockSpec((MEMKV_BATCHES, D, XATTN_HEADS * M), lambda b: (b, 0, 0)),
                   pl.BlockSpec((MEMKV_BATCHES, XATTN_HEADS * M, D), lambda b: (b, 0, 0))],
        out_shape=list(out),
        scratch_shapes=(_weight_scratch(D, 2 * D) + _weight_scratch(D, D) + _weight_scratch(D, D)
                        + [pltpu.SemaphoreType.DMA]),
        compiler_params=pltpu.CompilerParams(
            dimension_semantics=("arbitrary",), vmem_limit_bytes=VMEM_LIMIT),
        name="memkv",
    )(mem, norm_mem, w_kv, w_q, w_o)


def _xattn_kernel(h_ref, nx_ref, qk_ref, vo_ref, nf_ref, wr_ref, h2_ref, xn_ref, lg_ref, att_ref):
    h = h_ref[0]
    hn = _rms(h, nx_ref[...]).astype(BF16)
    scores = _dot(hn, qk_ref[0])
    n_mem = qk_ref.shape[2] // XATTN_HEADS
    for hd in range(XATTN_HEADS):
        ms = slice(hd * n_mem, (hd + 1) * n_mem)
        s = scores[:, ms]
        p = jnp.exp(s - jnp.max(s, axis=-1, keepdims=True))
        att_ref[:, ms] = (p / jnp.sum(p, axis=-1, keepdims=True)).astype(BF16)
    h2 = h + _dot(att_ref[...], vo_ref[0])
    h2_ref[0] = h2.astype(h2_ref.dtype)
    xn = _rms(h2, nf_ref[...])
    xn_ref[...] = _pack_pairs(xn)
    lg = _dot_nt(wr_ref[...], xn.astype(BF16))
    for j in range(ATT_ROWS // LANES):
        lg_ref[:, j, :] = lg[:, j * LANES:(j + 1) * LANES]


def _xattn(h1, norm_x, qk_mem, vo_mem, norm_ffn, w_router, batch0, batches):
    _, S, D = h1.shape
    n_s = S // ATT_ROWS
    n_att = qk_mem.shape[2]
    const2 = lambda b, s: (0, 0)
    return pl.pallas_call(
        _xattn_kernel,
        grid=(batches, n_s),
        in_specs=[
            pl.BlockSpec((1, ATT_ROWS, D), lambda b, s: (b + batch0, s, 0)),
            pl.BlockSpec((1, D), const2),
            pl.BlockSpec((1, D, n_att), lambda b, s: (b + batch0, 0, 0)),
            pl.BlockSpec((1, n_att, D), lambda b, s: (b + batch0, 0, 0)),
            pl.BlockSpec((1, D), const2),
            pl.BlockSpec((ROUTER_ROWS, D), const2),
        ],
        out_specs=[
            pl.BlockSpec((1, ATT_ROWS, D), lambda b, s: (b, s, 0)),
            pl.BlockSpec((ATT_ROWS, D // 2), lambda b, s: (b * n_s + s, 0)),
            pl.BlockSpec((ROUTER_ROWS, ATT_ROWS // LANES, LANES), lambda b, s: (0, b * n_s + s, 0)),
        ],
        out_shape=[
            jax.ShapeDtypeStruct((batches, S, D), BF16),
            jax.ShapeDtypeStruct((batches * S, D // 2), jnp.uint32),
            jax.ShapeDtypeStruct((ROUTER_ROWS, batches * S // LANES, LANES), F32),
        ],
        scratch_shapes=[pltpu.VMEM((ATT_ROWS, n_att), BF16)],
        compiler_params=pltpu.CompilerParams(
            dimension_semantics=("arbitrary", "arbitrary"), vmem_limit_bytes=VMEM_LIMIT),
        name="xattn",
    )(h1, norm_x, qk_mem, vo_mem, norm_ffn, w_router)


def _route_kernel(bias_ref, lg_ref, ids_ref, gates_ref, rank_ref, cnt_ref, base_ref):
    sub = lg_ref.shape[1]

    @pl.when(pl.program_id(0) == 0)
    def _():
        base_ref[...] = jnp.zeros_like(base_ref)

    best = lg_ref[0] + bias_ref[0]
    gl = [best]
    sel = jnp.zeros(best.shape, jnp.int32)
    for g in range(1, N_GROUPS):
        cur = lg_ref[g] + bias_ref[g]
        gl.append(cur)
        better = cur > best
        best = jnp.where(better, cur, best)
        sel = jnp.where(better, g, sel)
    denom = jnp.exp(gl[0] - best)
    for g in range(1, N_GROUPS):
        denom = denom + jnp.exp(gl[g] - best)
    g_gate = 1.0 / denom

    ev = []
    for j in range(EXPERTS_PER_GROUP):
        val = lg_ref[N_GROUPS + j] + bias_ref[N_GROUPS + j]
        for g in range(1, N_GROUPS):
            e = g * EXPERTS_PER_GROUP + j
            val = jnp.where(sel == g, lg_ref[N_GROUPS + e] + bias_ref[N_GROUPS + e], val)
        ev.append(val)
    v1, i1 = ev[0], jnp.zeros(best.shape, jnp.int32)
    for j in range(1, EXPERTS_PER_GROUP):
        better = ev[j] > v1
        v1 = jnp.where(better, ev[j], v1)
        i1 = jnp.where(better, j, i1)
    rest = [jnp.where(i1 == j, -jnp.inf, ev[j]) for j in range(EXPERTS_PER_GROUP)]
    v2, i2 = rest[0], jnp.zeros(best.shape, jnp.int32)
    for j in range(1, EXPERTS_PER_GROUP):
        better = rest[j] > v2
        v2 = jnp.where(better, rest[j], v2)
        i2 = jnp.where(better, j, i2)
    e2 = jnp.exp(v2 - v1)
    inv = 1.0 / (1.0 + e2)
    id1 = sel * EXPERTS_PER_GROUP + i1
    id2 = sel * EXPERTS_PER_GROUP + i2
    ids_ref[0] = id1
    ids_ref[1] = id2
    gates_ref[0] = inv * g_gate
    gates_ref[1] = e2 * inv * g_gate

    member = jnp.concatenate(
        [jnp.where((id1 == e) | (id2 == e), 1.0, 0.0) for e in range(N_EXPERTS)], axis=0).astype(BF16)
    n = N_EXPERTS * sub
    li = lax.broadcasted_iota(jnp.int32, (LANES, LANES), 0)
    lj = lax.broadcasted_iota(jnp.int32, (LANES, LANES), 1)
    before_lane = jnp.where(li < lj, 1.0, 0.0).astype(BF16)
    ones = jnp.ones((LANES, LANES), BF16)
    ri = lax.broadcasted_iota(jnp.int32, (n, n), 0)
    rj = lax.broadcasted_iota(jnp.int32, (n, n), 1)
    same = _block_id(ri, sub) == _block_id(rj, sub)
    before_row = jnp.where(same & (rj < ri), 1.0, 0.0).astype(BF16)
    all_row = jnp.where(same, 1.0, 0.0).astype(BF16)
    in_row = _dot(member, before_lane)
    prev_rows = _dot(_dot(before_row, member).astype(BF16), ones)
    total = _dot(_dot(all_row, member).astype(BF16), ones)
    base = base_ref[...]
    pos = base + prev_rows + in_row
    r1 = jnp.zeros(best.shape, F32)
    r2 = jnp.zeros(best.shape, F32)
    for e in range(N_EXPERTS):
        pe = pos[e * sub:(e + 1) * sub]
        r1 = jnp.where(id1 == e, pe, r1)
        r2 = jnp.where(id2 == e, pe, r2)
    rank_ref[0] = r1.astype(jnp.int32)
    rank_ref[1] = r2.astype(jnp.int32)
    base_ref[...] = base + total
    cnt_ref[...] = base + total


def _route(bias, logits3):
    rp, n_sub, _ = logits3.shape
    blk = lambda i: (0, i, 0)
    pair_i = jax.ShapeDtypeStruct((2, n_sub, LANES), jnp.int32)
    return pl.pallas_call(
        _route_kernel,
        grid=(n_sub // ROUTE_SUB,),
        in_specs=[
            pl.BlockSpec(memory_space=pltpu.SMEM),
            pl.BlockSpec((rp, ROUTE_SUB, LANES), blk),
        ],
        out_specs=[
            pl.BlockSpec((2, ROUTE_SUB, LANES), blk),
            pl.BlockSpec((2, ROUTE_SUB, LANES), blk),
            pl.BlockSpec((2, ROUTE_SUB, LANES), blk),
            pl.BlockSpec((N_EXPERTS * ROUTE_SUB, LANES), lambda i: (0, 0)),
        ],
        out_shape=[
            pair_i,
            jax.ShapeDtypeStruct((2, n_sub, LANES), F32),
            pair_i,
            jax.ShapeDtypeStruct((N_EXPERTS * ROUTE_SUB, LANES), F32),
        ],
        scratch_shapes=[pltpu.VMEM((N_EXPERTS * ROUTE_SUB, LANES), F32)],
        compiler_params=pltpu.CompilerParams(
            dimension_semantics=("arbitrary",), vmem_limit_bytes=VMEM_LIMIT),
        name="route",
    )(bias, logits3)


def _dest_kernel(cnt_ref, ids_ref, rank_ref, dest_ref, be_ref, nb_ref, slot_ref, next_ref, start_ref, succ_ref):
    n_blocks = be_ref.shape[0]
    shift = EXPERT_ROWS.bit_length() - 1
    assert 1 << shift == EXPERT_ROWS

    @pl.when(pl.program_id(0) == 0)
    def _():
        def tail(i, carry):
            be_ref[i] = N_EXPERTS - 1
            slot_ref[i] = 0
            next_ref[i] = -1
            return carry

        lax.fori_loop(0, n_blocks, tail, 0)

        def successor(k, nxt):
            e = N_EXPERTS - 1 - k
            succ_ref[e] = nxt
            return jnp.where(cnt_ref[e] > 0, e, nxt)

        lax.fori_loop(0, N_EXPERTS, successor, jnp.int32(-1))

        def segment(e, carry):
            block0, ordinal = carry
            blocks = lax.shift_right_logical(cnt_ref[e] + (EXPERT_ROWS - 1), shift)
            start_ref[e] = lax.shift_left(block0, shift)

            def fill(j, c):
                be_ref[block0 + j] = e
                slot_ref[block0 + j] = ordinal & 1
                next_ref[block0 + j] = succ_ref[e]
                return c

            lax.fori_loop(0, blocks, fill, 0)
            return block0 + blocks, ordinal + (cnt_ref[e] > 0).astype(jnp.int32)

        used, _ = lax.fori_loop(0, N_EXPERTS, segment, (jnp.int32(0), jnp.int32(0)))
        nb_ref[0] = used

    ids = ids_ref[...]
    off = jnp.zeros(ids.shape, jnp.int32)
    for e in range(N_EXPERTS):
        off = jnp.where(ids == e, start_ref[e], off)
    dest_ref[...] = rank_ref[...] + off


def _dest(counts, ids, rank, n_blocks):
    _, n_sub, _ = ids.shape
    blk = pl.BlockSpec((2, ROUTE_SUB, LANES), lambda i: (0, i, 0))
    smem = pl.BlockSpec(memory_space=pltpu.SMEM)
    table = jax.ShapeDtypeStruct((n_blocks,), jnp.int32)
    return pl.pallas_call(
        _dest_kernel,
        grid=(n_sub // ROUTE_SUB,),
        in_specs=[smem, blk, blk],
        out_specs=[blk, smem, smem, smem, smem],
        out_shape=[jax.ShapeDtypeStruct(ids.shape, jnp.int32), table, jax.ShapeDtypeStruct((1,), jnp.int32),
                   table, table],
        scratch_shapes=[pltpu.SMEM((N_EXPERTS,), jnp.int32), pltpu.SMEM((N_EXPERTS,), jnp.int32)],
        compiler_params=pltpu.CompilerParams(dimension_semantics=("arbitrary",)),
        name="dest",
    )(counts, ids, rank)


def _sc_mesh():
    return plsc.VectorSubcoreMesh(core_axis_name="core", subcore_axis_name="subcore")


def _sc_worker(rows_total):
    rows = rows_total // _SC_WORKERS
    wid = lax.axis_index("core") * SC_SUBCORES + lax.axis_index("subcore")
    return wid * rows, rows


def _dispatch(dest0, dest1, xn, n_slots):
    T, D = xn.shape
    W = SC_WINDOW
    rows = T // _SC_WORKERS
    assert T % (_SC_WORKERS * 2 * W) == 0

    @pl.kernel(out_type=jax.ShapeDtypeStruct((n_slots, D), xn.dtype), mesh=_sc_mesh(),
               scratch_types=[pltpu.VMEM((rows,), jnp.int32), pltpu.VMEM((rows,), jnp.int32),
                              pltpu.VMEM((2, W, D), xn.dtype),
                              pltpu.SemaphoreType.DMA((2,)), pltpu.SemaphoreType.DMA((2,))])
    def scatter_rows(x_hbm, d0_hbm, d1_hbm, xs_hbm, d0_v, d1_v, buf, in_sem, out_sem):
        base, _ = _sc_worker(T)
        pltpu.sync_copy(d0_hbm.at[pl.ds(base, rows)], d0_v)
        pltpu.sync_copy(d1_hbm.at[pl.ds(base, rows)], d1_v)

        def load(w, slot):
            return pltpu.make_async_copy(x_hbm.at[pl.ds(base + w * W, W)], buf.at[slot], in_sem.at[slot])

        def store(w, slot, d_v):
            return pltpu.make_async_copy(buf.at[slot], xs_hbm.at[d_v.at[pl.ds(w * W, W)]], out_sem.at[slot])

        def step(w, slot):
            load(w, slot).wait()
            store(w, slot, d0_v).start()
            store(w, slot, d1_v).start()
            store(w, slot, d0_v).wait()
            store(w, slot, d1_v).wait()

        n = rows // W
        load(0, 0).start()

        @pl.loop(0, n, step=2)
        def _(w):
            load(w + 1, 1).start()
            step(w, 0)

            @pl.when(w + 2 < n)
            def _():
                load(w + 2, 0).start()

            step(w + 1, 1)

    return scatter_rows(xn, dest0, dest1)


def _gather_rows(src, idx):
    M = idx.shape[0]
    D = src.shape[1]
    W = SC_WINDOW
    rows = M // _SC_WORKERS
    assert M % (_SC_WORKERS * 2 * W) == 0

    @pl.kernel(out_type=jax.ShapeDtypeStruct((M, D), src.dtype), mesh=_sc_mesh(),
               scratch_types=[pltpu.VMEM((rows,), jnp.int32), pltpu.VMEM((2, W, D), src.dtype),
                              pltpu.SemaphoreType.DMA((2,)), pltpu.SemaphoreType.DMA((2,))])
    def gather_rows(src_hbm, i_hbm, o_hbm, i_v, buf, in_sem, out_sem):
        base, _ = _sc_worker(M)
        pltpu.sync_copy(i_hbm.at[pl.ds(base, rows)], i_v)

        def load(w, slot):
            return pltpu.make_async_copy(src_hbm.at[i_v.at[pl.ds(w * W, W)]], buf.at[slot], in_sem.at[slot])

        def store(w, slot):
            return pltpu.make_async_copy(buf.at[slot], o_hbm.at[pl.ds(base + w * W, W)], out_sem.at[slot])

        n = rows // W
        load(0, 0).start()

        @pl.loop(0, n, step=2)
        def _(w):
            @pl.when(w > 0)
            def _():
                store(w - 1, 1).wait()

            load(w + 1, 1).start()
            load(w, 0).wait()
            store(w, 0).start()
            store(w, 0).wait()

            @pl.when(w + 2 < n)
            def _():
                load(w + 2, 0).start()

            load(w + 1, 1).wait()
            store(w + 1, 1).start()

        store(n - 1, 1).wait()

    return gather_rows(src, idx)


def _expert_kernel(be_ref, nb_ref, slot_ref, next_ref, x_ref, wg_hbm, wu_hbm, wd_hbm, y_ref,
                   wg_f32, wu_f32, wd_f32, wg_ref, wu_ref, wd_ref, sem):
    i = pl.program_id(0)
    used = i < nb_ref[0]
    new_expert = (i == 0) | (be_ref[i] != be_ref[jnp.maximum(i - 1, 0)])

    def fetch(e, slot):
        pairs = ((wg_hbm, wg_f32), (wu_hbm, wu_f32), (wd_hbm, wd_f32))
        return [pltpu.make_async_copy(w_hbm.at[e], w_f32.at[slot], sem.at[slot, k])
                for k, (w_hbm, w_f32) in enumerate(pairs)]

    @pl.when(used & (i == 0))
    def _():
        for copy in fetch(be_ref[0], slot_ref[0]):
            copy.start()

    @pl.when(used & new_expert)
    def _():
        slot = slot_ref[i]
        for copy in fetch(be_ref[i], slot):
            copy.wait()

        @pl.when(next_ref[i] >= 0)
        def _():
            for copy in fetch(next_ref[i], 1 - slot):
                copy.start()

        wg_ref[...] = wg_f32[slot].astype(BF16)
        wu_ref[...] = wu_f32[slot].astype(BF16)
        wd_ref[...] = wd_f32[slot].astype(BF16)

    @pl.when(used)
    def _():
        x = _unpack_pairs(x_ref[...]).astype(BF16)
        g = _dot(x, wg_ref[...])
        u = _dot(x, wu_ref[...])
        hid = (g * _sigmoid(g) * u).astype(BF16)
        y_ref[...] = _pack_pairs(_dot(hid, wd_ref[...]))

    @pl.when(jnp.logical_not(used))
    def _():
        y_ref[...] = jnp.zeros_like(y_ref)


def _experts(block_expert, n_used, block_slot, block_next, xs, w_gate, w_up, w_down):
    n_slots, half = xs.shape
    D = 2 * half
    n_blocks = n_slots // EXPERT_ROWS
    grid_spec = pltpu.PrefetchScalarGridSpec(
        num_scalar_prefetch=4,
        grid=(n_blocks,),
        in_specs=[
            pl.BlockSpec((EXPERT_ROWS, half), lambda i, be, nb, sl, nx: (jnp.minimum(i, nb[0] - 1), 0)),
            pl.BlockSpec(memory_space=pl.ANY),
            pl.BlockSpec(memory_space=pl.ANY),
            pl.BlockSpec(memory_space=pl.ANY),
        ],
        out_specs=pl.BlockSpec((EXPERT_ROWS, half), lambda i, be, nb, sl, nx: (i, 0)),
        scratch_shapes=[
            pltpu.VMEM((2, D, D_EXPERT), F32), pltpu.VMEM((2, D, D_EXPERT), F32),
            pltpu.VMEM((2, D_EXPERT, D), F32),
            pltpu.VMEM((D, D_EXPERT), BF16), pltpu.VMEM((D, D_EXPERT), BF16), pltpu.VMEM((D_EXPERT, D), BF16),
            pltpu.SemaphoreType.DMA((2, 3)),
        ],
    )
    return pl.pallas_call(
        _expert_kernel,
        grid_spec=grid_spec,
        out_shape=jax.ShapeDtypeStruct((n_slots, half), jnp.uint32),
        compiler_params=pltpu.CompilerParams(
            dimension_semantics=("arbitrary",), vmem_limit_bytes=VMEM_LIMIT),
        name="experts",
    )(block_expert, n_used, block_slot, block_next, xs, w_gate, w_up, w_down)


def _combine_kernel(y0_ref, y1_ref, h_ref, gates_ref, nfin_ref, *rest):
    o_ref = rest[-1]
    g0 = gates_ref[0].T
    g1 = gates_ref[1].T
    for r in range(gates_ref.shape[1]):
        rs = slice(r * LANES, (r + 1) * LANES)
        h = (h_ref[rs].astype(F32) + g0[:, r:r + 1] * _unpack_pairs(y0_ref[rs])
             + g1[:, r:r + 1] * _unpack_pairs(y1_ref[rs]))
        o_ref[rs] = _rms(h, nfin_ref[...])


def _combine(y01, h2, gates, norm_final, out_prev, row0, total_rows):
    T, D = h2.shape
    n_t = T // MOVE_ROWS
    in_specs = [
        pl.BlockSpec((MOVE_ROWS, D // 2), lambda i: (i, 0)),
        pl.BlockSpec((MOVE_ROWS, D // 2), lambda i: (i + n_t, 0)),
        pl.BlockSpec((MOVE_ROWS, D), lambda i: (i, 0)),
        pl.BlockSpec((2, MOVE_ROWS // LANES, LANES), lambda i: (0, i, 0)),
        pl.BlockSpec((1, D), lambda i: (0, 0)),
    ]
    args = [y01, y01, h2, gates, norm_final]
    aliases = {}
    if out_prev is not None:
        in_specs.append(pl.BlockSpec(memory_space=pl.ANY))
        args.append(out_prev)
        aliases = {len(args) - 1: 0}
    return pl.pallas_call(
        _combine_kernel,
        grid=(n_t,),
        in_specs=in_specs,
        out_specs=pl.BlockSpec((MOVE_ROWS, D), lambda i: (i + row0 // MOVE_ROWS, 0)),
        out_shape=jax.ShapeDtypeStruct((total_rows, D), F32),
        input_output_aliases=aliases,
        compiler_params=pltpu.CompilerParams(
            dimension_semantics=("arbitrary",), vmem_limit_bytes=VMEM_LIMIT),
        name="combine",
    )(*args)


def kernel(x, mem, norm_mix, w_in, gmlp_ln, gmlp_w_spatial, gmlp_b_spatial, gmlp_beta, hgrn_lb_logits, hgrn_out_gain, w_out, norm_xattn, norm_mem, w_xq, w_xkv, w_xo, norm_ffn, w_router_group, b_router_group, w_router_expert, b_router_expert, w_expert_gate, w_expert_up, w_expert_down, norm_final):
    B, S, D = x.shape
    T = B * S
    depth = w_in.shape[0]
    assert depth == 1 and hgrn_lb_logits.shape[0] == 2
    assert D == D_MODEL and mem.shape[1] == N_MEM and w_in.shape[2] == D_IN_PROJ
    assert S % MIX_ROWS == 0 and S % ATT_ROWS == 0 and B % MEMKV_BATCHES == 0
    l = 0
    row = lambda p: p.reshape(1, -1)

    h1 = _mixer(x, row(norm_mix[l]), w_in[l], row(gmlp_ln[l]), gmlp_w_spatial[l],
                gmlp_b_spatial[l].T, row(gmlp_beta[l]), hgrn_lb_logits, row(hgrn_out_gain[l]),
                w_out[l])
    qk_mem, vo_mem = _memkv(mem, row(norm_mem[l]), w_xkv[l], w_xq[l], w_xo[l])

    w_router = jnp.concatenate([w_router_group[l].T, w_router_expert[l].T], axis=0)
    w_router = jnp.pad(w_router, ((0, ROUTER_ROWS - w_router.shape[0]), (0, 0)))
    w_router = w_router.astype(BF16)
    bias = jnp.concatenate([b_router_group[l], b_router_expert[l]]).astype(F32)

    assert sum(PART_BATCHES) == B
    out = None
    b0 = 0
    for b_part in PART_BATCHES:
        t_part = b_part * S
        n_blocks = (2 * t_part) // EXPERT_ROWS + N_EXPERTS
        h2, xn, logits = _xattn(h1, row(norm_xattn[l]), qk_mem, vo_mem, row(norm_ffn[l]), w_router, b0, b_part)
        ids, gates, rank, counts = _route(bias, logits)

        counts = counts[::ROUTE_SUB, 0].astype(jnp.int32)
        dest, block_expert, n_used, block_slot, block_next = _dest(counts, ids, rank, n_blocks)
        dest = dest.reshape(2, t_part)
        xs = _dispatch(dest[0], dest[1], xn, n_blocks * EXPERT_ROWS)
        yb = _experts(block_expert, n_used, block_slot, block_next, xs,
                      w_expert_gate[l], w_expert_up[l], w_expert_down[l])
        y01 = _gather_rows(yb, dest.reshape(2 * t_part))
        out = _combine(y01, h2.reshape(t_part, D), gates, row(norm_final), out, b0 * S, T)
        b0 += b_part
    return out.reshape(B, S, D)
```

```python
import jax
import jax.numpy as jnp
from jax import lax
from jax.experimental import pallas as pl
from jax.experimental.pallas import tpu as pltpu
from jax.experimental.pallas import tpu_sc as plsc

F32 = jnp.float32
BF16 = jnp.bfloat16
EPS = 1e-6

D_MODEL = 1024
D_GMLP = 512
GMLP_GROUPS = 4
GMLP_CHUNK = 128
D_HGRN = 512
HGRN_HEADS = 4
HGRN_DK = 128
HGRN_CHUNK = 64
D_IN_PROJ = 2 * D_GMLP + 4 * D_HGRN
N_MEM = 256
XATTN_HEADS = 4
XATTN_HEAD_DIM = D_MODEL // XATTN_HEADS
N_GROUPS = 4
EXPERTS_PER_GROUP = 8
N_EXPERTS = N_GROUPS * EXPERTS_PER_GROUP
D_EXPERT = 512

LANES = 128
MIX_ROWS = 1024
MIX_CHAIN_ROWS = 256


def _mix_schedule(n_chains):
    order = [("in", 0), ("gmlp", 0), ("factors", 0)]
    for k in range(n_chains):
        more = k + 1 < n_chains
        order += [("in", k + 1)] * more + [("local", k), ("recurrence", k)] + [("gmlp", k + 1)] * more
        order += [("out", k)] + [("factors", k + 1)] * more
    return tuple(order)


MIX_SCHEDULE = _mix_schedule(MIX_ROWS // MIX_CHAIN_ROWS)
MEMKV_BATCHES = 2
ATT_ROWS = 1024
WEIGHT_STAGE_ROWS = 128
ROUTER_ROWS = 40
ROUTE_SUB = 16
EXPERT_ROWS = 512
MOVE_ROWS = 1024
PART_BATCHES = (12, 12, 8)
SC_WINDOW = 64
SC_CORES = 2
SC_SUBCORES = 16
_SC_WORKERS = SC_CORES * SC_SUBCORES
VMEM_LIMIT = 48 * 1024 * 1024


def _rms(x, gain):
    return x * lax.rsqrt(jnp.mean(x * x, axis=-1, keepdims=True) + EPS) * gain


def _dot(a, b):
    return jnp.dot(a, b, preferred_element_type=F32)


def _dot_nt(a, b):
    return lax.dot_general(a, b, (((1,), (1,)), ((), ())), preferred_element_type=F32)


def _dot_tn(a, b):
    return lax.dot_general(a, b, (((0,), (0,)), ((), ())), preferred_element_type=F32)


def _gelu(x):
    return 0.5 * x * (1.0 + jnp.tanh(0.7978845608028654 * (x + 0.044715 * (x * x * x))))


def _sigmoid(x):
    return 1.0 / (1.0 + jnp.exp(-x))


def _block_id(idx, size):
    assert size & (size - 1) == 0
    return lax.shift_right_logical(idx, size.bit_length() - 1)


def _stage_weight(w_hbm, w_bf16, stage_ref, sem):
    rows = stage_ref.shape[0]
    for k in range(w_hbm.shape[0] // rows):
        copy = pltpu.make_async_copy(w_hbm.at[pl.ds(k * rows, rows)], stage_ref, sem)
        copy.start()
        copy.wait()
        w_bf16[k * rows:(k + 1) * rows, :] = stage_ref[...].astype(BF16)


def _weight_scratch(k, n):
    return [pltpu.VMEM((k, n), BF16), pltpu.VMEM((WEIGHT_STAGE_ROWS, n), F32)]


_HIGH_HALF = 0xFFFF0000


def _pack_pairs(x):
    c = x.shape[1] // 2
    bits = lax.bitcast_convert_type(x.astype(BF16).astype(F32), jnp.uint32)
    return (bits[:, c:] & jnp.uint32(_HIGH_HALF)) | lax.shift_right_logical(bits[:, :c], jnp.uint32(16))


def _unpack_pairs(w):
    lo = lax.bitcast_convert_type(lax.shift_left(w, jnp.uint32(16)), F32)
    hi = lax.bitcast_convert_type(w & jnp.uint32(_HIGH_HALF), F32)
    return jnp.concatenate([lo, hi], axis=1)


def _mixer_kernel(x_ref, nmix_ref, win_hbm, gln_ref, ws_ref, bst_ref, beta_ref, lbl_ref, og_ref,
                  wout_hbm, o_ref, proj_ref, ycat_ref, state_ref, win_ref, win_stage, wout_ref, wout_stage,
                  sem):
    @pl.when((pl.program_id(0) == 0) & (pl.program_id(1) == 0))
    def _():
        _stage_weight(win_hbm, win_ref, win_stage, sem)
        _stage_weight(wout_hbm, wout_ref, wout_stage, sem)

    @pl.when(pl.program_id(1) == 0)
    def _():
        state_ref[...] = jnp.zeros_like(state_ref)

    n = MIX_CHAIN_ROWS
    r_i = lax.broadcasted_iota(jnp.int32, (GMLP_CHUNK, GMLP_CHUNK), 0)
    c_i = lax.broadcasted_iota(jnp.int32, (GMLP_CHUNK, GMLP_CHUNK), 1)
    causal = c_i <= r_i
    w_tril = [jnp.where(causal, ws_ref[g], 0.0).astype(BF16) for g in range(GMLP_GROUPS)]
    lbl = lbl_ref[...]
    e_lb = jnp.exp(lbl - jnp.max(lbl, axis=0, keepdims=True))
    lb = e_lb[0:1] / jnp.sum(e_lb, axis=0, keepdims=True)
    rr = lax.broadcasted_iota(jnp.int32, (n, n), 0)
    cc = lax.broadcasted_iota(jnp.int32, (n, n), 1)
    tri = jnp.where((_block_id(rr, HGRN_CHUNK) == _block_id(cc, HGRN_CHUNK)) & (cc <= rr),
                    1.0, 0.0).astype(BF16)
    r64 = lax.broadcasted_iota(jnp.int32, (HGRN_CHUNK, HGRN_CHUNK), 0)
    c64 = lax.broadcasted_iota(jnp.int32, (HGRN_CHUNK, HGRN_CHUNK), 1)
    causal64 = c64 <= r64
    base = 2 * D_GMLP

    n_chunks = n // HGRN_CHUNK
    chains = range(x_ref.shape[1] // n)
    env = {ch: {} for ch in chains}

    def rows(ch):
        return slice(ch * n, (ch + 1) * n)

    def in_proj(ch):
        a = _rms(x_ref[0, rows(ch)], nmix_ref[...]).astype(BF16)
        proj_ref[rows(ch)] = _dot(a, win_ref[...])

    def gmlp(ch):
        u = _gelu(proj_ref[rows(ch), 0:D_GMLP])
        v = _gelu(proj_ref[rows(ch), D_GMLP:2 * D_GMLP])
        vc = v - jnp.mean(v, axis=-1, keepdims=True)
        vn = (vc * lax.rsqrt(jnp.mean(vc * vc, axis=-1, keepdims=True) + EPS) * gln_ref[...]).astype(BF16)
        z_rows = []
        for c in range(n // GMLP_CHUNK):
            z_cols = []
            for g in range(GMLP_GROUPS):
                vg = vn[c * GMLP_CHUNK:(c + 1) * GMLP_CHUNK, g * LANES:(g + 1) * LANES]
                z_cols.append(_dot(w_tril[g], vg) + bst_ref[:, g:g + 1])
            z_rows.append(jnp.concatenate(z_cols, axis=1))
        z = jnp.concatenate(z_rows, axis=0)
        ycat_ref[rows(ch), 0:D_GMLP] = _rms(u * z, beta_ref[...]).astype(BF16)

    def hgrn_factors(ch):
        e = env[ch]
        f = lb + (1.0 - lb) * _sigmoid(proj_ref[rows(ch), base + D_HGRN:base + 2 * D_HGRN])
        log_f = jnp.log(f)
        lf_hi = log_f.astype(BF16)
        lf_lo = (log_f - lf_hi.astype(F32)).astype(BF16)
        b_all = _dot(tri, lf_hi) + _dot(tri, lf_lo)
        bl_rows = [b_all[c * HGRN_CHUNK + HGRN_CHUNK - 1:(c + 1) * HGRN_CHUNK] for c in range(n_chunks)]
        bl_all = jnp.concatenate([jnp.broadcast_to(r, (HGRN_CHUNK, D_HGRN)) for r in bl_rows], axis=0)
        q_all = proj_ref[rows(ch), base:base + D_HGRN]
        k_all = 1.0 - f
        e["qd"] = (q_all * _sigmoid(q_all) * jnp.exp(b_all)).astype(BF16)
        e["ki"] = (k_all * jnp.exp(-b_all)).astype(BF16)
        e["kte"] = (k_all * jnp.exp(bl_all - b_all)).astype(BF16)
        e["v"] = proj_ref[rows(ch), base + 2 * D_HGRN:base + 3 * D_HGRN].astype(BF16)
        e["decay"] = [jnp.exp(r) for r in bl_rows]

    def hgrn_local(ch):
        e = env[ch]
        e["o_intra"], e["d_state"] = {}, {}
        for c in range(n_chunks):
            rs = slice(c * HGRN_CHUNK, (c + 1) * HGRN_CHUNK)
            for h in range(HGRN_HEADS):
                cs = slice(h * HGRN_DK, (h + 1) * HGRN_DK)
                scores = jnp.where(causal64, _dot_nt(e["qd"][rs, cs], e["ki"][rs, cs]), 0.0).astype(BF16)
                e["o_intra"][c, h] = _dot(scores, e["v"][rs, cs])
                e["d_state"][c, h] = _dot_tn(e["v"][rs, cs], e["kte"][rs, cs])

    def hgrn_recurrence(ch):
        e = env[ch]
        for c in range(n_chunks):
            rs = slice(c * HGRN_CHUNK, (c + 1) * HGRN_CHUNK)
            ps = slice(ch * n + c * HGRN_CHUNK, ch * n + (c + 1) * HGRN_CHUNK)
            g_c = proj_ref[ps, base + 3 * D_HGRN:base + 4 * D_HGRN]
            gate = og_ref[...] * (g_c * _sigmoid(g_c))
            for h in range(HGRN_HEADS):
                cs = slice(h * HGRN_DK, (h + 1) * HGRN_DK)
                st = state_ref[h]
                o = e["o_intra"][c, h] + _dot_nt(e["qd"][rs, cs], st.astype(BF16))
                state_ref[h] = st * e["decay"][c][:, cs] + e["d_state"][c, h]
                o = o * lax.rsqrt(jnp.mean(o * o, axis=-1, keepdims=True) + EPS)
                ycat_ref[ps, D_GMLP + h * HGRN_DK:D_GMLP + (h + 1) * HGRN_DK] = (o * gate[:, cs]).astype(BF16)

    def out_proj(ch):
        o_ref[0, rows(ch)] = x_ref[0, rows(ch)] + _dot(ycat_ref[rows(ch)], wout_ref[...])

    stages = {"in": in_proj, "gmlp": gmlp, "factors": hgrn_factors, "local": hgrn_local,
              "recurrence": hgrn_recurrence, "out": out_proj}
    for stage, ch in MIX_SCHEDULE:
        stages[stage](ch)


def _mixer(x, norm_mix, w_in, gmlp_ln, w_s, b_s_t, beta, lb_logits, out_gain, w_out):
    B, S, D = x.shape
    const2 = lambda b, s: (0, 0)
    return pl.pallas_call(
        _mixer_kernel,
        grid=(B, S // MIX_ROWS),
        in_specs=[
            pl.BlockSpec((1, MIX_ROWS, D), lambda b, s: (b, s, 0)),
            pl.BlockSpec((1, D), const2),
            pl.BlockSpec(memory_space=pl.ANY),
            pl.BlockSpec((1, D_GMLP), const2),
            pl.BlockSpec((GMLP_GROUPS, GMLP_CHUNK, GMLP_CHUNK), lambda b, s: (0, 0, 0)),
            pl.BlockSpec((GMLP_CHUNK, GMLP_GROUPS), const2),
            pl.BlockSpec((1, D_GMLP), const2),
            pl.BlockSpec(lb_logits.shape, const2),
            pl.BlockSpec((1, D_HGRN), const2),
            pl.BlockSpec(memory_space=pl.ANY),
        ],
        out_specs=pl.BlockSpec((1, MIX_ROWS, D), lambda b, s: (b, s, 0)),
        out_shape=jax.ShapeDtypeStruct((B, S, D), F32),
        scratch_shapes=[
            pltpu.VMEM((MIX_ROWS, D_IN_PROJ), F32),
            pltpu.VMEM((MIX_ROWS, D), BF16),
            pltpu.VMEM((HGRN_HEADS, HGRN_DK, HGRN_DK), F32),
        ] + _weight_scratch(D, D_IN_PROJ) + _weight_scratch(D, D) + [pltpu.SemaphoreType.DMA],
        compiler_params=pltpu.CompilerParams(
            dimension_semantics=("arbitrary", "arbitrary"), vmem_limit_bytes=VMEM_LIMIT),
        name="mixer",
    )(x, norm_mix, w_in, gmlp_ln, w_s, b_s_t, beta, lb_logits, out_gain, w_out)


def _memkv_kernel(mem_ref, nm_ref, wkv_hbm, wq_hbm, wo_hbm, qk_ref, vo_ref,
                  wkv_ref, wkv_stage, wq_ref, wq_stage, wo_ref, wo_stage, sem):
    @pl.when(pl.program_id(0) == 0)
    def _():
        _stage_weight(wkv_hbm, wkv_ref, wkv_stage, sem)
        _stage_weight(wq_hbm, wq_ref, wq_stage, sem)
        _stage_weight(wo_hbm, wo_ref, wo_stage, sem)

    nb, M, D = mem_ref.shape
    m = _rms(mem_ref[...].reshape(nb * M, D), nm_ref[...]).astype(BF16)
    kv = _dot(m, wkv_ref[...])
    k = kv[:, :D_MODEL].astype(BF16)
    v = kv[:, D_MODEL:].astype(BF16)
    for b in range(nb):
        rs = slice(b * M, (b + 1) * M)
        for hd in range(XATTN_HEADS):
            cs = slice(hd * XATTN_HEAD_DIM, (hd + 1) * XATTN_HEAD_DIM)
            ms = slice(hd * M, (hd + 1) * M)
            qk_ref[b, :, ms] = (_dot_nt(wq_ref[:, cs], k[rs, cs]) * (XATTN_HEAD_DIM ** -0.5)).astype(BF16)
            vo_ref[b, ms, :] = _dot(v[rs, cs], wo_ref[cs, :]).astype(BF16)


def _memkv(mem, norm_mem, w_kv, w_q, w_o):
    B, M, D = mem.shape
    out = jax.ShapeDtypeStruct((B, D, XATTN_HEADS * M), BF16), jax.ShapeDtypeStruct((B, XATTN_HEADS * M, D), BF16)
    return pl.pallas_call(
        _memkv_kernel,
        grid=(B // MEMKV_BATCHES,),
        in_specs=[
            pl.BlockSpec((MEMKV_BATCHES, M, D), lambda b: (b, 0, 0)),
            pl.BlockSpec((1, D), lambda b: (0, 0)),
            pl.BlockSpec(memory_space=pl.ANY),
            pl.BlockSpec(memory_space=pl.ANY),
            pl.BlockSpec(memory_space=pl.ANY),
        ],
        out_specs=[pl.BlockSpec((MEMKV_BATCHES, D, XATTN_HEADS * M), lambda b: (b, 0, 0)),
                   pl.BlockSpec((MEMKV_BATCHES, XATTN_HEADS * M, D), lambda b: (b, 0, 0))],
        out_shape=list(out),
        scratch_shapes=(_weight_scratch(D, 2 * D) + _weight_scratch(D, D) + _weight_scratch(D, D)
                        + [pltpu.SemaphoreType.DMA]),
        compiler_params=pltpu.CompilerParams(
            dimension_semantics=("arbitrary",), vmem_limit_bytes=VMEM_LIMIT),
        name="memkv",
    )(mem, norm_mem, w_kv, w_q, w_o)


def _xattn_kernel(h_ref, nx_ref, qk_ref, vo_ref, nf_ref, wr_ref, h2_ref, xn_ref, lg_ref, att_ref):
    h = h_ref[0]
    hn = _rms(h, nx_ref[...]).astype(BF16)
    scores = _dot(hn, qk_ref[0])
    n_mem = qk_ref.shape[2] // XATTN_HEADS
    for hd in range(XATTN_HEADS):
        ms = slice(hd * n_mem, (hd + 1) * n_mem)
        s = scores[:, ms]
        p = jnp.exp(s - jnp.max(s, axis=-1, keepdims=True))
        att_ref[:, ms] = (p / jnp.sum(p, axis=-1, keepdims=True)).astype(BF16)
    h2 = h + _dot(att_ref[...], vo_ref[0])
    h2_ref[0] = h2.astype(h2_ref.dtype)
    xn = _rms(h2, nf_ref[...])
    xn_ref[...] = _pack_pairs(xn)
    lg = _dot_nt(wr_ref[...], xn.astype(BF16))
    for j in range(ATT_ROWS // LANES):
        lg_ref[:, j, :] = lg[:, j * LANES:(j + 1) * LANES]


def _xattn(h1, norm_x, qk_mem, vo_mem, norm_ffn, w_router, batch0, batches):
    _, S, D = h1.shape
    n_s = S // ATT_ROWS
    n_att = qk_mem.shape[2]
    const2 = lambda b, s: (0, 0)
    return pl.pallas_call(
        _xattn_kernel,
        grid=(batches, n_s),
        in_specs=[
            pl.BlockSpec((1, ATT_ROWS, D), lambda b, s: (b + batch0, s, 0)),
            pl.BlockSpec((1, D), const2),
            pl.BlockSpec((1, D, n_att), lambda b, s: (b + batch0, 0, 0)),
            pl.BlockSpec((1, n_att, D), lambda b, s: (b + batch0, 0, 0)),
            pl.BlockSpec((1, D), const2),
            pl.BlockSpec((ROUTER_ROWS, D), const2),
        ],
        out_specs=[
            pl.BlockSpec((1, ATT_ROWS, D), lambda b, s: (b, s, 0)),
            pl.BlockSpec((ATT_ROWS, D // 2), lambda b, s: (b * n_s + s, 0)),
            pl.BlockSpec((ROUTER_ROWS, ATT_ROWS // LANES, LANES), lambda b, s: (0, b * n_s + s, 0)),
        ],
        out_shape=[
            jax.ShapeDtypeStruct((batches, S, D), BF16),
            jax.ShapeDtypeStruct((batches * S, D // 2), jnp.uint32),
            jax.ShapeDtypeStruct((ROUTER_ROWS, batches * S // LANES, LANES), F32),
        ],
        scratch_shapes=[pltpu.VMEM((ATT_ROWS, n_att), BF16)],
        compiler_params=pltpu.CompilerParams(
            dimension_semantics=("arbitrary", "arbitrary"), vmem_limit_bytes=VMEM_LIMIT),
        name="xattn",
    )(h1, norm_x, qk_mem, vo_mem, norm_ffn, w_router)


def _route_kernel(bias_ref, lg_ref, ids_ref, gates_ref, rank_ref, cnt_ref, base_ref):
    sub = lg_ref.shape[1]

    @pl.when(pl.program_id(0) == 0)
    def _():
        base_ref[...] = jnp.zeros_like(base_ref)

    best = lg_ref[0] + bias_ref[0]
    gl = [best]
    sel = jnp.zeros(best.shape, jnp.int32)
    for g in range(1, N_GROUPS):
        cur = lg_ref[g] + bias_ref[g]
        gl.append(cur)
        better = cur > best
        best = jnp.where(better, cur, best)
        sel = jnp.where(better, g, sel)
    denom = jnp.exp(gl[0] - best)
    for g in range(1, N_GROUPS):
        denom = denom + jnp.exp(gl[g] - best)
    g_gate = 1.0 / denom

    ev = []
    for j in range(EXPERTS_PER_GROUP):
        val = lg_ref[N_GROUPS + j] + bias_ref[N_GROUPS + j]
        for g in range(1, N_GROUPS):
            e = g * EXPERTS_PER_GROUP + j
            val = jnp.where(sel == g, lg_ref[N_GROUPS + e] + bias_ref[N_GROUPS + e], val)
        ev.append(val)
    v1, i1 = ev[0], jnp.zeros(best.shape, jnp.int32)
    for j in range(1, EXPERTS_PER_GROUP):
        better = ev[j] > v1
        v1 = jnp.where(better, ev[j], v1)
        i1 = jnp.where(better, j, i1)
    rest = [jnp.where(i1 == j, -jnp.inf, ev[j]) for j in range(EXPERTS_PER_GROUP)]
    v2, i2 = rest[0], jnp.zeros(best.shape, jnp.int32)
    for j in range(1, EXPERTS_PER_GROUP):
        better = rest[j] > v2
        v2 = jnp.where(better, rest[j], v2)
        i2 = jnp.where(better, j, i2)
    e2 = jnp.exp(v2 - v1)
    inv = 1.0 / (1.0 + e2)
    id1 = sel * EXPERTS_PER_GROUP + i1
    id2 = sel * EXPERTS_PER_GROUP + i2
    ids_ref[0] = id1
    ids_ref[1] = id2
    gates_ref[0] = inv * g_gate
    gates_ref[1] = e2 * inv * g_gate

    member = jnp.concatenate(
        [jnp.where((id1 == e) | (id2 == e), 1.0, 0.0) for e in range(N_EXPERTS)], axis=0).astype(BF16)
    n = N_EXPERTS * sub
    li = lax.broadcasted_iota(jnp.int32, (LANES, LANES), 0)
    lj = lax.broadcasted_iota(jnp.int32, (LANES, LANES), 1)
    before_lane = jnp.where(li < lj, 1.0, 0.0).astype(BF16)
    ones = jnp.ones((LANES, LANES), BF16)
    ri = lax.broadcasted_iota(jnp.int32, (n, n), 0)
    rj = lax.broadcasted_iota(jnp.int32, (n, n), 1)
    same = _block_id(ri, sub) == _block_id(rj, sub)
    before_row = jnp.where(same & (rj < ri), 1.0, 0.0).astype(BF16)
    all_row = jnp.where(same, 1.0, 0.0).astype(BF16)
    in_row = _dot(member, before_lane)
    prev_rows = _dot(_dot(before_row, member).astype(BF16), ones)
    total = _dot(_dot(all_row, member).astype(BF16), ones)
    base = base_ref[...]
    pos = base + prev_rows + in_row
    r1 = jnp.zeros(best.shape, F32)
    r2 = jnp.zeros(best.shape, F32)
    for e in range(N_EXPERTS):
        pe = pos[e * sub:(e + 1) * sub]
        r1 = jnp.where(id1 == e, pe, r1)
        r2 = jnp.where(id2 == e, pe, r2)
    rank_ref[0] = r1.astype(jnp.int32)
    rank_ref[1] = r2.astype(jnp.int32)
    base_ref[...] = base + total
    cnt_ref[...] = base + total


def _route(bias, logits3):
    rp, n_sub, _ = logits3.shape
    blk = lambda i: (0, i, 0)
    pair_i = jax.ShapeDtypeStruct((2, n_sub, LANES), jnp.int32)
    return pl.pallas_call(
        _route_kernel,
        grid=(n_sub // ROUTE_SUB,),
        in_specs=[
            pl.BlockSpec(memory_space=pltpu.SMEM),
            pl.BlockSpec((rp, ROUTE_SUB, LANES), blk),
        ],
        out_specs=[
            pl.BlockSpec((2, ROUTE_SUB, LANES), blk),
            pl.BlockSpec((2, ROUTE_SUB, LANES), blk),
            pl.BlockSpec((2, ROUTE_SUB, LANES), blk),
            pl.BlockSpec((N_EXPERTS * ROUTE_SUB, LANES), lambda i: (0, 0)),
        ],
        out_shape=[
            pair_i,
            jax.ShapeDtypeStruct((2, n_sub, LANES), F32),
            pair_i,
            jax.ShapeDtypeStruct((N_EXPERTS * ROUTE_SUB, LANES), F32),
        ],
        scratch_shapes=[pltpu.VMEM((N_EXPERTS * ROUTE_SUB, LANES), F32)],
        compiler_params=pltpu.CompilerParams(
            dimension_semantics=("arbitrary",), vmem_limit_bytes=VMEM_LIMIT),
        name="route",
    )(bias, logits3)


def _dest_kernel(cnt_ref, ids_ref, rank_ref, dest_ref, be_ref, nb_ref, slot_ref, next_ref, start_ref, succ_ref):
    n_blocks = be_ref.shape[0]
    shift = EXPERT_ROWS.bit_length() - 1
    assert 1 << shift == EXPERT_ROWS

    @pl.when(pl.program_id(0) == 0)
    def _():
        def tail(i, carry):
            be_ref[i] = N_EXPERTS - 1
            slot_ref[i] = 0
            next_ref[i] = -1
            return carry

        lax.fori_loop(0, n_blocks, tail, 0)

        def successor(k, nxt):
            e = N_EXPERTS - 1 - k
            succ_ref[e] = nxt
            return jnp.where(cnt_ref[e] > 0, e, nxt)

        lax.fori_loop(0, N_EXPERTS, successor, jnp.int32(-1))

        def segment(e, carry):
            block0, ordinal = carry
            blocks = lax.shift_right_logical(cnt_ref[e] + (EXPERT_ROWS - 1), shift)
            start_ref[e] = lax.shift_left(block0, shift)

            def fill(j, c):
                be_ref[block0 + j] = e
                slot_ref[block0 + j] = ordinal & 1
                next_ref[block0 + j] = succ_ref[e]
                return c

            lax.fori_loop(0, blocks, fill, 0)
            return block0 + blocks, ordinal + (cnt_ref[e] > 0).astype(jnp.int32)

        used, _ = lax.fori_loop(0, N_EXPERTS, segment, (jnp.int32(0), jnp.int32(0)))
        nb_ref[0] = used

    ids = ids_ref[...]
    off = jnp.zeros(ids.shape, jnp.int32)
    for e in range(N_EXPERTS):
        off = jnp.where(ids == e, start_ref[e], off)
    dest_ref[...] = rank_ref[...] + off


def _dest(counts, ids, rank, n_blocks):
    _, n_sub, _ = ids.shape
    blk = pl.BlockSpec((2, ROUTE_SUB, LANES), lambda i: (0, i, 0))
    smem = pl.BlockSpec(memory_space=pltpu.SMEM)
    table = jax.ShapeDtypeStruct((n_blocks,), jnp.int32)
    return pl.pallas_call(
        _dest_kernel,
        grid=(n_sub // ROUTE_SUB,),
        in_specs=[smem, blk, blk],
        out_specs=[blk, smem, smem, smem, smem],
        out_shape=[jax.ShapeDtypeStruct(ids.shape, jnp.int32), table, jax.ShapeDtypeStruct((1,), jnp.int32),
                   table, table],
        scratch_shapes=[pltpu.SMEM((N_EXPERTS,), jnp.int32), pltpu.SMEM((N_EXPERTS,), jnp.int32)],
        compiler_params=pltpu.CompilerParams(dimension_semantics=("arbitrary",)),
        name="dest",
    )(counts, ids, rank)


def _sc_mesh():
    return plsc.VectorSubcoreMesh(core_axis_name="core", subcore_axis_name="subcore")


def _sc_worker(rows_total):
    rows = rows_total // _SC_WORKERS
    wid = lax.axis_index("core") * SC_SUBCORES + lax.axis_index("subcore")
    return wid * rows, rows


def _dispatch(dest0, dest1, xn, n_slots):
    T, D = xn.shape
    W = SC_WINDOW
    rows = T // _SC_WORKERS
    assert T % (_SC_WORKERS * 2 * W) == 0

    @pl.kernel(out_type=jax.ShapeDtypeStruct((n_slots, D), xn.dtype), mesh=_sc_mesh(),
               scratch_types=[pltpu.VMEM((rows,), jnp.int32), pltpu.VMEM((rows,), jnp.int32),
                              pltpu.VMEM((2, W, D), xn.dtype),
                              pltpu.SemaphoreType.DMA((2,)), pltpu.SemaphoreType.DMA((2,))])
    def scatter_rows(x_hbm, d0_hbm, d1_hbm, xs_hbm, d0_v, d1_v, buf, in_sem, out_sem):
        base, _ = _sc_worker(T)
        pltpu.sync_copy(d0_hbm.at[pl.ds(base, rows)], d0_v)
        pltpu.sync_copy(d1_hbm.at[pl.ds(base, rows)], d1_v)

        def load(w, slot):
            return pltpu.make_async_copy(x_hbm.at[pl.ds(base + w * W, W)], buf.at[slot], in_sem.at[slot])

        def store(w, slot, d_v):
            return pltpu.make_async_copy(buf.at[slot], xs_hbm.at[d_v.at[pl.ds(w * W, W)]], out_sem.at[slot])

        def step(w, slot):
            load(w, slot).wait()
            store(w, slot, d0_v).start()
            store(w, slot, d1_v).start()
            store(w, slot, d0_v).wait()
            store(w, slot, d1_v).wait()

        n = rows // W
        load(0, 0).start()

        @pl.loop(0, n, step=2)
        def _(w):
            load(w + 1, 1).start()
            step(w, 0)

            @pl.when(w + 2 < n)
            def _():
                load(w + 2, 0).start()

            step(w + 1, 1)

    return scatter_rows(xn, dest0, dest1)


def _gather_rows(src, idx):
    M = idx.shape[0]
    D = src.shape[1]
    W = SC_WINDOW
    rows = M // _SC_WORKERS
    assert M % (_SC_WORKERS * 2 * W) == 0

    @pl.kernel(out_type=jax.ShapeDtypeStruct((M, D), src.dtype), mesh=_sc_mesh(),
               scratch_types=[pltpu.VMEM((rows,), jnp.int32), pltpu.VMEM((2, W, D), src.dtype),
                              pltpu.SemaphoreType.DMA((2,)), pltpu.SemaphoreType.DMA((2,))])
    def gather_rows(src_hbm, i_hbm, o_hbm, i_v, buf, in_sem, out_sem):
        base, _ = _sc_worker(M)
        pltpu.sync_copy(i_hbm.at[pl.ds(base, rows)], i_v)

        def load(w, slot):
            return pltpu.make_async_copy(src_hbm.at[i_v.at[pl.ds(w * W, W)]], buf.at[slot], in_sem.at[slot])

        def store(w, slot):
            return pltpu.make_async_copy(buf.at[slot], o_hbm.at[pl.ds(base + w * W, W)], out_sem.at[slot])

        n = rows // W
        load(0, 0).start()

        @pl.loop(0, n, step=2)
        def _(w):
            @pl.when(w > 0)
            def _():
                store(w - 1, 1).wait()

            load(w + 1, 1).start()
            load(w, 0).wait()
            store(w, 0).start()
            store(w, 0).wait()

            @pl.when(w + 2 < n)
            def _():
                load(w + 2, 0).start()

            load(w + 1, 1).wait()
            store(w + 1, 1).start()

        store(n - 1, 1).wait()

    return gather_rows(src, idx)


def _expert_kernel(be_ref, nb_ref, slot_ref, next_ref, x_ref, wg_hbm, wu_hbm, wd_hbm, y_ref,
                   wg_f32, wu_f32, wd_f32, wg_ref, wu_ref, wd_ref, sem):
    i = pl.program_id(0)
    used = i < nb_ref[0]
    new_expert = (i == 0) | (be_ref[i] != be_ref[jnp.maximum(i - 1, 0)])

    def fetch(e, slot):
        pairs = ((wg_hbm, wg_f32), (wu_hbm, wu_f32), (wd_hbm, wd_f32))
        return [pltpu.make_async_copy(w_hbm.at[e], w_f32.at[slot], sem.at[slot, k])
                for k, (w_hbm, w_f32) in enumerate(pairs)]

    @pl.when(used & (i == 0))
    def _():
        for copy in fetch(be_ref[0], slot_ref[0]):
            copy.start()

    @pl.when(used & new_expert)
    def _():
        slot = slot_ref[i]
        for copy in fetch(be_ref[i], slot):
            copy.wait()

        @pl.when(next_ref[i] >= 0)
        def _():
            for copy in fetch(next_ref[i], 1 - slot):
                copy.start()

        wg_ref[...] = wg_f32[slot].astype(BF16)
        wu_ref[...] = wu_f32[slot].astype(BF16)
        wd_ref[...] = wd_f32[slot].astype(BF16)

    @pl.when(used)
    def _():
        x = _unpack_pairs(x_ref[...]).astype(BF16)
        g = _dot(x, wg_ref[...])
        u = _dot(x, wu_ref[...])
        hid = (g * _sigmoid(g) * u).astype(BF16)
        y_ref[...] = _pack_pairs(_dot(hid, wd_ref[...]))

    @pl.when(jnp.logical_not(used))
    def _():
        y_ref[...] = jnp.zeros_like(y_ref)


def _experts(block_expert, n_used, block_slot, block_next, xs, w_gate, w_up, w_down):
    n_slots, half = xs.shape
    D = 2 * half
    n_blocks = n_slots // EXPERT_ROWS
    grid_spec = pltpu.PrefetchScalarGridSpec(
        num_scalar_prefetch=4,
        grid=(n_blocks,),
        in_specs=[
            pl.BlockSpec((EXPERT_ROWS, half), lambda i, be, nb, sl, nx: (jnp.minimum(i, nb[0] - 1), 0)),
            pl.BlockSpec(memory_space=pl.ANY),
            pl.BlockSpec(memory_space=pl.ANY),
            pl.BlockSpec(memory_space=pl.ANY),
        ],
        out_specs=pl.BlockSpec((EXPERT_ROWS, half), lambda i, be, nb, sl, nx: (i, 0)),
        scratch_shapes=[
            pltpu.VMEM((2, D, D_EXPERT), F32), pltpu.VMEM((2, D, D_EXPERT), F32),
            pltpu.VMEM((2, D_EXPERT, D), F32),
            pltpu.VMEM((D, D_EXPERT), BF16), pltpu.VMEM((D, D_EXPERT), BF16), pltpu.VMEM((D_EXPERT, D), BF16),
            pltpu.SemaphoreType.DMA((2, 3)),
        ],
    )
    return pl.pallas_call(
        _expert_kernel,
        grid_spec=grid_spec,
        out_shape=jax.ShapeDtypeStruct((n_slots, half), jnp.uint32),
        compiler_params=pltpu.CompilerParams(
            dimension_semantics=("arbitrary",), vmem_limit_bytes=VMEM_LIMIT),
        name="experts",
    )(block_expert, n_used, block_slot, block_next, xs, w_gate, w_up, w_down)


def _combine_kernel(y0_ref, y1_ref, h_ref, gates_ref, nfin_ref, *rest):
    o_ref = rest[-1]
    g0 = gates_ref[0].T
    g1 = gates_ref[1].T
    for r in range(gates_ref.shape[1]):
        rs = slice(r * LANES, (r + 1) * LANES)
        h = (h_ref[rs].astype(F32) + g0[:, r:r + 1] * _unpack_pairs(y0_ref[rs])
             + g1[:, r:r + 1] * _unpack_pairs(y1_ref[rs]))
        o_ref[rs] = _rms(h, nfin_ref[...])


def _combine(y01, h2, gates, norm_final, out_prev, row0, total_rows):
    T, D = h2.shape
    n_t = T // MOVE_ROWS
    in_specs = [
        pl.BlockSpec((MOVE_ROWS, D // 2), lambda i: (i, 0)),
        pl.BlockSpec((MOVE_ROWS, D // 2), lambda i: (i + n_t, 0)),
        pl.BlockSpec((MOVE_ROWS, D), lambda i: (i, 0)),
        pl.BlockSpec((2, MOVE_ROWS // LANES, LANES), lambda i: (0, i, 0)),
        pl.BlockSpec((1, D), lambda i: (0, 0)),
    ]
    args = [y01, y01, h2, gates, norm_final]
    aliases = {}
    if out_prev is not None:
        in_specs.append(pl.BlockSpec(memory_space=pl.ANY))
        args.append(out_prev)
        aliases = {len(args) - 1: 0}
    return pl.pallas_call(
        _combine_kernel,
        grid=(n_t,),
        in_specs=in_specs,
        out_specs=pl.BlockSpec((MOVE_ROWS, D), lambda i: (i + row0 // MOVE_ROWS, 0)),
        out_shape=jax.ShapeDtypeStruct((total_rows, D), F32),
        input_output_aliases=aliases,
        compiler_params=pltpu.CompilerParams(
            dimension_semantics=("arbitrary",), vmem_limit_bytes=VMEM_LIMIT),
        name="combine",
    )(*args)


def kernel(x, mem, norm_mix, w_in, gmlp_ln, gmlp_w_spatial, gmlp_b_spatial, gmlp_beta, hgrn_lb_logits, hgrn_out_gain, w_out, norm_xattn, norm_mem, w_xq, w_xkv, w_xo, norm_ffn, w_router_group, b_router_group, w_router_expert, b_router_expert, w_expert_gate, w_expert_up, w_expert_down, norm_final):
    B, S, D = x.shape
    T = B * S
    depth = w_in.shape[0]
    assert depth == 1 and hgrn_lb_logits.shape[0] == 2
    assert D == D_MODEL and mem.shape[1] == N_MEM and w_in.shape[2] == D_IN_PROJ
    assert S % MIX_ROWS == 0 and S % ATT_ROWS == 0 and B % MEMKV_BATCHES == 0
    l = 0
    row = lambda p: p.reshape(1, -1)

    h1 = _mixer(x, row(norm_mix[l]), w_in[l], row(gmlp_ln[l]), gmlp_w_spatial[l],
                gmlp_b_spatial[l].T, row(gmlp_beta[l]), hgrn_lb_logits, row(hgrn_out_gain[l]),
                w_out[l])
    qk_mem, vo_mem = _memkv(mem, row(norm_mem[l]), w_xkv[l], w_xq[l], w_xo[l])

    w_router = jnp.concatenate([w_router_group[l].T, w_router_expert[l].T], axis=0)
    w_router = jnp.pad(w_router, ((0, ROUTER_ROWS - w_router.shape[0]), (0, 0)))
    w_router = w_router.astype(BF16)
    bias = jnp.concatenate([b_router_group[l], b_router_expert[l]]).astype(F32)

    assert sum(PART_BATCHES) == B
    out = None
    b0 = 0
    for b_part in PART_BATCHES:
        t_part = b_part * S
        n_blocks = (2 * t_part) // EXPERT_ROWS + N_EXPERTS
        h2, xn, logits = _xattn(h1, row(norm_xattn[l]), qk_mem, vo_mem, row(norm_ffn[l]), w_router, b0, b_part)
        ids, gates, rank, counts = _route(bias, logits)

        counts = counts[::ROUTE_SUB, 0].astype(jnp.int32)
        dest, block_expert, n_used, block_slot, block_next = _dest(counts, ids, rank, n_blocks)
        dest = dest.reshape(2, t_part)
        xs = _dispatch(dest[0], dest[1], xn, n_blocks * EXPERT_ROWS)
        yb = _experts(block_expert, n_used, block_slot, block_next, xs,
                      w_expert_gate[l], w_expert_up[l], w_expert_down[l])
        y01 = _gather_rows(yb, dest.reshape(2 * t_part))
        out = _combine(y01, h2.reshape(t_part, D), gates, row(norm_final), out, b0 * S, T)
        b0 += b_part
    return out.reshape(B, S, D)
```

```python
import jax
import jax.numpy as jnp
from jax import lax
from jax.experimental import pallas as pl
from jax.experimental.pallas import tpu as pltpu
from jax.experimental.pallas import tpu_sc as plsc

F32 = jnp.float32
BF16 = jnp.bfloat16
EPS = 1e-6

D_MODEL = 1024
D_GMLP = 512
GMLP_GROUPS = 4
GMLP_CHUNK = 128
D_HGRN = 512
HGRN_HEADS = 4
HGRN_DK = 128
HGRN_CHUNK = 64
D_IN_PROJ = 2 * D_GMLP + 4 * D_HGRN
N_MEM = 256
XATTN_HEADS = 4
XATTN_HEAD_DIM = D_MODEL // XATTN_HEADS
N_GROUPS = 4
EXPERTS_PER_GROUP = 8
N_EXPERTS = N_GROUPS * EXPERTS_PER_GROUP
D_EXPERT = 512

LANES = 128
MIX_ROWS = 1024
MIX_CHAIN_ROWS = 256


def _mix_schedule(n_chains):
    order = [("in", 0), ("gmlp", 0), ("factors", 0)]
    for k in range(n_chains):
        more = k + 1 < n_chains
        order += [("in", k + 1)] * more + [("local", k), ("recurrence", k)] + [("gmlp", k + 1)] * more
        order += [("out", k)] + [("factors", k + 1)] * more
    return tuple(order)


MIX_SCHEDULE = _mix_schedule(MIX_ROWS // MIX_CHAIN_ROWS)
MEMKV_BATCHES = 2
ATT_ROWS = 1024
WEIGHT_STAGE_ROWS = 128
ROUTER_ROWS = 40
ROUTE_SUB = 16
ROUTE_TILES = 4
EXPERT_ROWS = 512
MOVE_ROWS = 1024
PART_BATCHES = (20, 12)
SC_WINDOW = 64
SC_CORES = 2
SC_SUBCORES = 16
_SC_WORKERS = SC_CORES * SC_SUBCORES
VMEM_LIMIT = 48 * 1024 * 1024


def _rms(x, gain):
    return x * lax.rsqrt(jnp.mean(x * x, axis=-1, keepdims=True) + EPS) * gain


def _dot(a, b):
    return jnp.dot(a, b, preferred_element_type=F32)


def _dot_nt(a, b):
    return lax.dot_general(a, b, (((1,), (1,)), ((), ())), preferred_element_type=F32)


def _dot_tn(a, b):
    return lax.dot_general(a, b, (((0,), (0,)), ((), ())), preferred_element_type=F32)


def _gelu(x):
    return 0.5 * x * (1.0 + jnp.tanh(0.7978845608028654 * (x + 0.044715 * (x * x * x))))


def _sigmoid(x):
    return 1.0 / (1.0 + jnp.exp(-x))


def _block_id(idx, size):
    assert size & (size - 1) == 0
    return lax.shift_right_logical(idx, size.bit_length() - 1)


def _stage_weight(w_hbm, w_bf16, stage_ref, sem):
    rows = stage_ref.shape[0]
    for k in range(w_hbm.shape[0] // rows):
        copy = pltpu.make_async_copy(w_hbm.at[pl.ds(k * rows, rows)], stage_ref, sem)
        copy.start()
        copy.wait()
        w_bf16[k * rows:(k + 1) * rows, :] = stage_ref[...].astype(BF16)


def _weight_scratch(k, n):
    return [pltpu.VMEM((k, n), BF16), pltpu.VMEM((WEIGHT_STAGE_ROWS, n), F32)]


_HIGH_HALF = 0xFFFF0000


def _pack_pairs(x):
    c = x.shape[1] // 2
    bits = lax.bitcast_convert_type(x.astype(BF16).astype(F32), jnp.uint32)
    return (bits[:, c:] & jnp.uint32(_HIGH_HALF)) | lax.shift_right_logical(bits[:, :c], jnp.uint32(16))


def _unpack_pairs(w):
    lo = lax.bitcast_convert_type(lax.shift_left(w, jnp.uint32(16)), F32)
    hi = lax.bitcast_convert_type(w & jnp.uint32(_HIGH_HALF), F32)
    return jnp.concatenate([lo, hi], axis=1)


def _mixer_kernel(x_ref, nmix_ref, win_hbm, gln_ref, ws_ref, bst_ref, beta_ref, lbl_ref, og_ref,
                  wout_hbm, o_ref, proj_ref, ycat_ref, state_ref, win_ref, win_stage, wout_ref, wout_stage,
                  sem):
    @pl.when((pl.program_id(0) == 0) & (pl.program_id(1) == 0))
    def _():
        _stage_weight(win_hbm, win_ref, win_stage, sem)
        _stage_weight(wout_hbm, wout_ref, wout_stage, sem)

    @pl.when(pl.program_id(1) == 0)
    def _():
        state_ref[...] = jnp.zeros_like(state_ref)

    n = MIX_CHAIN_ROWS
    r_i = lax.broadcasted_iota(jnp.int32, (GMLP_CHUNK, GMLP_CHUNK), 0)
    c_i = lax.broadcasted_iota(jnp.int32, (GMLP_CHUNK, GMLP_CHUNK), 1)
    causal = c_i <= r_i
    w_tril = [jnp.where(causal, ws_ref[g], 0.0).astype(BF16) for g in range(GMLP_GROUPS)]
    lbl = lbl_ref[...]
    e_lb = jnp.exp(lbl - jnp.max(lbl, axis=0, keepdims=True))
    lb = e_lb[0:1] / jnp.sum(e_lb, axis=0, keepdims=True)
    rr = lax.broadcasted_iota(jnp.int32, (n, n), 0)
    cc = lax.broadcasted_iota(jnp.int32, (n, n), 1)
    tri = jnp.where((_block_id(rr, HGRN_CHUNK) == _block_id(cc, HGRN_CHUNK)) & (cc <= rr),
                    1.0, 0.0).astype(BF16)
    r64 = lax.broadcasted_iota(jnp.int32, (HGRN_CHUNK, HGRN_CHUNK), 0)
    c64 = lax.broadcasted_iota(jnp.int32, (HGRN_CHUNK, HGRN_CHUNK), 1)
    causal64 = c64 <= r64
    base = 2 * D_GMLP

    n_chunks = n // HGRN_CHUNK
    chains = range(x_ref.shape[1] // n)
    env = {ch: {} for ch in chains}

    def rows(ch):
        return slice(ch * n, (ch + 1) * n)

    def in_proj(ch):
        a = _rms(x_ref[0, rows(ch)], nmix_ref[...]).astype(BF16)
        proj_ref[rows(ch)] = _dot(a, win_ref[...])

    def gmlp(ch):
        u = _gelu(proj_ref[rows(ch), 0:D_GMLP])
        v = _gelu(proj_ref[rows(ch), D_GMLP:2 * D_GMLP])
        vc = v - jnp.mean(v, axis=-1, keepdims=True)
        vn = (vc * lax.rsqrt(jnp.mean(vc * vc, axis=-1, keepdims=True) + EPS) * gln_ref[...]).astype(BF16)
        z_rows = []
        for c in range(n // GMLP_CHUNK):
            z_cols = []
            for g in range(GMLP_GROUPS):
                vg = vn[c * GMLP_CHUNK:(c + 1) * GMLP_CHUNK, g * LANES:(g + 1) * LANES]
                z_cols.append(_dot(w_tril[g], vg) + bst_ref[:, g:g + 1])
            z_rows.append(jnp.concatenate(z_cols, axis=1))
        z = jnp.concatenate(z_rows, axis=0)
        ycat_ref[rows(ch), 0:D_GMLP] = _rms(u * z, beta_ref[...]).astype(BF16)

    def hgrn_factors(ch):
        e = env[ch]
        f = lb + (1.0 - lb) * _sigmoid(proj_ref[rows(ch), base + D_HGRN:base + 2 * D_HGRN])
        log_f = jnp.log(f)
        lf_hi = log_f.astype(BF16)
        lf_lo = (log_f - lf_hi.astype(F32)).astype(BF16)
        b_all = _dot(tri, lf_hi) + _dot(tri, lf_lo)
        bl_rows = [b_all[c * HGRN_CHUNK + HGRN_CHUNK - 1:(c + 1) * HGRN_CHUNK] for c in range(n_chunks)]
        bl_all = jnp.concatenate([jnp.broadcast_to(r, (HGRN_CHUNK, D_HGRN)) for r in bl_rows], axis=0)
        q_all = proj_ref[rows(ch), base:base + D_HGRN]
        k_all = 1.0 - f
        e["qd"] = (q_all * _sigmoid(q_all) * jnp.exp(b_all)).astype(BF16)
        e["ki"] = (k_all * jnp.exp(-b_all)).astype(BF16)
        e["kte"] = (k_all * jnp.exp(bl_all - b_all)).astype(BF16)
        e["v"] = proj_ref[rows(ch), base + 2 * D_HGRN:base + 3 * D_HGRN].astype(BF16)
        e["decay"] = [jnp.exp(r) for r in bl_rows]

    def hgrn_local(ch):
        e = env[ch]
        e["o_intra"], e["d_state"] = {}, {}
        for c in range(n_chunks):
            rs = slice(c * HGRN_CHUNK, (c + 1) * HGRN_CHUNK)
            for h in range(HGRN_HEADS):
                cs = slice(h * HGRN_DK, (h + 1) * HGRN_DK)
                scores = jnp.where(causal64, _dot_nt(e["qd"][rs, cs], e["ki"][rs, cs]), 0.0).astype(BF16)
                e["o_intra"][c, h] = _dot(scores, e["v"][rs, cs])
                e["d_state"][c, h] = _dot_tn(e["v"][rs, cs], e["kte"][rs, cs])

    def hgrn_recurrence(ch):
        e = env[ch]
        for c in range(n_chunks):
            rs = slice(c * HGRN_CHUNK, (c + 1) * HGRN_CHUNK)
            ps = slice(ch * n + c * HGRN_CHUNK, ch * n + (c + 1) * HGRN_CHUNK)
            g_c = proj_ref[ps, base + 3 * D_HGRN:base + 4 * D_HGRN]
            gate = og_ref[...] * (g_c * _sigmoid(g_c))
            for h in range(HGRN_HEADS):
                cs = slice(h * HGRN_DK, (h + 1) * HGRN_DK)
                st = state_ref[h]
                o = e["o_intra"][c, h] + _dot_nt(e["qd"][rs, cs], st.astype(BF16))
                state_ref[h] = st * e["decay"][c][:, cs] + e["d_state"][c, h]
                o = o * lax.rsqrt(jnp.mean(o * o, axis=-1, keepdims=True) + EPS)
                ycat_ref[ps, D_GMLP + h * HGRN_DK:D_GMLP + (h + 1) * HGRN_DK] = (o * gate[:, cs]).astype(BF16)

    def out_proj(ch):
        o_ref[0, rows(ch)] = x_ref[0, rows(ch)] + _dot(ycat_ref[rows(ch)], wout_ref[...])

    stages = {"in": in_proj, "gmlp": gmlp, "factors": hgrn_factors, "local": hgrn_local,
              "recurrence": hgrn_recurrence, "out": out_proj}
    for stage, ch in MIX_SCHEDULE:
        stages[stage](ch)


def _mixer(x, norm_mix, w_in, gmlp_ln, w_s, b_s_t, beta, lb_logits, out_gain, w_out):
    B, S, D = x.shape
    const2 = lambda b, s: (0, 0)
    return pl.pallas_call(
        _mixer_kernel,
        grid=(B, S // MIX_ROWS),
        in_specs=[
            pl.BlockSpec((1, MIX_ROWS, D), lambda b, s: (b, s, 0)),
            pl.BlockSpec((1, D), const2),
            pl.BlockSpec(memory_space=pl.ANY),
            pl.BlockSpec((1, D_GMLP), const2),
            pl.BlockSpec((GMLP_GROUPS, GMLP_CHUNK, GMLP_CHUNK), lambda b, s: (0, 0, 0)),
            pl.BlockSpec((GMLP_CHUNK, GMLP_GROUPS), const2),
            pl.BlockSpec((1, D_GMLP), const2),
            pl.BlockSpec(lb_logits.shape, const2),
            pl.BlockSpec((1, D_HGRN), const2),
            pl.BlockSpec(memory_space=pl.ANY),
        ],
        out_specs=pl.BlockSpec((1, MIX_ROWS, D), lambda b, s: (b, s, 0)),
        out_shape=jax.ShapeDtypeStruct((B, S, D), F32),
        scratch_shapes=[
            pltpu.VMEM((MIX_ROWS, D_IN_PROJ), F32),
            pltpu.VMEM((MIX_ROWS, D), BF16),
            pltpu.VMEM((HGRN_HEADS, HGRN_DK, HGRN_DK), F32),
        ] + _weight_scratch(D, D_IN_PROJ) + _weight_scratch(D, D) + [pltpu.SemaphoreType.DMA],
        compiler_params=pltpu.CompilerParams(
            dimension_semantics=("arbitrary", "arbitrary"), vmem_limit_bytes=VMEM_LIMIT),
        name="mixer",
    )(x, norm_mix, w_in, gmlp_ln, w_s, b_s_t, beta, lb_logits, out_gain, w_out)


def _memkv_kernel(mem_ref, nm_ref, wkv_hbm, wq_hbm, wo_hbm, qk_ref, vo_ref,
                  wkv_ref, wkv_stage, wq_ref, wq_stage, wo_ref, wo_stage, sem):
    @pl.when(pl.program_id(0) == 0)
    def _():
        _stage_weight(wkv_hbm, wkv_ref, wkv_stage, sem)
        _stage_weight(wq_hbm, wq_ref, wq_stage, sem)
        _stage_weight(wo_hbm, wo_ref, wo_stage, sem)

    nb, M, D = mem_ref.shape
    m = _rms(mem_ref[...].reshape(nb * M, D), nm_ref[...]).astype(BF16)
    kv = _dot(m, wkv_ref[...])
    k = kv[:, :D_MODEL].astype(BF16)
    v = kv[:, D_MODEL:].astype(BF16)
    for b in range(nb):
        rs = slice(b * M, (b + 1) * M)
        for hd in range(XATTN_HEADS):
            cs = slice(hd * XATTN_HEAD_DIM, (hd + 1) * XATTN_HEAD_DIM)
            ms = slice(hd * M, (hd + 1) * M)
            qk_ref[b, :, ms] = (_dot_nt(wq_ref[:, cs], k[rs, cs]) * (XATTN_HEAD_DIM ** -0.5)).astype(BF16)
            vo_ref[b, ms, :] = _dot(v[rs, cs], wo_ref[cs, :]).astype(BF16)


def _memkv(mem, norm_mem, w_kv, w_q, w_o):
    B, M, D = mem.shape
    out = jax.ShapeDtypeStruct((B, D, XATTN_HEADS * M), BF16), jax.ShapeDtypeStruct((B, XATTN_HEADS * M, D), BF16)
    return pl.pallas_call(
        _memkv_kernel,
        grid=(B // MEMKV_BATCHES,),
        in_specs=[
            pl.BlockSpec((MEMKV_BATCHES, M, D), lambda b: (b, 0, 0)),
            pl.BlockSpec((1, D), lambda b: (0, 0)),
            pl.BlockSpec(memory_space=pl.ANY),
            pl.BlockSpec(memory_space=pl.ANY),
            pl.BlockSpec(memory_space=pl.ANY),
        ],
        out_specs=[pl.BlockSpec((MEMKV_BATCHES, D, XATTN_HEADS * M), lambda b: (b, 0, 0)),
                   pl.BlockSpec((MEMKV_BATCHES, XATTN_HEADS * M, D), lambda b: (b, 0, 0))],
        out_shape=list(out),
        scratch_shapes=(_weight_scratch(D, 2 * D) + _weight_scratch(D, D) + _weight_scratch(D, D)
                        + [pltpu.SemaphoreType.DMA]),
        compiler_params=pltpu.CompilerParams(
            dimension_semantics=("arbitrary",), vmem_limit_bytes=VMEM_LIMIT),
        name="memkv",
    )(mem, norm_mem, w_kv, w_q, w_o)


def _xattn_kernel(h_ref, nx_ref, qk_ref, vo_ref, nf_ref, wr_ref, h2_ref, xn_ref, lg_ref, att_ref):
    h = h_ref[0]
    hn = _rms(h, nx_ref[...]).astype(BF16)
    scores = _dot(hn, qk_ref[0])
    n_mem = qk_ref.shape[2] // XATTN_HEADS
    for hd in range(XATTN_HEADS):
        ms = slice(hd * n_mem, (hd + 1) * n_mem)
        s = scores[:, ms]
        p = jnp.exp(s - jnp.max(s, axis=-1, keepdims=True))
        att_ref[:, ms] = (p / jnp.sum(p, axis=-1, keepdims=True)).astype(BF16)
    h2 = h + _dot(att_ref[...], vo_ref[0])
    h2_ref[0] = h2.astype(h2_ref.dtype)
    xn = _rms(h2, nf_ref[...])
    xn_ref[...] = _pack_pairs(xn)
    lg = _dot_nt(wr_ref[...], xn.astype(BF16))
    for j in range(ATT_ROWS // LANES):
        lg_ref[:, j, :] = lg[:, j * LANES:(j + 1) * LANES]


def _xattn(h1, norm_x, qk_mem, vo_mem, norm_ffn, w_router, batch0, batches):
    _, S, D = h1.shape
    n_s = S // ATT_ROWS
    n_att = qk_mem.shape[2]
    const2 = lambda b, s: (0, 0)
    return pl.pallas_call(
        _xattn_kernel,
        grid=(batches, n_s),
        in_specs=[
            pl.BlockSpec((1, ATT_ROWS, D), lambda b, s: (b + batch0, s, 0)),
            pl.BlockSpec((1, D), const2),
            pl.BlockSpec((1, D, n_att), lambda b, s: (b + batch0, 0, 0)),
            pl.BlockSpec((1, n_att, D), lambda b, s: (b + batch0, 0, 0)),
            pl.BlockSpec((1, D), const2),
            pl.BlockSpec((ROUTER_ROWS, D), const2),
        ],
        out_specs=[
            pl.BlockSpec((1, ATT_ROWS, D), lambda b, s: (b, s, 0)),
            pl.BlockSpec((ATT_ROWS, D // 2), lambda b, s: (b * n_s + s, 0)),
            pl.BlockSpec((ROUTER_ROWS, ATT_ROWS // LANES, LANES), lambda b, s: (0, b * n_s + s, 0)),
        ],
        out_shape=[
            jax.ShapeDtypeStruct((batches, S, D), BF16),
            jax.ShapeDtypeStruct((batches * S, D // 2), jnp.uint32),
            jax.ShapeDtypeStruct((ROUTER_ROWS, batches * S // LANES, LANES), F32),
        ],
        scratch_shapes=[pltpu.VMEM((ATT_ROWS, n_att), BF16)],
        compiler_params=pltpu.CompilerParams(
            dimension_semantics=("arbitrary", "arbitrary"), vmem_limit_bytes=VMEM_LIMIT),
        name="xattn",
    )(h1, norm_x, qk_mem, vo_mem, norm_ffn, w_router)


def _route_kernel(bias_ref, lg_ref, ids_ref, gates_ref, rank_ref, cnt_ref, base_ref):
    @pl.when(pl.program_id(0) == 0)
    def _():
        base_ref[...] = jnp.zeros_like(base_ref)

    best = lg_ref[0] + bias_ref[0]
    gl = [best]
    sel = jnp.zeros(best.shape, jnp.int32)
    for g in range(1, N_GROUPS):
        cur = lg_ref[g] + bias_ref[g]
        gl.append(cur)
        better = cur > best
        best = jnp.where(better, cur, best)
        sel = jnp.where(better, g, sel)
    denom = jnp.exp(gl[0] - best)
    for g in range(1, N_GROUPS):
        denom = denom + jnp.exp(gl[g] - best)
    g_gate = 1.0 / denom

    ev = []
    for j in range(EXPERTS_PER_GROUP):
        val = lg_ref[N_GROUPS + j] + bias_ref[N_GROUPS + j]
        for g in range(1, N_GROUPS):
            e = g * EXPERTS_PER_GROUP + j
            val = jnp.where(sel == g, lg_ref[N_GROUPS + e] + bias_ref[N_GROUPS + e], val)
        ev.append(val)
    v1, i1 = ev[0], jnp.zeros(best.shape, jnp.int32)
    for j in range(1, EXPERTS_PER_GROUP):
        better = ev[j] > v1
        v1 = jnp.where(better, ev[j], v1)
        i1 = jnp.where(better, j, i1)
    rest = [jnp.where(i1 == j, -jnp.inf, ev[j]) for j in range(EXPERTS_PER_GROUP)]
    v2, i2 = rest[0], jnp.zeros(best.shape, jnp.int32)
    for j in range(1, EXPERTS_PER_GROUP):
        better = rest[j] > v2
        v2 = jnp.where(better, rest[j], v2)
        i2 = jnp.where(better, j, i2)
    e2 = jnp.exp(v2 - v1)
    inv = 1.0 / (1.0 + e2)
    id1 = sel * EXPERTS_PER_GROUP + i1
    id2 = sel * EXPERTS_PER_GROUP + i2
    ids_ref[0] = id1
    ids_ref[1] = id2
    gates_ref[0] = inv * g_gate
    gates_ref[1] = e2 * inv * g_gate

    sub = ROUTE_SUB
    n = N_EXPERTS * sub
    li = lax.broadcasted_iota(jnp.int32, (LANES, LANES), 0)
    lj = lax.broadcasted_iota(jnp.int32, (LANES, LANES), 1)
    before_lane = jnp.where(li < lj, 1.0, 0.0).astype(BF16)
    ones = jnp.ones((LANES, LANES), BF16)
    ri = lax.broadcasted_iota(jnp.int32, (n, n), 0)
    rj = lax.broadcasted_iota(jnp.int32, (n, n), 1)
    same = _block_id(ri, sub) == _block_id(rj, sub)
    before_row = jnp.where(same & (rj < ri), 1.0, 0.0).astype(BF16)
    all_row = jnp.where(same, 1.0, 0.0).astype(BF16)
    tiles = [slice(k * sub, (k + 1) * sub) for k in range(ROUTE_TILES)]
    members = [jnp.concatenate(
        [jnp.where((id1[t] == e) | (id2[t] == e), 1.0, 0.0) for e in range(N_EXPERTS)], axis=0).astype(BF16)
        for t in tiles]
    in_row = [_dot(m, before_lane) for m in members]
    col_prev = [_dot(before_row, m).astype(BF16) for m in members]
    col_all = [_dot(all_row, m).astype(BF16) for m in members]
    prev_rows = [_dot(c, ones) for c in col_prev]
    totals = [_dot(c, ones) for c in col_all]
    base = base_ref[...]
    r1_tiles, r2_tiles = [], []
    for k, t in enumerate(tiles):
        pos = base + prev_rows[k] + in_row[k]
        r1 = jnp.zeros((sub, LANES), F32)
        r2 = jnp.zeros((sub, LANES), F32)
        for e in range(N_EXPERTS):
            pe = pos[e * sub:(e + 1) * sub]
            r1 = jnp.where(id1[t] == e, pe, r1)
            r2 = jnp.where(id2[t] == e, pe, r2)
        r1_tiles.append(r1)
        r2_tiles.append(r2)
        base = base + totals[k]
    rank_ref[0] = jnp.concatenate(r1_tiles, axis=0).astype(jnp.int32)
    rank_ref[1] = jnp.concatenate(r2_tiles, axis=0).astype(jnp.int32)
    base_ref[...] = base
    cnt_ref[...] = base


def _route(bias, logits3):
    rp, n_sub, _ = logits3.shape
    blk = lambda i: (0, i, 0)
    pair_i = jax.ShapeDtypeStruct((2, n_sub, LANES), jnp.int32)
    return pl.pallas_call(
        _route_kernel,
        grid=(n_sub // (ROUTE_SUB * ROUTE_TILES),),
        in_specs=[
            pl.BlockSpec(memory_space=pltpu.SMEM),
            pl.BlockSpec((rp, ROUTE_SUB * ROUTE_TILES, LANES), blk),
        ],
        out_specs=[
            pl.BlockSpec((2, ROUTE_SUB * ROUTE_TILES, LANES), blk),
            pl.BlockSpec((2, ROUTE_SUB * ROUTE_TILES, LANES), blk),
            pl.BlockSpec((2, ROUTE_SUB * ROUTE_TILES, LANES), blk),
            pl.BlockSpec((N_EXPERTS * ROUTE_SUB, LANES), lambda i: (0, 0)),
        ],
        out_shape=[
            pair_i,
            jax.ShapeDtypeStruct((2, n_sub, LANES), F32),
            pair_i,
            jax.ShapeDtypeStruct((N_EXPERTS * ROUTE_SUB, LANES), F32),
        ],
        scratch_shapes=[pltpu.VMEM((N_EXPERTS * ROUTE_SUB, LANES), F32)],
        compiler_params=pltpu.CompilerParams(
            dimension_semantics=("arbitrary",), vmem_limit_bytes=VMEM_LIMIT),
        name="route",
    )(bias, logits3)


def _dest_kernel(cnt_ref, ids_ref, rank_ref, dest_ref, be_ref, nb_ref, slot_ref, next_ref, start_ref, succ_ref):
    n_blocks = be_ref.shape[0]
    shift = EXPERT_ROWS.bit_length() - 1
    assert 1 << shift == EXPERT_ROWS

    @pl.when(pl.program_id(0) == 0)
    def _():
        def tail(i, carry):
            be_ref[i] = N_EXPERTS - 1
            slot_ref[i] = 0
            next_ref[i] = -1
            return carry

        lax.fori_loop(0, n_blocks, tail, 0)

        def successor(k, nxt):
            e = N_EXPERTS - 1 - k
            succ_ref[e] = nxt
            return jnp.where(cnt_ref[e] > 0, e, nxt)

        lax.fori_loop(0, N_EXPERTS, successor, jnp.int32(-1))

        def segment(e, carry):
            block0, ordinal = carry
            blocks = lax.shift_right_logical(cnt_ref[e] + (EXPERT_ROWS - 1), shift)
            start_ref[e] = lax.shift_left(block0, shift)

            def fill(j, c):
                be_ref[block0 + j] = e
                slot_ref[block0 + j] = ordinal & 1
                next_ref[block0 + j] = succ_ref[e]
                return c

            lax.fori_loop(0, blocks, fill, 0)
            return block0 + blocks, ordinal + (cnt_ref[e] > 0).astype(jnp.int32)

        used, _ = lax.fori_loop(0, N_EXPERTS, segment, (jnp.int32(0), jnp.int32(0)))
        nb_ref[0] = used

    ids = ids_ref[...]
    off = jnp.zeros(ids.shape, jnp.int32)
    for e in range(N_EXPERTS):
        off = jnp.where(ids == e, start_ref[e], off)
    dest_ref[...] = rank_ref[...] + off


def _dest(counts, ids, rank, n_blocks):
    _, n_sub, _ = ids.shape
    blk = pl.BlockSpec((2, ROUTE_SUB, LANES), lambda i: (0, i, 0))
    smem = pl.BlockSpec(memory_space=pltpu.SMEM)
    table = jax.ShapeDtypeStruct((n_blocks,), jnp.int32)
    return pl.pallas_call(
        _dest_kernel,
        grid=(n_sub // ROUTE_SUB,),
        in_specs=[smem, blk, blk],
        out_specs=[blk, smem, smem, smem, smem],
        out_shape=[jax.ShapeDtypeStruct(ids.shape, jnp.int32), table, jax.ShapeDtypeStruct((1,), jnp.int32),
                   table, table],
        scratch_shapes=[pltpu.SMEM((N_EXPERTS,), jnp.int32), pltpu.SMEM((N_EXPERTS,), jnp.int32)],
        compiler_params=pltpu.CompilerParams(dimension_semantics=("arbitrary",)),
        name="dest",
    )(counts, ids, rank)


def _sc_mesh():
    return plsc.VectorSubcoreMesh(core_axis_name="core", subcore_axis_name="subcore")


def _sc_worker(rows_total):
    rows = rows_total // _SC_WORKERS
    wid = lax.axis_index("core") * SC_SUBCORES + lax.axis_index("subcore")
    return wid * rows, rows


def _dispatch(dest0, dest1, xn, n_slots):
    T, D = xn.shape
    W = SC_WINDOW
    rows = T // _SC_WORKERS
    assert T % (_SC_WORKERS * 2 * W) == 0

    @pl.kernel(out_type=jax.ShapeDtypeStruct((n_slots, D), xn.dtype), mesh=_sc_mesh(),
               scratch_types=[pltpu.VMEM((rows,), jnp.int32), pltpu.VMEM((rows,), jnp.int32),
                              pltpu.VMEM((2, W, D), xn.dtype),
                              pltpu.SemaphoreType.DMA((2,)), pltpu.SemaphoreType.DMA((2,))])
    def scatter_rows(x_hbm, d0_hbm, d1_hbm, xs_hbm, d0_v, d1_v, buf, in_sem, out_sem):
        base, _ = _sc_worker(T)
        pltpu.sync_copy(d0_hbm.at[pl.ds(base, rows)], d0_v)
        pltpu.sync_copy(d1_hbm.at[pl.ds(base, rows)], d1_v)

        def load(w, slot):
            return pltpu.make_async_copy(x_hbm.at[pl.ds(base + w * W, W)], buf.at[slot], in_sem.at[slot])

        def store(w, slot, d_v):
            return pltpu.make_async_copy(buf.at[slot], xs_hbm.at[d_v.at[pl.ds(w * W, W)]], out_sem.at[slot])

        def step(w, slot):
            load(w, slot).wait()
            store(w, slot, d0_v).start()
            store(w, slot, d1_v).start()
            store(w, slot, d0_v).wait()
            store(w, slot, d1_v).wait()

        n = rows // W
        load(0, 0).start()

        @pl.loop(0, n, step=2)
        def _(w):
            load(w + 1, 1).start()
            step(w, 0)

            @pl.when(w + 2 < n)
            def _():
                load(w + 2, 0).start()

            step(w + 1, 1)

    return scatter_rows(xn, dest0, dest1)


def _gather_rows(src, idx):
    M = idx.shape[0]
    D = src.shape[1]
    W = SC_WINDOW
    rows = M // _SC_WORKERS
    assert M % (_SC_WORKERS * 2 * W) == 0

    @pl.kernel(out_type=jax.ShapeDtypeStruct((M, D), src.dtype), mesh=_sc_mesh(),
               scratch_types=[pltpu.VMEM((rows,), jnp.int32), pltpu.VMEM((2, W, D), src.dtype),
                              pltpu.SemaphoreType.DMA((2,)), pltpu.SemaphoreType.DMA((2,))])
    def gather_rows(src_hbm, i_hbm, o_hbm, i_v, buf, in_sem, out_sem):
        base, _ = _sc_worker(M)
        pltpu.sync_copy(i_hbm.at[pl.ds(base, rows)], i_v)

        def load(w, slot):
            return pltpu.make_async_copy(src_hbm.at[i_v.at[pl.ds(w * W, W)]], buf.at[slot], in_sem.at[slot])

        def store(w, slot):
            return pltpu.make_async_copy(buf.at[slot], o_hbm.at[pl.ds(base + w * W, W)], out_sem.at[slot])

        n = rows // W
        load(0, 0).start()

        @pl.loop(0, n, step=2)
        def _(w):
            @pl.when(w > 0)
            def _():
                store(w - 1, 1).wait()

            load(w + 1, 1).start()
            load(w, 0).wait()
            store(w, 0).start()
            store(w, 0).wait()

            @pl.when(w + 2 < n)
            def _():
                load(w + 2, 0).start()

            load(w + 1, 1).wait()
            store(w + 1, 1).start()

        store(n - 1, 1).wait()

    return gather_rows(src, idx)


def _expert_kernel(be_ref, nb_ref, slot_ref, next_ref, x_ref, wg_hbm, wu_hbm, wd_hbm, y_ref,
                   wg_f32, wu_f32, wd_f32, wg_ref, wu_ref, wd_ref, sem):
    i = pl.program_id(0)
    used = i < nb_ref[0]
    new_expert = (i == 0) | (be_ref[i] != be_ref[jnp.maximum(i - 1, 0)])

    def fetch(e, slot):
        pairs = ((wg_hbm, wg_f32), (wu_hbm, wu_f32), (wd_hbm, wd_f32))
        return [pltpu.make_async_copy(w_hbm.at[e], w_f32.at[slot], sem.at[slot, k])
                for k, (w_hbm, w_f32) in enumerate(pairs)]

    @pl.when(used & (i == 0))
    def _():
        for copy in fetch(be_ref[0], slot_ref[0]):
            copy.start()

    @pl.when(used & new_expert)
    def _():
        slot = slot_ref[i]
        for copy in fetch(be_ref[i], slot):
            copy.wait()

        @pl.when(next_ref[i] >= 0)
        def _():
            for copy in fetch(next_ref[i], 1 - slot):
                copy.start()

        wg_ref[...] = wg_f32[slot].astype(BF16)
        wu_ref[...] = wu_f32[slot].astype(BF16)
        wd_ref[...] = wd_f32[slot].astype(BF16)

    @pl.when(used)
    def _():
        x = _unpack_pairs(x_ref[...]).astype(BF16)
        g = _dot(x, wg_ref[...])
        u = _dot(x, wu_ref[...])
        hid = (g * _sigmoid(g) * u).astype(BF16)
        y_ref[...] = _pack_pairs(_dot(hid, wd_ref[...]))

    @pl.when(jnp.logical_not(used))
    def _():
        y_ref[...] = jnp.zeros_like(y_ref)


def _experts(block_expert, n_used, block_slot, block_next, xs, w_gate, w_up, w_down):
    n_slots, half = xs.shape
    D = 2 * half
    n_blocks = n_slots // EXPERT_ROWS
    grid_spec = pltpu.PrefetchScalarGridSpec(
        num_scalar_prefetch=4,
        grid=(n_blocks,),
        in_specs=[
            pl.BlockSpec((EXPERT_ROWS, half), lambda i, be, nb, sl, nx: (jnp.minimum(i, nb[0] - 1), 0)),
            pl.BlockSpec(memory_space=pl.ANY),
            pl.BlockSpec(memory_space=pl.ANY),
            pl.BlockSpec(memory_space=pl.ANY),
        ],
        out_specs=pl.BlockSpec((EXPERT_ROWS, half), lambda i, be, nb, sl, nx: (i, 0)),
        scratch_shapes=[
            pltpu.VMEM((2, D, D_EXPERT), F32), pltpu.VMEM((2, D, D_EXPERT), F32),
            pltpu.VMEM((2, D_EXPERT, D), F32),
            pltpu.VMEM((D, D_EXPERT), BF16), pltpu.VMEM((D, D_EXPERT), BF16), pltpu.VMEM((D_EXPERT, D), BF16),
            pltpu.SemaphoreType.DMA((2, 3)),
        ],
    )
    return pl.pallas_call(
        _expert_kernel,
        grid_spec=grid_spec,
        out_shape=jax.ShapeDtypeStruct((n_slots, half), jnp.uint32),
        compiler_params=pltpu.CompilerParams(
            dimension_semantics=("arbitrary",), vmem_limit_bytes=VMEM_LIMIT),
        name="experts",
    )(block_expert, n_used, block_slot, block_next, xs, w_gate, w_up, w_down)


def _combine_kernel(y0_ref, y1_ref, h_ref, gates_ref, nfin_ref, *rest):
    o_ref = rest[-1]
    g0 = gates_ref[0].T
    g1 = gates_ref[1].T
    for r in range(gates_ref.shape[1]):
        rs = slice(r * LANES, (r + 1) * LANES)
        h = (h_ref[rs].astype(F32) + g0[:, r:r + 1] * _unpack_pairs(y0_ref[rs])
             + g1[:, r:r + 1] * _unpack_pairs(y1_ref[rs]))
        o_ref[rs] = _rms(h, nfin_ref[...])


def _combine(y01, h2, gates, norm_final, out_prev, row0, total_rows):
    T, D = h2.shape
    n_t = T // MOVE_ROWS
    in_specs = [
        pl.BlockSpec((MOVE_ROWS, D // 2), lambda i: (i, 0)),
        pl.BlockSpec((MOVE_ROWS, D // 2), lambda i: (i + n_t, 0)),
        pl.BlockSpec((MOVE_ROWS, D), lambda i: (i, 0)),
        pl.BlockSpec((2, MOVE_ROWS // LANES, LANES), lambda i: (0, i, 0)),
        pl.BlockSpec((1, D), lambda i: (0, 0)),
    ]
    args = [y01, y01, h2, gates, norm_final]
    aliases = {}
    if out_prev is not None:
        in_specs.append(pl.BlockSpec(memory_space=pl.ANY))
        args.append(out_prev)
        aliases = {len(args) - 1: 0}
    return pl.pallas_call(
        _combine_kernel,
        grid=(n_t,),
        in_specs=in_specs,
        out_specs=pl.BlockSpec((MOVE_ROWS, D), lambda i: (i + row0 // MOVE_ROWS, 0)),
        out_shape=jax.ShapeDtypeStruct((total_rows, D), F32),
        input_output_aliases=aliases,
        compiler_params=pltpu.CompilerParams(
            dimension_semantics=("arbitrary",), vmem_limit_bytes=VMEM_LIMIT),
        name="combine",
    )(*args)


def kernel(x, mem, norm_mix, w_in, gmlp_ln, gmlp_w_spatial, gmlp_b_spatial, gmlp_beta, hgrn_lb_logits, hgrn_out_gain, w_out, norm_xattn, norm_mem, w_xq, w_xkv, w_xo, norm_ffn, w_router_group, b_router_group, w_router_expert, b_router_expert, w_expert_gate, w_expert_up, w_expert_down, norm_final):
    B, S, D = x.shape
    T = B * S
    depth = w_in.shape[0]
    assert depth == 1 and hgrn_lb_logits.shape[0] == 2
    assert D == D_MODEL and mem.shape[1] == N_MEM and w_in.shape[2] == D_IN_PROJ
    assert S % MIX_ROWS == 0 and S % ATT_ROWS == 0 and B % MEMKV_BATCHES == 0
    l = 0
    row = lambda p: p.reshape(1, -1)

    h1 = _mixer(x, row(norm_mix[l]), w_in[l], row(gmlp_ln[l]), gmlp_w_spatial[l],
                gmlp_b_spatial[l].T, row(gmlp_beta[l]), hgrn_lb_logits, row(hgrn_out_gain[l]),
                w_out[l])
    qk_mem, vo_mem = _memkv(mem, row(norm_mem[l]), w_xkv[l], w_xq[l], w_xo[l])

    w_router = jnp.concatenate([w_router_group[l].T, w_router_expert[l].T], axis=0)
    w_router = jnp.pad(w_router, ((0, ROUTER_ROWS - w_router.shape[0]), (0, 0)))
    w_router = w_router.astype(BF16)
    bias = jnp.concatenate([b_router_group[l], b_router_expert[l]]).astype(F32)

    assert sum(PART_BATCHES) == B
    out = None
    b0 = 0
    for b_part in PART_BATCHES:
        t_part = b_part * S
        n_blocks = (2 * t_part) // EXPERT_ROWS + N_EXPERTS
        h2, xn, logits = _xattn(h1, row(norm_xattn[l]), qk_mem, vo_mem, row(norm_ffn[l]), w_router, b0, b_part)
        ids, gates, rank, counts = _route(bias, logits)

        counts = counts[::ROUTE_SUB, 0].astype(jnp.int32)
        dest, block_expert, n_used, block_slot, block_next = _dest(counts, ids, rank, n_blocks)
        dest = dest.reshape(2, t_part)
        xs = _dispatch(dest[0], dest[1], xn, n_blocks * EXPERT_ROWS)
        yb = _experts(block_expert, n_used, block_slot, block_next, xs,
                      w_expert_gate[l], w_expert_up[l], w_expert_down[l])
        y01 = _gather_rows(yb, dest.reshape(2 * t_part))
        out = _combine(y01, h2.reshape(t_part, D), gates, row(norm_final), out, b0 * S, T)
        b0 += b_part
    return out.reshape(B, S, D)
```

```python
import jax
import jax.numpy as jnp
from jax import lax
from jax.experimental import pallas as pl
from jax.experimental.pallas import tpu as pltpu
from jax.experimental.pallas import tpu_sc as plsc

F32 = jnp.float32
BF16 = jnp.bfloat16
EPS = 1e-6

D_MODEL = 1024
D_GMLP = 512
GMLP_GROUPS = 4
GMLP_CHUNK = 128
D_HGRN = 512
HGRN_HEADS = 4
HGRN_DK = 128
HGRN_CHUNK = 64
D_IN_PROJ = 2 * D_GMLP + 4 * D_HGRN
N_MEM = 256
XATTN_HEADS = 4
XATTN_HEAD_DIM = D_MODEL // XATTN_HEADS
N_GROUPS = 4
EXPERTS_PER_GROUP = 8
N_EXPERTS = N_GROUPS * EXPERTS_PER_GROUP
D_EXPERT = 512

LANES = 128
MIX_ROWS = 1024
MIX_CHAIN_ROWS = 256


def _mix_schedule(n_chains):
    order = [("in", 0), ("gmlp", 0), ("factors", 0)]
    for k in range(n_chains):
        more = k + 1 < n_chains
        order += [("in", k + 1)] * more + [("local", k), ("recurrence", k)] + [("gmlp", k + 1)] * more
        order += [("out", k)] + [("factors", k + 1)] * more
    return tuple(order)


MIX_SCHEDULE = _mix_schedule(MIX_ROWS // MIX_CHAIN_ROWS)
MEMKV_BATCHES = 2
ATT_ROWS = 1024
WEIGHT_STAGE_ROWS = 128
ROUTER_ROWS = 40
ROUTE_SUB = 16
ROUTE_TILES = 4
EXPERT_ROWS = 512
MOVE_ROWS = 1024
PART_BATCHES = (20, 12)
SC_WINDOW = 64
SC_CORES = 2
SC_SUBCORES = 16
_SC_WORKERS = SC_CORES * SC_SUBCORES
VMEM_LIMIT = 48 * 1024 * 1024


def _rms(x, gain):
    return x * lax.rsqrt(jnp.mean(x * x, axis=-1, keepdims=True) + EPS) * gain


def _dot(a, b):
    return jnp.dot(a, b, preferred_element_type=F32)


def _dot_nt(a, b):
    return lax.dot_general(a, b, (((1,), (1,)), ((), ())), preferred_element_type=F32)


def _dot_tn(a, b):
    return lax.dot_general(a, b, (((0,), (0,)), ((), ())), preferred_element_type=F32)


def _gelu(x):
    return 0.5 * x * (1.0 + jnp.tanh(0.7978845608028654 * (x + 0.044715 * (x * x * x))))


def _sigmoid(x):
    return 1.0 / (1.0 + jnp.exp(-x))


def _block_id(idx, size):
    assert size & (size - 1) == 0
    return lax.shift_right_logical(idx, size.bit_length() - 1)


def _stage_weight(w_hbm, w_bf16, stage_ref, sem):
    rows = stage_ref.shape[0]
    for k in range(w_hbm.shape[0] // rows):
        copy = pltpu.make_async_copy(w_hbm.at[pl.ds(k * rows, rows)], stage_ref, sem)
        copy.start()
        copy.wait()
        w_bf16[k * rows:(k + 1) * rows, :] = stage_ref[...].astype(BF16)


def _weight_scratch(k, n):
    return [pltpu.VMEM((k, n), BF16), pltpu.VMEM((WEIGHT_STAGE_ROWS, n), F32)]


_HIGH_HALF = 0xFFFF0000


def _pack_pairs(x):
    c = x.shape[1] // 2
    bits = lax.bitcast_convert_type(x.astype(BF16).astype(F32), jnp.uint32)
    return (bits[:, c:] & jnp.uint32(_HIGH_HALF)) | lax.shift_right_logical(bits[:, :c], jnp.uint32(16))


def _unpack_pairs(w):
    lo = lax.bitcast_convert_type(lax.shift_left(w, jnp.uint32(16)), F32)
    hi = lax.bitcast_convert_type(w & jnp.uint32(_HIGH_HALF), F32)
    return jnp.concatenate([lo, hi], axis=1)


def _mixer_kernel(x_ref, nmix_ref, win_hbm, gln_ref, ws_ref, bst_ref, beta_ref, lbl_ref, og_ref,
                  wout_hbm, o_ref, proj_ref, ycat_ref, state_ref, win_ref, win_stage, wout_ref, wout_stage,
                  sem):
    @pl.when((pl.program_id(0) == 0) & (pl.program_id(1) == 0))
    def _():
        _stage_weight(win_hbm, win_ref, win_stage, sem)
        _stage_weight(wout_hbm, wout_ref, wout_stage, sem)

    @pl.when(pl.program_id(1) == 0)
    def _():
        state_ref[...] = jnp.zeros_like(state_ref)

    n = MIX_CHAIN_ROWS
    r_i = lax.broadcasted_iota(jnp.int32, (GMLP_CHUNK, GMLP_CHUNK), 0)
    c_i = lax.broadcasted_iota(jnp.int32, (GMLP_CHUNK, GMLP_CHUNK), 1)
    causal = c_i <= r_i
    w_tril = [jnp.where(causal, ws_ref[g], 0.0).astype(BF16) for g in range(GMLP_GROUPS)]
    lbl = lbl_ref[...]
    e_lb = jnp.exp(lbl - jnp.max(lbl, axis=0, keepdims=True))
    lb = e_lb[0:1] / jnp.sum(e_lb, axis=0, keepdims=True)
    rr = lax.broadcasted_iota(jnp.int32, (n, n), 0)
    cc = lax.broadcasted_iota(jnp.int32, (n, n), 1)
    tri = jnp.where((_block_id(rr, HGRN_CHUNK) == _block_id(cc, HGRN_CHUNK)) & (cc <= rr),
                    1.0, 0.0).astype(BF16)
    r64 = lax.broadcasted_iota(jnp.int32, (HGRN_CHUNK, HGRN_CHUNK), 0)
    c64 = lax.broadcasted_iota(jnp.int32, (HGRN_CHUNK, HGRN_CHUNK), 1)
    causal64 = c64 <= r64
    base = 2 * D_GMLP

    n_chunks = n // HGRN_CHUNK
    chains = range(x_ref.shape[1] // n)
    env = {ch: {} for ch in chains}

    def rows(ch):
        return slice(ch * n, (ch + 1) * n)

    def in_proj(ch):
        a = _rms(x_ref[0, rows(ch)], nmix_ref[...]).astype(BF16)
        proj_ref[rows(ch)] = _dot(a, win_ref[...])

    def gmlp(ch):
        u = _gelu(proj_ref[rows(ch), 0:D_GMLP])
        v = _gelu(proj_ref[rows(ch), D_GMLP:2 * D_GMLP])
        vc = v - jnp.mean(v, axis=-1, keepdims=True)
        vn = (vc * lax.rsqrt(jnp.mean(vc * vc, axis=-1, keepdims=True) + EPS) * gln_ref[...]).astype(BF16)
        z_rows = []
        for c in range(n // GMLP_CHUNK):
            z_cols = []
            for g in range(GMLP_GROUPS):
                vg = vn[c * GMLP_CHUNK:(c + 1) * GMLP_CHUNK, g * LANES:(g + 1) * LANES]
                z_cols.append(_dot(w_tril[g], vg) + bst_ref[:, g:g + 1])
            z_rows.append(jnp.concatenate(z_cols, axis=1))
        z = jnp.concatenate(z_rows, axis=0)
        ycat_ref[rows(ch), 0:D_GMLP] = _rms(u * z, beta_ref[...]).astype(BF16)

    def hgrn_factors(ch):
        e = env[ch]
        f = lb + (1.0 - lb) * _sigmoid(proj_ref[rows(ch), base + D_HGRN:base + 2 * D_HGRN])
        log_f = jnp.log(f)
        lf_hi = log_f.astype(BF16)
        lf_lo = (log_f - lf_hi.astype(F32)).astype(BF16)
        b_all = _dot(tri, lf_hi) + _dot(tri, lf_lo)
        bl_rows = [b_all[c * HGRN_CHUNK + HGRN_CHUNK - 1:(c + 1) * HGRN_CHUNK] for c in range(n_chunks)]
        bl_all = jnp.concatenate([jnp.broadcast_to(r, (HGRN_CHUNK, D_HGRN)) for r in bl_rows], axis=0)
        q_all = proj_ref[rows(ch), base:base + D_HGRN]
        k_all = 1.0 - f
        e["qd"] = (q_all * _sigmoid(q_all) * jnp.exp(b_all)).astype(BF16)
        e["ki"] = (k_all * jnp.exp(-b_all)).astype(BF16)
        e["kte"] = (k_all * jnp.exp(bl_all - b_all)).astype(BF16)
        e["v"] = proj_ref[rows(ch), base + 2 * D_HGRN:base + 3 * D_HGRN].astype(BF16)
        e["decay"] = [jnp.exp(r) for r in bl_rows]

    def hgrn_local(ch):
        e = env[ch]
        e["o_intra"], e["d_state"] = {}, {}
        for c in range(n_chunks):
            rs = slice(c * HGRN_CHUNK, (c + 1) * HGRN_CHUNK)
            for h in range(HGRN_HEADS):
                cs = slice(h * HGRN_DK, (h + 1) * HGRN_DK)
                scores = jnp.where(causal64, _dot_nt(e["qd"][rs, cs], e["ki"][rs, cs]), 0.0).astype(BF16)
                e["o_intra"][c, h] = _dot(scores, e["v"][rs, cs])
                e["d_state"][c, h] = _dot_tn(e["v"][rs, cs], e["kte"][rs, cs])

    def hgrn_recurrence(ch):
        e = env[ch]
        for c in range(n_chunks):
            rs = slice(c * HGRN_CHUNK, (c + 1) * HGRN_CHUNK)
            ps = slice(ch * n + c * HGRN_CHUNK, ch * n + (c + 1) * HGRN_CHUNK)
            g_c = proj_ref[ps, base + 3 * D_HGRN:base + 4 * D_HGRN]
            gate = og_ref[...] * (g_c * _sigmoid(g_c))
            for h in range(HGRN_HEADS):
                cs = slice(h * HGRN_DK, (h + 1) * HGRN_DK)
                st = state_ref[h]
                o = e["o_intra"][c, h] + _dot_nt(e["qd"][rs, cs], st.astype(BF16))
                state_ref[h] = st * e["decay"][c][:, cs] + e["d_state"][c, h]
                o = o * lax.rsqrt(jnp.mean(o * o, axis=-1, keepdims=True) + EPS)
                ycat_ref[ps, D_GMLP + h * HGRN_DK:D_GMLP + (h + 1) * HGRN_DK] = (o * gate[:, cs]).astype(BF16)

    def out_proj(ch):
        o_ref[0, rows(ch)] = x_ref[0, rows(ch)] + _dot(ycat_ref[rows(ch)], wout_ref[...])

    stages = {"in": in_proj, "gmlp": gmlp, "factors": hgrn_factors, "local": hgrn_local,
              "recurrence": hgrn_recurrence, "out": out_proj}
    for stage, ch in MIX_SCHEDULE:
        stages[stage](ch)


def _mixer(x, norm_mix, w_in, gmlp_ln, w_s, b_s_t, beta, lb_logits, out_gain, w_out):
    B, S, D = x.shape
    const2 = lambda b, s: (0, 0)
    return pl.pallas_call(
        _mixer_kernel,
        grid=(B, S // MIX_ROWS),
        in_specs=[
            pl.BlockSpec((1, MIX_ROWS, D), lambda b, s: (b, s, 0)),
            pl.BlockSpec((1, D), const2),
            pl.BlockSpec(memory_space=pl.ANY),
            pl.BlockSpec((1, D_GMLP), const2),
            pl.BlockSpec((GMLP_GROUPS, GMLP_CHUNK, GMLP_CHUNK), lambda b, s: (0, 0, 0)),
            pl.BlockSpec((GMLP_CHUNK, GMLP_GROUPS), const2),
            pl.BlockSpec((1, D_GMLP), const2),
            pl.BlockSpec(lb_logits.shape, const2),
            pl.BlockSpec((1, D_HGRN), const2),
            pl.BlockSpec(memory_space=pl.ANY),
        ],
        out_specs=pl.BlockSpec((1, MIX_ROWS, D), lambda b, s: (b, s, 0)),
        out_shape=jax.ShapeDtypeStruct((B, S, D), F32),
        scratch_shapes=[
            pltpu.VMEM((MIX_ROWS, D_IN_PROJ), F32),
            pltpu.VMEM((MIX_ROWS, D), BF16),
            pltpu.VMEM((HGRN_HEADS, HGRN_DK, HGRN_DK), F32),
        ] + _weight_scratch(D, D_IN_PROJ) + _weight_scratch(D, D) + [pltpu.SemaphoreType.DMA],
        compiler_params=pltpu.CompilerParams(
            dimension_semantics=("arbitrary", "arbitrary"), vmem_limit_bytes=VMEM_LIMIT),
        name="mixer",
    )(x, norm_mix, w_in, gmlp_ln, w_s, b_s_t, beta, lb_logits, out_gain, w_out)


def _memkv_kernel(mem_ref, nm_ref, wkv_hbm, wq_hbm, wo_hbm, qk_ref, vo_ref,
                  wkv_ref, wkv_stage, wq_ref, wq_stage, wo_ref, wo_stage, sem):
    @pl.when(pl.program_id(0) == 0)
    def _():
        _stage_weight(wkv_hbm, wkv_ref, wkv_stage, sem)
        _stage_weight(wq_hbm, wq_ref, wq_stage, sem)
        _stage_weight(wo_hbm, wo_ref, wo_stage, sem)

    nb, M, D = mem_ref.shape
    m = _rms(mem_ref[...].reshape(nb * M, D), nm_ref[...]).astype(BF16)
    kv = _dot(m, wkv_ref[...])
    k = kv[:, :D_MODEL].astype(BF16)
    v = kv[:, D_MODEL:].astype(BF16)
    for b in range(nb):
        rs = slice(b * M, (b + 1) * M)
        for hd in range(XATTN_HEADS):
            cs = slice(hd * XATTN_HEAD_DIM, (hd + 1) * XATTN_HEAD_DIM)
            ms = slice(hd * M, (hd + 1) * M)
            qk_ref[b, :, ms] = (_dot_nt(wq_ref[:, cs], k[rs, cs]) * (XATTN_HEAD_DIM ** -0.5)).astype(BF16)
            vo_ref[b, ms, :] = _dot(v[rs, cs], wo_ref[cs, :]).astype(BF16)


def _memkv(mem, norm_mem, w_kv, w_q, w_o):
    B, M, D = mem.shape
    out = jax.ShapeDtypeStruct((B, D, XATTN_HEADS * M), BF16), jax.ShapeDtypeStruct((B, XATTN_HEADS * M, D), BF16)
    return pl.pallas_call(
        _memkv_kernel,
        grid=(B // MEMKV_BATCHES,),
        in_specs=[
            pl.BlockSpec((MEMKV_BATCHES, M, D), lambda b: (b, 0, 0)),
            pl.BlockSpec((1, D), lambda b: (0, 0)),
            pl.BlockSpec(memory_space=pl.ANY),
            pl.BlockSpec(memory_space=pl.ANY),
            pl.BlockSpec(memory_space=pl.ANY),
        ],
        out_specs=[pl.BlockSpec((MEMKV_BATCHES, D, XATTN_HEADS * M), lambda b: (b, 0, 0)),
                   pl.BlockSpec((MEMKV_BATCHES, XATTN_HEADS * M, D), lambda b: (b, 0, 0))],
        out_shape=list(out),
        scratch_shapes=(_weight_scratch(D, 2 * D) + _weight_scratch(D, D) + _weight_scratch(D, D)
                        + [pltpu.SemaphoreType.DMA]),
        compiler_params=pltpu.CompilerParams(
            dimension_semantics=("arbitrary",), vmem_limit_bytes=VMEM_LIMIT),
        name="memkv",
    )(mem, norm_mem, w_kv, w_q, w_o)


def _xattn_kernel(h_ref, nx_ref, qk_ref, vo_ref, nf_ref, wr_ref, h2_ref, xn_ref, lg_ref, att_ref):
    h = h_ref[0]
    hn = _rms(h, nx_ref[...]).astype(BF16)
    scores = _dot(hn, qk_ref[0])
    n_mem = qk_ref.shape[2] // XATTN_HEADS
    for hd in range(XATTN_HEADS):
        ms = slice(hd * n_mem, (hd + 1) * n_mem)
        s = scores[:, ms]
        p = jnp.exp(s - jnp.max(s, axis=-1, keepdims=True))
        att_ref[:, ms] = (p / jnp.sum(p, axis=-1, keepdims=True)).astype(BF16)
    h2 = h + _dot(att_ref[...], vo_ref[0])
    h2_ref[0] = h2.astype(h2_ref.dtype)
    xn = _rms(h2, nf_ref[...])
    xn_ref[...] = _pack_pairs(xn)
    lg = _dot_nt(wr_ref[...], xn.astype(BF16))
    for j in range(ATT_ROWS // LANES):
        lg_ref[:, j, :] = lg[:, j * LANES:(j + 1) * LANES]


def _xattn(h1, norm_x, qk_mem, vo_mem, norm_ffn, w_router, batch0, batches):
    _, S, D = h1.shape
    n_s = S // ATT_ROWS
    n_att = qk_mem.shape[2]
    const2 = lambda b, s: (0, 0)
    return pl.pallas_call(
        _xattn_kernel,
        grid=(batches, n_s),
        in_specs=[
            pl.BlockSpec((1, ATT_ROWS, D), lambda b, s: (b + batch0, s, 0)),
            pl.BlockSpec((1, D), const2),
            pl.BlockSpec((1, D, n_att), lambda b, s: (b + batch0, 0, 0)),
            pl.BlockSpec((1, n_att, D), lambda b, s: (b + batch0, 0, 0)),
            pl.BlockSpec((1, D), const2),
            pl.BlockSpec((ROUTER_ROWS, D), const2),
        ],
        out_specs=[
            pl.BlockSpec((1, ATT_ROWS, D), lambda b, s: (b, s, 0)),
            pl.BlockSpec((ATT_ROWS, D // 2), lambda b, s: (b * n_s + s, 0)),
            pl.BlockSpec((ROUTER_ROWS, ATT_ROWS // LANES, LANES), lambda b, s: (0, b * n_s + s, 0)),
        ],
        out_shape=[
            jax.ShapeDtypeStruct((batches, S, D), BF16),
            jax.ShapeDtypeStruct((batches * S, D // 2), jnp.uint32),
            jax.ShapeDtypeStruct((ROUTER_ROWS, batches * S // LANES, LANES), F32),
        ],
        scratch_shapes=[pltpu.VMEM((ATT_ROWS, n_att), BF16)],
        compiler_params=pltpu.CompilerParams(
            dimension_semantics=("arbitrary", "arbitrary"), vmem_limit_bytes=VMEM_LIMIT),
        name="xattn",
    )(h1, norm_x, qk_mem, vo_mem, norm_ffn, w_router)


def _route_kernel(bias_ref, lg_ref, ids_ref, gates_ref, rank_ref, cnt_ref, base_ref):
    @pl.when(pl.program_id(0) == 0)
    def _():
        base_ref[...] = jnp.zeros_like(base_ref)

    best = lg_ref[0] + bias_ref[0]
    gl = [best]
    sel = jnp.zeros(best.shape, jnp.int32)
    for g in range(1, N_GROUPS):
        cur = lg_ref[g] + bias_ref[g]
        gl.append(cur)
        better = cur > best
        best = jnp.where(better, cur, best)
        sel = jnp.where(better, g, sel)
    denom = jnp.exp(gl[0] - best)
    for g in range(1, N_GROUPS):
        denom = denom + jnp.exp(gl[g] - best)
    g_gate = 1.0 / denom

    ev = []
    for j in range(EXPERTS_PER_GROUP):
        val = lg_ref[N_GROUPS + j] + bias_ref[N_GROUPS + j]
        for g in range(1, N_GROUPS):
            e = g * EXPERTS_PER_GROUP + j
            val = jnp.where(sel == g, lg_ref[N_GROUPS + e] + bias_ref[N_GROUPS + e], val)
        ev.append(val)
    v1, i1 = ev[0], jnp.zeros(best.shape, jnp.int32)
    for j in range(1, EXPERTS_PER_GROUP):
        better = ev[j] > v1
        v1 = jnp.where(better, ev[j], v1)
        i1 = jnp.where(better, j, i1)
    rest = [jnp.where(i1 == j, -jnp.inf, ev[j]) for j in range(EXPERTS_PER_GROUP)]
    v2, i2 = rest[0], jnp.zeros(best.shape, jnp.int32)
    for j in range(1, EXPERTS_PER_GROUP):
        better = rest[j] > v2
        v2 = jnp.where(better, rest[j], v2)
        i2 = jnp.where(better, j, i2)
    e2 = jnp.exp(v2 - v1)
    inv = 1.0 / (1.0 + e2)
    id1 = sel * EXPERTS_PER_GROUP + i1
    id2 = sel * EXPERTS_PER_GROUP + i2
    ids_ref[0] = id1
    ids_ref[1] = id2
    gates_ref[0] = inv * g_gate
    gates_ref[1] = e2 * inv * g_gate

    sub = ROUTE_SUB
    n = N_EXPERTS * sub
    li = lax.broadcasted_iota(jnp.int32, (LANES, LANES), 0)
    lj = lax.broadcasted_iota(jnp.int32, (LANES, LANES), 1)
    before_lane = jnp.where(li < lj, 1.0, 0.0).astype(BF16)
    ones = jnp.ones((LANES, LANES), BF16)
    ri = lax.broadcasted_iota(jnp.int32, (n, n), 0)
    rj = lax.broadcasted_iota(jnp.int32, (n, n), 1)
    same = _block_id(ri, sub) == _block_id(rj, sub)
    before_row = jnp.where(same & (rj < ri), 1.0, 0.0).astype(BF16)
    all_row = jnp.where(same, 1.0, 0.0).astype(BF16)
    tiles = [slice(k * sub, (k + 1) * sub) for k in range(ROUTE_TILES)]
    members = [jnp.concatenate(
        [jnp.where((id1[t] == e) | (id2[t] == e), 1.0, 0.0) for e in range(N_EXPERTS)], axis=0).astype(BF16)
        for t in tiles]
    in_row = [_dot(m, before_lane) for m in members]
    col_prev = [_dot(before_row, m).astype(BF16) for m in members]
    col_all = [_dot(all_row, m).astype(BF16) for m in members]
    prev_rows = [_dot(c, ones) for c in col_prev]
    totals = [_dot(c, ones) for c in col_all]
    base = base_ref[...]
    r1_tiles, r2_tiles = [], []
    for k, t in enumerate(tiles):
        pos = base + prev_rows[k] + in_row[k]
        r1 = jnp.zeros((sub, LANES), F32)
        r2 = jnp.zeros((sub, LANES), F32)
        for e in range(N_EXPERTS):
            pe = pos[e * sub:(e + 1) * sub]
            r1 = jnp.where(id1[t] == e, pe, r1)
            r2 = jnp.where(id2[t] == e, pe, r2)
        r1_tiles.append(r1)
        r2_tiles.append(r2)
        base = base + totals[k]
    rank_ref[0] = jnp.concatenate(r1_tiles, axis=0).astype(jnp.int32)
    rank_ref[1] = jnp.concatenate(r2_tiles, axis=0).astype(jnp.int32)
    base_ref[...] = base
    cnt_ref[...] = base


def _route(bias, logits3):
    rp, n_sub, _ = logits3.shape
    blk = lambda i: (0, i, 0)
    pair_i = jax.ShapeDtypeStruct((2, n_sub, LANES), jnp.int32)
    return pl.pallas_call(
        _route_kernel,
        grid=(n_sub // (ROUTE_SUB * ROUTE_TILES),),
        in_specs=[
            pl.BlockSpec(memory_space=pltpu.SMEM),
            pl.BlockSpec((rp, ROUTE_SUB * ROUTE_TILES, LANES), blk),
        ],
        out_specs=[
            pl.BlockSpec((2, ROUTE_SUB * ROUTE_TILES, LANES), blk),
            pl.BlockSpec((2, ROUTE_SUB * ROUTE_TILES, LANES), blk),
            pl.BlockSpec((2, ROUTE_SUB * ROUTE_TILES, LANES), blk),
            pl.BlockSpec((N_EXPERTS * ROUTE_SUB, LANES), lambda i: (0, 0)),
        ],
        out_shape=[
            pair_i,
            jax.ShapeDtypeStruct((2, n_sub, LANES), F32),
            pair_i,
            jax.ShapeDtypeStruct((N_EXPERTS * ROUTE_SUB, LANES), F32),
        ],
        scratch_shapes=[pltpu.VMEM((N_EXPERTS * ROUTE_SUB, LANES), F32)],
        compiler_params=pltpu.CompilerParams(
            dimension_semantics=("arbitrary",), vmem_limit_bytes=VMEM_LIMIT),
        name="route",
    )(bias, logits3)


def _dest_kernel(cnt_ref, ids_ref, rank_ref, dest_ref, be_ref, nb_ref, slot_ref, next_ref, start_ref, succ_ref):
    n_blocks = be_ref.shape[0]
    shift = EXPERT_ROWS.bit_length() - 1
    assert 1 << shift == EXPERT_ROWS

    @pl.when(pl.program_id(0) == 0)
    def _():
        def tail(i, carry):
            be_ref[i] = N_EXPERTS - 1
            slot_ref[i] = 0
            next_ref[i] = -1
            return carry

        lax.fori_loop(0, n_blocks, tail, 0)

        def successor(k, nxt):
            e = N_EXPERTS - 1 - k
            succ_ref[e] = nxt
            return jnp.where(cnt_ref[e] > 0, e, nxt)

        lax.fori_loop(0, N_EXPERTS, successor, jnp.int32(-1))

        def segment(e, carry):
            block0, ordinal = carry
            blocks = lax.shift_right_logical(cnt_ref[e] + (EXPERT_ROWS - 1), shift)
            start_ref[e] = lax.shift_left(block0, shift)

            def fill(j, c):
                be_ref[block0 + j] = e
                slot_ref[block0 + j] = ordinal & 1
                next_ref[block0 + j] = succ_ref[e]
                return c

            lax.fori_loop(0, blocks, fill, 0)
            return block0 + blocks, ordinal + (cnt_ref[e] > 0).astype(jnp.int32)

        used, _ = lax.fori_loop(0, N_EXPERTS, segment, (jnp.int32(0), jnp.int32(0)))
        nb_ref[0] = used

    ids = ids_ref[...]
    off = jnp.zeros(ids.shape, jnp.int32)
    for e in range(N_EXPERTS):
        off = jnp.where(ids == e, start_ref[e], off)
    dest_ref[...] = rank_ref[...] + off


def _dest(counts, ids, rank, n_blocks):
    _, n_sub, _ = ids.shape
    blk = pl.BlockSpec((2, ROUTE_SUB * ROUTE_TILES, LANES), lambda i: (0, i, 0))
    smem = pl.BlockSpec(memory_space=pltpu.SMEM)
    table = jax.ShapeDtypeStruct((n_blocks,), jnp.int32)
    return pl.pallas_call(
        _dest_kernel,
        grid=(n_sub // (ROUTE_SUB * ROUTE_TILES),),
        in_specs=[smem, blk, blk],
        out_specs=[blk, smem, smem, smem, smem],
        out_shape=[jax.ShapeDtypeStruct(ids.shape, jnp.int32), table, jax.ShapeDtypeStruct((1,), jnp.int32),
                   table, table],
        scratch_shapes=[pltpu.SMEM((N_EXPERTS,), jnp.int32), pltpu.SMEM((N_EXPERTS,), jnp.int32)],
        compiler_params=pltpu.CompilerParams(dimension_semantics=("arbitrary",)),
        name="dest",
    )(counts, ids, rank)


def _sc_mesh():
    return plsc.VectorSubcoreMesh(core_axis_name="core", subcore_axis_name="subcore")


def _sc_worker(rows_total):
    rows = rows_total // _SC_WORKERS
    wid = lax.axis_index("core") * SC_SUBCORES + lax.axis_index("subcore")
    return wid * rows, rows


def _dispatch(dest0, dest1, xn, n_slots):
    T, D = xn.shape
    W = SC_WINDOW
    rows = T // _SC_WORKERS
    assert T % (_SC_WORKERS * 2 * W) == 0

    @pl.kernel(out_type=jax.ShapeDtypeStruct((n_slots, D), xn.dtype), mesh=_sc_mesh(),
               scratch_types=[pltpu.VMEM((rows,), jnp.int32), pltpu.VMEM((rows,), jnp.int32),
                              pltpu.VMEM((2, W, D), xn.dtype),
                              pltpu.SemaphoreType.DMA((2,)), pltpu.SemaphoreType.DMA((2,))])
    def scatter_rows(x_hbm, d0_hbm, d1_hbm, xs_hbm, d0_v, d1_v, buf, in_sem, out_sem):
        base, _ = _sc_worker(T)
        pltpu.sync_copy(d0_hbm.at[pl.ds(base, rows)], d0_v)
        pltpu.sync_copy(d1_hbm.at[pl.ds(base, rows)], d1_v)

        def load(w, slot):
            return pltpu.make_async_copy(x_hbm.at[pl.ds(base + w * W, W)], buf.at[slot], in_sem.at[slot])

        def store(w, slot, d_v):
            return pltpu.make_async_copy(buf.at[slot], xs_hbm.at[d_v.at[pl.ds(w * W, W)]], out_sem.at[slot])

        def step(w, slot):
            load(w, slot).wait()
            store(w, slot, d0_v).start()
            store(w, slot, d1_v).start()
            store(w, slot, d0_v).wait()
            store(w, slot, d1_v).wait()

        n = rows // W
        load(0, 0).start()

        @pl.loop(0, n, step=2)
        def _(w):
            load(w + 1, 1).start()
            step(w, 0)

            @pl.when(w + 2 < n)
            def _():
                load(w + 2, 0).start()

            step(w + 1, 1)

    return scatter_rows(xn, dest0, dest1)


def _gather_rows(src, idx):
    M = idx.shape[0]
    D = src.shape[1]
    W = SC_WINDOW
    rows = M // _SC_WORKERS
    assert M % (_SC_WORKERS * 2 * W) == 0

    @pl.kernel(out_type=jax.ShapeDtypeStruct((M, D), src.dtype), mesh=_sc_mesh(),
               scratch_types=[pltpu.VMEM((rows,), jnp.int32), pltpu.VMEM((2, W, D), src.dtype),
                              pltpu.SemaphoreType.DMA((2,)), pltpu.SemaphoreType.DMA((2,))])
    def gather_rows(src_hbm, i_hbm, o_hbm, i_v, buf, in_sem, out_sem):
        base, _ = _sc_worker(M)
        pltpu.sync_copy(i_hbm.at[pl.ds(base, rows)], i_v)

        def load(w, slot):
            return pltpu.make_async_copy(src_hbm.at[i_v.at[pl.ds(w * W, W)]], buf.at[slot], in_sem.at[slot])

        def store(w, slot):
            return pltpu.make_async_copy(buf.at[slot], o_hbm.at[pl.ds(base + w * W, W)], out_sem.at[slot])

        n = rows // W
        load(0, 0).start()

        @pl.loop(0, n, step=2)
        def _(w):
            @pl.when(w > 0)
            def _():
                store(w - 1, 1).wait()

            load(w + 1, 1).start()
            load(w, 0).wait()
            store(w, 0).start()
            store(w, 0).wait()

            @pl.when(w + 2 < n)
            def _():
                load(w + 2, 0).start()

            load(w + 1, 1).wait()
            store(w + 1, 1).start()

        store(n - 1, 1).wait()

    return gather_rows(src, idx)


def _expert_kernel(be_ref, nb_ref, slot_ref, next_ref, x_ref, wg_hbm, wu_hbm, wd_hbm, y_ref,
                   wg_f32, wu_f32, wd_f32, wg_ref, wu_ref, wd_ref, sem):
    i = pl.program_id(0)
    used = i < nb_ref[0]
    new_expert = (i == 0) | (be_ref[i] != be_ref[jnp.maximum(i - 1, 0)])

    def fetch(e, slot):
        pairs = ((wg_hbm, wg_f32), (wu_hbm, wu_f32), (wd_hbm, wd_f32))
        return [pltpu.make_async_copy(w_hbm.at[e], w_f32.at[slot], sem.at[slot, k])
                for k, (w_hbm, w_f32) in enumerate(pairs)]

    @pl.when(used & (i == 0))
    def _():
        for copy in fetch(be_ref[0], slot_ref[0]):
            copy.start()

    @pl.when(used & new_expert)
    def _():
        slot = slot_ref[i]
        for copy in fetch(be_ref[i], slot):
            copy.wait()

        @pl.when(next_ref[i] >= 0)
        def _():
            for copy in fetch(next_ref[i], 1 - slot):
                copy.start()

        wg_ref[...] = wg_f32[slot].astype(BF16)
        wu_ref[...] = wu_f32[slot].astype(BF16)
        wd_ref[...] = wd_f32[slot].astype(BF16)

    @pl.when(used)
    def _():
        x = _unpack_pairs(x_ref[...]).astype(BF16)
        g = _dot(x, wg_ref[...])
        u = _dot(x, wu_ref[...])
        hid = (g * _sigmoid(g) * u).astype(BF16)
        y_ref[...] = _pack_pairs(_dot(hid, wd_ref[...]))

    @pl.when(jnp.logical_not(used))
    def _():
        y_ref[...] = jnp.zeros_like(y_ref)


def _experts(block_expert, n_used, block_slot, block_next, xs, w_gate, w_up, w_down):
    n_slots, half = xs.shape
    D = 2 * half
    n_blocks = n_slots // EXPERT_ROWS
    grid_spec = pltpu.PrefetchScalarGridSpec(
        num_scalar_prefetch=4,
        grid=(n_blocks,),
        in_specs=[
            pl.BlockSpec((EXPERT_ROWS, half), lambda i, be, nb, sl, nx: (jnp.minimum(i, nb[0] - 1), 0)),
            pl.BlockSpec(memory_space=pl.ANY),
            pl.BlockSpec(memory_space=pl.ANY),
            pl.BlockSpec(memory_space=pl.ANY),
        ],
        out_specs=pl.BlockSpec((EXPERT_ROWS, half), lambda i, be, nb, sl, nx: (i, 0)),
        scratch_shapes=[
            pltpu.VMEM((2, D, D_EXPERT), F32), pltpu.VMEM((2, D, D_EXPERT), F32),
            pltpu.VMEM((2, D_EXPERT, D), F32),
            pltpu.VMEM((D, D_EXPERT), BF16), pltpu.VMEM((D, D_EXPERT), BF16), pltpu.VMEM((D_EXPERT, D), BF16),
            pltpu.SemaphoreType.DMA((2, 3)),
        ],
    )
    return pl.pallas_call(
        _expert_kernel,
        grid_spec=grid_spec,
        out_shape=jax.ShapeDtypeStruct((n_slots, half), jnp.uint32),
        compiler_params=pltpu.CompilerParams(
            dimension_semantics=("arbitrary",), vmem_limit_bytes=VMEM_LIMIT),
        name="experts",
    )(block_expert, n_used, block_slot, block_next, xs, w_gate, w_up, w_down)


def _combine_kernel(y0_ref, y1_ref, h_ref, gates_ref, nfin_ref, *rest):
    o_ref = rest[-1]
    g0 = gates_ref[0].T
    g1 = gates_ref[1].T
    for r in range(gates_ref.shape[1]):
        rs = slice(r * LANES, (r + 1) * LANES)
        h = (h_ref[rs].astype(F32) + g0[:, r:r + 1] * _unpack_pairs(y0_ref[rs])
             + g1[:, r:r + 1] * _unpack_pairs(y1_ref[rs]))
        o_ref[rs] = _rms(h, nfin_ref[...])


def _combine(y01, h2, gates, norm_final, out_prev, row0, total_rows):
    T, D = h2.shape
    n_t = T // MOVE_ROWS
    in_specs = [
        pl.BlockSpec((MOVE_ROWS, D // 2), lambda i: (i, 0)),
        pl.BlockSpec((MOVE_ROWS, D // 2), lambda i: (i + n_t, 0)),
        pl.BlockSpec((MOVE_ROWS, D), lambda i: (i, 0)),
        pl.BlockSpec((2, MOVE_ROWS // LANES, LANES), lambda i: (0, i, 0)),
        pl.BlockSpec((1, D), lambda i: (0, 0)),
    ]
    args = [y01, y01, h2, gates, norm_final]
    aliases = {}
    if out_prev is not None:
        in_specs.append(pl.BlockSpec(memory_space=pl.ANY))
        args.append(out_prev)
        aliases = {len(args) - 1: 0}
    return pl.pallas_call(
        _combine_kernel,
        grid=(n_t,),
        in_specs=in_specs,
        out_specs=pl.BlockSpec((MOVE_ROWS, D), lambda i: (i + row0 // MOVE_ROWS, 0)),
        out_shape=jax.ShapeDtypeStruct((total_rows, D), F32),
        input_output_aliases=aliases,
        compiler_params=pltpu.CompilerParams(
            dimension_semantics=("arbitrary",), vmem_limit_bytes=VMEM_LIMIT),
        name="combine",
    )(*args)


def kernel(x, mem, norm_mix, w_in, gmlp_ln, gmlp_w_spatial, gmlp_b_spatial, gmlp_beta, hgrn_lb_logits, hgrn_out_gain, w_out, norm_xattn, norm_mem, w_xq, w_xkv, w_xo, norm_ffn, w_router_group, b_router_group, w_router_expert, b_router_expert, w_expert_gate, w_expert_up, w_expert_down, norm_final):
    B, S, D = x.shape
    T = B * S
    depth = w_in.shape[0]
    assert depth == 1 and hgrn_lb_logits.shape[0] == 2
    assert D == D_MODEL and mem.shape[1] == N_MEM and w_in.shape[2] == D_IN_PROJ
    assert S % MIX_ROWS == 0 and S % ATT_ROWS == 0 and B % MEMKV_BATCHES == 0
    l = 0
    row = lambda p: p.reshape(1, -1)

    h1 = _mixer(x, row(norm_mix[l]), w_in[l], row(gmlp_ln[l]), gmlp_w_spatial[l],
                gmlp_b_spatial[l].T, row(gmlp_beta[l]), hgrn_lb_logits, row(hgrn_out_gain[l]),
                w_out[l])
    qk_mem, vo_mem = _memkv(mem, row(norm_mem[l]), w_xkv[l], w_xq[l], w_xo[l])

    w_router = jnp.concatenate([w_router_group[l].T, w_router_expert[l].T], axis=0)
    w_router = jnp.pad(w_router, ((0, ROUTER_ROWS - w_router.shape[0]), (0, 0)))
    w_router = w_router.astype(BF16)
    bias = jnp.concatenate([b_router_group[l], b_router_expert[l]]).astype(F32)

    assert sum(PART_BATCHES) == B
    out = None
    b0 = 0
    for b_part in PART_BATCHES:
        t_part = b_part * S
        n_blocks = (2 * t_part) // EXPERT_ROWS + N_EXPERTS
        h2, xn, logits = _xattn(h1, row(norm_xattn[l]), qk_mem, vo_mem, row(norm_ffn[l]), w_router, b0, b_part)
        ids, gates, rank, counts = _route(bias, logits)

        counts = counts[::ROUTE_SUB, 0].astype(jnp.int32)
        dest, block_expert, n_used, block_slot, block_next = _dest(counts, ids, rank, n_blocks)
        dest = dest.reshape(2, t_part)
        xs = _dispatch(dest[0], dest[1], xn, n_blocks * EXPERT_ROWS)
        yb = _experts(block_expert, n_used, block_slot, block_next, xs,
                      w_expert_gate[l], w_expert_up[l], w_expert_down[l])
        y01 = _gather_rows(yb, dest.reshape(2 * t_part))
        out = _combine(y01, h2.reshape(t_part, D), gates, row(norm_final), out, b0 * S, T)
        b0 += b_part
    return out.reshape(B, S, D)
```
